```python
import math
import jax, jax.numpy as jnp
from jax import lax
import numpy as np

D_MODEL = 1024
BATCH = 8
SEQ = 2048
DEPTH = 4
DEC_BATCH = 128
DEC_SEQ = 1
PAST_LEN = 16384
PAGE_SIZE = 128

SSD_INNER = D_MODEL
SSD_HEAD_DIM = 64
SSD_HEADS = SSD_INNER // SSD_HEAD_DIM
SSD_STATE = 128
SSD_GROUPS = 2
SSD_CONV = 4
SSD_CONV_DIM = SSD_INNER + 2 * SSD_GROUPS * SSD_STATE
GDN_HEAD_K = 128
GDN_HEAD_V = 128
GDN_HEADS = D_MODEL // GDN_HEAD_V
GDN_KEY_DIM = GDN_HEADS * GDN_HEAD_K
GDN_VAL_DIM = GDN_HEADS * GDN_HEAD_V
GDN_CONV = 4
GDN_CONV_DIM = 2 * GDN_KEY_DIM + GDN_VAL_DIM
MIX_WIDTH = SSD_INNER + GDN_VAL_DIM
IN_SPLITS = (SSD_INNER, SSD_CONV_DIM, SSD_HEADS, GDN_CONV_DIM, GDN_VAL_DIM, GDN_HEADS, GDN_HEADS)
IN_DIM = sum(IN_SPLITS)
CHUNK = 64
N_MEM = 256
XA_HEADS = 4
XA_HEAD_DIM = D_MODEL // XA_HEADS
D_FF = ((8 * D_MODEL // 3 + 127) // 128) * 128
FFN_CONV = 3
RMS_EPS = 1e-6

kernel_name = 'hymba_ssd_gdn_memxattn_convffn_step'


def _rmsnorm(x, w, eps=RMS_EPS):
    xf = x.astype(jnp.float32)
    y = xf * lax.rsqrt(jnp.mean(xf * xf, axis=-1, keepdims=True) + eps)
    return (y * w.astype(jnp.float32)).astype(x.dtype)


def _l2norm(x, eps=1e-6):
    xf = x.astype(jnp.float32)
    return xf * lax.rsqrt(jnp.sum(xf * xf, axis=-1, keepdims=True) + eps)


def _causal_dwconv(x_full, w, b=None):
    k_w = w.shape[0]
    length = x_full.shape[1] - k_w + 1
    out = x_full[:, 0:length] * w[0]
    for k in range(1, k_w):
        out = out + x_full[:, k:k + length] * w[k]
    if b is not None:
        out = out + b
    return out


def _chunking(length):
    cs = length if length <= CHUNK else CHUNK
    return cs, (-length) % cs


def _pad_time(t, pad):
    return jnp.pad(t, [(0, 0), (0, pad)] + [(0, 0)] * (t.ndim - 2))


def _split_points():
    pts, acc = [], 0
    for s in IN_SPLITS[:-1]:
        acc += s
        pts.append(acc)
    return pts


def _ssd_scan(x, dt, a_neg, bm, cm, h0):
    bsz, length, nh, hp = x.shape
    cs, pad = _chunking(length)
    x, dt, bm, cm = (_pad_time(t, pad) for t in (x, dt, bm, cm))
    nc = (length + pad) // cs
    xc = x.reshape(bsz, nc, cs, nh, hp)
    dtc = dt.reshape(bsz, nc, cs, nh)
    bc = bm.reshape(bsz, nc, cs, nh, -1)
    cc = cm.reshape(bsz, nc, cs, nh, -1)
    cum = jnp.cumsum(dtc * a_neg, axis=2)
    cum_h = jnp.moveaxis(cum, 3, 2)
    causal = jnp.tril(jnp.ones((cs, cs), dtype=bool))
    seg = jnp.exp(jnp.where(causal, cum_h[..., :, None] - cum_h[..., None, :], -jnp.inf))
    xdt = xc * dtc[..., None]
    scores = jnp.einsum('bclhn,bcshn->bchls', cc, bc) * seg
    y = jnp.einsum('bchls,bcshp->bclhp', scores, xdt)
    to_end = jnp.exp(cum[:, :, -1:, :] - cum)
    chunk_states = jnp.einsum('bcshn,bcsh,bcshp->bchpn', bc, to_end, xdt)
    chunk_decay = jnp.exp(cum[:, :, -1, :])

    def step(h, inp):
        dec, st = inp
        return h * dec[:, :, None, None] + st, h

    h_last, h_in = lax.scan(step, h0, (jnp.moveaxis(chunk_decay, 1, 0), jnp.moveaxis(chunk_states, 1, 0)))
    h_in = jnp.moveaxis(h_in, 0, 1)
    y = y + jnp.einsum('bclhn,bchpn->bclhp', cc, h_in) * jnp.exp(cum)[..., None]
    return y.reshape(bsz, nc * cs, nh, hp)[:, :length], h_last


def _gdn_scan(q, k, v, g, beta, s0):
    bsz, length, nh, dk = q.shape
    cs, pad = _chunking(length)
    q = q * dk ** -0.5
    q, k, v, g, beta = (_pad_time(t, pad) for t in (q, k, v, g, beta))
    nc = (length + pad) // cs

    def chunks(t):
        return jnp.moveaxis(t.reshape((bsz, nc, cs) + t.shape[2:]), 3, 1)

    qc, kc, vc = chunks(q), chunks(k), chunks(v)
    gc = jnp.cumsum(chunks(g), axis=-1)
    bc = chunks(beta)
    causal = jnp.tril(jnp.ones((cs, cs), dtype=bool))
    strict = jnp.tril(jnp.ones((cs, cs), dtype=bool), k=-1)
    decay = jnp.exp(jnp.where(causal, gc[..., :, None] - gc[..., None, :], -jnp.inf))
    kb = kc * bc[..., None]
    a_strict = jnp.where(strict, jnp.einsum('bhcik,bhcjk->bhcij', kb, kc) * decay, 0.0)
    t_mat = a_strict + jnp.eye(cs, dtype=a_strict.dtype)
    u = lax.linalg.triangular_solve(t_mat, vc * bc[..., None], left_side=True, lower=True, unit_diagonal=True)
    w = lax.linalg.triangular_solve(t_mat, kb * jnp.exp(gc)[..., None], left_side=True, lower=True, unit_diagonal=True)
    attn = jnp.einsum('bhcik,bhcjk->bhcij', qc, kc) * decay
    q_dec = qc * jnp.exp(gc)[..., None]
    k_dec = kc * jnp.exp(gc[..., -1:] - gc)[..., None]
    g_last = jnp.exp(gc[..., -1])

    def step(s, inp):
        u_c, w_c, attn_c, qd_c, kd_c, gl_c = inp
        v_new = u_c - jnp.einsum('bhsk,bhkv->bhsv', w_c, s)
        o = jnp.einsum('bhsk,bhkv->bhsv', qd_c, s) + jnp.einsum('bhij,bhjv->bhiv', attn_c, v_new)
        s = s * gl_c[..., None, None] + jnp.einsum('bhsk,bhsv->bhkv', kd_c, v_new)
        return s, o

    xs = tuple(jnp.moveaxis(t, 2, 0) for t in (u, w, attn, q_dec, k_dec, g_last))
    s_last, o = lax.scan(step, s0, xs)
    o = jnp.transpose(o, (1, 0, 3, 2, 4)).reshape(bsz, nc * cs, nh, -1)[:, :length]
    return o, s_last


def _mixer(h, ssd_hist, ssd_h, gdn_hist, gdn_s, lp):
    f32 = jnp.float32
    bsz, length, _ = h.shape
    proj = h @ lp['w_in']
    z_ssd, xbc, dt_raw, qkv, z_gdn, b_raw, a_raw = jnp.split(proj, _split_points(), axis=-1)
    xbc_full = jnp.concatenate([ssd_hist.astype(h.dtype), xbc], axis=1)
    new_ssd_hist = xbc_full[:, -(SSD_CONV - 1):]
    xbc = jax.nn.silu(_causal_dwconv(xbc_full, lp['ssd_conv_w'], lp['ssd_conv_b']))
    xs, bm, cm = jnp.split(xbc, [SSD_INNER, SSD_INNER + SSD_GROUPS * SSD_STATE], axis=-1)
    rep = SSD_HEADS // SSD_GROUPS
    xs = xs.astype(f32).reshape(bsz, length, SSD_HEADS, SSD_HEAD_DIM)
    bm = jnp.repeat(bm.astype(f32).reshape(bsz, length, SSD_GROUPS, SSD_STATE), rep, axis=2)
    cm = jnp.repeat(cm.astype(f32).reshape(bsz, length, SSD_GROUPS, SSD_STATE), rep, axis=2)
    dt = jax.nn.softplus(dt_raw.astype(f32) + lp['ssd_dt_bias'].astype(f32))
    a_neg = -jnp.exp(lp['ssd_a_log'].astype(f32))
    y, ssd_h_new = _ssd_scan(xs, dt, a_neg, bm, cm, ssd_h.astype(f32))
    y = y + lp['ssd_d'].astype(f32)[:, None] * xs
    gs = SSD_INNER // SSD_GROUPS
    y = y.reshape(bsz, length, SSD_GROUPS, gs) * jax.nn.silu(z_ssd.astype(f32)).reshape(bsz, length, SSD_GROUPS, gs)
    y = _rmsnorm(y, lp['ssd_norm_w'].reshape(SSD_GROUPS, gs)).reshape(bsz, length, SSD_INNER)
    qkv_full = jnp.concatenate([gdn_hist.astype(h.dtype), qkv], axis=1)
    new_gdn_hist = qkv_full[:, -(GDN_CONV - 1):]
    qkv = jax.nn.silu(_causal_dwconv(qkv_full, lp['gdn_conv_w']))
    q, k, v = jnp.split(qkv, [GDN_KEY_DIM, 2 * GDN_KEY_DIM], axis=-1)
    q = _l2norm(q.reshape(bsz, length, GDN_HEADS, GDN_HEAD_K))
    k = _l2norm(k.reshape(bsz, length, GDN_HEADS, GDN_HEAD_K))
    v = v.astype(f32).reshape(bsz, length, GDN_HEADS, GDN_HEAD_V)
    beta = jax.nn.sigmoid(b_raw.astype(f32))
    g = -jnp.exp(lp['gdn_a_log'].astype(f32)) * jax.nn.softplus(a_raw.astype(f32) + lp['gdn_dt_bias'].astype(f32))
    o, gdn_s_new = _gdn_scan(q, k, v, g, beta, gdn_s.astype(f32))
    o = _rmsnorm(o, lp['gdn_norm_w']) * jax.nn.silu(z_gdn.astype(f32).reshape(bsz, length, GDN_HEADS, GDN_HEAD_V))
    o = o.reshape(bsz, length, GDN_VAL_DIM)
    mixed = jnp.concatenate([y, o], axis=-1).astype(h.dtype) @ lp['w_out']
    return (mixed, new_ssd_hist.astype(ssd_hist.dtype), ssd_h_new.astype(ssd_h.dtype),
            new_gdn_hist.astype(gdn_hist.dtype), gdn_s_new.astype(gdn_s.dtype))


def _cross_attn(h, mem_k, mem_v, wq, wo):
    bsz, length, _ = h.shape
    q = (h @ wq).reshape(bsz, length, XA_HEADS, XA_HEAD_DIM)
    s = jnp.einsum('blhd,bmhd->bhlm', q.astype(jnp.float32), mem_k.astype(jnp.float32)) * XA_HEAD_DIM ** -0.5
    p = jax.nn.softmax(s, axis=-1)
    o = jnp.einsum('bhlm,bmhd->blhd', p, mem_v.astype(jnp.float32)).reshape(bsz, length, D_MODEL)
    return o.astype(h.dtype) @ wo


def _conv_ffn(h, ffn_hist, lp):
    gate = h @ lp['ffn_w_gate']
    up = h @ lp['ffn_w_up']
    g_full = jnp.concatenate([ffn_hist.astype(h.dtype), gate], axis=1)
    new_hist = g_full[:, -(FFN_CONV - 1):]
    gate = _causal_dwconv(g_full, lp['ffn_conv_w'], lp['ffn_conv_b'])
    return (jax.nn.silu(gate) * up) @ lp['ffn_w_down'], new_hist.astype(ffn_hist.dtype)


def _decoder_layer(x, mem_k, mem_v, ssd_hist, ssd_h, gdn_hist, gdn_s, ffn_hist, lp):
    m, ssd_hist, ssd_h, gdn_hist, gdn_s = _mixer(_rmsnorm(x, lp['norm_mix_w']), ssd_hist, ssd_h, gdn_hist, gdn_s, lp)
    x = x + m
    x = x + _cross_attn(_rmsnorm(x, lp['norm_xa_w']), mem_k, mem_v, lp['xa_wq'], lp['xa_wo'])
    f, ffn_hist = _conv_ffn(_rmsnorm(x, lp['norm_ffn_w']), ffn_hist, lp)
    x = x + f
    return x, ssd_hist, ssd_h, gdn_hist, gdn_s, ffn_hist


def setup_inputs(seed: int = 0) -> dict:
    key = jax.random.key(seed)
    ks = iter(jax.random.split(key, 48))
    f32 = jnp.float32

    def nrm(shape, scale):
        return jax.random.normal(next(ks), shape, f32) * scale

    def gain(shape):
        return 1.0 + nrm(shape, 0.02)

    def dt_bias(nh):
        u = jax.random.uniform(next(ks), (DEPTH, nh), f32)
        dt = jnp.exp(u * (math.log(0.1) - math.log(0.001)) + math.log(0.001))
        return dt + jnp.log(-jnp.expm1(-dt))

    def a_log(nh):
        return jnp.log(jax.random.uniform(next(ks), (DEPTH, nh), f32, minval=1.0, maxval=16.0))

    return {
        'x_prompt': nrm((BATCH, SEQ, D_MODEL), 1.0),
        'x_sample': nrm((DEC_BATCH, DEC_SEQ, D_MODEL), 1.0),
        'mem_prompt': nrm((BATCH, N_MEM, D_MODEL), 1.0),
        'cache_mem_k': nrm((DEPTH, DEC_BATCH, N_MEM, XA_HEADS, XA_HEAD_DIM), 1.0),
        'cache_mem_v': nrm((DEPTH, DEC_BATCH, N_MEM, XA_HEADS, XA_HEAD_DIM), 1.0),
        'state_ssd_conv': nrm((DEPTH, DEC_BATCH, SSD_CONV - 1, SSD_CONV_DIM), 1.0),
        'state_ssd': nrm((DEPTH, DEC_BATCH, SSD_HEADS, SSD_HEAD_DIM, SSD_STATE), 0.1),
        'state_gdn_conv': nrm((DEPTH, DEC_BATCH, GDN_CONV - 1, GDN_CONV_DIM), 1.0),
        'state_gdn': nrm((DEPTH, DEC_BATCH, GDN_HEADS, GDN_HEAD_K, GDN_HEAD_V), 0.1),
        'state_ffn_conv': nrm((DEPTH, DEC_BATCH, FFN_CONV - 1, D_FF), 1.0),
        'norm_mix_w': gain((DEPTH, D_MODEL)),
        'w_in': nrm((DEPTH, D_MODEL, IN_DIM), D_MODEL ** -0.5),
        'ssd_conv_w': nrm((DEPTH, SSD_CONV, SSD_CONV_DIM), SSD_CONV ** -0.5),
        'ssd_conv_b': nrm((DEPTH, SSD_CONV_DIM), 0.02),
        'ssd_dt_bias': dt_bias(SSD_HEADS),
        'ssd_a_log': a_log(SSD_HEADS),
        'ssd_d': gain((DEPTH, SSD_HEADS)),
        'ssd_norm_w': gain((DEPTH, SSD_INNER)),
        'gdn_conv_w': nrm((DEPTH, GDN_CONV, GDN_CONV_DIM), GDN_CONV ** -0.5),
        'gdn_dt_bias': dt_bias(GDN_HEADS),
        'gdn_a_log': a_log(GDN_HEADS),
        'gdn_norm_w': gain((DEPTH, GDN_HEAD_V)),
        'w_out': nrm((DEPTH, MIX_WIDTH, D_MODEL), MIX_WIDTH ** -0.5),
        'norm_xa_w': gain((DEPTH, D_MODEL)),
        'norm_mem_w': gain((DEPTH, D_MODEL)),
        'xa_wq': nrm((DEPTH, D_MODEL, D_MODEL), D_MODEL ** -0.5),
        'xa_wk': nrm((DEPTH, D_MODEL, D_MODEL), D_MODEL ** -0.5),
        'xa_wv': nrm((DEPTH, D_MODEL, D_MODEL), D_MODEL ** -0.5),
        'xa_wo': nrm((DEPTH, D_MODEL, D_MODEL), D_MODEL ** -0.5),
        'norm_ffn_w': gain((DEPTH, D_MODEL)),
        'ffn_w_gate': nrm((DEPTH, D_MODEL, D_FF), D_MODEL ** -0.5),
        'ffn_w_up': nrm((DEPTH, D_MODEL, D_FF), D_MODEL ** -0.5),
        'ffn_conv_w': nrm((DEPTH, FFN_CONV, D_FF), FFN_CONV ** -0.5),
        'ffn_conv_b': nrm((DEPTH, D_FF), 0.02),
        'ffn_w_down': nrm((DEPTH, D_FF, D_MODEL), D_FF ** -0.5),
        'final_norm_w': gain((D_MODEL,)),
    }


def reference(x_prompt, x_sample, mem_prompt, cache_mem_k, cache_mem_v, state_ssd_conv, state_ssd,
              state_gdn_conv, state_gdn, state_ffn_conv, norm_mix_w, w_in, ssd_conv_w, ssd_conv_b,
              ssd_dt_bias, ssd_a_log, ssd_d, ssd_norm_w, gdn_conv_w, gdn_dt_bias, gdn_a_log, gdn_norm_w,
              w_out, norm_xa_w, norm_mem_w, xa_wq, xa_wk, xa_wv, xa_wo, norm_ffn_w, ffn_w_gate, ffn_w_up,
              ffn_conv_w, ffn_conv_b, ffn_w_down, final_norm_w):
    bp = x_prompt.shape[0]
    n_mem = mem_prompt.shape[1]
    pdt = x_prompt.dtype
    p_ssd_hist0 = jnp.zeros((bp, SSD_CONV - 1, SSD_CONV_DIM), pdt)
    p_ssd_h0 = jnp.zeros((bp, SSD_HEADS, SSD_HEAD_DIM, SSD_STATE), pdt)
    p_gdn_hist0 = jnp.zeros((bp, GDN_CONV - 1, GDN_CONV_DIM), pdt)
    p_gdn_s0 = jnp.zeros((bp, GDN_HEADS, GDN_HEAD_K, GDN_HEAD_V), pdt)
    p_ffn_hist0 = jnp.zeros((bp, FFN_CONV - 1, D_FF), pdt)

    xp, xs = x_prompt, x_sample
    mkp, mvp = [], []
    p_sc, p_sh, p_gc, p_gs, p_fc = [], [], [], [], []
    s_sc, s_sh, s_gc, s_gs, s_fc = [], [], [], [], []
    for i in range(DEPTH):
        lp = {
            'norm_mix_w': norm_mix_w[i], 'w_in': w_in[i], 'ssd_conv_w': ssd_conv_w[i], 'ssd_conv_b': ssd_conv_b[i],
            'ssd_dt_bias': ssd_dt_bias[i], 'ssd_a_log': ssd_a_log[i], 'ssd_d': ssd_d[i], 'ssd_norm_w': ssd_norm_w[i],
            'gdn_conv_w': gdn_conv_w[i], 'gdn_dt_bias': gdn_dt_bias[i], 'gdn_a_log': gdn_a_log[i],
            'gdn_norm_w': gdn_norm_w[i], 'w_out': w_out[i], 'norm_xa_w': norm_xa_w[i], 'xa_wq': xa_wq[i],
            'xa_wo': xa_wo[i], 'norm_ffn_w': norm_ffn_w[i], 'ffn_w_gate': ffn_w_gate[i], 'ffn_w_up': ffn_w_up[i],
            'ffn_conv_w': ffn_conv_w[i], 'ffn_conv_b': ffn_conv_b[i], 'ffn_w_down': ffn_w_down[i],
        }
        mem_n = _rmsnorm(mem_prompt, norm_mem_w[i])
        mk = (mem_n @ xa_wk[i]).reshape(bp, n_mem, XA_HEADS, XA_HEAD_DIM)
        mv = (mem_n @ xa_wv[i]).reshape(bp, n_mem, XA_HEADS, XA_HEAD_DIM)
        mkp.append(mk)
        mvp.append(mv)
        xp, a, b, c, d, e = _decoder_layer(xp, mk, mv, p_ssd_hist0, p_ssd_h0, p_gdn_hist0, p_gdn_s0, p_ffn_hist0, lp)
        p_sc.append(a); p_sh.append(b); p_gc.append(c); p_gs.append(d); p_fc.append(e)
        xs, a, b, c, d, e = _decoder_layer(xs, cache_mem_k[i], cache_mem_v[i], state_ssd_conv[i], state_ssd[i],
                                           state_gdn_conv[i], state_gdn[i], state_ffn_conv[i], lp)
        s_sc.append(a); s_sh.append(b); s_gc.append(c); s_gs.append(d); s_fc.append(e)
    y_prompt = _rmsnorm(xp, final_norm_w)
    y_sample = _rmsnorm(xs, final_norm_w)
    return (y_prompt, y_sample, jnp.stack(mkp), jnp.stack(mvp),
            jnp.stack(p_sc), jnp.stack(p_sh), jnp.stack(p_gc), jnp.stack(p_gs), jnp.stack(p_fc),
            jnp.stack(s_sc), jnp.stack(s_sh), jnp.stack(s_gc), jnp.stack(s_gs), jnp.stack(s_fc))
```

```python
import functools

import jax
import jax.numpy as jnp
from jax import lax
from jax.experimental import pallas as pl
from jax.experimental.pallas import tpu as pltpu

F32 = jnp.float32
BF16 = jnp.bfloat16

D_MODEL = 1024
SSD_HEADS = 16
SSD_HEAD_DIM = 64
SSD_STATE = 128
SSD_GROUPS = 2
SSD_INNER = SSD_HEADS * SSD_HEAD_DIM
SSD_BC = SSD_GROUPS * SSD_STATE
SSD_CONV = 4
GDN_HEADS = 8
GDN_HEAD_K = 128
GDN_HEAD_V = 128
GDN_DIM = GDN_HEADS * GDN_HEAD_V
GDN_CONV = 4
XA_HEADS = 4
XA_HEAD_DIM = D_MODEL // XA_HEADS
FFN_CONV = 3
RMS_EPS = 1e-6
L2_EPS = 1e-6

SUBLANES = 8
LANES = 128
VMEM_LIMIT_BYTES = 56 * 1024 * 1024

COL_Z_SSD = 0
COL_XS = 1024
COL_Q = 2048
COL_K = 3072
COL_V = 4096
COL_Z_GDN = 5120
COL_B = 6144
COL_C = 6400
COL_SMALL = 6656
PROJ_COLS = 7168
LANE_DT = 0
LANE_BETA = 16
LANE_G = 24

SSD_CHUNK = 256
GDN_CHUNK = 128
SOLVE_PASSES = 3
DEC_TILE = 8


def _params(*sem):
    return pltpu.CompilerParams(dimension_semantics=sem, vmem_limit_bytes=VMEM_LIMIT_BYTES)


def _mm(a, b):
    return jnp.dot(a.astype(BF16), b.astype(BF16), preferred_element_type=F32)


def _mm_nt(a, b):
    return lax.dot_general(a.astype(BF16), b.astype(BF16), (((1,), (1,)), ((), ())),
                           preferred_element_type=F32)


def _split(x, parts):
    out = []
    r = x
    for _ in range(parts - 1):
        p = r.astype(BF16)
        out.append(p)
        r = r - p.astype(F32)
    out.append(r.astype(BF16))
    return out


def _mm_rhs_split(a_exact, b, parts=3):
    a16 = a_exact.astype(BF16)
    acc = None
    for p in _split(b, parts):
        t = jnp.dot(a16, p, preferred_element_type=F32)
        acc = t if acc is None else acc + t
    return acc


def _mm_lhs_split(a, b_exact, parts=3):
    b16 = b_exact.astype(BF16)
    acc = None
    for p in _split(a, parts):
        t = jnp.dot(p, b16, preferred_element_type=F32)
        acc = t if acc is None else acc + t
    return acc


def _mm_multi(a, b, passes):
    if passes == 1:
        return _mm(a, b)
    ah, al = _split(a, 2)
    bh, bl = _split(b, 2)
    return (jnp.dot(ah, bh, preferred_element_type=F32)
            + jnp.dot(ah, bl, preferred_element_type=F32)
            + jnp.dot(al, bh, preferred_element_type=F32))


def _silu(x):
    return x / (1.0 + jnp.exp(-x))


def _sigmoid(x):
    return 1.0 / (1.0 + jnp.exp(-x))


def _softplus(x):
    return jnp.maximum(x, 0.0) + jnp.log1p(jnp.exp(-jnp.abs(x)))


def _col(arr, lane):
    idx = lax.broadcasted_iota(jnp.int32, arr.shape, 1)
    return jnp.sum(jnp.where(idx == lane, arr, 0.0), axis=1, keepdims=True)


def _gate_rows(sm, bias, alog):
    sp = _softplus(sm + bias)
    sg = _sigmoid(sm)
    lane = lax.broadcasted_iota(jnp.int32, alog.shape, 1)
    has_decay = (lane < LANE_BETA) | ((lane >= LANE_G) & (lane < LANE_G + GDN_HEADS))
    neg_a = jnp.where(has_decay, -jnp.exp(alog), 0.0)
    return sp, sg, sp * neg_a


def _norm_mm_kernel(x_ref, nw_ref, w_ref, o_ref, xn_ref):
    @pl.when(pl.program_id(1) == 0)
    def _():
        x = x_ref[...]
        ms = jnp.mean(x * x, axis=-1, keepdims=True)
        xn_ref[...] = (x * lax.rsqrt(ms + RMS_EPS) * nw_ref[...]).astype(BF16)

    o_ref[...] = jnp.dot(xn_ref[...], w_ref[...], preferred_element_type=F32)


def _norm_mm(x, nw, w, tm, tn):
    m, k = x.shape
    n = w.shape[1]
    assert m % tm == 0 and n % tn == 0
    return pl.pallas_call(
        _norm_mm_kernel,
        grid=(m // tm, n // tn),
        in_specs=[pl.BlockSpec((tm, k), lambda i, j: (i, 0)),
                  pl.BlockSpec((1, k), lambda i, j: (0, 0)),
                  pl.BlockSpec((k, tn), lambda i, j: (0, j))],
        out_specs=pl.BlockSpec((tm, tn), lambda i, j: (i, j)),
        out_shape=jax.ShapeDtypeStruct((m, n), F32),
        scratch_shapes=[pltpu.VMEM((tm, k), BF16)],
        compiler_params=_params("parallel", "arbitrary"),
    )(x, nw.reshape(1, k), w)


def _mm_res_kernel(*refs, n_in):
    res_ref = refs[0]
    a_refs = refs[1:1 + n_in]
    w_refs = refs[1 + n_in:1 + 2 * n_in]
    o_ref = refs[1 + 2 * n_in]
    acc = res_ref[...]
    for a_ref, w_ref in zip(a_refs, w_refs):
        acc = acc + jnp.dot(a_ref[...].astype(BF16), w_ref[...], preferred_element_type=F32)
    o_ref[...] = acc


def _mm_res(res, a_list, w_list, tm):
    m, n = res.shape
    assert m % tm == 0
    n_in = len(a_list)
    in_specs = [pl.BlockSpec((tm, n), lambda i: (i, 0))]
    in_specs += [pl.BlockSpec((tm, a.shape[1]), lambda i: (i, 0)) for a in a_list]
    in_specs += [pl.BlockSpec(w.shape, lambda i: (0, 0)) for w in w_list]
    return pl.pallas_call(
        functools.partial(_mm_res_kernel, n_in=n_in),
        grid=(m // tm,),
        in_specs=in_specs,
        out_specs=pl.BlockSpec((tm, n), lambda i: (i, 0)),
        out_shape=jax.ShapeDtypeStruct((m, n), F32),
        compiler_params=_params("parallel"),
    )(res, *a_list, *w_list)


def _rmsnorm_kernel(x_ref, nw_ref, o_ref):
    x = x_ref[...]
    ms = jnp.mean(x * x, axis=-1, keepdims=True)
    o_ref[...] = x * lax.rsqrt(ms + RMS_EPS) * nw_ref[...]


def _rmsnorm(x, nw, tm):
    m, k = x.shape
    return pl.pallas_call(
        _rmsnorm_kernel,
        grid=(m // tm,),
        in_specs=[pl.BlockSpec((tm, k), lambda i: (i, 0)), pl.BlockSpec((1, k), lambda i: (0, 0))],
        out_specs=pl.BlockSpec((tm, k), lambda i: (i, 0)),
        out_shape=jax.ShapeDtypeStruct((m, k), F32),
        compiler_params=_params("parallel"),
    )(x, nw.reshape(1, k))


HIST_ROWS = SUBLANES


def _chunk_conv(ext_ref, x_ref, cw_ref, cb, first, taps):
    cs = x_ref.shape[0]

    @pl.when(first)
    def _():
        ext_ref[0:HIST_ROWS, :] = jnp.zeros((HIST_ROWS, ext_ref.shape[1]), F32)

    ext_ref[HIST_ROWS:HIST_ROWS + cs, :] = x_ref[...]
    acc = None
    for k in range(taps):
        start = HIST_ROWS - (taps - 1) + k
        t = ext_ref[start:start + cs, :] * cw_ref[k:k + 1, :]
        acc = t if acc is None else acc + t
    if cb is not None:
        acc = acc + cb
    ext_ref[0:HIST_ROWS, :] = ext_ref[cs:cs + HIST_ROWS, :]
    return acc


def _causal_mask(cs, strict=False):
    r = lax.broadcasted_iota(jnp.int32, (cs, cs), 0)
    c = lax.broadcasted_iota(jnp.int32, (cs, cs), 1)
    return (r > c) if strict else (r >= c)


def _ssd_prefill_kernel(z_ref, xs_ref, b_ref, c_ref, sm_ref,
                        cwx_ref, cwb_ref, cwc_ref, cbx_ref, cbb_ref, cbc_ref,
                        bias_ref, alog_ref, dexp_ref, nw_ref,
                        y_ref, st_ref,
                        extx, extb, extc, hst):
    c_idx = pl.program_id(1)
    first = c_idx == 0
    cs = xs_ref.shape[0]

    @pl.when(first)
    def _():
        hst[...] = jnp.zeros(hst.shape, F32)

    xs = _silu(_chunk_conv(extx, xs_ref, cwx_ref, cbx_ref[...], first, SSD_CONV))
    bm = _silu(_chunk_conv(extb, b_ref, cwb_ref, cbb_ref[...], first, SSD_CONV))
    cm = _silu(_chunk_conv(extc, c_ref, cwc_ref, cbc_ref[...], first, SSD_CONV))

    sp, _, dta = _gate_rows(sm_ref[...], bias_ref[...], alog_ref[...])
    causal = _causal_mask(cs)
    cum = _mm_rhs_split(jnp.where(causal, 1.0, 0.0), dta)
    cum_row = cum.T

    lane_lo = lax.broadcasted_iota(jnp.int32, (1, LANES), 1) < SSD_HEAD_DIM
    row_lo = lax.broadcasted_iota(jnp.int32, (LANES, 1), 0) < SSD_HEAD_DIM
    heads_per_group = SSD_HEADS // SSD_GROUPS
    for g in range(SSD_GROUPS):
        cg = cm[:, g * SSD_STATE:(g + 1) * SSD_STATE]
        bg = bm[:, g * SSD_STATE:(g + 1) * SSD_STATE]
        cb = _mm_nt(cg, bg)
        rows_g = heads_per_group * SSD_HEAD_DIM
        y_state = _mm_nt(cg, hst[g * rows_g:(g + 1) * rows_g, :])
        for hp in range(heads_per_group // 2):
            h0 = g * heads_per_group + 2 * hp
            lo = h0 * SSD_HEAD_DIM
            xs_pair = xs[:, lo:lo + LANES]
            ci = [_col(cum, LANE_DT + h0), _col(cum, LANE_DT + h0 + 1)]
            dti = [_col(sp, LANE_DT + h0), _col(sp, LANE_DT + h0 + 1)]
            xdt_pair = xs_pair * jnp.where(lane_lo, dti[0], dti[1])
            y_pair = None
            for e in range(2):
                cj = cum_row[LANE_DT + h0 + e:LANE_DT + h0 + e + 1, :]
                seg = jnp.exp(jnp.where(causal, ci[e] - cj, -jnp.inf))
                keep = lane_lo if e == 0 else jnp.logical_not(lane_lo)
                t = _mm(cb * seg, jnp.where(keep, xdt_pair, 0.0))
                y_pair = t if y_pair is None else y_pair + t
            y_pair = y_pair + y_state[:, 2 * hp * SSD_HEAD_DIM:2 * hp * SSD_HEAD_DIM + LANES] * \
                jnp.where(lane_lo, jnp.exp(ci[0]), jnp.exp(ci[1]))
            y_pair = y_pair + dexp_ref[:, lo:lo + LANES] * xs_pair
            y_ref[:, lo:lo + LANES] = y_pair
            cl = [ci[0][cs - 1:cs, :], ci[1][cs - 1:cs, :]]
            to_end = jnp.where(lane_lo, jnp.exp(cl[0] - ci[0]), jnp.exp(cl[1] - ci[1]))
            contrib = _mm((xdt_pair * to_end).T, bg)
            dec = jnp.where(row_lo, jnp.exp(cl[0]), jnp.exp(cl[1]))
            hst[lo:lo + LANES, :] = hst[lo:lo + LANES, :] * dec + contrib

    gs = SSD_INNER // SSD_GROUPS
    for g in range(SSD_GROUPS):
        yg = y_ref[:, g * gs:(g + 1) * gs] * _silu(z_ref[:, g * gs:(g + 1) * gs])
        ms = jnp.mean(yg * yg, axis=-1, keepdims=True)
        y_ref[:, g * gs:(g + 1) * gs] = yg * lax.rsqrt(ms + RMS_EPS) * nw_ref[:, g * gs:(g + 1) * gs]

    @pl.when(c_idx == pl.num_programs(1) - 1)
    def _():
        st_ref[...] = hst[...]


def _ssd_prefill(proj, bsz, seq, lw):
    cs = SSD_CHUNK
    assert seq % cs == 0
    nc = seq // cs
    row = lambda b, c: b * nc + c
    pcol = lambda width, off: pl.BlockSpec((cs, width), lambda b, c: (row(b, c), off // width))
    full = lambda a: pl.BlockSpec(a.shape, lambda b, c: (0,) * a.ndim)
    weights = [lw['ssd_cw_x'], lw['ssd_cw_b'], lw['ssd_cw_c'], lw['ssd_cb_x'], lw['ssd_cb_b'], lw['ssd_cb_c'],
               lw['gate_bias'], lw['gate_alog'], lw['ssd_d_exp'], lw['ssd_norm_w']]
    y, st = pl.pallas_call(
        _ssd_prefill_kernel,
        grid=(bsz, nc),
        in_specs=[pcol(SSD_INNER, COL_Z_SSD), pcol(SSD_INNER, COL_XS), pcol(SSD_BC, COL_B), pcol(SSD_BC, COL_C),
                  pcol(LANES, COL_SMALL)] + [full(w) for w in weights],
        out_specs=[pl.BlockSpec((cs, SSD_INNER), lambda b, c: (row(b, c), 0)),
                   pl.BlockSpec((None, SSD_INNER, SSD_STATE), lambda b, c: (b, 0, 0))],
        out_shape=[jax.ShapeDtypeStruct((bsz * seq, SSD_INNER), F32),
                   jax.ShapeDtypeStruct((bsz, SSD_INNER, SSD_STATE), F32)],
        scratch_shapes=[pltpu.VMEM((HIST_ROWS + cs, SSD_INNER), F32),
                        pltpu.VMEM((HIST_ROWS + cs, SSD_BC), F32),
                        pltpu.VMEM((HIST_ROWS + cs, SSD_BC), F32),
                        pltpu.VMEM((SSD_INNER, SSD_STATE), F32)],
        compiler_params=_params("parallel", "arbitrary"),
    )(proj, proj, proj, proj, proj, *weights)
    return y, st


def _unit_lower_inverse(a_strict, passes):
    cs = a_strict.shape[0]
    eye = jnp.where(_causal_mask(cs) & jnp.logical_not(_causal_mask(cs, strict=True)), 1.0, 0.0)
    x = -a_strict
    p = eye + x
    span = 2
    while span < cs:
        x = _mm_multi(x, x, passes)
        p = p + _mm_multi(p, x, passes)
        span *= 2
    return p


def _gdn_prefill_kernel(q_ref, k_ref, v_ref, z_ref, sm_ref,
                        cwq_ref, cwk_ref, cwv_ref, bias_ref, alog_ref, nw_ref,
                        o_ref, st_ref,
                        extq, extk, extv, sst):
    c_idx = pl.program_id(1)
    first = c_idx == 0
    cs = q_ref.shape[0]

    @pl.when(first)
    def _():
        sst[...] = jnp.zeros(sst.shape, F32)

    qc = _silu(_chunk_conv(extq, q_ref, cwq_ref, None, first, GDN_CONV))
    kc = _silu(_chunk_conv(extk, k_ref, cwk_ref, None, first, GDN_CONV))
    vc = _silu(_chunk_conv(extv, v_ref, cwv_ref, None, first, GDN_CONV))

    _, sg, g_step = _gate_rows(sm_ref[...], bias_ref[...], alog_ref[...])
    causal = _causal_mask(cs)
    strict = _causal_mask(cs, strict=True)
    gc = _mm_rhs_split(jnp.where(causal, 1.0, 0.0), g_step)
    gc_row = gc.T

    for h in range(GDN_HEADS):
        lo = h * GDN_HEAD_K
        qh = qc[:, lo:lo + GDN_HEAD_K]
        kh = kc[:, lo:lo + GDN_HEAD_K]
        vh = vc[:, lo:lo + GDN_HEAD_V]
        qn = qh * lax.rsqrt(jnp.sum(qh * qh, axis=-1, keepdims=True) + L2_EPS) * (GDN_HEAD_K ** -0.5)
        kn = kh * lax.rsqrt(jnp.sum(kh * kh, axis=-1, keepdims=True) + L2_EPS)
        gi = _col(gc, LANE_G + h)
        bi = _col(sg, LANE_BETA + h)
        gj = gc_row[LANE_G + h:LANE_G + h + 1, :]
        decay = jnp.exp(jnp.where(causal, gi - gj, -jnp.inf))
        kb = kn * bi
        a_strict = jnp.where(strict, _mm_nt(kb, kn) * decay, 0.0)
        t_inv = _unit_lower_inverse(a_strict, SOLVE_PASSES)
        egi = jnp.exp(gi)
        u = _mm_multi(t_inv, vh * bi, SOLVE_PASSES)
        w = _mm_multi(t_inv, kb * egi, SOLVE_PASSES)
        attn = _mm_nt(qn, kn) * decay
        s = sst[lo:lo + GDN_HEAD_K, :]
        v_new = u - _mm(w, s)
        o_h = _mm(qn * egi, s) + _mm(attn, v_new)
        g_last = gi[cs - 1:cs, :]
        k_dec = kn * jnp.exp(g_last - gi)
        sst[lo:lo + GDN_HEAD_K, :] = s * jnp.exp(g_last) + _mm(k_dec.T, v_new)
        ms = jnp.mean(o_h * o_h, axis=-1, keepdims=True)
        o_ref[:, lo:lo + GDN_HEAD_V] = (o_h * lax.rsqrt(ms + RMS_EPS) * nw_ref[...]) * \
            _silu(z_ref[:, lo:lo + GDN_HEAD_V])

    @pl.when(c_idx == pl.num_programs(1) - 1)
    def _():
        st_ref[...] = sst[...]


def _gdn_prefill(proj, bsz, seq, lw):
    cs = GDN_CHUNK
    assert seq % cs == 0
    nc = seq // cs
    row = lambda b, c: b * nc + c
    pcol = lambda width, off: pl.BlockSpec((cs, width), lambda b, c: (row(b, c), off // width))
    full = lambda a: pl.BlockSpec(a.shape, lambda b, c: (0,) * a.ndim)
    weights = [lw['gdn_cw_q'], lw['gdn_cw_k'], lw['gdn_cw_v'], lw['gate_bias'], lw['gate_alog'], lw['gdn_norm_w']]
    o, st = pl.pallas_call(
        _gdn_prefill_kernel,
        grid=(bsz, nc),
        in_specs=[pcol(GDN_DIM, COL_Q), pcol(GDN_DIM, COL_K), pcol(GDN_DIM, COL_V), pcol(GDN_DIM, COL_Z_GDN),
                  pcol(LANES, COL_SMALL)] + [full(w) for w in weights],
        out_specs=[pl.BlockSpec((cs, GDN_DIM), lambda b, c: (row(b, c), 0)),
                   pl.BlockSpec((None, GDN_HEADS * GDN_HEAD_K, GDN_HEAD_V), lambda b, c: (b, 0, 0))],
        out_shape=[jax.ShapeDtypeStruct((bsz * seq, GDN_DIM), F32),
                   jax.ShapeDtypeStruct((bsz, GDN_HEADS * GDN_HEAD_K, GDN_HEAD_V), F32)],
        scratch_shapes=[pltpu.VMEM((HIST_ROWS + cs, GDN_DIM), F32),
                        pltpu.VMEM((HIST_ROWS + cs, GDN_DIM), F32),
                        pltpu.VMEM((HIST_ROWS + cs, GDN_DIM), F32),
                        pltpu.VMEM((GDN_HEADS * GDN_HEAD_K, GDN_HEAD_V), F32)],
        compiler_params=_params("parallel", "arbitrary"),
    )(proj, proj, proj, proj, proj, *weights)
    return o, st


def _softmax_rows(s):
    m = jnp.max(s, axis=-1, keepdims=True)
    e = jnp.exp(s - m)
    return e / jnp.sum(e, axis=-1, keepdims=True)


def _xattn_prefill_kernel(x_ref, nw_ref, wq_ref, mk_ref, mv_ref, wo_ref, o_ref):
    x = x_ref[...]
    ms = jnp.mean(x * x, axis=-1, keepdims=True)
    h = x * lax.rsqrt(ms + RMS_EPS) * nw_ref[...]
    q = jnp.dot(h.astype(BF16), wq_ref[...], preferred_element_type=F32)
    outs = []
    for hd in range(XA_HEADS):
        lo = hd * XA_HEAD_DIM
        s = _mm_nt(q[:, lo:lo + XA_HEAD_DIM], mk_ref[:, lo:lo + XA_HEAD_DIM]) * (XA_HEAD_DIM ** -0.5)
        outs.append(_mm(_softmax_rows(s), mv_ref[:, lo:lo + XA_HEAD_DIM]))
    att = jnp.concatenate(outs, axis=-1)
    o_ref[...] = x + jnp.dot(att.astype(BF16), wo_ref[...], preferred_element_type=F32)


def _xattn_prefill(x, nw, wq, mk, mv, wo, bsz, seq, tq):
    n_mem = mk.shape[1]
    nt = seq // tq
    return pl.pallas_call(
        _xattn_prefill_kernel,
        grid=(bsz, nt),
        in_specs=[pl.BlockSpec((tq, D_MODEL), lambda b, t: (b * nt + t, 0)),
                  pl.BlockSpec((1, D_MODEL), lambda b, t: (0, 0)),
                  pl.BlockSpec((D_MODEL, D_MODEL), lambda b, t: (0, 0)),
                  pl.BlockSpec((None, n_mem, D_MODEL), lambda b, t: (b, 0, 0)),
                  pl.BlockSpec((None, n_mem, D_MODEL), lambda b, t: (b, 0, 0)),
                  pl.BlockSpec((D_MODEL, D_MODEL), lambda b, t: (0, 0))],
        out_specs=pl.BlockSpec((tq, D_MODEL), lambda b, t: (b * nt + t, 0)),
        out_shape=jax.ShapeDtypeStruct(x.shape, F32),
        compiler_params=_params("parallel", "arbitrary"),
    )(x, nw.reshape(1, D_MODEL), wq, mk, mv, wo)


def _xattn_decode_kernel(q_ref, mk_ref, mv_ref, o_ref):
    tb = q_ref.shape[0]
    lane_head = lax.broadcasted_iota(jnp.int32, (SUBLANES, D_MODEL), 1) // XA_HEAD_DIM
    row = lax.broadcasted_iota(jnp.int32, (SUBLANES, D_MODEL), 0)
    own = lane_head == row
    for i in range(tb):
        qm = jnp.where(own, q_ref[i], 0.0)
        s = _mm_nt(qm, mk_ref[i]) * (XA_HEAD_DIM ** -0.5)
        full = _mm(_softmax_rows(s), mv_ref[i])
        o_ref[i] = jnp.sum(jnp.where(own, full, 0.0), axis=0, keepdims=True)


def _xattn_decode(q, mk, mv, tb):
    nb, n_mem, _ = mk.shape
    return pl.pallas_call(
        _xattn_decode_kernel,
        grid=(nb // tb,),
        in_specs=[pl.BlockSpec((tb, 1, D_MODEL), lambda i: (i, 0, 0)),
                  pl.BlockSpec((tb, n_mem, D_MODEL), lambda i: (i, 0, 0)),
                  pl.BlockSpec((tb, n_mem, D_MODEL), lambda i: (i, 0, 0))],
        out_specs=pl.BlockSpec((tb, 1, D_MODEL), lambda i: (i, 0, 0)),
        out_shape=jax.ShapeDtypeStruct((nb, 1, D_MODEL), F32),
        compiler_params=_params("parallel"),
    )(q.reshape(nb, 1, D_MODEL), mk, mv).reshape(nb, D_MODEL)


def _ffn_prefill_kernel(x_ref, g_ref, gprev_ref, u_ref, cw_ref, cb_ref, wd_ref, o_ref, ext, *, tiles_per_seq):
    tm = g_ref.shape[0]
    starts_seq = pl.program_id(0) % tiles_per_seq == 0
    ext[0:HIST_ROWS, :] = jnp.where(starts_seq, 0.0, gprev_ref[...])
    ext[HIST_ROWS:HIST_ROWS + tm, :] = g_ref[...]
    acc = cb_ref[...]
    for k in range(FFN_CONV):
        start = HIST_ROWS - (FFN_CONV - 1) + k
        acc = acc + ext[start:start + tm, :] * cw_ref[k:k + 1, :]
    hmid = _silu(acc) * u_ref[...]
    o_ref[...] = x_ref[...] + jnp.dot(hmid.astype(BF16), wd_ref[...], preferred_element_type=F32)


def _ffn_prefill(x, gu, cw, cb, wd, seq, tm):
    m = x.shape[0]
    d_ff = wd.shape[0]
    assert seq % tm == 0 and tm % HIST_ROWS == 0
    prev_blocks = tm // HIST_ROWS
    return pl.pallas_call(
        functools.partial(_ffn_prefill_kernel, tiles_per_seq=seq // tm),
        grid=(m // tm,),
        in_specs=[pl.BlockSpec((tm, D_MODEL), lambda i: (i, 0)),
                  pl.BlockSpec((tm, d_ff), lambda i: (i, 0)),
                  pl.BlockSpec((HIST_ROWS, d_ff), lambda i: (jnp.maximum(i * prev_blocks - 1, 0), 0)),
                  pl.BlockSpec((tm, d_ff), lambda i: (i, 1)),
                  pl.BlockSpec((FFN_CONV, d_ff), lambda i: (0, 0)),
                  pl.BlockSpec((1, d_ff), lambda i: (0, 0)),
                  pl.BlockSpec((d_ff, D_MODEL), lambda i: (0, 0))],
        out_specs=pl.BlockSpec((tm, D_MODEL), lambda i: (i, 0)),
        out_shape=jax.ShapeDtypeStruct((m, D_MODEL), F32),
        scratch_shapes=[pltpu.VMEM((HIST_ROWS + tm, d_ff), F32)],
        compiler_params=_params("parallel"),
    )(x, gu, gu, gu, cw, cb.reshape(1, d_ff), wd)


def _ffn_decode_kernel(x_ref, g_ref, h0_ref, h1_ref, u_ref, cw_ref, cb_ref, wd_ref, o_ref):
    acc = cb_ref[...] + h0_ref[...] * cw_ref[0:1, :] + h1_ref[...] * cw_ref[1:2, :] + g_ref[...] * cw_ref[2:3, :]
    hmid = _silu(acc) * u_ref[...]
    o_ref[...] = x_ref[...] + jnp.dot(hmid.astype(BF16), wd_ref[...], preferred_element_type=F32)


def _ffn_decode(x, gu, hist0, hist1, cw, cb, wd):
    m = x.shape[0]
    d_ff = wd.shape[0]
    rows = lambda width, col: pl.BlockSpec((m, width), lambda i: (0, col))
    return pl.pallas_call(
        _ffn_decode_kernel,
        grid=(1,),
        in_specs=[rows(D_MODEL, 0), rows(d_ff, 0), rows(d_ff, 0), rows(d_ff, 0), rows(d_ff, 1),
                  pl.BlockSpec((FFN_CONV, d_ff), lambda i: (0, 0)),
                  pl.BlockSpec((1, d_ff), lambda i: (0, 0)),
                  pl.BlockSpec((d_ff, D_MODEL), lambda i: (0, 0))],
        out_specs=rows(D_MODEL, 0),
        out_shape=jax.ShapeDtypeStruct((m, D_MODEL), F32),
        compiler_params=_params("arbitrary"),
    )(x, gu, hist0, hist1, gu, cw, cb.reshape(1, d_ff), wd)


def _expand_lanes(rows, first_lane, width, n_out):
    k = lax.broadcasted_iota(jnp.int32, (LANES, n_out), 0)
    l = lax.broadcasted_iota(jnp.int32, (LANES, n_out), 1)
    sel = jnp.where(l // width == k - first_lane, 1.0, 0.0)
    return _mm_lhs_split(rows, sel)


def _decode_rows_kernel(proj_ref, sh0, sh1, sh2, gh0, gh1, gh2,
                        scw_ref, scb_ref, gcw_ref, bias_ref, alog_ref,
                        xs_ref, xdt_ref, b_ref, c_ref, sdec_ref,
                        q_ref, k_ref, v_ref, beta_ref, gdec_ref):
    def conv(hists, new, cw, cb):
        acc = new * cw[3:4, :]
        for t, hr in enumerate(hists):
            acc = acc + hr[...] * cw[t:t + 1, :]
        return acc if cb is None else acc + cb

    xbc_new = jnp.concatenate([proj_ref[:, COL_XS:COL_XS + SSD_INNER],
                               proj_ref[:, COL_B:COL_B + SSD_BC],
                               proj_ref[:, COL_C:COL_C + SSD_BC]], axis=-1)
    xbc = _silu(conv((sh0, sh1, sh2), xbc_new, scw_ref, scb_ref[...]))
    sp, sg, log_dec = _gate_rows(proj_ref[:, COL_SMALL:COL_SMALL + LANES], bias_ref[...], alog_ref[...])
    xs = xbc[:, 0:SSD_INNER]
    xs_ref[...] = xs
    xdt_ref[...] = xs * _expand_lanes(sp, LANE_DT, SSD_HEAD_DIM, SSD_INNER)
    b_ref[...] = xbc[:, SSD_INNER:SSD_INNER + SSD_BC]
    c_ref[...] = xbc[:, SSD_INNER + SSD_BC:SSD_INNER + 2 * SSD_BC]
    sdec_ref[...] = jnp.exp(_expand_lanes(log_dec, LANE_DT, SSD_HEAD_DIM, SSD_INNER))

    qkv_new = proj_ref[:, COL_Q:COL_Q + 3 * GDN_DIM]
    qkv = _silu(conv((gh0, gh1, gh2), qkv_new, gcw_ref, None))
    for h in range(GDN_HEADS):
        lo = h * GDN_HEAD_K
        qh = qkv[:, lo:lo + GDN_HEAD_K]
        kh = qkv[:, GDN_DIM + lo:GDN_DIM + lo + GDN_HEAD_K]
        q_ref[:, lo:lo + GDN_HEAD_K] = qh * lax.rsqrt(jnp.sum(qh * qh, axis=-1, keepdims=True) + L2_EPS) * \
            (GDN_HEAD_K ** -0.5)
        k_ref[:, lo:lo + GDN_HEAD_K] = kh * lax.rsqrt(jnp.sum(kh * kh, axis=-1, keepdims=True) + L2_EPS)
    v_ref[...] = qkv[:, 2 * GDN_DIM:3 * GDN_DIM]
    beta_ref[...] = _expand_lanes(sg, LANE_BETA, GDN_HEAD_V, GDN_DIM)
    gdec_ref[...] = jnp.exp(_expand_lanes(log_dec, LANE_G, GDN_HEAD_V, GDN_DIM))


def _decode_rows(proj, ssd_hist, gdn_hist, lw):
    m = proj.shape[0]
    ins = [proj] + [ssd_hist[:, t] for t in range(SSD_CONV - 1)] + [gdn_hist[:, t] for t in range(GDN_CONV - 1)]
    ins += [lw['ssd_conv_w'], lw['ssd_conv_b'], lw['gdn_conv_w'], lw['gate_bias'], lw['gate_alog']]
    widths = [SSD_INNER, SSD_INNER, SSD_BC, SSD_BC, SSD_INNER, GDN_DIM, GDN_DIM, GDN_DIM, GDN_DIM, GDN_DIM]
    return pl.pallas_call(
        _decode_rows_kernel,
        grid=(1,),
        in_specs=[pl.BlockSpec(a.shape, lambda i, nd=a.ndim: (0,) * nd) for a in ins],
        out_specs=[pl.BlockSpec((m, w), lambda i: (0, 0)) for w in widths],
        out_shape=[jax.ShapeDtypeStruct((m, w), F32) for w in widths],
        compiler_params=_params("arbitrary"),
    )(*ins)


def _rows_to_cols(rows, i, parts):
    tb = rows.shape[0]
    r = lax.broadcasted_iota(jnp.int32, (tb, LANES), 0)
    sel = jnp.where(r == i, 1.0, 0.0).astype(BF16)
    acc = None
    for p in _split(rows, parts):
        t = lax.dot_general(p, sel, (((0,), (0,)), ((), ())), preferred_element_type=F32)
        acc = t if acc is None else acc + t
    return acc


def _only_row(rows, i):
    r = lax.broadcasted_iota(jnp.int32, rows.shape, 0)
    return jnp.where(r == i, rows, 0.0)


def _decode_state_kernel(xdt_ref, b_ref, c_ref, sdec_ref, q_ref, k_ref, v_ref, beta_ref, gdec_ref,
                         hs_ref, ss_ref,
                         y_ref, o_ref, hs_out, ss_out):
    tb = xdt_ref.shape[0]
    xdt = xdt_ref[...]
    bm = b_ref[...]
    cm = c_ref[...]
    sdec = sdec_ref[...]
    qn = q_ref[...]
    kn = k_ref[...]
    vv = v_ref[...]
    beta = beta_ref[...]
    gdec = gdec_ref[...]
    rows_g = SSD_INNER // SSD_GROUPS

    y_acc = jnp.zeros((tb, SSD_INNER), F32)
    ks_acc = jnp.zeros((tb, GDN_DIM), F32)
    for i in range(tb):
        dec_c = _rows_to_cols(sdec, i, 2)
        xdt_c = _rows_to_cols(xdt, i, 1)
        pieces = []
        for g in range(SSD_GROUPS):
            rs = slice(g * rows_g, (g + 1) * rows_g)
            bg = bm[i:i + 1, g * SSD_STATE:(g + 1) * SSD_STATE]
            h_new = hs_ref[i, rs, :] * dec_c[rs, :] + xdt_c[rs, :] * bg
            hs_out[i, rs, :] = h_new
            cg = _only_row(cm[:, g * SSD_STATE:(g + 1) * SSD_STATE], i)
            pieces.append(_mm_nt(cg, h_new))
        y_acc = y_acc + jnp.concatenate(pieces, axis=-1)
        g_c = _rows_to_cols(gdec, i, 2)
        s_dec = ss_ref[i] * g_c
        ss_out[i] = s_dec
        pieces = []
        for h in range(GDN_HEADS):
            rs = slice(h * GDN_HEAD_K, (h + 1) * GDN_HEAD_K)
            pieces.append(_mm(_only_row(kn[:, rs], i), s_dec[rs, :]))
        ks_acc = ks_acc + jnp.concatenate(pieces, axis=-1)
    y_ref[...] = y_acc

    delta = beta * (vv - ks_acc)
    o_acc = jnp.zeros((tb, GDN_DIM), F32)
    for i in range(tb):
        k_c = _rows_to_cols(kn, i, 1)
        pieces = []
        for h in range(GDN_HEADS):
            rs = slice(h * GDN_HEAD_K, (h + 1) * GDN_HEAD_K)
            d_row = delta[i:i + 1, rs]
            s_new = ss_out[i, rs, :] + k_c[rs, :] * d_row
            ss_out[i, rs, :] = s_new
            pieces.append(_mm(_only_row(qn[:, rs], i), s_new))
        o_acc = o_acc + jnp.concatenate(pieces, axis=-1)
    o_ref[...] = o_acc


def _decode_state(rows, hs, ss, tb):
    xs, xdt, bm, cm, sdec, qn, kn, vv, beta, gdec = rows
    nb = xdt.shape[0]
    row_ins = [xdt, bm, cm, sdec, qn, kn, vv, beta, gdec]
    rspec = lambda a: pl.BlockSpec((tb, a.shape[1]), lambda i: (i, 0))
    sspec = lambda a: pl.BlockSpec((tb,) + a.shape[1:], lambda i: (i, 0, 0))
    return pl.pallas_call(
        _decode_state_kernel,
        grid=(nb // tb,),
        in_specs=[rspec(a) for a in row_ins] + [sspec(hs), sspec(ss)],
        out_specs=[rspec(xdt), rspec(qn), sspec(hs), sspec(ss)],
        out_shape=[jax.ShapeDtypeStruct(xdt.shape, F32), jax.ShapeDtypeStruct(qn.shape, F32),
                   jax.ShapeDtypeStruct(hs.shape, F32), jax.ShapeDtypeStruct(ss.shape, F32)],
        compiler_params=_params("parallel"),
    )(*row_ins, hs, ss)


def _decode_out_kernel(x_ref, y_ref, xs_ref, o_ref, proj_ref, dexp_ref, snw_ref, gnw_ref, w1_ref, w2_ref, out_ref):
    y = y_ref[...] + dexp_ref[...] * xs_ref[...]
    gs = SSD_INNER // SSD_GROUPS
    ys = []
    for g in range(SSD_GROUPS):
        yg = y[:, g * gs:(g + 1) * gs] * _silu(proj_ref[:, COL_Z_SSD + g * gs:COL_Z_SSD + (g + 1) * gs])
        ms = jnp.mean(yg * yg, axis=-1, keepdims=True)
        ys.append(yg * lax.rsqrt(ms + RMS_EPS) * snw_ref[:, g * gs:(g + 1) * gs])
    os = []
    for h in range(GDN_HEADS):
        lo = h * GDN_HEAD_V
        oh = o_ref[:, lo:lo + GDN_HEAD_V]
        ms = jnp.mean(oh * oh, axis=-1, keepdims=True)
        os.append(oh * lax.rsqrt(ms + RMS_EPS) * gnw_ref[...] *
                  _silu(proj_ref[:, COL_Z_GDN + lo:COL_Z_GDN + lo + GDN_HEAD_V]))
    yn = jnp.concatenate(ys, axis=-1)
    on = jnp.concatenate(os, axis=-1)
    out_ref[...] = x_ref[...] + _mm(yn, w1_ref[...]) + _mm(on, w2_ref[...])


def _decode_out(x, y, xs, o, proj, lw):
    ins = [x, y, xs, o, proj, lw['ssd_d_exp'], lw['ssd_norm_w'], lw['gdn_norm_w'], lw['w_out_ssd'], lw['w_out_gdn']]
    return pl.pallas_call(
        _decode_out_kernel,
        grid=(1,),
        in_specs=[pl.BlockSpec(a.shape, lambda i, nd=a.ndim: (0,) * nd) for a in ins],
        out_specs=pl.BlockSpec(x.shape, lambda i: (0, 0)),
        out_shape=jax.ShapeDtypeStruct(x.shape, F32),
        compiler_params=_params("arbitrary"),
    )(*ins)


def _prep_layer(i, p):
    w_in = p['w_in'][i]
    o_zs = 0
    o_xbc = o_zs + SSD_INNER
    o_dt = o_xbc + SSD_INNER + 2 * SSD_BC
    o_qkv = o_dt + SSD_HEADS
    o_zg = o_qkv + 3 * GDN_DIM
    o_b = o_zg + GDN_DIM
    o_a = o_b + GDN_HEADS
    used = SSD_HEADS + 2 * GDN_HEADS
    w_perm = jnp.concatenate([
        w_in[:, o_zs:o_zs + SSD_INNER],
        w_in[:, o_xbc:o_xbc + SSD_INNER],
        w_in[:, o_qkv:o_qkv + 3 * GDN_DIM],
        w_in[:, o_zg:o_zg + GDN_DIM],
        w_in[:, o_xbc + SSD_INNER:o_xbc + SSD_INNER + 2 * SSD_BC],
        w_in[:, o_dt:o_dt + SSD_HEADS],
        w_in[:, o_b:o_b + GDN_HEADS],
        w_in[:, o_a:o_a + GDN_HEADS],
        jnp.zeros((D_MODEL, PROJ_COLS - COL_SMALL - used), F32)], axis=1).astype(BF16)
    lane_pad = jnp.zeros((LANES - used,), F32)
    scw = p['ssd_conv_w'][i]
    scb = p['ssd_conv_b'][i].reshape(1, -1)
    gcw = p['gdn_conv_w'][i]
    return {
        'w_in': w_perm,
        'norm_mix_w': p['norm_mix_w'][i],
        'ssd_conv_w': scw, 'ssd_conv_b': scb, 'gdn_conv_w': gcw,
        'ssd_cw_x': scw[:, :SSD_INNER], 'ssd_cw_b': scw[:, SSD_INNER:SSD_INNER + SSD_BC],
        'ssd_cw_c': scw[:, SSD_INNER + SSD_BC:],
        'ssd_cb_x': scb[:, :SSD_INNER], 'ssd_cb_b': scb[:, SSD_INNER:SSD_INNER + SSD_BC],
        'ssd_cb_c': scb[:, SSD_INNER + SSD_BC:],
        'gdn_cw_q': gcw[:, :GDN_DIM], 'gdn_cw_k': gcw[:, GDN_DIM:2 * GDN_DIM], 'gdn_cw_v': gcw[:, 2 * GDN_DIM:],
        'gate_bias': jnp.concatenate([p['ssd_dt_bias'][i], jnp.zeros((GDN_HEADS,), F32), p['gdn_dt_bias'][i],
                                      lane_pad]).reshape(1, LANES),
        'gate_alog': jnp.concatenate([p['ssd_a_log'][i], jnp.zeros((GDN_HEADS,), F32), p['gdn_a_log'][i],
                                      lane_pad]).reshape(1, LANES),
        'ssd_d_exp': jnp.repeat(p['ssd_d'][i], SSD_HEAD_DIM).reshape(1, SSD_INNER),
        'ssd_norm_w': p['ssd_norm_w'][i].reshape(1, SSD_INNER),
        'gdn_norm_w': p['gdn_norm_w'][i].reshape(1, GDN_HEAD_V),
        'w_out_ssd': p['w_out'][i, :SSD_INNER].astype(BF16),
        'w_out_gdn': p['w_out'][i, SSD_INNER:].astype(BF16),
        'norm_xa_w': p['norm_xa_w'][i], 'norm_mem_w': p['norm_mem_w'][i],
        'xa_wq': p['xa_wq'][i].astype(BF16), 'xa_wo': p['xa_wo'][i].astype(BF16),
        'xa_wkv': jnp.concatenate([p['xa_wk'][i], p['xa_wv'][i]], axis=1).astype(BF16),
        'norm_ffn_w': p['norm_ffn_w'][i],
        'ffn_w_gu': jnp.concatenate([p['ffn_w_gate'][i], p['ffn_w_up'][i]], axis=1).astype(BF16),
        'ffn_conv_w': p['ffn_conv_w'][i], 'ffn_conv_b': p['ffn_conv_b'][i],
        'ffn_w_down': p['ffn_w_down'][i].astype(BF16),
    }


def _unpermute_conv_rows(proj_rows):
    ssd = jnp.concatenate([proj_rows[..., COL_XS:COL_XS + SSD_INNER],
                           proj_rows[..., COL_B:COL_B + SSD_BC],
                           proj_rows[..., COL_C:COL_C + SSD_BC]], axis=-1)
    gdn = proj_rows[..., COL_Q:COL_Q + 3 * GDN_DIM]
    return ssd, gdn


def _tile(m, pref):
    t = min(m, pref)
    while m % t:
        t //= 2
    return t


def kernel(x_prompt, x_sample, mem_prompt, cache_mem_k, cache_mem_v, state_ssd_conv, state_ssd, state_gdn_conv, state_gdn, state_ffn_conv, norm_mix_w, w_in, ssd_conv_w, ssd_conv_b, ssd_dt_bias, ssd_a_log, ssd_d, ssd_norm_w, gdn_conv_w, gdn_dt_bias, gdn_a_log, gdn_norm_w, w_out, norm_xa_w, norm_mem_w, xa_wq, xa_wk, xa_wv, xa_wo, norm_ffn_w, ffn_w_gate, ffn_w_up, ffn_conv_w, ffn_conv_b, ffn_w_down, final_norm_w):
    params = dict(norm_mix_w=norm_mix_w, w_in=w_in, ssd_conv_w=ssd_conv_w, ssd_conv_b=ssd_conv_b,
                  ssd_dt_bias=ssd_dt_bias, ssd_a_log=ssd_a_log, ssd_d=ssd_d, ssd_norm_w=ssd_norm_w,
                  gdn_conv_w=gdn_conv_w, gdn_dt_bias=gdn_dt_bias, gdn_a_log=gdn_a_log, gdn_norm_w=gdn_norm_w,
                  w_out=w_out, norm_xa_w=norm_xa_w, norm_mem_w=norm_mem_w, xa_wq=xa_wq, xa_wk=xa_wk, xa_wv=xa_wv,
                  xa_wo=xa_wo, norm_ffn_w=norm_ffn_w, ffn_w_gate=ffn_w_gate, ffn_w_up=ffn_w_up,
                  ffn_conv_w=ffn_conv_w, ffn_conv_b=ffn_conv_b, ffn_w_down=ffn_w_down)
    depth = w_in.shape[0]
    bsz, seq, _ = x_prompt.shape
    nb = x_sample.shape[0]
    n_mem = mem_prompt.shape[1]
    d_ff = ffn_w_down.shape[1]
    mp = bsz * seq

    xp = x_prompt.reshape(mp, D_MODEL)
    xs = x_sample.reshape(nb, D_MODEL)
    mem = mem_prompt.reshape(bsz * n_mem, D_MODEL)
    tm_p = _tile(mp, 1024)
    tm_mem = _tile(bsz * n_mem, 1024)
    dec_tile = _tile(nb, DEC_TILE)

    mkp, mvp = [], []
    p_sc, p_sh, p_gc, p_gs, p_fc = [], [], [], [], []
    s_sc, s_sh, s_gc, s_gs, s_fc = [], [], [], [], []
    for i in range(depth):
        lw = _prep_layer(i, params)

        kv = _norm_mm(mem, lw['norm_mem_w'], lw['xa_wkv'], tm_mem, 512)
        mk = kv[:, :D_MODEL].reshape(bsz, n_mem, D_MODEL)
        mv = kv[:, D_MODEL:].reshape(bsz, n_mem, D_MODEL)
        mkp.append(mk.reshape(bsz, n_mem, XA_HEADS, XA_HEAD_DIM))
        mvp.append(mv.reshape(bsz, n_mem, XA_HEADS, XA_HEAD_DIM))

        proj = _norm_mm(xp, lw['norm_mix_w'], lw['w_in'], tm_p, 512)
        y, h_new = _ssd_prefill(proj, bsz, seq, lw)
        o, s_new = _gdn_prefill(proj, bsz, seq, lw)
        tail = proj.reshape(bsz, seq, PROJ_COLS)[:, seq - (SSD_CONV - 1):]
        ssd_tail, gdn_tail = _unpermute_conv_rows(tail)
        p_sc.append(ssd_tail)
        p_gc.append(gdn_tail)
        p_sh.append(h_new.reshape(bsz, SSD_HEADS, SSD_HEAD_DIM, SSD_STATE))
        p_gs.append(s_new.reshape(bsz, GDN_HEADS, GDN_HEAD_K, GDN_HEAD_V))
        xp = _mm_res(xp, [y, o], [lw['w_out_ssd'], lw['w_out_gdn']], _tile(mp, 512))
        xp = _xattn_prefill(xp, lw['norm_xa_w'], lw['xa_wq'], mk, mv, lw['xa_wo'], bsz, seq, _tile(seq, 512))
        gu = _norm_mm(xp, lw['norm_ffn_w'], lw['ffn_w_gu'], tm_p, 512)
        p_fc.append(gu.reshape(bsz, seq, 2 * d_ff)[:, seq - (FFN_CONV - 1):, :d_ff])
        xp = _ffn_prefill(xp, gu, lw['ffn_conv_w'], lw['ffn_conv_b'], lw['ffn_w_down'], seq, _tile(seq, 256))

        proj_s = _norm_mm(xs, lw['norm_mix_w'], lw['w_in'], nb, 512)
        ssd_new, gdn_new = _unpermute_conv_rows(proj_s)
        s_sc.append(jnp.concatenate([state_ssd_conv[i][:, 1:], ssd_new[:, None]], axis=1))
        s_gc.append(jnp.concatenate([state_gdn_conv[i][:, 1:], gdn_new[:, None]], axis=1))
        rows = _decode_rows(proj_s, state_ssd_conv[i], state_gdn_conv[i], lw)
        y_s, o_s, hs_new, ss_new = _decode_state(
            rows, state_ssd[i].reshape(nb, SSD_INNER, SSD_STATE),
            state_gdn[i].reshape(nb, GDN_HEADS * GDN_HEAD_K, GDN_HEAD_V), dec_tile)
        s_sh.append(hs_new.reshape(nb, SSD_HEADS, SSD_HEAD_DIM, SSD_STATE))
        s_gs.append(ss_new.reshape(nb, GDN_HEADS, GDN_HEAD_K, GDN_HEAD_V))
        xs = _decode_out(xs, y_s, rows[0], o_s, proj_s, lw)
        q_s = _norm_mm(xs, lw['norm_xa_w'], lw['xa_wq'], nb, 512)
        att = _xattn_decode(q_s, cache_mem_k[i].reshape(nb, n_mem, D_MODEL),
                            cache_mem_v[i].reshape(nb, n_mem, D_MODEL), _tile(nb, 4))
        xs = _mm_res(xs, [att], [lw['xa_wo']], nb)
        gu_s = _norm_mm(xs, lw['norm_ffn_w'], lw['ffn_w_gu'], nb, 512)
        s_fc.append(jnp.concatenate([state_ffn_conv[i][:, 1:], gu_s[:, None, :d_ff]], axis=1))
        xs = _ffn_decode(xs, gu_s, state_ffn_conv[i][:, 0], state_ffn_conv[i][:, 1],
                         lw['ffn_conv_w'], lw['ffn_conv_b'], lw['ffn_w_down'])

    y_prompt = _rmsnorm(xp, final_norm_w, tm_p).reshape(bsz, seq, D_MODEL)
    y_sample = _rmsnorm(xs, final_norm_w, nb).reshape(nb, 1, D_MODEL)
    return (y_prompt, y_sample, jnp.stack(mkp), jnp.stack(mvp),
            jnp.stack(p_sc), jnp.stack(p_sh), jnp.stack(p_gc), jnp.stack(p_gs), jnp.stack(p_fc),
            jnp.stack(s_sc), jnp.stack(s_sh), jnp.stack(s_gc), jnp.stack(s_gs), jnp.stack(s_fc))
```

```python
import functools

import jax
import jax.numpy as jnp
from jax import lax
from jax.experimental import pallas as pl
from jax.experimental.pallas import tpu as pltpu

F32 = jnp.float32
BF16 = jnp.bfloat16

D_MODEL = 1024
SSD_HEADS = 16
SSD_HEAD_DIM = 64
SSD_STATE = 128
SSD_GROUPS = 2
SSD_INNER = SSD_HEADS * SSD_HEAD_DIM
SSD_BC = SSD_GROUPS * SSD_STATE
SSD_CONV = 4
GDN_HEADS = 8
GDN_HEAD_K = 128
GDN_HEAD_V = 128
GDN_DIM = GDN_HEADS * GDN_HEAD_V
GDN_CONV = 4
XA_HEADS = 4
XA_HEAD_DIM = D_MODEL // XA_HEADS
FFN_CONV = 3
RMS_EPS = 1e-6
L2_EPS = 1e-6

SUBLANES = 8
LANES = 128
VMEM_LIMIT_BYTES = 56 * 1024 * 1024

COL_Z_SSD = 0
COL_XS = 1024
COL_Q = 2048
COL_K = 3072
COL_V = 4096
COL_Z_GDN = 5120
COL_B = 6144
COL_C = 6400
COL_SMALL = 6656
PROJ_COLS = 7168
LANE_DT = 0
LANE_BETA = 16
LANE_G = 24

SSD_CHUNK = 256
GDN_CHUNK = 128
DEC_TILE = 8


def _params(*sem):
    return pltpu.CompilerParams(dimension_semantics=sem, vmem_limit_bytes=VMEM_LIMIT_BYTES)


def _mm(a, b):
    return jnp.dot(a.astype(BF16), b.astype(BF16), preferred_element_type=F32)


def _mm_nt(a, b):
    return lax.dot_general(a.astype(BF16), b.astype(BF16), (((1,), (1,)), ((), ())),
                           preferred_element_type=F32)


def _split(x, parts):
    out = []
    r = x
    for _ in range(parts - 1):
        p = r.astype(BF16)
        out.append(p)
        r = r - p.astype(F32)
    out.append(r.astype(BF16))
    return out


def _mm_rhs_split(a_exact, b, parts=3):
    a16 = a_exact.astype(BF16)
    acc = None
    for p in _split(b, parts):
        t = jnp.dot(a16, p, preferred_element_type=F32)
        acc = t if acc is None else acc + t
    return acc


def _mm_lhs_split(a, b_exact, parts=3):
    b16 = b_exact.astype(BF16)
    acc = None
    for p in _split(a, parts):
        t = jnp.dot(p, b16, preferred_element_type=F32)
        acc = t if acc is None else acc + t
    return acc


def _silu(x):
    return x / (1.0 + jnp.exp(-x))


def _sigmoid(x):
    return 1.0 / (1.0 + jnp.exp(-x))


def _softplus(x):
    return jnp.maximum(x, 0.0) + jnp.log1p(jnp.exp(-jnp.abs(x)))


def _col(arr, lane):
    idx = lax.broadcasted_iota(jnp.int32, arr.shape, 1)
    return jnp.sum(jnp.where(idx == lane, arr, 0.0), axis=1, keepdims=True)


def _gate_rows(sm, bias, alog):
    sp = _softplus(sm + bias)
    sg = _sigmoid(sm)
    lane = lax.broadcasted_iota(jnp.int32, alog.shape, 1)
    has_decay = (lane < LANE_BETA) | ((lane >= LANE_G) & (lane < LANE_G + GDN_HEADS))
    neg_a = jnp.where(has_decay, -jnp.exp(alog), 0.0)
    return sp, sg, sp * neg_a


def _norm_mm_kernel(x_ref, nw_ref, w_ref, o_ref, xn_ref):
    @pl.when(pl.program_id(1) == 0)
    def _():
        x = x_ref[...]
        ms = jnp.mean(x * x, axis=-1, keepdims=True)
        xn_ref[...] = (x * lax.rsqrt(ms + RMS_EPS) * nw_ref[...]).astype(BF16)

    o_ref[...] = jnp.dot(xn_ref[...], w_ref[...], preferred_element_type=F32)


def _norm_mm(x, nw, w, tm, tn):
    m, k = x.shape
    n = w.shape[1]
    assert m % tm == 0 and n % tn == 0
    return pl.pallas_call(
        _norm_mm_kernel,
        grid=(m // tm, n // tn),
        in_specs=[pl.BlockSpec((tm, k), lambda i, j: (i, 0)),
                  pl.BlockSpec((1, k), lambda i, j: (0, 0)),
                  pl.BlockSpec((k, tn), lambda i, j: (0, j))],
        out_specs=pl.BlockSpec((tm, tn), lambda i, j: (i, j)),
        out_shape=jax.ShapeDtypeStruct((m, n), F32),
        scratch_shapes=[pltpu.VMEM((tm, k), BF16)],
        compiler_params=_params("parallel", "arbitrary"),
    )(x, nw.reshape(1, k), w)


def _mm_res_kernel(*refs, n_in):
    res_ref = refs[0]
    a_refs = refs[1:1 + n_in]
    w_refs = refs[1 + n_in:1 + 2 * n_in]
    o_ref = refs[1 + 2 * n_in]
    acc = res_ref[...]
    for a_ref, w_ref in zip(a_refs, w_refs):
        acc = acc + jnp.dot(a_ref[...].astype(BF16), w_ref[...], preferred_element_type=F32)
    o_ref[...] = acc


def _mm_res(res, a_list, w_list, tm):
    m, n = res.shape
    assert m % tm == 0
    n_in = len(a_list)
    in_specs = [pl.BlockSpec((tm, n), lambda i: (i, 0))]
    in_specs += [pl.BlockSpec((tm, a.shape[1]), lambda i: (i, 0)) for a in a_list]
    in_specs += [pl.BlockSpec(w.shape, lambda i: (0, 0)) for w in w_list]
    return pl.pallas_call(
        functools.partial(_mm_res_kernel, n_in=n_in),
        grid=(m // tm,),
        in_specs=in_specs,
        out_specs=pl.BlockSpec((tm, n), lambda i: (i, 0)),
        out_shape=jax.ShapeDtypeStruct((m, n), F32),
        compiler_params=_params("parallel"),
    )(res, *a_list, *w_list)


def _rmsnorm_kernel(x_ref, nw_ref, o_ref):
    x = x_ref[...]
    ms = jnp.mean(x * x, axis=-1, keepdims=True)
    o_ref[...] = x * lax.rsqrt(ms + RMS_EPS) * nw_ref[...]


def _rmsnorm(x, nw, tm):
    m, k = x.shape
    return pl.pallas_call(
        _rmsnorm_kernel,
        grid=(m // tm,),
        in_specs=[pl.BlockSpec((tm, k), lambda i: (i, 0)), pl.BlockSpec((1, k), lambda i: (0, 0))],
        out_specs=pl.BlockSpec((tm, k), lambda i: (i, 0)),
        out_shape=jax.ShapeDtypeStruct((m, k), F32),
        compiler_params=_params("parallel"),
    )(x, nw.reshape(1, k))


HIST_ROWS = SUBLANES


def _chunk_conv(ext_ref, x_ref, cw_ref, cb, first, taps):
    cs = x_ref.shape[0]

    @pl.when(first)
    def _():
        ext_ref[0:HIST_ROWS, :] = jnp.zeros((HIST_ROWS, ext_ref.shape[1]), F32)

    ext_ref[HIST_ROWS:HIST_ROWS + cs, :] = x_ref[...]
    acc = None
    for k in range(taps):
        start = HIST_ROWS - (taps - 1) + k
        t = ext_ref[start:start + cs, :] * cw_ref[k:k + 1, :]
        acc = t if acc is None else acc + t
    if cb is not None:
        acc = acc + cb
    ext_ref[0:HIST_ROWS, :] = ext_ref[cs:cs + HIST_ROWS, :]
    return acc


def _causal_mask(cs, strict=False):
    r = lax.broadcasted_iota(jnp.int32, (cs, cs), 0)
    c = lax.broadcasted_iota(jnp.int32, (cs, cs), 1)
    return (r > c) if strict else (r >= c)


def _ssd_prefill_kernel(z_ref, xs_ref, b_ref, c_ref, sm_ref,
                        cwx_ref, cwb_ref, cwc_ref, cbx_ref, cbb_ref, cbc_ref,
                        bias_ref, alog_ref, dexp_ref, nw_ref,
                        y_ref, st_ref,
                        extx, extb, extc, hst):
    c_idx = pl.program_id(1)
    first = c_idx == 0
    cs = xs_ref.shape[0]

    @pl.when(first)
    def _():
        hst[...] = jnp.zeros(hst.shape, F32)

    xs = _silu(_chunk_conv(extx, xs_ref, cwx_ref, cbx_ref[...], first, SSD_CONV))
    bm = _silu(_chunk_conv(extb, b_ref, cwb_ref, cbb_ref[...], first, SSD_CONV))
    cm = _silu(_chunk_conv(extc, c_ref, cwc_ref, cbc_ref[...], first, SSD_CONV))

    sp, _, dta = _gate_rows(sm_ref[...], bias_ref[...], alog_ref[...])
    causal = _causal_mask(cs)
    cum = _mm_rhs_split(jnp.where(causal, 1.0, 0.0), dta)
    cum_row = cum.T

    lane_lo = lax.broadcasted_iota(jnp.int32, (1, LANES), 1) < SSD_HEAD_DIM
    row_lo = lax.broadcasted_iota(jnp.int32, (LANES, 1), 0) < SSD_HEAD_DIM
    heads_per_group = SSD_HEADS // SSD_GROUPS
    for g in range(SSD_GROUPS):
        cg = cm[:, g * SSD_STATE:(g + 1) * SSD_STATE]
        bg = bm[:, g * SSD_STATE:(g + 1) * SSD_STATE]
        cb = _mm_nt(cg, bg)
        rows_g = heads_per_group * SSD_HEAD_DIM
        y_state = _mm_nt(cg, hst[g * rows_g:(g + 1) * rows_g, :])
        for hp in range(heads_per_group // 2):
            h0 = g * heads_per_group + 2 * hp
            lo = h0 * SSD_HEAD_DIM
            xs_pair = xs[:, lo:lo + LANES]
            ci = [_col(cum, LANE_DT + h0), _col(cum, LANE_DT + h0 + 1)]
            dti = [_col(sp, LANE_DT + h0), _col(sp, LANE_DT + h0 + 1)]
            xdt_pair = xs_pair * jnp.where(lane_lo, dti[0], dti[1])
            y_pair = None
            for e in range(2):
                cj = cum_row[LANE_DT + h0 + e:LANE_DT + h0 + e + 1, :]
                seg = jnp.exp(jnp.where(causal, ci[e] - cj, -jnp.inf))
                keep = lane_lo if e == 0 else jnp.logical_not(lane_lo)
                t = _mm(cb * seg, jnp.where(keep, xdt_pair, 0.0))
                y_pair = t if y_pair is None else y_pair + t
            y_pair = y_pair + y_state[:, 2 * hp * SSD_HEAD_DIM:2 * hp * SSD_HEAD_DIM + LANES] * \
                jnp.where(lane_lo, jnp.exp(ci[0]), jnp.exp(ci[1]))
            y_pair = y_pair + dexp_ref[:, lo:lo + LANES] * xs_pair
            y_ref[:, lo:lo + LANES] = y_pair
            cl = [ci[0][cs - 1:cs, :], ci[1][cs - 1:cs, :]]
            to_end = jnp.where(lane_lo, jnp.exp(cl[0] - ci[0]), jnp.exp(cl[1] - ci[1]))
            contrib = _mm((xdt_pair * to_end).T, bg)
            dec = jnp.where(row_lo, jnp.exp(cl[0]), jnp.exp(cl[1]))
            hst[lo:lo + LANES, :] = hst[lo:lo + LANES, :] * dec + contrib

    gs = SSD_INNER // SSD_GROUPS
    for g in range(SSD_GROUPS):
        yg = y_ref[:, g * gs:(g + 1) * gs] * _silu(z_ref[:, g * gs:(g + 1) * gs])
        ms = jnp.mean(yg * yg, axis=-1, keepdims=True)
        y_ref[:, g * gs:(g + 1) * gs] = yg * lax.rsqrt(ms + RMS_EPS) * nw_ref[:, g * gs:(g + 1) * gs]

    @pl.when(c_idx == pl.num_programs(1) - 1)
    def _():
        st_ref[...] = hst[...]


def _ssd_prefill(proj, bsz, seq, lw):
    cs = SSD_CHUNK
    assert seq % cs == 0
    nc = seq // cs
    row = lambda b, c: b * nc + c
    pcol = lambda width, off: pl.BlockSpec((cs, width), lambda b, c: (row(b, c), off // width))
    full = lambda a: pl.BlockSpec(a.shape, lambda b, c: (0,) * a.ndim)
    weights = [lw['ssd_cw_x'], lw['ssd_cw_b'], lw['ssd_cw_c'], lw['ssd_cb_x'], lw['ssd_cb_b'], lw['ssd_cb_c'],
               lw['gate_bias'], lw['gate_alog'], lw['ssd_d_exp'], lw['ssd_norm_w']]
    y, st = pl.pallas_call(
        _ssd_prefill_kernel,
        grid=(bsz, nc),
        in_specs=[pcol(SSD_INNER, COL_Z_SSD), pcol(SSD_INNER, COL_XS), pcol(SSD_BC, COL_B), pcol(SSD_BC, COL_C),
                  pcol(LANES, COL_SMALL)] + [full(w) for w in weights],
        out_specs=[pl.BlockSpec((cs, SSD_INNER), lambda b, c: (row(b, c), 0)),
                   pl.BlockSpec((None, SSD_INNER, SSD_STATE), lambda b, c: (b, 0, 0))],
        out_shape=[jax.ShapeDtypeStruct((bsz * seq, SSD_INNER), F32),
                   jax.ShapeDtypeStruct((bsz, SSD_INNER, SSD_STATE), F32)],
        scratch_shapes=[pltpu.VMEM((HIST_ROWS + cs, SSD_INNER), F32),
                        pltpu.VMEM((HIST_ROWS + cs, SSD_BC), F32),
                        pltpu.VMEM((HIST_ROWS + cs, SSD_BC), F32),
                        pltpu.VMEM((SSD_INNER, SSD_STATE), F32)],
        compiler_params=_params("parallel", "arbitrary"),
    )(proj, proj, proj, proj, proj, *weights)
    return y, st


INV_BASE = 16


def _same_block(cs, size):
    r = lax.broadcasted_iota(jnp.int32, (cs, cs), 0) // size
    c = lax.broadcasted_iota(jnp.int32, (cs, cs), 1) // size
    return r == c


def _bdot(a16, b16):
    return jnp.dot(a16, b16, preferred_element_type=F32)


def _gdn_prefill_kernel(q_ref, k_ref, v_ref, z_ref, sm_ref,
                        cwq_ref, cwk_ref, cwv_ref, bias_ref, alog_ref, nw_ref,
                        o_ref, st_ref,
                        extq, extk, extv, sst,
                        qd_s, kd_s, rhs_s, a_s, attn_s, x_s, d_s, u_s, w_s, glast_s):
    c_idx = pl.program_id(1)
    first = c_idx == 0
    cs = q_ref.shape[0]
    heads = range(GDN_HEADS)

    @pl.when(first)
    def _():
        sst[...] = jnp.zeros(sst.shape, F32)

    qc = _silu(_chunk_conv(extq, q_ref, cwq_ref, None, first, GDN_CONV))
    kc = _silu(_chunk_conv(extk, k_ref, cwk_ref, None, first, GDN_CONV))
    vc = _silu(_chunk_conv(extv, v_ref, cwv_ref, None, first, GDN_CONV))

    _, sg, g_step = _gate_rows(sm_ref[...], bias_ref[...], alog_ref[...])
    causal = _causal_mask(cs)
    strict = _causal_mask(cs, strict=True)
    gc = _mm_rhs_split(jnp.where(causal, 1.0, 0.0), g_step)
    gc_row = gc.T
    glast_s[...] = gc[cs - 1:cs, :]

    for h in heads:
        lo = h * GDN_HEAD_K
        qh = qc[:, lo:lo + GDN_HEAD_K]
        kh = kc[:, lo:lo + GDN_HEAD_K]
        vh = vc[:, lo:lo + GDN_HEAD_V]
        qn = qh * lax.rsqrt(jnp.sum(qh * qh, axis=-1, keepdims=True) + L2_EPS) * (GDN_HEAD_K ** -0.5)
        kn = kh * lax.rsqrt(jnp.sum(kh * kh, axis=-1, keepdims=True) + L2_EPS)
        gi = _col(gc, LANE_G + h)
        bi = _col(sg, LANE_BETA + h)
        gj = gc_row[LANE_G + h:LANE_G + h + 1, :]
        decay = jnp.exp(jnp.where(causal, gi - gj, -jnp.inf))
        egi = jnp.exp(gi)
        kb = kn * bi
        kn16 = kn.astype(BF16)
        a_s[h] = jnp.where(strict, lax.dot_general(kb.astype(BF16), kn16, (((1,), (1,)), ((), ())),
                                                   preferred_element_type=F32) * decay, 0.0)
        attn_s[h] = (lax.dot_general(qn.astype(BF16), kn16, (((1,), (1,)), ((), ())),
                                     preferred_element_type=F32) * decay).astype(BF16)
        qd_s[:, lo:lo + GDN_HEAD_K] = (qn * egi).astype(BF16)
        kd_s[:, lo:lo + GDN_HEAD_K] = (kn * jnp.exp(gi[cs - 1:cs, :] - gi)).astype(BF16)
        rhs_s[:, 2 * lo:2 * lo + GDN_HEAD_V] = (vh * bi).astype(BF16)
        rhs_s[:, 2 * lo + GDN_HEAD_V:2 * lo + 2 * GDN_HEAD_V] = (kb * egi).astype(BF16)

    base = _same_block(cs, INV_BASE)
    eye = jnp.where(causal & jnp.logical_not(strict), 1.0, 0.0)
    for h in heads:
        x = jnp.where(base, -a_s[h], 0.0)
        x_s[h] = x.astype(BF16)
        d_s[h] = eye + x
    span = 2
    while span < INV_BASE:
        for h in heads:
            x16 = x_s[h]
            x2 = _bdot(x16, x16).astype(BF16)
            x_s[h] = x2
            d = d_s[h]
            d_s[h] = d + _bdot(d.astype(BF16), x2)
        span *= 2
    size = INV_BASE
    while size < cs:
        off = _same_block(cs, 2 * size) & jnp.logical_not(_same_block(cs, size))
        for h in heads:
            d = d_s[h]
            d16 = d.astype(BF16)
            low = jnp.where(off, a_s[h], 0.0).astype(BF16)
            d_s[h] = d - _bdot(_bdot(d16, low).astype(BF16), d16)
        size *= 2

    for h in heads:
        lo = h * GDN_HEAD_K
        uw = _bdot(d_s[h].astype(BF16), rhs_s[:, 2 * lo:2 * lo + 2 * GDN_HEAD_V])
        u_s[:, lo:lo + GDN_HEAD_V] = uw[:, :GDN_HEAD_V]
        w_s[:, lo:lo + GDN_HEAD_V] = uw[:, GDN_HEAD_V:].astype(BF16)

    for h in heads:
        lo = h * GDN_HEAD_K
        s = sst[lo:lo + GDN_HEAD_K, :]
        s16 = s.astype(BF16)
        v_new = u_s[:, lo:lo + GDN_HEAD_V] - _bdot(w_s[:, lo:lo + GDN_HEAD_V], s16)
        v16 = v_new.astype(BF16)
        o_h = _bdot(qd_s[:, lo:lo + GDN_HEAD_K], s16) + _bdot(attn_s[h], v16)
        g_last = _col(glast_s[...], LANE_G + h)
        sst[lo:lo + GDN_HEAD_K, :] = s * jnp.exp(g_last) + lax.dot_general(
            kd_s[:, lo:lo + GDN_HEAD_K], v16, (((0,), (0,)), ((), ())), preferred_element_type=F32)
        ms = jnp.mean(o_h * o_h, axis=-1, keepdims=True)
        o_ref[:, lo:lo + GDN_HEAD_V] = (o_h * lax.rsqrt(ms + RMS_EPS) * nw_ref[...]) * \
            _silu(z_ref[:, lo:lo + GDN_HEAD_V])

    @pl.when(c_idx == pl.num_programs(1) - 1)
    def _():
        st_ref[...] = sst[...]


def _gdn_prefill(proj, bsz, seq, lw):
    cs = GDN_CHUNK
    assert seq % cs == 0
    nc = seq // cs
    row = lambda b, c: b * nc + c
    pcol = lambda width, off: pl.BlockSpec((cs, width), lambda b, c: (row(b, c), off // width))
    full = lambda a: pl.BlockSpec(a.shape, lambda b, c: (0,) * a.ndim)
    weights = [lw['gdn_cw_q'], lw['gdn_cw_k'], lw['gdn_cw_v'], lw['gate_bias'], lw['gate_alog'], lw['gdn_norm_w']]
    o, st = pl.pallas_call(
        _gdn_prefill_kernel,
        grid=(bsz, nc),
        in_specs=[pcol(GDN_DIM, COL_Q), pcol(GDN_DIM, COL_K), pcol(GDN_DIM, COL_V), pcol(GDN_DIM, COL_Z_GDN),
                  pcol(LANES, COL_SMALL)] + [full(w) for w in weights],
        out_specs=[pl.BlockSpec((cs, GDN_DIM), lambda b, c: (row(b, c), 0)),
                   pl.BlockSpec((None, GDN_HEADS * GDN_HEAD_K, GDN_HEAD_V), lambda b, c: (b, 0, 0))],
        out_shape=[jax.ShapeDtypeStruct((bsz * seq, GDN_DIM), F32),
                   jax.ShapeDtypeStruct((bsz, GDN_HEADS * GDN_HEAD_K, GDN_HEAD_V), F32)],
        scratch_shapes=[pltpu.VMEM((HIST_ROWS + cs, GDN_DIM), F32),
                        pltpu.VMEM((HIST_ROWS + cs, GDN_DIM), F32),
                        pltpu.VMEM((HIST_ROWS + cs, GDN_DIM), F32),
                        pltpu.VMEM((GDN_HEADS * GDN_HEAD_K, GDN_HEAD_V), F32),
                        pltpu.VMEM((cs, GDN_DIM), BF16),
                        pltpu.VMEM((cs, GDN_DIM), BF16),
                        pltpu.VMEM((cs, 2 * GDN_DIM), BF16),
                        pltpu.VMEM((GDN_HEADS, cs, cs), F32),
                        pltpu.VMEM((GDN_HEADS, cs, cs), BF16),
                        pltpu.VMEM((GDN_HEADS, cs, cs), BF16),
                        pltpu.VMEM((GDN_HEADS, cs, cs), F32),
                        pltpu.VMEM((cs, GDN_DIM), F32),
                        pltpu.VMEM((cs, GDN_DIM), BF16),
                        pltpu.VMEM((1, LANES), F32)],
        compiler_params=_params("parallel", "arbitrary"),
    )(proj, proj, proj, proj, proj, *weights)
    return o, st


def _softmax_rows(s):
    m = jnp.max(s, axis=-1, keepdims=True)
    e = jnp.exp(s - m)
    return e / jnp.sum(e, axis=-1, keepdims=True)


def _xattn_prefill_kernel(x_ref, nw_ref, wq_ref, mk_ref, mv_ref, wo_ref, o_ref):
    x = x_ref[...]
    ms = jnp.mean(x * x, axis=-1, keepdims=True)
    h = x * lax.rsqrt(ms + RMS_EPS) * nw_ref[...]
    q = jnp.dot(h.astype(BF16), wq_ref[...], preferred_element_type=F32)
    outs = []
    for hd in range(XA_HEADS):
        lo = hd * XA_HEAD_DIM
        s = _mm_nt(q[:, lo:lo + XA_HEAD_DIM], mk_ref[:, lo:lo + XA_HEAD_DIM]) * (XA_HEAD_DIM ** -0.5)
        outs.append(_mm(_softmax_rows(s), mv_ref[:, lo:lo + XA_HEAD_DIM]))
    att = jnp.concatenate(outs, axis=-1)
    o_ref[...] = x + jnp.dot(att.astype(BF16), wo_ref[...], preferred_element_type=F32)


def _xattn_prefill(x, nw, wq, mk, mv, wo, bsz, seq, tq):
    n_mem = mk.shape[1]
    nt = seq // tq
    return pl.pallas_call(
        _xattn_prefill_kernel,
        grid=(bsz, nt),
        in_specs=[pl.BlockSpec((tq, D_MODEL), lambda b, t: (b * nt + t, 0)),
                  pl.BlockSpec((1, D_MODEL), lambda b, t: (0, 0)),
                  pl.BlockSpec((D_MODEL, D_MODEL), lambda b, t: (0, 0)),
                  pl.BlockSpec((None, n_mem, D_MODEL), lambda b, t: (b, 0, 0)),
                  pl.BlockSpec((None, n_mem, D_MODEL), lambda b, t: (b, 0, 0)),
                  pl.BlockSpec((D_MODEL, D_MODEL), lambda b, t: (0, 0))],
        out_specs=pl.BlockSpec((tq, D_MODEL), lambda b, t: (b * nt + t, 0)),
        out_shape=jax.ShapeDtypeStruct(x.shape, F32),
        compiler_params=_params("parallel", "arbitrary"),
    )(x, nw.reshape(1, D_MODEL), wq, mk, mv, wo)


def _xattn_decode_kernel(q_ref, mk_ref, mv_ref, o_ref):
    tb = q_ref.shape[0]
    lane_head = lax.broadcasted_iota(jnp.int32, (SUBLANES, D_MODEL), 1) // XA_HEAD_DIM
    row = lax.broadcasted_iota(jnp.int32, (SUBLANES, D_MODEL), 0)
    own = lane_head == row
    for i in range(tb):
        qm = jnp.where(own, q_ref[i], 0.0)
        s = _mm_nt(qm, mk_ref[i]) * (XA_HEAD_DIM ** -0.5)
        full = _mm(_softmax_rows(s), mv_ref[i])
        o_ref[i] = jnp.sum(jnp.where(own, full, 0.0), axis=0, keepdims=True)


def _xattn_decode(q, mk, mv, tb):
    nb, n_mem, _ = mk.shape
    return pl.pallas_call(
        _xattn_decode_kernel,
        grid=(nb // tb,),
        in_specs=[pl.BlockSpec((tb, 1, D_MODEL), lambda i: (i, 0, 0)),
                  pl.BlockSpec((tb, n_mem, D_MODEL), lambda i: (i, 0, 0)),
                  pl.BlockSpec((tb, n_mem, D_MODEL), lambda i: (i, 0, 0))],
        out_specs=pl.BlockSpec((tb, 1, D_MODEL), lambda i: (i, 0, 0)),
        out_shape=jax.ShapeDtypeStruct((nb, 1, D_MODEL), F32),
        compiler_params=_params("parallel"),
    )(q.reshape(nb, 1, D_MODEL), mk, mv).reshape(nb, D_MODEL)


def _ffn_prefill_kernel(x_ref, g_ref, gprev_ref, u_ref, cw_ref, cb_ref, wd_ref, o_ref, ext, *, tiles_per_seq):
    tm = g_ref.shape[0]
    starts_seq = pl.program_id(0) % tiles_per_seq == 0
    ext[0:HIST_ROWS, :] = jnp.where(starts_seq, 0.0, gprev_ref[...])
    ext[HIST_ROWS:HIST_ROWS + tm, :] = g_ref[...]
    acc = cb_ref[...]
    for k in range(FFN_CONV):
        start = HIST_ROWS - (FFN_CONV - 1) + k
        acc = acc + ext[start:start + tm, :] * cw_ref[k:k + 1, :]
    hmid = _silu(acc) * u_ref[...]
    o_ref[...] = x_ref[...] + jnp.dot(hmid.astype(BF16), wd_ref[...], preferred_element_type=F32)


def _ffn_prefill(x, gu, cw, cb, wd, seq, tm):
    m = x.shape[0]
    d_ff = wd.shape[0]
    assert seq % tm == 0 and tm % HIST_ROWS == 0
    prev_blocks = tm // HIST_ROWS
    return pl.pallas_call(
        functools.partial(_ffn_prefill_kernel, tiles_per_seq=seq // tm),
        grid=(m // tm,),
        in_specs=[pl.BlockSpec((tm, D_MODEL), lambda i: (i, 0)),
                  pl.BlockSpec((tm, d_ff), lambda i: (i, 0)),
                  pl.BlockSpec((HIST_ROWS, d_ff), lambda i: (jnp.maximum(i * prev_blocks - 1, 0), 0)),
                  pl.BlockSpec((tm, d_ff), lambda i: (i, 1)),
                  pl.BlockSpec((FFN_CONV, d_ff), lambda i: (0, 0)),
                  pl.BlockSpec((1, d_ff), lambda i: (0, 0)),
                  pl.BlockSpec((d_ff, D_MODEL), lambda i: (0, 0))],
        out_specs=pl.BlockSpec((tm, D_MODEL), lambda i: (i, 0)),
        out_shape=jax.ShapeDtypeStruct((m, D_MODEL), F32),
        scratch_shapes=[pltpu.VMEM((HIST_ROWS + tm, d_ff), F32)],
        compiler_params=_params("parallel"),
    )(x, gu, gu, gu, cw, cb.reshape(1, d_ff), wd)


def _ffn_decode_kernel(x_ref, g_ref, h0_ref, h1_ref, u_ref, cw_ref, cb_ref, wd_ref, o_ref):
    acc = cb_ref[...] + h0_ref[...] * cw_ref[0:1, :] + h1_ref[...] * cw_ref[1:2, :] + g_ref[...] * cw_ref[2:3, :]
    hmid = _silu(acc) * u_ref[...]
    o_ref[...] = x_ref[...] + jnp.dot(hmid.astype(BF16), wd_ref[...], preferred_element_type=F32)


def _ffn_decode(x, gu, hist0, hist1, cw, cb, wd):
    m = x.shape[0]
    d_ff = wd.shape[0]
    rows = lambda width, col: pl.BlockSpec((m, width), lambda i: (0, col))
    return pl.pallas_call(
        _ffn_decode_kernel,
        grid=(1,),
        in_specs=[rows(D_MODEL, 0), rows(d_ff, 0), rows(d_ff, 0), rows(d_ff, 0), rows(d_ff, 1),
                  pl.BlockSpec((FFN_CONV, d_ff), lambda i: (0, 0)),
                  pl.BlockSpec((1, d_ff), lambda i: (0, 0)),
                  pl.BlockSpec((d_ff, D_MODEL), lambda i: (0, 0))],
        out_specs=rows(D_MODEL, 0),
        out_shape=jax.ShapeDtypeStruct((m, D_MODEL), F32),
        compiler_params=_params("arbitrary"),
    )(x, gu, hist0, hist1, gu, cw, cb.reshape(1, d_ff), wd)


def _expand_lanes(rows, first_lane, width, n_out):
    k = lax.broadcasted_iota(jnp.int32, (LANES, n_out), 0)
    l = lax.broadcasted_iota(jnp.int32, (LANES, n_out), 1)
    sel = jnp.where(l // width == k - first_lane, 1.0, 0.0)
    return _mm_lhs_split(rows, sel)


def _decode_rows_kernel(proj_ref, sh0, sh1, sh2, gh0, gh1, gh2,
                        scw_ref, scb_ref, gcw_ref, bias_ref, alog_ref,
                        xs_ref, xdt_ref, b_ref, c_ref, sdec_ref,
                        q_ref, k_ref, v_ref, beta_ref, gdec_ref):
    def conv(hists, new, cw, cb):
        acc = new * cw[3:4, :]
        for t, hr in enumerate(hists):
            acc = acc + hr[...] * cw[t:t + 1, :]
        return acc if cb is None else acc + cb

    xbc_new = jnp.concatenate([proj_ref[:, COL_XS:COL_XS + SSD_INNER],
                               proj_ref[:, COL_B:COL_B + SSD_BC],
                               proj_ref[:, COL_C:COL_C + SSD_BC]], axis=-1)
    xbc = _silu(conv((sh0, sh1, sh2), xbc_new, scw_ref, scb_ref[...]))
    sp, sg, log_dec = _gate_rows(proj_ref[:, COL_SMALL:COL_SMALL + LANES], bias_ref[...], alog_ref[...])
    xs = xbc[:, 0:SSD_INNER]
    xs_ref[...] = xs
    xdt_ref[...] = xs * _expand_lanes(sp, LANE_DT, SSD_HEAD_DIM, SSD_INNER)
    b_ref[...] = xbc[:, SSD_INNER:SSD_INNER + SSD_BC]
    c_ref[...] = xbc[:, SSD_INNER + SSD_BC:SSD_INNER + 2 * SSD_BC]
    sdec_ref[...] = jnp.exp(_expand_lanes(log_dec, LANE_DT, SSD_HEAD_DIM, SSD_INNER))

    qkv_new = proj_ref[:, COL_Q:COL_Q + 3 * GDN_DIM]
    qkv = _silu(conv((gh0, gh1, gh2), qkv_new, gcw_ref, None))
    for h in range(GDN_HEADS):
        lo = h * GDN_HEAD_K
        qh = qkv[:, lo:lo + GDN_HEAD_K]
        kh = qkv[:, GDN_DIM + lo:GDN_DIM + lo + GDN_HEAD_K]
        q_ref[:, lo:lo + GDN_HEAD_K] = qh * lax.rsqrt(jnp.sum(qh * qh, axis=-1, keepdims=True) + L2_EPS) * \
            (GDN_HEAD_K ** -0.5)
        k_ref[:, lo:lo + GDN_HEAD_K] = kh * lax.rsqrt(jnp.sum(kh * kh, axis=-1, keepdims=True) + L2_EPS)
    v_ref[...] = qkv[:, 2 * GDN_DIM:3 * GDN_DIM]
    beta_ref[...] = _expand_lanes(sg, LANE_BETA, GDN_HEAD_V, GDN_DIM)
    gdec_ref[...] = jnp.exp(_expand_lanes(log_dec, LANE_G, GDN_HEAD_V, GDN_DIM))


def _decode_rows(proj, ssd_hist, gdn_hist, lw):
    m = proj.shape[0]
    ins = [proj] + [ssd_hist[:, t] for t in range(SSD_CONV - 1)] + [gdn_hist[:, t] for t in range(GDN_CONV - 1)]
    ins += [lw['ssd_conv_w'], lw['ssd_conv_b'], lw['gdn_conv_w'], lw['gate_bias'], lw['gate_alog']]
    widths = [SSD_INNER, SSD_INNER, SSD_BC, SSD_BC, SSD_INNER, GDN_DIM, GDN_DIM, GDN_DIM, GDN_DIM, GDN_DIM]
    return pl.pallas_call(
        _decode_rows_kernel,
        grid=(1,),
        in_specs=[pl.BlockSpec(a.shape, lambda i, nd=a.ndim: (0,) * nd) for a in ins],
        out_specs=[pl.BlockSpec((m, w), lambda i: (0, 0)) for w in widths],
        out_shape=[jax.ShapeDtypeStruct((m, w), F32) for w in widths],
        compiler_params=_params("arbitrary"),
    )(*ins)


def _rows_to_cols(rows, i, parts):
    tb = rows.shape[0]
    r = lax.broadcasted_iota(jnp.int32, (tb, LANES), 0)
    sel = jnp.where(r == i, 1.0, 0.0).astype(BF16)
    acc = None
    for p in _split(rows, parts):
        t = lax.dot_general(p, sel, (((0,), (0,)), ((), ())), preferred_element_type=F32)
        acc = t if acc is None else acc + t
    return acc


def _only_row(rows, i):
    r = lax.broadcasted_iota(jnp.int32, rows.shape, 0)
    return jnp.where(r == i, rows, 0.0)


def _decode_state_kernel(xdt_ref, b_ref, c_ref, sdec_ref, q_ref, k_ref, v_ref, beta_ref, gdec_ref,
                         hs_ref, ss_ref,
                         y_ref, o_ref, hs_out, ss_out):
    tb = xdt_ref.shape[0]
    xdt = xdt_ref[...]
    bm = b_ref[...]
    cm = c_ref[...]
    sdec = sdec_ref[...]
    qn = q_ref[...]
    kn = k_ref[...]
    vv = v_ref[...]
    beta = beta_ref[...]
    gdec = gdec_ref[...]
    rows_g = SSD_INNER // SSD_GROUPS

    y_acc = jnp.zeros((tb, SSD_INNER), F32)
    ks_acc = jnp.zeros((tb, GDN_DIM), F32)
    for i in range(tb):
        dec_c = _rows_to_cols(sdec, i, 2)
        xdt_c = _rows_to_cols(xdt, i, 1)
        pieces = []
        for g in range(SSD_GROUPS):
            rs = slice(g * rows_g, (g + 1) * rows_g)
            bg = bm[i:i + 1, g * SSD_STATE:(g + 1) * SSD_STATE]
            h_new = hs_ref[i, rs, :] * dec_c[rs, :] + xdt_c[rs, :] * bg
            hs_out[i, rs, :] = h_new
            cg = _only_row(cm[:, g * SSD_STATE:(g + 1) * SSD_STATE], i)
            pieces.append(_mm_nt(cg, h_new))
        y_acc = y_acc + jnp.concatenate(pieces, axis=-1)
        g_c = _rows_to_cols(gdec, i, 2)
        s_dec = ss_ref[i] * g_c
        ss_out[i] = s_dec
        pieces = []
        for h in range(GDN_HEADS):
            rs = slice(h * GDN_HEAD_K, (h + 1) * GDN_HEAD_K)
            pieces.append(_mm(_only_row(kn[:, rs], i), s_dec[rs, :]))
        ks_acc = ks_acc + jnp.concatenate(pieces, axis=-1)
    y_ref[...] = y_acc

    delta = beta * (vv - ks_acc)
    o_acc = jnp.zeros((tb, GDN_DIM), F32)
    for i in range(tb):
        k_c = _rows_to_cols(kn, i, 1)
        pieces = []
        for h in range(GDN_HEADS):
            rs = slice(h * GDN_HEAD_K, (h + 1) * GDN_HEAD_K)
            d_row = delta[i:i + 1, rs]
            s_new = ss_out[i, rs, :] + k_c[rs, :] * d_row
            ss_out[i, rs, :] = s_new
            pieces.append(_mm(_only_row(qn[:, rs], i), s_new))
        o_acc = o_acc + jnp.concatenate(pieces, axis=-1)
    o_ref[...] = o_acc


def _decode_state(rows, hs, ss, tb):
    xs, xdt, bm, cm, sdec, qn, kn, vv, beta, gdec = rows
    nb = xdt.shape[0]
    row_ins = [xdt, bm, cm, sdec, qn, kn, vv, beta, gdec]
    rspec = lambda a: pl.BlockSpec((tb, a.shape[1]), lambda i: (i, 0))
    sspec = lambda a: pl.BlockSpec((tb,) + a.shape[1:], lambda i: (i, 0, 0))
    return pl.pallas_call(
        _decode_state_kernel,
        grid=(nb // tb,),
        in_specs=[rspec(a) for a in row_ins] + [sspec(hs), sspec(ss)],
        out_specs=[rspec(xdt), rspec(qn), sspec(hs), sspec(ss)],
        out_shape=[jax.ShapeDtypeStruct(xdt.shape, F32), jax.ShapeDtypeStruct(qn.shape, F32),
                   jax.ShapeDtypeStruct(hs.shape, F32), jax.ShapeDtypeStruct(ss.shape, F32)],
        compiler_params=_params("parallel"),
    )(*row_ins, hs, ss)


def _decode_out_kernel(x_ref, y_ref, xs_ref, o_ref, proj_ref, dexp_ref, snw_ref, gnw_ref, w1_ref, w2_ref, out_ref):
    y = y_ref[...] + dexp_ref[...] * xs_ref[...]
    gs = SSD_INNER // SSD_GROUPS
    ys = []
    for g in range(SSD_GROUPS):
        yg = y[:, g * gs:(g + 1) * gs] * _silu(proj_ref[:, COL_Z_SSD + g * gs:COL_Z_SSD + (g + 1) * gs])
        ms = jnp.mean(yg * yg, axis=-1, keepdims=True)
        ys.append(yg * lax.rsqrt(ms + RMS_EPS) * snw_ref[:, g * gs:(g + 1) * gs])
    os = []
    for h in range(GDN_HEADS):
        lo = h * GDN_HEAD_V
        oh = o_ref[:, lo:lo + GDN_HEAD_V]
        ms = jnp.mean(oh * oh, axis=-1, keepdims=True)
        os.append(oh * lax.rsqrt(ms + RMS_EPS) * gnw_ref[...] *
                  _silu(proj_ref[:, COL_Z_GDN + lo:COL_Z_GDN + lo + GDN_HEAD_V]))
    yn = jnp.concatenate(ys, axis=-1)
    on = jnp.concatenate(os, axis=-1)
    out_ref[...] = x_ref[...] + _mm(yn, w1_ref[...]) + _mm(on, w2_ref[...])


def _decode_out(x, y, xs, o, proj, lw):
    ins = [x, y, xs, o, proj, lw['ssd_d_exp'], lw['ssd_norm_w'], lw['gdn_norm_w'], lw['w_out_ssd'], lw['w_out_gdn']]
    return pl.pallas_call(
        _decode_out_kernel,
        grid=(1,),
        in_specs=[pl.BlockSpec(a.shape, lambda i, nd=a.ndim: (0,) * nd) for a in ins],
        out_specs=pl.BlockSpec(x.shape, lambda i: (0, 0)),
        out_shape=jax.ShapeDtypeStruct(x.shape, F32),
        compiler_params=_params("arbitrary"),
    )(*ins)


def _prep_layer(i, p):
    w_in = p['w_in'][i]
    o_zs = 0
    o_xbc = o_zs + SSD_INNER
    o_dt = o_xbc + SSD_INNER + 2 * SSD_BC
    o_qkv = o_dt + SSD_HEADS
    o_zg = o_qkv + 3 * GDN_DIM
    o_b = o_zg + GDN_DIM
    o_a = o_b + GDN_HEADS
    used = SSD_HEADS + 2 * GDN_HEADS
    w_perm = jnp.concatenate([
        w_in[:, o_zs:o_zs + SSD_INNER],
        w_in[:, o_xbc:o_xbc + SSD_INNER],
        w_in[:, o_qkv:o_qkv + 3 * GDN_DIM],
        w_in[:, o_zg:o_zg + GDN_DIM],
        w_in[:, o_xbc + SSD_INNER:o_xbc + SSD_INNER + 2 * SSD_BC],
        w_in[:, o_dt:o_dt + SSD_HEADS],
        w_in[:, o_b:o_b + GDN_HEADS],
        w_in[:, o_a:o_a + GDN_HEADS],
        jnp.zeros((D_MODEL, PROJ_COLS - COL_SMALL - used), F32)], axis=1).astype(BF16)
    lane_pad = jnp.zeros((LANES - used,), F32)
    scw = p['ssd_conv_w'][i]
    scb = p['ssd_conv_b'][i].reshape(1, -1)
    gcw = p['gdn_conv_w'][i]
    return {
        'w_in': w_perm,
        'norm_mix_w': p['norm_mix_w'][i],
        'ssd_conv_w': scw, 'ssd_conv_b': scb, 'gdn_conv_w': gcw,
        'ssd_cw_x': scw[:, :SSD_INNER], 'ssd_cw_b': scw[:, SSD_INNER:SSD_INNER + SSD_BC],
        'ssd_cw_c': scw[:, SSD_INNER + SSD_BC:],
        'ssd_cb_x': scb[:, :SSD_INNER], 'ssd_cb_b': scb[:, SSD_INNER:SSD_INNER + SSD_BC],
        'ssd_cb_c': scb[:, SSD_INNER + SSD_BC:],
        'gdn_cw_q': gcw[:, :GDN_DIM], 'gdn_cw_k': gcw[:, GDN_DIM:2 * GDN_DIM], 'gdn_cw_v': gcw[:, 2 * GDN_DIM:],
        'gate_bias': jnp.concatenate([p['ssd_dt_bias'][i], jnp.zeros((GDN_HEADS,), F32), p['gdn_dt_bias'][i],
                                      lane_pad]).reshape(1, LANES),
        'gate_alog': jnp.concatenate([p['ssd_a_log'][i], jnp.zeros((GDN_HEADS,), F32), p['gdn_a_log'][i],
                                      lane_pad]).reshape(1, LANES),
        'ssd_d_exp': jnp.repeat(p['ssd_d'][i], SSD_HEAD_DIM).reshape(1, SSD_INNER),
        'ssd_norm_w': p['ssd_norm_w'][i].reshape(1, SSD_INNER),
        'gdn_norm_w': p['gdn_norm_w'][i].reshape(1, GDN_HEAD_V),
        'w_out_ssd': p['w_out'][i, :SSD_INNER].astype(BF16),
        'w_out_gdn': p['w_out'][i, SSD_INNER:].astype(BF16),
        'norm_xa_w': p['norm_xa_w'][i], 'norm_mem_w': p['norm_mem_w'][i],
        'xa_wq': p['xa_wq'][i].astype(BF16), 'xa_wo': p['xa_wo'][i].astype(BF16),
        'xa_wkv': jnp.concatenate([p['xa_wk'][i], p['xa_wv'][i]], axis=1).astype(BF16),
        'norm_ffn_w': p['norm_ffn_w'][i],
        'ffn_w_gu': jnp.concatenate([p['ffn_w_gate'][i], p['ffn_w_up'][i]], axis=1).astype(BF16),
        'ffn_conv_w': p['ffn_conv_w'][i], 'ffn_conv_b': p['ffn_conv_b'][i],
        'ffn_w_down': p['ffn_w_down'][i].astype(BF16),
    }


def _unpermute_conv_rows(proj_rows):
    ssd = jnp.concatenate([proj_rows[..., COL_XS:COL_XS + SSD_INNER],
                           proj_rows[..., COL_B:COL_B + SSD_BC],
                           proj_rows[..., COL_C:COL_C + SSD_BC]], axis=-1)
    gdn = proj_rows[..., COL_Q:COL_Q + 3 * GDN_DIM]
    return ssd, gdn


def _tile(m, pref):
    t = min(m, pref)
    while m % t:
        t //= 2
    return t


def kernel(x_prompt, x_sample, mem_prompt, cache_mem_k, cache_mem_v, state_ssd_conv, state_ssd, state_gdn_conv, state_gdn, state_ffn_conv, norm_mix_w, w_in, ssd_conv_w, ssd_conv_b, ssd_dt_bias, ssd_a_log, ssd_d, ssd_norm_w, gdn_conv_w, gdn_dt_bias, gdn_a_log, gdn_norm_w, w_out, norm_xa_w, norm_mem_w, xa_wq, xa_wk, xa_wv, xa_wo, norm_ffn_w, ffn_w_gate, ffn_w_up, ffn_conv_w, ffn_conv_b, ffn_w_down, final_norm_w):
    params = dict(norm_mix_w=norm_mix_w, w_in=w_in, ssd_conv_w=ssd_conv_w, ssd_conv_b=ssd_conv_b,
                  ssd_dt_bias=ssd_dt_bias, ssd_a_log=ssd_a_log, ssd_d=ssd_d, ssd_norm_w=ssd_norm_w,
                  gdn_conv_w=gdn_conv_w, gdn_dt_bias=gdn_dt_bias, gdn_a_log=gdn_a_log, gdn_norm_w=gdn_norm_w,
                  w_out=w_out, norm_xa_w=norm_xa_w, norm_mem_w=norm_mem_w, xa_wq=xa_wq, xa_wk=xa_wk, xa_wv=xa_wv,
                  xa_wo=xa_wo, norm_ffn_w=norm_ffn_w, ffn_w_gate=ffn_w_gate, ffn_w_up=ffn_w_up,
                  ffn_conv_w=ffn_conv_w, ffn_conv_b=ffn_conv_b, ffn_w_down=ffn_w_down)
    depth = w_in.shape[0]
    bsz, seq, _ = x_prompt.shape
    nb = x_sample.shape[0]
    n_mem = mem_prompt.shape[1]
    d_ff = ffn_w_down.shape[1]
    mp = bsz * seq

    xp = x_prompt.reshape(mp, D_MODEL)
    xs = x_sample.reshape(nb, D_MODEL)
    mem = mem_prompt.reshape(bsz * n_mem, D_MODEL)
    tm_p = _tile(mp, 1024)
    tm_mem = _tile(bsz * n_mem, 1024)
    dec_tile = _tile(nb, DEC_TILE)

    mkp, mvp = [], []
    p_sc, p_sh, p_gc, p_gs, p_fc = [], [], [], [], []
    s_sc, s_sh, s_gc, s_gs, s_fc = [], [], [], [], []
    for i in range(depth):
        lw = _prep_layer(i, params)

        kv = _norm_mm(mem, lw['norm_mem_w'], lw['xa_wkv'], tm_mem, 512)
        mk = kv[:, :D_MODEL].reshape(bsz, n_mem, D_MODEL)
        mv = kv[:, D_MODEL:].reshape(bsz, n_mem, D_MODEL)
        mkp.append(mk.reshape(bsz, n_mem, XA_HEADS, XA_HEAD_DIM))
        mvp.append(mv.reshape(bsz, n_mem, XA_HEADS, XA_HEAD_DIM))

        proj = _norm_mm(xp, lw['norm_mix_w'], lw['w_in'], tm_p, 512)
        y, h_new = _ssd_prefill(proj, bsz, seq, lw)
        o, s_new = _gdn_prefill(proj, bsz, seq, lw)
        tail = proj.reshape(bsz, seq, PROJ_COLS)[:, seq - (SSD_CONV - 1):]
        ssd_tail, gdn_tail = _unpermute_conv_rows(tail)
        p_sc.append(ssd_tail)
        p_gc.append(gdn_tail)
        p_sh.append(h_new.reshape(bsz, SSD_HEADS, SSD_HEAD_DIM, SSD_STATE))
        p_gs.append(s_new.reshape(bsz, GDN_HEADS, GDN_HEAD_K, GDN_HEAD_V))
        xp = _mm_res(xp, [y, o], [lw['w_out_ssd'], lw['w_out_gdn']], _tile(mp, 512))
        xp = _xattn_prefill(xp, lw['norm_xa_w'], lw['xa_wq'], mk, mv, lw['xa_wo'], bsz, seq, _tile(seq, 512))
        gu = _norm_mm(xp, lw['norm_ffn_w'], lw['ffn_w_gu'], tm_p, 512)
        p_fc.append(gu.reshape(bsz, seq, 2 * d_ff)[:, seq - (FFN_CONV - 1):, :d_ff])
        xp = _ffn_prefill(xp, gu, lw['ffn_conv_w'], lw['ffn_conv_b'], lw['ffn_w_down'], seq, _tile(seq, 256))

        proj_s = _norm_mm(xs, lw['norm_mix_w'], lw['w_in'], nb, 512)
        ssd_new, gdn_new = _unpermute_conv_rows(proj_s)
        s_sc.append(jnp.concatenate([state_ssd_conv[i][:, 1:], ssd_new[:, None]], axis=1))
        s_gc.append(jnp.concatenate([state_gdn_conv[i][:, 1:], gdn_new[:, None]], axis=1))
        rows = _decode_rows(proj_s, state_ssd_conv[i], state_gdn_conv[i], lw)
        y_s, o_s, hs_new, ss_new = _decode_state(
            rows, state_ssd[i].reshape(nb, SSD_INNER, SSD_STATE),
            state_gdn[i].reshape(nb, GDN_HEADS * GDN_HEAD_K, GDN_HEAD_V), dec_tile)
        s_sh.append(hs_new.reshape(nb, SSD_HEADS, SSD_HEAD_DIM, SSD_STATE))
        s_gs.append(ss_new.reshape(nb, GDN_HEADS, GDN_HEAD_K, GDN_HEAD_V))
        xs = _decode_out(xs, y_s, rows[0], o_s, proj_s, lw)
        q_s = _norm_mm(xs, lw['norm_xa_w'], lw['xa_wq'], nb, 512)
        att = _xattn_decode(q_s, cache_mem_k[i].reshape(nb, n_mem, D_MODEL),
                            cache_mem_v[i].reshape(nb, n_mem, D_MODEL), _tile(nb, 4))
        xs = _mm_res(xs, [att], [lw['xa_wo']], nb)
        gu_s = _norm_mm(xs, lw['norm_ffn_w'], lw['ffn_w_gu'], nb, 512)
        s_fc.append(jnp.concatenate([state_ffn_conv[i][:, 1:], gu_s[:, None, :d_ff]], axis=1))
        xs = _ffn_decode(xs, gu_s, state_ffn_conv[i][:, 0], state_ffn_conv[i][:, 1],
                         lw['ffn_conv_w'], lw['ffn_conv_b'], lw['ffn_w_down'])

    y_prompt = _rmsnorm(xp, final_norm_w, tm_p).reshape(bsz, seq, D_MODEL)
    y_sample = _rmsnorm(xs, final_norm_w, nb).reshape(nb, 1, D_MODEL)
    return (y_prompt, y_sample, jnp.stack(mkp), jnp.stack(mvp),
            jnp.stack(p_sc), jnp.stack(p_sh), jnp.stack(p_gc), jnp.stack(p_gs), jnp.stack(p_fc),
            jnp.stack(s_sc), jnp.stack(s_sh), jnp.stack(s_gc), jnp.stack(s_gs), jnp.stack(s_fc))
```

```python
import functools

import jax
import jax.numpy as jnp
from jax import lax
from jax.experimental import pallas as pl
from jax.experimental.pallas import tpu as pltpu

F32 = jnp.float32
BF16 = jnp.bfloat16

D_MODEL = 1024
SSD_HEADS = 16
SSD_HEAD_DIM = 64
SSD_STATE = 128
SSD_GROUPS = 2
SSD_INNER = SSD_HEADS * SSD_HEAD_DIM
SSD_BC = SSD_GROUPS * SSD_STATE
SSD_CONV = 4
GDN_HEADS = 8
GDN_HEAD_K = 128
GDN_HEAD_V = 128
GDN_DIM = GDN_HEADS * GDN_HEAD_V
GDN_CONV = 4
XA_HEADS = 4
XA_HEAD_DIM = D_MODEL // XA_HEADS
FFN_CONV = 3
RMS_EPS = 1e-6
L2_EPS = 1e-6

SUBLANES = 8
LANES = 128
VMEM_LIMIT_BYTES = 56 * 1024 * 1024

COL_Z_SSD = 0
COL_XS = 1024
COL_Q = 2048
COL_K = 3072
COL_V = 4096
COL_Z_GDN = 5120
COL_B = 6144
COL_C = 6400
COL_SMALL = 6656
PROJ_COLS = 7168
LANE_DT = 0
LANE_BETA = 16
LANE_G = 24

SSD_CHUNK = 256
GDN_CHUNK = 128
DEC_TILE = 8


def _params(*sem):
    return pltpu.CompilerParams(dimension_semantics=sem, vmem_limit_bytes=VMEM_LIMIT_BYTES)


def _mm(a, b):
    return jnp.dot(a.astype(BF16), b.astype(BF16), preferred_element_type=F32)


def _mm_nt(a, b):
    return lax.dot_general(a.astype(BF16), b.astype(BF16), (((1,), (1,)), ((), ())),
                           preferred_element_type=F32)


def _split(x, parts):
    out = []
    r = x
    for _ in range(parts - 1):
        p = r.astype(BF16)
        out.append(p)
        r = r - p.astype(F32)
    out.append(r.astype(BF16))
    return out


def _mm_rhs_split(a_exact, b, parts=3):
    a16 = a_exact.astype(BF16)
    acc = None
    for p in _split(b, parts):
        t = jnp.dot(a16, p, preferred_element_type=F32)
        acc = t if acc is None else acc + t
    return acc


def _mm_lhs_split(a, b_exact, parts=3):
    b16 = b_exact.astype(BF16)
    acc = None
    for p in _split(a, parts):
        t = jnp.dot(p, b16, preferred_element_type=F32)
        acc = t if acc is None else acc + t
    return acc


def _silu(x):
    return x / (1.0 + jnp.exp(-x))


def _sigmoid(x):
    return 1.0 / (1.0 + jnp.exp(-x))


def _softplus(x):
    return jnp.maximum(x, 0.0) + jnp.log1p(jnp.exp(-jnp.abs(x)))


def _col(arr, lane):
    idx = lax.broadcasted_iota(jnp.int32, arr.shape, 1)
    return jnp.sum(jnp.where(idx == lane, arr, 0.0), axis=1, keepdims=True)


def _gate_rows(sm, bias, alog):
    sp = _softplus(sm + bias)
    sg = _sigmoid(sm)
    lane = lax.broadcasted_iota(jnp.int32, alog.shape, 1)
    has_decay = (lane < LANE_BETA) | ((lane >= LANE_G) & (lane < LANE_G + GDN_HEADS))
    neg_a = jnp.where(has_decay, -jnp.exp(alog), 0.0)
    return sp, sg, sp * neg_a


def _norm_mm_kernel(x_ref, nw_ref, w_ref, o_ref, xn_ref):
    @pl.when(pl.program_id(1) == 0)
    def _():
        x = x_ref[...]
        ms = jnp.mean(x * x, axis=-1, keepdims=True)
        xn_ref[...] = (x * lax.rsqrt(ms + RMS_EPS) * nw_ref[...]).astype(BF16)

    o_ref[...] = jnp.dot(xn_ref[...], w_ref[...], preferred_element_type=F32)


MAX_COL_TILE_LANES = 11 * LANES


def _col_tile(n):
    return max(t for t in range(LANES, MAX_COL_TILE_LANES + 1, LANES) if n % t == 0)


def _norm_mm(x, nw, w, tm):
    m, k = x.shape
    n = w.shape[1]
    tn = _col_tile(n)
    assert m % tm == 0
    return pl.pallas_call(
        _norm_mm_kernel,
        grid=(m // tm, n // tn),
        in_specs=[pl.BlockSpec((tm, k), lambda i, j: (i, 0)),
                  pl.BlockSpec((1, k), lambda i, j: (0, 0)),
                  pl.BlockSpec((k, tn), lambda i, j: (0, j))],
        out_specs=pl.BlockSpec((tm, tn), lambda i, j: (i, j)),
        out_shape=jax.ShapeDtypeStruct((m, n), F32),
        scratch_shapes=[pltpu.VMEM((tm, k), BF16)],
        compiler_params=_params("parallel", "arbitrary"),
    )(x, nw.reshape(1, k), w)


def _mm_res_kernel(*refs, n_in):
    res_ref = refs[0]
    a_refs = refs[1:1 + n_in]
    w_refs = refs[1 + n_in:1 + 2 * n_in]
    o_ref = refs[1 + 2 * n_in]
    acc = res_ref[...]
    for a_ref, w_ref in zip(a_refs, w_refs):
        acc = acc + jnp.dot(a_ref[...].astype(BF16), w_ref[...], preferred_element_type=F32)
    o_ref[...] = acc


def _mm_res(res, a_list, w_list, tm):
    m, n = res.shape
    assert m % tm == 0
    n_in = len(a_list)
    in_specs = [pl.BlockSpec((tm, n), lambda i: (i, 0))]
    in_specs += [pl.BlockSpec((tm, a.shape[1]), lambda i: (i, 0)) for a in a_list]
    in_specs += [pl.BlockSpec(w.shape, lambda i: (0, 0)) for w in w_list]
    return pl.pallas_call(
        functools.partial(_mm_res_kernel, n_in=n_in),
        grid=(m // tm,),
        in_specs=in_specs,
        out_specs=pl.BlockSpec((tm, n), lambda i: (i, 0)),
        out_shape=jax.ShapeDtypeStruct((m, n), F32),
        compiler_params=_params("parallel"),
    )(res, *a_list, *w_list)


def _rmsnorm_kernel(x_ref, nw_ref, o_ref):
    x = x_ref[...]
    ms = jnp.mean(x * x, axis=-1, keepdims=True)
    o_ref[...] = x * lax.rsqrt(ms + RMS_EPS) * nw_ref[...]


def _rmsnorm(x, nw, tm):
    m, k = x.shape
    return pl.pallas_call(
        _rmsnorm_kernel,
        grid=(m // tm,),
        in_specs=[pl.BlockSpec((tm, k), lambda i: (i, 0)), pl.BlockSpec((1, k), lambda i: (0, 0))],
        out_specs=pl.BlockSpec((tm, k), lambda i: (i, 0)),
        out_shape=jax.ShapeDtypeStruct((m, k), F32),
        compiler_params=_params("parallel"),
    )(x, nw.reshape(1, k))


HIST_ROWS = SUBLANES


def _chunk_conv(ext_ref, x_ref, cw_ref, cb, first, taps):
    cs = x_ref.shape[0]

    @pl.when(first)
    def _():
        ext_ref[0:HIST_ROWS, :] = jnp.zeros((HIST_ROWS, ext_ref.shape[1]), F32)

    ext_ref[HIST_ROWS:HIST_ROWS + cs, :] = x_ref[...]
    acc = None
    for k in range(taps):
        start = HIST_ROWS - (taps - 1) + k
        t = ext_ref[start:start + cs, :] * cw_ref[k:k + 1, :]
        acc = t if acc is None else acc + t
    if cb is not None:
        acc = acc + cb
    ext_ref[0:HIST_ROWS, :] = ext_ref[cs:cs + HIST_ROWS, :]
    return acc


def _causal_mask(cs, strict=False):
    r = lax.broadcasted_iota(jnp.int32, (cs, cs), 0)
    c = lax.broadcasted_iota(jnp.int32, (cs, cs), 1)
    return (r > c) if strict else (r >= c)


def _ssd_prefill_kernel(z_ref, xs_ref, b_ref, c_ref, sm_ref,
                        cwx_ref, cwb_ref, cwc_ref, cbx_ref, cbb_ref, cbc_ref,
                        bias_ref, alog_ref, dexp_ref, nw_ref,
                        y_ref, st_ref,
                        extx, extb, extc, hst):
    c_idx = pl.program_id(1)
    first = c_idx == 0
    cs = xs_ref.shape[0]

    @pl.when(first)
    def _():
        hst[...] = jnp.zeros(hst.shape, F32)

    xs = _silu(_chunk_conv(extx, xs_ref, cwx_ref, cbx_ref[...], first, SSD_CONV))
    bm = _silu(_chunk_conv(extb, b_ref, cwb_ref, cbb_ref[...], first, SSD_CONV))
    cm = _silu(_chunk_conv(extc, c_ref, cwc_ref, cbc_ref[...], first, SSD_CONV))

    sp, _, dta = _gate_rows(sm_ref[...], bias_ref[...], alog_ref[...])
    causal = _causal_mask(cs)
    cum = _mm_rhs_split(jnp.where(causal, 1.0, 0.0), dta)
    cum_row = cum.T

    lane_lo = lax.broadcasted_iota(jnp.int32, (1, LANES), 1) < SSD_HEAD_DIM
    row_lo = lax.broadcasted_iota(jnp.int32, (LANES, 1), 0) < SSD_HEAD_DIM
    heads_per_group = SSD_HEADS // SSD_GROUPS
    for g in range(SSD_GROUPS):
        cg = cm[:, g * SSD_STATE:(g + 1) * SSD_STATE]
        bg = bm[:, g * SSD_STATE:(g + 1) * SSD_STATE]
        cb = _mm_nt(cg, bg)
        rows_g = heads_per_group * SSD_HEAD_DIM
        y_state = _mm_nt(cg, hst[g * rows_g:(g + 1) * rows_g, :])
        for hp in range(heads_per_group // 2):
            h0 = g * heads_per_group + 2 * hp
            lo = h0 * SSD_HEAD_DIM
            xs_pair = xs[:, lo:lo + LANES]
            ci = [_col(cum, LANE_DT + h0), _col(cum, LANE_DT + h0 + 1)]
            dti = [_col(sp, LANE_DT + h0), _col(sp, LANE_DT + h0 + 1)]
            xdt_pair = xs_pair * jnp.where(lane_lo, dti[0], dti[1])
            y_pair = None
            for e in range(2):
                cj = cum_row[LANE_DT + h0 + e:LANE_DT + h0 + e + 1, :]
                seg = jnp.exp(jnp.where(causal, ci[e] - cj, -jnp.inf))
                keep = lane_lo if e == 0 else jnp.logical_not(lane_lo)
                t = _mm(cb * seg, jnp.where(keep, xdt_pair, 0.0))
                y_pair = t if y_pair is None else y_pair + t
            y_pair = y_pair + y_state[:, 2 * hp * SSD_HEAD_DIM:2 * hp * SSD_HEAD_DIM + LANES] * \
                jnp.where(lane_lo, jnp.exp(ci[0]), jnp.exp(ci[1]))
            y_pair = y_pair + dexp_ref[:, lo:lo + LANES] * xs_pair
            y_ref[:, lo:lo + LANES] = y_pair
            cl = [ci[0][cs - 1:cs, :], ci[1][cs - 1:cs, :]]
            to_end = jnp.where(lane_lo, jnp.exp(cl[0] - ci[0]), jnp.exp(cl[1] - ci[1]))
            contrib = _mm((xdt_pair * to_end).T, bg)
            dec = jnp.where(row_lo, jnp.exp(cl[0]), jnp.exp(cl[1]))
            hst[lo:lo + LANES, :] = hst[lo:lo + LANES, :] * dec + contrib

    gs = SSD_INNER // SSD_GROUPS
    for g in range(SSD_GROUPS):
        yg = y_ref[:, g * gs:(g + 1) * gs] * _silu(z_ref[:, g * gs:(g + 1) * gs])
        ms = jnp.mean(yg * yg, axis=-1, keepdims=True)
        y_ref[:, g * gs:(g + 1) * gs] = yg * lax.rsqrt(ms + RMS_EPS) * nw_ref[:, g * gs:(g + 1) * gs]

    @pl.when(c_idx == pl.num_programs(1) - 1)
    def _():
        st_ref[...] = hst[...]


def _ssd_prefill(proj, bsz, seq, lw):
    cs = SSD_CHUNK
    assert seq % cs == 0
    nc = seq // cs
    row = lambda b, c: b * nc + c
    pcol = lambda width, off: pl.BlockSpec((cs, width), lambda b, c: (row(b, c), off // width))
    full = lambda a: pl.BlockSpec(a.shape, lambda b, c: (0,) * a.ndim)
    weights = [lw['ssd_cw_x'], lw['ssd_cw_b'], lw['ssd_cw_c'], lw['ssd_cb_x'], lw['ssd_cb_b'], lw['ssd_cb_c'],
               lw['gate_bias'], lw['gate_alog'], lw['ssd_d_exp'], lw['ssd_norm_w']]
    y, st = pl.pallas_call(
        _ssd_prefill_kernel,
        grid=(bsz, nc),
        in_specs=[pcol(SSD_INNER, COL_Z_SSD), pcol(SSD_INNER, COL_XS), pcol(SSD_BC, COL_B), pcol(SSD_BC, COL_C),
                  pcol(LANES, COL_SMALL)] + [full(w) for w in weights],
        out_specs=[pl.BlockSpec((cs, SSD_INNER), lambda b, c: (row(b, c), 0)),
                   pl.BlockSpec((None, SSD_INNER, SSD_STATE), lambda b, c: (b, 0, 0))],
        out_shape=[jax.ShapeDtypeStruct((bsz * seq, SSD_INNER), F32),
                   jax.ShapeDtypeStruct((bsz, SSD_INNER, SSD_STATE), F32)],
        scratch_shapes=[pltpu.VMEM((HIST_ROWS + cs, SSD_INNER), F32),
                        pltpu.VMEM((HIST_ROWS + cs, SSD_BC), F32),
                        pltpu.VMEM((HIST_ROWS + cs, SSD_BC), F32),
                        pltpu.VMEM((SSD_INNER, SSD_STATE), F32)],
        compiler_params=_params("parallel", "arbitrary"),
    )(proj, proj, proj, proj, proj, *weights)
    return y, st


INV_BASE = 16


def _same_block(cs, size):
    r = lax.broadcasted_iota(jnp.int32, (cs, cs), 0) // size
    c = lax.broadcasted_iota(jnp.int32, (cs, cs), 1) // size
    return r == c


def _bdot(a16, b16):
    return jnp.dot(a16, b16, preferred_element_type=F32)


def _gdn_prefill_kernel(q_ref, k_ref, v_ref, z_ref, sm_ref,
                        cwq_ref, cwk_ref, cwv_ref, bias_ref, alog_ref, nw_ref,
                        o_ref, st_ref,
                        extq, extk, extv, sst,
                        qd_s, kd_s, rhs_s, a_s, attn_s, x_s, d_s, u_s, w_s, glast_s):
    c_idx = pl.program_id(1)
    first = c_idx == 0
    cs = q_ref.shape[0]
    heads = range(GDN_HEADS)

    @pl.when(first)
    def _():
        sst[...] = jnp.zeros(sst.shape, F32)

    qc = _silu(_chunk_conv(extq, q_ref, cwq_ref, None, first, GDN_CONV))
    kc = _silu(_chunk_conv(extk, k_ref, cwk_ref, None, first, GDN_CONV))
    vc = _silu(_chunk_conv(extv, v_ref, cwv_ref, None, first, GDN_CONV))

    _, sg, g_step = _gate_rows(sm_ref[...], bias_ref[...], alog_ref[...])
    causal = _causal_mask(cs)
    strict = _causal_mask(cs, strict=True)
    gc = _mm_rhs_split(jnp.where(causal, 1.0, 0.0), g_step)
    gc_row = gc.T
    glast_s[...] = gc[cs - 1:cs, :]

    for h in heads:
        lo = h * GDN_HEAD_K
        qh = qc[:, lo:lo + GDN_HEAD_K]
        kh = kc[:, lo:lo + GDN_HEAD_K]
        vh = vc[:, lo:lo + GDN_HEAD_V]
        qn = qh * lax.rsqrt(jnp.sum(qh * qh, axis=-1, keepdims=True) + L2_EPS) * (GDN_HEAD_K ** -0.5)
        kn = kh * lax.rsqrt(jnp.sum(kh * kh, axis=-1, keepdims=True) + L2_EPS)
        gi = _col(gc, LANE_G + h)
        bi = _col(sg, LANE_BETA + h)
        gj = gc_row[LANE_G + h:LANE_G + h + 1, :]
        decay = jnp.exp(jnp.where(causal, gi - gj, -jnp.inf))
        egi = jnp.exp(gi)
        kb = kn * bi
        kn16 = kn.astype(BF16)
        a_s[h] = jnp.where(strict, lax.dot_general(kb.astype(BF16), kn16, (((1,), (1,)), ((), ())),
                                                   preferred_element_type=F32) * decay, 0.0)
        attn_s[h] = (lax.dot_general(qn.astype(BF16), kn16, (((1,), (1,)), ((), ())),
                                     preferred_element_type=F32) * decay).astype(BF16)
        qd_s[:, lo:lo + GDN_HEAD_K] = (qn * egi).astype(BF16)
        kd_s[:, lo:lo + GDN_HEAD_K] = (kn * jnp.exp(gi[cs - 1:cs, :] - gi)).astype(BF16)
        rhs_s[:, 2 * lo:2 * lo + GDN_HEAD_V] = (vh * bi).astype(BF16)
        rhs_s[:, 2 * lo + GDN_HEAD_V:2 * lo + 2 * GDN_HEAD_V] = (kb * egi).astype(BF16)

    base = _same_block(cs, INV_BASE)
    eye = jnp.where(causal & jnp.logical_not(strict), 1.0, 0.0)
    for h in heads:
        x = jnp.where(base, -a_s[h], 0.0)
        x_s[h] = x.astype(BF16)
        d_s[h] = eye + x
    span = 2
    while span < INV_BASE:
        for h in heads:
            x16 = x_s[h]
            x2 = _bdot(x16, x16).astype(BF16)
            x_s[h] = x2
            d = d_s[h]
            d_s[h] = d + _bdot(d.astype(BF16), x2)
        span *= 2
    size = INV_BASE
    while size < cs:
        off = _same_block(cs, 2 * size) & jnp.logical_not(_same_block(cs, size))
        for h in heads:
            d = d_s[h]
            d16 = d.astype(BF16)
            low = jnp.where(off, a_s[h], 0.0).astype(BF16)
            d_s[h] = d - _bdot(_bdot(d16, low).astype(BF16), d16)
        size *= 2

    for h in heads:
        lo = h * GDN_HEAD_K
        uw = _bdot(d_s[h].astype(BF16), rhs_s[:, 2 * lo:2 * lo + 2 * GDN_HEAD_V])
        u_s[:, lo:lo + GDN_HEAD_V] = uw[:, :GDN_HEAD_V]
        w_s[:, lo:lo + GDN_HEAD_V] = uw[:, GDN_HEAD_V:].astype(BF16)

    for h in heads:
        lo = h * GDN_HEAD_K
        s = sst[lo:lo + GDN_HEAD_K, :]
        s16 = s.astype(BF16)
        v_new = u_s[:, lo:lo + GDN_HEAD_V] - _bdot(w_s[:, lo:lo + GDN_HEAD_V], s16)
        v16 = v_new.astype(BF16)
        o_h = _bdot(qd_s[:, lo:lo + GDN_HEAD_K], s16) + _bdot(attn_s[h], v16)
        g_last = _col(glast_s[...], LANE_G + h)
        sst[lo:lo + GDN_HEAD_K, :] = s * jnp.exp(g_last) + lax.dot_general(
            kd_s[:, lo:lo + GDN_HEAD_K], v16, (((0,), (0,)), ((), ())), preferred_element_type=F32)
        ms = jnp.mean(o_h * o_h, axis=-1, keepdims=True)
        o_ref[:, lo:lo + GDN_HEAD_V] = (o_h * lax.rsqrt(ms + RMS_EPS) * nw_ref[...]) * \
            _silu(z_ref[:, lo:lo + GDN_HEAD_V])

    @pl.when(c_idx == pl.num_programs(1) - 1)
    def _():
        st_ref[...] = sst[...]


def _gdn_prefill(proj, bsz, seq, lw):
    cs = GDN_CHUNK
    assert seq % cs == 0
    nc = seq // cs
    row = lambda b, c: b * nc + c
    pcol = lambda width, off: pl.BlockSpec((cs, width), lambda b, c: (row(b, c), off // width))
    full = lambda a: pl.BlockSpec(a.shape, lambda b, c: (0,) * a.ndim)
    weights = [lw['gdn_cw_q'], lw['gdn_cw_k'], lw['gdn_cw_v'], lw['gate_bias'], lw['gate_alog'], lw['gdn_norm_w']]
    o, st = pl.pallas_call(
        _gdn_prefill_kernel,
        grid=(bsz, nc),
        in_specs=[pcol(GDN_DIM, COL_Q), pcol(GDN_DIM, COL_K), pcol(GDN_DIM, COL_V), pcol(GDN_DIM, COL_Z_GDN),
                  pcol(LANES, COL_SMALL)] + [full(w) for w in weights],
        out_specs=[pl.BlockSpec((cs, GDN_DIM), lambda b, c: (row(b, c), 0)),
                   pl.BlockSpec((None, GDN_HEADS * GDN_HEAD_K, GDN_HEAD_V), lambda b, c: (b, 0, 0))],
        out_shape=[jax.ShapeDtypeStruct((bsz * seq, GDN_DIM), F32),
                   jax.ShapeDtypeStruct((bsz, GDN_HEADS * GDN_HEAD_K, GDN_HEAD_V), F32)],
        scratch_shapes=[pltpu.VMEM((HIST_ROWS + cs, GDN_DIM), F32),
                        pltpu.VMEM((HIST_ROWS + cs, GDN_DIM), F32),
                        pltpu.VMEM((HIST_ROWS + cs, GDN_DIM), F32),
                        pltpu.VMEM((GDN_HEADS * GDN_HEAD_K, GDN_HEAD_V), F32),
                        pltpu.VMEM((cs, GDN_DIM), BF16),
                        pltpu.VMEM((cs, GDN_DIM), BF16),
                        pltpu.VMEM((cs, 2 * GDN_DIM), BF16),
                        pltpu.VMEM((GDN_HEADS, cs, cs), F32),
                        pltpu.VMEM((GDN_HEADS, cs, cs), BF16),
                        pltpu.VMEM((GDN_HEADS, cs, cs), BF16),
                        pltpu.VMEM((GDN_HEADS, cs, cs), F32),
                        pltpu.VMEM((cs, GDN_DIM), F32),
                        pltpu.VMEM((cs, GDN_DIM), BF16),
                        pltpu.VMEM((1, LANES), F32)],
        compiler_params=_params("parallel", "arbitrary"),
    )(proj, proj, proj, proj, proj, *weights)
    return o, st


def _softmax_rows(s):
    m = jnp.max(s, axis=-1, keepdims=True)
    e = jnp.exp(s - m)
    return e / jnp.sum(e, axis=-1, keepdims=True)


def _xattn_prefill_kernel(x_ref, nw_ref, wq_ref, mk_ref, mv_ref, wo_ref, o_ref):
    x = x_ref[...]
    ms = jnp.mean(x * x, axis=-1, keepdims=True)
    h = x * lax.rsqrt(ms + RMS_EPS) * nw_ref[...]
    q = jnp.dot(h.astype(BF16), wq_ref[...], preferred_element_type=F32)
    outs = []
    for hd in range(XA_HEADS):
        lo = hd * XA_HEAD_DIM
        s = _mm_nt(q[:, lo:lo + XA_HEAD_DIM], mk_ref[:, lo:lo + XA_HEAD_DIM]) * (XA_HEAD_DIM ** -0.5)
        outs.append(_mm(_softmax_rows(s), mv_ref[:, lo:lo + XA_HEAD_DIM]))
    att = jnp.concatenate(outs, axis=-1)
    o_ref[...] = x + jnp.dot(att.astype(BF16), wo_ref[...], preferred_element_type=F32)


def _xattn_prefill(x, nw, wq, mk, mv, wo, bsz, seq, tq):
    n_mem = mk.shape[1]
    nt = seq // tq
    return pl.pallas_call(
        _xattn_prefill_kernel,
        grid=(bsz, nt),
        in_specs=[pl.BlockSpec((tq, D_MODEL), lambda b, t: (b * nt + t, 0)),
                  pl.BlockSpec((1, D_MODEL), lambda b, t: (0, 0)),
                  pl.BlockSpec((D_MODEL, D_MODEL), lambda b, t: (0, 0)),
                  pl.BlockSpec((None, n_mem, D_MODEL), lambda b, t: (b, 0, 0)),
                  pl.BlockSpec((None, n_mem, D_MODEL), lambda b, t: (b, 0, 0)),
                  pl.BlockSpec((D_MODEL, D_MODEL), lambda b, t: (0, 0))],
        out_specs=pl.BlockSpec((tq, D_MODEL), lambda b, t: (b * nt + t, 0)),
        out_shape=jax.ShapeDtypeStruct(x.shape, F32),
        compiler_params=_params("parallel", "arbitrary"),
    )(x, nw.reshape(1, D_MODEL), wq, mk, mv, wo)


def _xattn_decode_kernel(q_ref, mk_ref, mv_ref, o_ref):
    tb = q_ref.shape[0]
    lane_head = lax.broadcasted_iota(jnp.int32, (SUBLANES, D_MODEL), 1) // XA_HEAD_DIM
    row = lax.broadcasted_iota(jnp.int32, (SUBLANES, D_MODEL), 0)
    own = lane_head == row
    for i in range(tb):
        qm = jnp.where(own, q_ref[i], 0.0)
        s = _mm_nt(qm, mk_ref[i]) * (XA_HEAD_DIM ** -0.5)
        full = _mm(_softmax_rows(s), mv_ref[i])
        o_ref[i] = jnp.sum(jnp.where(own, full, 0.0), axis=0, keepdims=True)


def _xattn_decode(q, mk_all, mv_all, layer, tb):
    _, nb, n_mem, _ = mk_all.shape
    mk, mv = mk_all, mv_all
    return pl.pallas_call(
        _xattn_decode_kernel,
        grid=(nb // tb,),
        in_specs=[pl.BlockSpec((tb, 1, D_MODEL), lambda i: (i, 0, 0)),
                  pl.BlockSpec((None, tb, n_mem, D_MODEL), lambda i: (layer, i, 0, 0)),
                  pl.BlockSpec((None, tb, n_mem, D_MODEL), lambda i: (layer, i, 0, 0))],
        out_specs=pl.BlockSpec((tb, 1, D_MODEL), lambda i: (i, 0, 0)),
        out_shape=jax.ShapeDtypeStruct((nb, 1, D_MODEL), F32),
        compiler_params=_params("parallel"),
    )(q.reshape(nb, 1, D_MODEL), mk, mv).reshape(nb, D_MODEL)


def _ffn_prefill_kernel(x_ref, g_ref, gprev_ref, u_ref, cw_ref, cb_ref, wd_ref, o_ref, ext, *, tiles_per_seq):
    tm = g_ref.shape[0]
    starts_seq = pl.program_id(0) % tiles_per_seq == 0
    ext[0:HIST_ROWS, :] = jnp.where(starts_seq, 0.0, gprev_ref[...])
    ext[HIST_ROWS:HIST_ROWS + tm, :] = g_ref[...]
    acc = cb_ref[...]
    for k in range(FFN_CONV):
        start = HIST_ROWS - (FFN_CONV - 1) + k
        acc = acc + ext[start:start + tm, :] * cw_ref[k:k + 1, :]
    hmid = _silu(acc) * u_ref[...]
    o_ref[...] = x_ref[...] + jnp.dot(hmid.astype(BF16), wd_ref[...], preferred_element_type=F32)


def _ffn_prefill(x, gu, cw, cb, wd, seq, tm):
    m = x.shape[0]
    d_ff = wd.shape[0]
    assert seq % tm == 0 and tm % HIST_ROWS == 0
    prev_blocks = tm // HIST_ROWS
    return pl.pallas_call(
        functools.partial(_ffn_prefill_kernel, tiles_per_seq=seq // tm),
        grid=(m // tm,),
        in_specs=[pl.BlockSpec((tm, D_MODEL), lambda i: (i, 0)),
                  pl.BlockSpec((tm, d_ff), lambda i: (i, 0)),
                  pl.BlockSpec((HIST_ROWS, d_ff), lambda i: (jnp.maximum(i * prev_blocks - 1, 0), 0)),
                  pl.BlockSpec((tm, d_ff), lambda i: (i, 1)),
                  pl.BlockSpec((FFN_CONV, d_ff), lambda i: (0, 0)),
                  pl.BlockSpec((1, d_ff), lambda i: (0, 0)),
                  pl.BlockSpec((d_ff, D_MODEL), lambda i: (0, 0))],
        out_specs=pl.BlockSpec((tm, D_MODEL), lambda i: (i, 0)),
        out_shape=jax.ShapeDtypeStruct((m, D_MODEL), F32),
        scratch_shapes=[pltpu.VMEM((HIST_ROWS + tm, d_ff), F32)],
        compiler_params=_params("parallel"),
    )(x, gu, gu, gu, cw, cb.reshape(1, d_ff), wd)


def _ffn_decode_kernel(x_ref, g_ref, h0_ref, h1_ref, u_ref, cw_ref, cb_ref, wd_ref, o_ref):
    acc = cb_ref[...] + h0_ref[...] * cw_ref[0:1, :] + h1_ref[...] * cw_ref[1:2, :] + g_ref[...] * cw_ref[2:3, :]
    hmid = _silu(acc) * u_ref[...]
    o_ref[...] = x_ref[...] + jnp.dot(hmid.astype(BF16), wd_ref[...], preferred_element_type=F32)


def _ffn_decode(x, gu, hist0, hist1, cw, cb, wd):
    m = x.shape[0]
    d_ff = wd.shape[0]
    rows = lambda width, col: pl.BlockSpec((m, width), lambda i: (0, col))
    return pl.pallas_call(
        _ffn_decode_kernel,
        grid=(1,),
        in_specs=[rows(D_MODEL, 0), rows(d_ff, 0), rows(d_ff, 0), rows(d_ff, 0), rows(d_ff, 1),
                  pl.BlockSpec((FFN_CONV, d_ff), lambda i: (0, 0)),
                  pl.BlockSpec((1, d_ff), lambda i: (0, 0)),
                  pl.BlockSpec((d_ff, D_MODEL), lambda i: (0, 0))],
        out_specs=rows(D_MODEL, 0),
        out_shape=jax.ShapeDtypeStruct((m, D_MODEL), F32),
        compiler_params=_params("arbitrary"),
    )(x, gu, hist0, hist1, gu, cw, cb.reshape(1, d_ff), wd)


def _expand_lanes(rows, first_lane, width, n_out, parts):
    k = lax.broadcasted_iota(jnp.int32, (LANES, n_out), 0)
    l = lax.broadcasted_iota(jnp.int32, (LANES, n_out), 1)
    sel = jnp.where(l // width == k - first_lane, 1.0, 0.0)
    return _mm_lhs_split(rows, sel, parts)


def _decode_rows_kernel(proj_ref, sh0, sh1, sh2, gh0, gh1, gh2,
                        scw_ref, scb_ref, gcw_ref, bias_ref, alog_ref,
                        xs_ref, xdt_ref, b_ref, c_ref, sdec_ref,
                        q_ref, k_ref, v_ref, beta_ref, gdec_ref):
    def conv(hists, new, cw, cb):
        acc = new * cw[3:4, :]
        for t, hr in enumerate(hists):
            acc = acc + hr[...] * cw[t:t + 1, :]
        return acc if cb is None else acc + cb

    xbc_new = jnp.concatenate([proj_ref[:, COL_XS:COL_XS + SSD_INNER],
                               proj_ref[:, COL_B:COL_B + SSD_BC],
                               proj_ref[:, COL_C:COL_C + SSD_BC]], axis=-1)
    xbc = _silu(conv((sh0, sh1, sh2), xbc_new, scw_ref, scb_ref[...]))
    sp, sg, log_dec = _gate_rows(proj_ref[:, COL_SMALL:COL_SMALL + LANES], bias_ref[...], alog_ref[...])
    xs = xbc[:, 0:SSD_INNER]
    xs_ref[...] = xs
    xdt_ref[...] = xs * _expand_lanes(sp, LANE_DT, SSD_HEAD_DIM, SSD_INNER, 3)
    b_ref[...] = xbc[:, SSD_INNER:SSD_INNER + SSD_BC]
    c_ref[...] = xbc[:, SSD_INNER + SSD_BC:SSD_INNER + 2 * SSD_BC]
    sdec_ref[...] = jnp.exp(_expand_lanes(log_dec, LANE_DT, SSD_HEAD_DIM, SSD_INNER, 3))

    qkv_new = proj_ref[:, COL_Q:COL_Q + 3 * GDN_DIM]
    qkv = _silu(conv((gh0, gh1, gh2), qkv_new, gcw_ref, None))
    for h in range(GDN_HEADS):
        lo = h * GDN_HEAD_K
        qh = qkv[:, lo:lo + GDN_HEAD_K]
        kh = qkv[:, GDN_DIM + lo:GDN_DIM + lo + GDN_HEAD_K]
        q_ref[:, lo:lo + GDN_HEAD_K] = qh * lax.rsqrt(jnp.sum(qh * qh, axis=-1, keepdims=True) + L2_EPS) * \
            (GDN_HEAD_K ** -0.5)
        k_ref[:, lo:lo + GDN_HEAD_K] = kh * lax.rsqrt(jnp.sum(kh * kh, axis=-1, keepdims=True) + L2_EPS)
    v_ref[...] = qkv[:, 2 * GDN_DIM:3 * GDN_DIM]
    beta_ref[...] = _expand_lanes(sg, LANE_BETA, GDN_HEAD_V, GDN_DIM, 3)
    gdec_ref[...] = jnp.exp(_expand_lanes(log_dec, LANE_G, GDN_HEAD_V, GDN_DIM, 3))


def _decode_rows(proj, ssd_hist, gdn_hist, lw):
    m = proj.shape[0]
    ins = [proj] + [ssd_hist[:, t] for t in range(SSD_CONV - 1)] + [gdn_hist[:, t] for t in range(GDN_CONV - 1)]
    ins += [lw['ssd_conv_w'], lw['ssd_conv_b'], lw['gdn_conv_w'], lw['gate_bias'], lw['gate_alog']]
    widths = [SSD_INNER, SSD_INNER, SSD_BC, SSD_BC, SSD_INNER, GDN_DIM, GDN_DIM, GDN_DIM, GDN_DIM, GDN_DIM]
    return pl.pallas_call(
        _decode_rows_kernel,
        grid=(1,),
        in_specs=[pl.BlockSpec(a.shape, lambda i, nd=a.ndim: (0,) * nd) for a in ins],
        out_specs=[pl.BlockSpec((m, w), lambda i: (0, 0)) for w in widths],
        out_shape=[jax.ShapeDtypeStruct((m, w), F32) for w in widths],
        compiler_params=_params("arbitrary"),
    )(*ins)


def _rows_to_cols(rows, i, parts):
    tb = rows.shape[0]
    r = lax.broadcasted_iota(jnp.int32, (tb, LANES), 0)
    sel = jnp.where(r == i, 1.0, 0.0).astype(BF16)
    acc = None
    for p in _split(rows, parts):
        t = lax.dot_general(p, sel, (((0,), (0,)), ((), ())), preferred_element_type=F32)
        acc = t if acc is None else acc + t
    return acc


def _only_row(rows, i):
    r = lax.broadcasted_iota(jnp.int32, rows.shape, 0)
    return jnp.where(r == i, rows, 0.0)


def _decode_state_kernel(xdt_ref, b_ref, c_ref, sdec_ref, q_ref, k_ref, v_ref, beta_ref, gdec_ref,
                         hs_ref, ss_ref, *rest):
    y_ref, o_ref, hs_out, ss_out = rest[-4:]
    tb = xdt_ref.shape[0]
    xdt = xdt_ref[...]
    bm = b_ref[...]
    cm = c_ref[...]
    sdec = sdec_ref[...]
    qn = q_ref[...]
    kn = k_ref[...]
    vv = v_ref[...]
    beta = beta_ref[...]
    gdec = gdec_ref[...]
    rows_g = SSD_INNER // SSD_GROUPS

    y_acc = jnp.zeros((tb, SSD_INNER), F32)
    ks_acc = jnp.zeros((tb, GDN_DIM), F32)
    for i in range(tb):
        dec_c = _rows_to_cols(sdec, i, 2)
        xdt_c = _rows_to_cols(xdt, i, 1)
        pieces = []
        for g in range(SSD_GROUPS):
            rs = slice(g * rows_g, (g + 1) * rows_g)
            bg = bm[i:i + 1, g * SSD_STATE:(g + 1) * SSD_STATE]
            h_new = hs_ref[i, rs, :] * dec_c[rs, :] + xdt_c[rs, :] * bg
            hs_out[i, rs, :] = h_new
            cg = _only_row(cm[:, g * SSD_STATE:(g + 1) * SSD_STATE], i)
            pieces.append(_mm_nt(cg, h_new))
        y_acc = y_acc + jnp.concatenate(pieces, axis=-1)
        g_c = _rows_to_cols(gdec, i, 2)
        s_dec = ss_ref[i] * g_c
        ss_out[i] = s_dec
        pieces = []
        for h in range(GDN_HEADS):
            rs = slice(h * GDN_HEAD_K, (h + 1) * GDN_HEAD_K)
            pieces.append(_mm(_only_row(kn[:, rs], i), s_dec[rs, :]))
        ks_acc = ks_acc + jnp.concatenate(pieces, axis=-1)
    y_ref[...] = y_acc

    delta = beta * (vv - ks_acc)
    o_acc = jnp.zeros((tb, GDN_DIM), F32)
    for i in range(tb):
        k_c = _rows_to_cols(kn, i, 1)
        pieces = []
        for h in range(GDN_HEADS):
            rs = slice(h * GDN_HEAD_K, (h + 1) * GDN_HEAD_K)
            d_row = delta[i:i + 1, rs]
            s_new = ss_out[i, rs, :] + k_c[rs, :] * d_row
            ss_out[i, rs, :] = s_new
            pieces.append(_mm(_only_row(qn[:, rs], i), s_new))
        o_acc = o_acc + jnp.concatenate(pieces, axis=-1)
    o_ref[...] = o_acc


def _decode_state(rows, hs_all, ss_all, layer, tb, carried):
    xs, xdt, bm, cm, sdec, qn, kn, vv, beta, gdec = rows
    nb = xdt.shape[0]
    row_ins = [xdt, bm, cm, sdec, qn, kn, vv, beta, gdec]
    rspec = lambda a: pl.BlockSpec((tb, a.shape[1]), lambda i: (i, 0))
    sspec = lambda a: pl.BlockSpec((None, tb) + a.shape[2:], lambda i: (layer, i, 0, 0))
    n_in = len(row_ins) + 2
    aliases = {n_in + j: 2 + j for j in range(len(carried))}
    return pl.pallas_call(
        _decode_state_kernel,
        grid=(nb // tb,),
        in_specs=[rspec(a) for a in row_ins] + [sspec(hs_all), sspec(ss_all)] +
                 [pl.BlockSpec(memory_space=pl.ANY) for _ in carried],
        out_specs=[rspec(xdt), rspec(qn), sspec(hs_all), sspec(ss_all)],
        out_shape=[jax.ShapeDtypeStruct(xdt.shape, F32), jax.ShapeDtypeStruct(qn.shape, F32),
                   jax.ShapeDtypeStruct(hs_all.shape, F32), jax.ShapeDtypeStruct(ss_all.shape, F32)],
        input_output_aliases=aliases,
        compiler_params=_params("parallel"),
    )(*row_ins, hs_all, ss_all, *carried)


def _decode_out_kernel(x_ref, y_ref, xs_ref, o_ref, proj_ref, dexp_ref, snw_ref, gnw_ref, w1_ref, w2_ref, out_ref):
    y = y_ref[...] + dexp_ref[...] * xs_ref[...]
    gs = SSD_INNER // SSD_GROUPS
    ys = []
    for g in range(SSD_GROUPS):
        yg = y[:, g * gs:(g + 1) * gs] * _silu(proj_ref[:, COL_Z_SSD + g * gs:COL_Z_SSD + (g + 1) * gs])
        ms = jnp.mean(yg * yg, axis=-1, keepdims=True)
        ys.append(yg * lax.rsqrt(ms + RMS_EPS) * snw_ref[:, g * gs:(g + 1) * gs])
    os = []
    for h in range(GDN_HEADS):
        lo = h * GDN_HEAD_V
        oh = o_ref[:, lo:lo + GDN_HEAD_V]
        ms = jnp.mean(oh * oh, axis=-1, keepdims=True)
        os.append(oh * lax.rsqrt(ms + RMS_EPS) * gnw_ref[...] *
                  _silu(proj_ref[:, COL_Z_GDN + lo:COL_Z_GDN + lo + GDN_HEAD_V]))
    yn = jnp.concatenate(ys, axis=-1)
    on = jnp.concatenate(os, axis=-1)
    out_ref[...] = x_ref[...] + _mm(yn, w1_ref[...]) + _mm(on, w2_ref[...])


def _decode_out(x, y, xs, o, proj, lw):
    ins = [x, y, xs, o, proj, lw['ssd_d_exp'], lw['ssd_norm_w'], lw['gdn_norm_w'], lw['w_out_ssd'], lw['w_out_gdn']]
    return pl.pallas_call(
        _decode_out_kernel,
        grid=(1,),
        in_specs=[pl.BlockSpec(a.shape, lambda i, nd=a.ndim: (0,) * nd) for a in ins],
        out_specs=pl.BlockSpec(x.shape, lambda i: (0, 0)),
        out_shape=jax.ShapeDtypeStruct(x.shape, F32),
        compiler_params=_params("arbitrary"),
    )(*ins)


def _prep_layer(i, p):
    w_in = p['w_in'][i]
    o_zs = 0
    o_xbc = o_zs + SSD_INNER
    o_dt = o_xbc + SSD_INNER + 2 * SSD_BC
    o_qkv = o_dt + SSD_HEADS
    o_zg = o_qkv + 3 * GDN_DIM
    o_b = o_zg + GDN_DIM
    o_a = o_b + GDN_HEADS
    used = SSD_HEADS + 2 * GDN_HEADS
    w_perm = jnp.concatenate([
        w_in[:, o_zs:o_zs + SSD_INNER],
        w_in[:, o_xbc:o_xbc + SSD_INNER],
        w_in[:, o_qkv:o_qkv + 3 * GDN_DIM],
        w_in[:, o_zg:o_zg + GDN_DIM],
        w_in[:, o_xbc + SSD_INNER:o_xbc + SSD_INNER + 2 * SSD_BC],
        w_in[:, o_dt:o_dt + SSD_HEADS],
        w_in[:, o_b:o_b + GDN_HEADS],
        w_in[:, o_a:o_a + GDN_HEADS],
        jnp.zeros((D_MODEL, PROJ_COLS - COL_SMALL - used), F32)], axis=1).astype(BF16)
    lane_pad = jnp.zeros((LANES - used,), F32)
    scw = p['ssd_conv_w'][i]
    scb = p['ssd_conv_b'][i].reshape(1, -1)
    gcw = p['gdn_conv_w'][i]
    return {
        'w_in': w_perm,
        'norm_mix_w': p['norm_mix_w'][i],
        'ssd_conv_w': scw, 'ssd_conv_b': scb, 'gdn_conv_w': gcw,
        'ssd_cw_x': scw[:, :SSD_INNER], 'ssd_cw_b': scw[:, SSD_INNER:SSD_INNER + SSD_BC],
        'ssd_cw_c': scw[:, SSD_INNER + SSD_BC:],
        'ssd_cb_x': scb[:, :SSD_INNER], 'ssd_cb_b': scb[:, SSD_INNER:SSD_INNER + SSD_BC],
        'ssd_cb_c': scb[:, SSD_INNER + SSD_BC:],
        'gdn_cw_q': gcw[:, :GDN_DIM], 'gdn_cw_k': gcw[:, GDN_DIM:2 * GDN_DIM], 'gdn_cw_v': gcw[:, 2 * GDN_DIM:],
        'gate_bias': jnp.concatenate([p['ssd_dt_bias'][i], jnp.zeros((GDN_HEADS,), F32), p['gdn_dt_bias'][i],
                                      lane_pad]).reshape(1, LANES),
        'gate_alog': jnp.concatenate([p['ssd_a_log'][i], jnp.zeros((GDN_HEADS,), F32), p['gdn_a_log'][i],
                                      lane_pad]).reshape(1, LANES),
        'ssd_d_exp': jnp.repeat(p['ssd_d'][i], SSD_HEAD_DIM).reshape(1, SSD_INNER),
        'ssd_norm_w': p['ssd_norm_w'][i].reshape(1, SSD_INNER),
        'gdn_norm_w': p['gdn_norm_w'][i].reshape(1, GDN_HEAD_V),
        'w_out_ssd': p['w_out'][i, :SSD_INNER].astype(BF16),
        'w_out_gdn': p['w_out'][i, SSD_INNER:].astype(BF16),
        'norm_xa_w': p['norm_xa_w'][i], 'norm_mem_w': p['norm_mem_w'][i],
        'xa_wq': p['xa_wq'][i].astype(BF16), 'xa_wo': p['xa_wo'][i].astype(BF16),
        'xa_wkv': jnp.concatenate([p['xa_wk'][i], p['xa_wv'][i]], axis=1).astype(BF16),
        'norm_ffn_w': p['norm_ffn_w'][i],
        'ffn_w_gu': jnp.concatenate([p['ffn_w_gate'][i], p['ffn_w_up'][i]], axis=1).astype(BF16),
        'ffn_conv_w': p['ffn_conv_w'][i], 'ffn_conv_b': p['ffn_conv_b'][i],
        'ffn_w_down': p['ffn_w_down'][i].astype(BF16),
    }


def _unpermute_conv_rows(proj_rows):
    ssd = jnp.concatenate([proj_rows[..., COL_XS:COL_XS + SSD_INNER],
                           proj_rows[..., COL_B:COL_B + SSD_BC],
                           proj_rows[..., COL_C:COL_C + SSD_BC]], axis=-1)
    gdn = proj_rows[..., COL_Q:COL_Q + 3 * GDN_DIM]
    return ssd, gdn


def _tile(m, pref):
    t = min(m, pref)
    while m % t:
        t //= 2
    return t


def kernel(x_prompt, x_sample, mem_prompt, cache_mem_k, cache_mem_v, state_ssd_conv, state_ssd, state_gdn_conv, state_gdn, state_ffn_conv, norm_mix_w, w_in, ssd_conv_w, ssd_conv_b, ssd_dt_bias, ssd_a_log, ssd_d, ssd_norm_w, gdn_conv_w, gdn_dt_bias, gdn_a_log, gdn_norm_w, w_out, norm_xa_w, norm_mem_w, xa_wq, xa_wk, xa_wv, xa_wo, norm_ffn_w, ffn_w_gate, ffn_w_up, ffn_conv_w, ffn_conv_b, ffn_w_down, final_norm_w):
    params = dict(norm_mix_w=norm_mix_w, w_in=w_in, ssd_conv_w=ssd_conv_w, ssd_conv_b=ssd_conv_b,
                  ssd_dt_bias=ssd_dt_bias, ssd_a_log=ssd_a_log, ssd_d=ssd_d, ssd_norm_w=ssd_norm_w,
                  gdn_conv_w=gdn_conv_w, gdn_dt_bias=gdn_dt_bias, gdn_a_log=gdn_a_log, gdn_norm_w=gdn_norm_w,
                  w_out=w_out, norm_xa_w=norm_xa_w, norm_mem_w=norm_mem_w, xa_wq=xa_wq, xa_wk=xa_wk, xa_wv=xa_wv,
                  xa_wo=xa_wo, norm_ffn_w=norm_ffn_w, ffn_w_gate=ffn_w_gate, ffn_w_up=ffn_w_up,
                  ffn_conv_w=ffn_conv_w, ffn_conv_b=ffn_conv_b, ffn_w_down=ffn_w_down)
    depth = w_in.shape[0]
    bsz, seq, _ = x_prompt.shape
    nb = x_sample.shape[0]
    n_mem = mem_prompt.shape[1]
    d_ff = ffn_w_down.shape[1]
    mp = bsz * seq

    xp = x_prompt.reshape(mp, D_MODEL)
    xs = x_sample.reshape(nb, D_MODEL)
    mem = mem_prompt.reshape(bsz * n_mem, D_MODEL)
    tm_p = _tile(mp, 1024)
    tm_mem = _tile(bsz * n_mem, 1024)
    dec_tile = _tile(nb, DEC_TILE)

    hs_all = state_ssd.reshape(depth, nb, SSD_INNER, SSD_STATE)
    ss_all = state_gdn.reshape(depth, nb, GDN_HEADS * GDN_HEAD_K, GDN_HEAD_V)
    mk_all = cache_mem_k.reshape(depth, nb, n_mem, D_MODEL)
    mv_all = cache_mem_v.reshape(depth, nb, n_mem, D_MODEL)
    new_states = ()

    mkp, mvp = [], []
    p_sc, p_sh, p_gc, p_gs, p_fc = [], [], [], [], []
    s_sc, s_gc, s_fc = [], [], []
    for i in range(depth):
        lw = _prep_layer(i, params)

        kv = _norm_mm(mem, lw['norm_mem_w'], lw['xa_wkv'], tm_mem)
        mk = kv[:, :D_MODEL].reshape(bsz, n_mem, D_MODEL)
        mv = kv[:, D_MODEL:].reshape(bsz, n_mem, D_MODEL)
        mkp.append(mk.reshape(bsz, n_mem, XA_HEADS, XA_HEAD_DIM))
        mvp.append(mv.reshape(bsz, n_mem, XA_HEADS, XA_HEAD_DIM))

        proj = _norm_mm(xp, lw['norm_mix_w'], lw['w_in'], tm_p)
        y, h_new = _ssd_prefill(proj, bsz, seq, lw)
        o, s_new = _gdn_prefill(proj, bsz, seq, lw)
        tail = proj.reshape(bsz, seq, PROJ_COLS)[:, seq - (SSD_CONV - 1):]
        ssd_tail, gdn_tail = _unpermute_conv_rows(tail)
        p_sc.append(ssd_tail)
        p_gc.append(gdn_tail)
        p_sh.append(h_new.reshape(bsz, SSD_HEADS, SSD_HEAD_DIM, SSD_STATE))
        p_gs.append(s_new.reshape(bsz, GDN_HEADS, GDN_HEAD_K, GDN_HEAD_V))
        xp = _mm_res(xp, [y, o], [lw['w_out_ssd'], lw['w_out_gdn']], _tile(mp, 512))
        xp = _xattn_prefill(xp, lw['norm_xa_w'], lw['xa_wq'], mk, mv, lw['xa_wo'], bsz, seq, _tile(seq, 512))
        gu = _norm_mm(xp, lw['norm_ffn_w'], lw['ffn_w_gu'], tm_p)
        p_fc.append(gu.reshape(bsz, seq, 2 * d_ff)[:, seq - (FFN_CONV - 1):, :d_ff])
        xp = _ffn_prefill(xp, gu, lw['ffn_conv_w'], lw['ffn_conv_b'], lw['ffn_w_down'], seq, _tile(seq, 256))

        proj_s = _norm_mm(xs, lw['norm_mix_w'], lw['w_in'], nb)
        ssd_new, gdn_new = _unpermute_conv_rows(proj_s)
        s_sc.append(jnp.concatenate([state_ssd_conv[i][:, 1:], ssd_new[:, None]], axis=1))
        s_gc.append(jnp.concatenate([state_gdn_conv[i][:, 1:], gdn_new[:, None]], axis=1))
        rows = _decode_rows(proj_s, state_ssd_conv[i], state_gdn_conv[i], lw)
        y_s, o_s, hs_new, ss_new = _decode_state(rows, hs_all, ss_all, i, dec_tile, new_states)
        new_states = (hs_new, ss_new)
        xs = _decode_out(xs, y_s, rows[0], o_s, proj_s, lw)
        q_s = _norm_mm(xs, lw['norm_xa_w'], lw['xa_wq'], nb)
        att = _xattn_decode(q_s, mk_all, mv_all, i, _tile(nb, 4))
        xs = _mm_res(xs, [att], [lw['xa_wo']], nb)
        gu_s = _norm_mm(xs, lw['norm_ffn_w'], lw['ffn_w_gu'], nb)
        s_fc.append(jnp.concatenate([state_ffn_conv[i][:, 1:], gu_s[:, None, :d_ff]], axis=1))
        xs = _ffn_decode(xs, gu_s, state_ffn_conv[i][:, 0], state_ffn_conv[i][:, 1],
                         lw['ffn_conv_w'], lw['ffn_conv_b'], lw['ffn_w_down'])

    y_prompt = _rmsnorm(xp, final_norm_w, tm_p).reshape(bsz, seq, D_MODEL)
    y_sample = _rmsnorm(xs, final_norm_w, nb).reshape(nb, 1, D_MODEL)
    return (y_prompt, y_sample, jnp.stack(mkp), jnp.stack(mvp),
            jnp.stack(p_sc), jnp.stack(p_sh), jnp.stack(p_gc), jnp.stack(p_gs), jnp.stack(p_fc),
            jnp.stack(s_sc), new_states[0].reshape(state_ssd.shape), jnp.stack(s_gc),
            new_states[1].reshape(state_gdn.shape), jnp.stack(s_fc))
```

```python
import functools

import jax
import jax.numpy as jnp
from jax import lax
from jax.experimental import pallas as pl
from jax.experimental.pallas import tpu as pltpu

F32 = jnp.float32
BF16 = jnp.bfloat16

D_MODEL = 1024
SSD_HEADS = 16
SSD_HEAD_DIM = 64
SSD_STATE = 128
SSD_GROUPS = 2
SSD_INNER = SSD_HEADS * SSD_HEAD_DIM
SSD_BC = SSD_GROUPS * SSD_STATE
SSD_CONV = 4
GDN_HEADS = 8
GDN_HEAD_K = 128
GDN_HEAD_V = 128
GDN_DIM = GDN_HEADS * GDN_HEAD_V
GDN_CONV = 4
XA_HEADS = 4
XA_HEAD_DIM = D_MODEL // XA_HEADS
FFN_CONV = 3
RMS_EPS = 1e-6
L2_EPS = 1e-6

SUBLANES = 8
LANES = 128
VMEM_LIMIT_BYTES = 56 * 1024 * 1024

COL_Z_SSD = 0
COL_XS = 1024
COL_Q = 2048
COL_K = 3072
COL_V = 4096
COL_Z_GDN = 5120
COL_B = 6144
COL_C = 6400
COL_SMALL = 6656
PROJ_COLS = 7168
LANE_DT = 0
LANE_BETA = 16
LANE_G = 24

SSD_CHUNK = 256
GDN_CHUNK = 128
DEC_TILE = 8


def _params(*sem):
    return pltpu.CompilerParams(dimension_semantics=sem, vmem_limit_bytes=VMEM_LIMIT_BYTES)


def _mm(a, b):
    return jnp.dot(a.astype(BF16), b.astype(BF16), preferred_element_type=F32)


def _mm_nt(a, b):
    return lax.dot_general(a.astype(BF16), b.astype(BF16), (((1,), (1,)), ((), ())),
                           preferred_element_type=F32)


def _split(x, parts):
    out = []
    r = x
    for _ in range(parts - 1):
        p = r.astype(BF16)
        out.append(p)
        r = r - p.astype(F32)
    out.append(r.astype(BF16))
    return out


def _mm_rhs_split(a_exact, b, parts=3):
    a16 = a_exact.astype(BF16)
    acc = None
    for p in _split(b, parts):
        t = jnp.dot(a16, p, preferred_element_type=F32)
        acc = t if acc is None else acc + t
    return acc


def _mm_lhs_split(a, b_exact, parts=3):
    b16 = b_exact.astype(BF16)
    acc = None
    for p in _split(a, parts):
        t = jnp.dot(p, b16, preferred_element_type=F32)
        acc = t if acc is None else acc + t
    return acc


def _silu(x):
    return x / (1.0 + jnp.exp(-x))


def _sigmoid(x):
    return 1.0 / (1.0 + jnp.exp(-x))


def _softplus(x):
    return jnp.maximum(x, 0.0) + jnp.log1p(jnp.exp(-jnp.abs(x)))


def _col(arr, lane):
    idx = lax.broadcasted_iota(jnp.int32, arr.shape, 1)
    return jnp.sum(jnp.where(idx == lane, arr, 0.0), axis=1, keepdims=True)


def _gate_rows(sm, bias, alog):
    sp = _softplus(sm + bias)
    sg = _sigmoid(sm)
    lane = lax.broadcasted_iota(jnp.int32, alog.shape, 1)
    has_decay = (lane < LANE_BETA) | ((lane >= LANE_G) & (lane < LANE_G + GDN_HEADS))
    neg_a = jnp.where(has_decay, -jnp.exp(alog), 0.0)
    return sp, sg, sp * neg_a


def _norm_mm_kernel(x_ref, nw_ref, w_ref, o_ref, xn_ref):
    @pl.when(pl.program_id(1) == 0)
    def _():
        x = x_ref[...]
        ms = jnp.mean(x * x, axis=-1, keepdims=True)
        xn_ref[...] = (x * lax.rsqrt(ms + RMS_EPS) * nw_ref[...]).astype(BF16)

    o_ref[...] = jnp.dot(xn_ref[...], w_ref[...], preferred_element_type=F32)


MAX_COL_TILE_LANES = 11 * LANES


def _col_tile(n):
    return max(t for t in range(LANES, MAX_COL_TILE_LANES + 1, LANES) if n % t == 0)


def _norm_mm(x, nw, w, tm):
    m, k = x.shape
    n = w.shape[1]
    tn = _col_tile(n)
    assert m % tm == 0
    return pl.pallas_call(
        _norm_mm_kernel,
        grid=(m // tm, n // tn),
        in_specs=[pl.BlockSpec((tm, k), lambda i, j: (i, 0)),
                  pl.BlockSpec((1, k), lambda i, j: (0, 0)),
                  pl.BlockSpec((k, tn), lambda i, j: (0, j))],
        out_specs=pl.BlockSpec((tm, tn), lambda i, j: (i, j)),
        out_shape=jax.ShapeDtypeStruct((m, n), F32),
        scratch_shapes=[pltpu.VMEM((tm, k), BF16)],
        compiler_params=_params("parallel", "arbitrary"),
    )(x, nw.reshape(1, k), w)


def _mm_res_kernel(*refs, n_in):
    res_ref = refs[0]
    a_refs = refs[1:1 + n_in]
    w_refs = refs[1 + n_in:1 + 2 * n_in]
    o_ref = refs[1 + 2 * n_in]
    acc = res_ref[...]
    for a_ref, w_ref in zip(a_refs, w_refs):
        acc = acc + jnp.dot(a_ref[...].astype(BF16), w_ref[...], preferred_element_type=F32)
    o_ref[...] = acc


def _mm_res(res, a_list, w_list, tm):
    m, n = res.shape
    assert m % tm == 0
    n_in = len(a_list)
    in_specs = [pl.BlockSpec((tm, n), lambda i: (i, 0))]
    in_specs += [pl.BlockSpec((tm, a.shape[1]), lambda i: (i, 0)) for a in a_list]
    in_specs += [pl.BlockSpec(w.shape, lambda i: (0, 0)) for w in w_list]
    return pl.pallas_call(
        functools.partial(_mm_res_kernel, n_in=n_in),
        grid=(m // tm,),
        in_specs=in_specs,
        out_specs=pl.BlockSpec((tm, n), lambda i: (i, 0)),
        out_shape=jax.ShapeDtypeStruct((m, n), F32),
        compiler_params=_params("parallel"),
    )(res, *a_list, *w_list)


def _rmsnorm_kernel(x_ref, nw_ref, o_ref):
    x = x_ref[...]
    ms = jnp.mean(x * x, axis=-1, keepdims=True)
    o_ref[...] = x * lax.rsqrt(ms + RMS_EPS) * nw_ref[...]


def _rmsnorm(x, nw, tm):
    m, k = x.shape
    return pl.pallas_call(
        _rmsnorm_kernel,
        grid=(m // tm,),
        in_specs=[pl.BlockSpec((tm, k), lambda i: (i, 0)), pl.BlockSpec((1, k), lambda i: (0, 0))],
        out_specs=pl.BlockSpec((tm, k), lambda i: (i, 0)),
        out_shape=jax.ShapeDtypeStruct((m, k), F32),
        compiler_params=_params("parallel"),
    )(x, nw.reshape(1, k))


HIST_ROWS = SUBLANES


def _chunk_conv(ext_ref, x_ref, cw_ref, cb, first, taps):
    cs = x_ref.shape[0]

    @pl.when(first)
    def _():
        ext_ref[0:HIST_ROWS, :] = jnp.zeros((HIST_ROWS, ext_ref.shape[1]), F32)

    ext_ref[HIST_ROWS:HIST_ROWS + cs, :] = x_ref[...]
    acc = None
    for k in range(taps):
        start = HIST_ROWS - (taps - 1) + k
        t = ext_ref[start:start + cs, :] * cw_ref[k:k + 1, :]
        acc = t if acc is None else acc + t
    if cb is not None:
        acc = acc + cb
    ext_ref[0:HIST_ROWS, :] = ext_ref[cs:cs + HIST_ROWS, :]
    return acc


def _causal_mask(cs, strict=False):
    r = lax.broadcasted_iota(jnp.int32, (cs, cs), 0)
    c = lax.broadcasted_iota(jnp.int32, (cs, cs), 1)
    return (r > c) if strict else (r >= c)


def _ssd_prefill_kernel(z_ref, xs_ref, b_ref, c_ref, sm_ref,
                        cwx_ref, cwb_ref, cwc_ref, cbx_ref, cbb_ref, cbc_ref,
                        bias_ref, alog_ref, dexp_ref, nw_ref,
                        y_ref, st_ref,
                        extx, extb, extc, hst):
    c_idx = pl.program_id(1)
    first = c_idx == 0
    cs = xs_ref.shape[0]

    @pl.when(first)
    def _():
        hst[...] = jnp.zeros(hst.shape, F32)

    xs = _silu(_chunk_conv(extx, xs_ref, cwx_ref, cbx_ref[...], first, SSD_CONV))
    bm = _silu(_chunk_conv(extb, b_ref, cwb_ref, cbb_ref[...], first, SSD_CONV))
    cm = _silu(_chunk_conv(extc, c_ref, cwc_ref, cbc_ref[...], first, SSD_CONV))

    sp, _, dta = _gate_rows(sm_ref[...], bias_ref[...], alog_ref[...])
    causal = _causal_mask(cs)
    cum = _mm_rhs_split(jnp.where(causal, 1.0, 0.0), dta)
    cum_row = cum.T

    lane_lo = lax.broadcasted_iota(jnp.int32, (1, LANES), 1) < SSD_HEAD_DIM
    row_lo = lax.broadcasted_iota(jnp.int32, (LANES, 1), 0) < SSD_HEAD_DIM
    heads_per_group = SSD_HEADS // SSD_GROUPS
    for g in range(SSD_GROUPS):
        cg = cm[:, g * SSD_STATE:(g + 1) * SSD_STATE]
        bg = bm[:, g * SSD_STATE:(g + 1) * SSD_STATE]
        cb = _mm_nt(cg, bg)
        rows_g = heads_per_group * SSD_HEAD_DIM
        y_state = _mm_nt(cg, hst[g * rows_g:(g + 1) * rows_g, :])
        for hp in range(heads_per_group // 2):
            h0 = g * heads_per_group + 2 * hp
            lo = h0 * SSD_HEAD_DIM
            xs_pair = xs[:, lo:lo + LANES]
            ci = [_col(cum, LANE_DT + h0), _col(cum, LANE_DT + h0 + 1)]
            dti = [_col(sp, LANE_DT + h0), _col(sp, LANE_DT + h0 + 1)]
            xdt_pair = xs_pair * jnp.where(lane_lo, dti[0], dti[1])
            y_pair = None
            for e in range(2):
                cj = cum_row[LANE_DT + h0 + e:LANE_DT + h0 + e + 1, :]
                seg = jnp.exp(jnp.where(causal, ci[e] - cj, -jnp.inf))
                keep = lane_lo if e == 0 else jnp.logical_not(lane_lo)
                t = _mm(cb * seg, jnp.where(keep, xdt_pair, 0.0))
                y_pair = t if y_pair is None else y_pair + t
            y_pair = y_pair + y_state[:, 2 * hp * SSD_HEAD_DIM:2 * hp * SSD_HEAD_DIM + LANES] * \
                jnp.where(lane_lo, jnp.exp(ci[0]), jnp.exp(ci[1]))
            y_pair = y_pair + dexp_ref[:, lo:lo + LANES] * xs_pair
            y_ref[:, lo:lo + LANES] = y_pair
            cl = [ci[0][cs - 1:cs, :], ci[1][cs - 1:cs, :]]
            to_end = jnp.where(lane_lo, jnp.exp(cl[0] - ci[0]), jnp.exp(cl[1] - ci[1]))
            contrib = _mm((xdt_pair * to_end).T, bg)
            dec = jnp.where(row_lo, jnp.exp(cl[0]), jnp.exp(cl[1]))
            hst[lo:lo + LANES, :] = hst[lo:lo + LANES, :] * dec + contrib

    gs = SSD_INNER // SSD_GROUPS
    for g in range(SSD_GROUPS):
        yg = y_ref[:, g * gs:(g + 1) * gs] * _silu(z_ref[:, g * gs:(g + 1) * gs])
        ms = jnp.mean(yg * yg, axis=-1, keepdims=True)
        y_ref[:, g * gs:(g + 1) * gs] = yg * lax.rsqrt(ms + RMS_EPS) * nw_ref[:, g * gs:(g + 1) * gs]

    @pl.when(c_idx == pl.num_programs(1) - 1)
    def _():
        st_ref[...] = hst[...]


def _ssd_prefill(proj, bsz, seq, lw):
    cs = SSD_CHUNK
    assert seq % cs == 0
    nc = seq // cs
    row = lambda b, c: b * nc + c
    pcol = lambda width, off: pl.BlockSpec((cs, width), lambda b, c: (row(b, c), off // width))
    full = lambda a: pl.BlockSpec(a.shape, lambda b, c: (0,) * a.ndim)
    weights = [lw['ssd_cw_x'], lw['ssd_cw_b'], lw['ssd_cw_c'], lw['ssd_cb_x'], lw['ssd_cb_b'], lw['ssd_cb_c'],
               lw['gate_bias'], lw['gate_alog'], lw['ssd_d_exp'], lw['ssd_norm_w']]
    y, st = pl.pallas_call(
        _ssd_prefill_kernel,
        grid=(bsz, nc),
        in_specs=[pcol(SSD_INNER, COL_Z_SSD), pcol(SSD_INNER, COL_XS), pcol(SSD_BC, COL_B), pcol(SSD_BC, COL_C),
                  pcol(LANES, COL_SMALL)] + [full(w) for w in weights],
        out_specs=[pl.BlockSpec((cs, SSD_INNER), lambda b, c: (row(b, c), 0)),
                   pl.BlockSpec((None, SSD_INNER, SSD_STATE), lambda b, c: (b, 0, 0))],
        out_shape=[jax.ShapeDtypeStruct((bsz * seq, SSD_INNER), F32),
                   jax.ShapeDtypeStruct((bsz, SSD_INNER, SSD_STATE), F32)],
        scratch_shapes=[pltpu.VMEM((HIST_ROWS + cs, SSD_INNER), F32),
                        pltpu.VMEM((HIST_ROWS + cs, SSD_BC), F32),
                        pltpu.VMEM((HIST_ROWS + cs, SSD_BC), F32),
                        pltpu.VMEM((SSD_INNER, SSD_STATE), F32)],
        compiler_params=_params("parallel", "arbitrary"),
    )(proj, proj, proj, proj, proj, *weights)
    return y, st


INV_BASE = 16


def _same_block(cs, size):
    r = lax.broadcasted_iota(jnp.int32, (cs, cs), 0) // size
    c = lax.broadcasted_iota(jnp.int32, (cs, cs), 1) // size
    return r == c


def _bdot(a16, b16):
    return jnp.dot(a16, b16, preferred_element_type=F32)


def _gdn_prefill_kernel(q_ref, k_ref, v_ref, z_ref, sm_ref,
                        cwq_ref, cwk_ref, cwv_ref, bias_ref, alog_ref, nw_ref,
                        o_ref, st_ref,
                        extq, extk, extv, sst,
                        qd_s, kd_s, rhs_s, a_s, attn_s, x_s, d_s, u_s, w_s, glast_s):
    c_idx = pl.program_id(1)
    first = c_idx == 0
    cs = q_ref.shape[0]
    heads = range(GDN_HEADS)

    @pl.when(first)
    def _():
        sst[...] = jnp.zeros(sst.shape, F32)

    qc = _silu(_chunk_conv(extq, q_ref, cwq_ref, None, first, GDN_CONV))
    kc = _silu(_chunk_conv(extk, k_ref, cwk_ref, None, first, GDN_CONV))
    vc = _silu(_chunk_conv(extv, v_ref, cwv_ref, None, first, GDN_CONV))

    _, sg, g_step = _gate_rows(sm_ref[...], bias_ref[...], alog_ref[...])
    causal = _causal_mask(cs)
    strict = _causal_mask(cs, strict=True)
    gc = _mm_rhs_split(jnp.where(causal, 1.0, 0.0), g_step)
    gc_row = gc.T
    glast_s[...] = gc[cs - 1:cs, :]

    for h in heads:
        lo = h * GDN_HEAD_K
        qh = qc[:, lo:lo + GDN_HEAD_K]
        kh = kc[:, lo:lo + GDN_HEAD_K]
        vh = vc[:, lo:lo + GDN_HEAD_V]
        qn = qh * lax.rsqrt(jnp.sum(qh * qh, axis=-1, keepdims=True) + L2_EPS) * (GDN_HEAD_K ** -0.5)
        kn = kh * lax.rsqrt(jnp.sum(kh * kh, axis=-1, keepdims=True) + L2_EPS)
        gi = _col(gc, LANE_G + h)
        bi = _col(sg, LANE_BETA + h)
        gj = gc_row[LANE_G + h:LANE_G + h + 1, :]
        decay = jnp.exp(jnp.where(causal, gi - gj, -jnp.inf))
        egi = jnp.exp(gi)
        kb = kn * bi
        kn16 = kn.astype(BF16)
        a_s[h] = jnp.where(strict, lax.dot_general(kb.astype(BF16), kn16, (((1,), (1,)), ((), ())),
                                                   preferred_element_type=F32) * decay, 0.0)
        attn_s[h] = (lax.dot_general(qn.astype(BF16), kn16, (((1,), (1,)), ((), ())),
                                     preferred_element_type=F32) * decay).astype(BF16)
        qd_s[:, lo:lo + GDN_HEAD_K] = (qn * egi).astype(BF16)
        kd_s[:, lo:lo + GDN_HEAD_K] = (kn * jnp.exp(gi[cs - 1:cs, :] - gi)).astype(BF16)
        rhs_s[:, 2 * lo:2 * lo + GDN_HEAD_V] = (vh * bi).astype(BF16)
        rhs_s[:, 2 * lo + GDN_HEAD_V:2 * lo + 2 * GDN_HEAD_V] = (kb * egi).astype(BF16)

    base = _same_block(cs, INV_BASE)
    eye = jnp.where(causal & jnp.logical_not(strict), 1.0, 0.0)
    for h in heads:
        x = jnp.where(base, -a_s[h], 0.0)
        x_s[h] = x.astype(BF16)
        d_s[h] = eye + x
    span = 2
    while span < INV_BASE:
        for h in heads:
            x16 = x_s[h]
            x2 = _bdot(x16, x16).astype(BF16)
            x_s[h] = x2
            d = d_s[h]
            d_s[h] = d + _bdot(d.astype(BF16), x2)
        span *= 2
    size = INV_BASE
    while size < cs:
        off = _same_block(cs, 2 * size) & jnp.logical_not(_same_block(cs, size))
        for h in heads:
            d = d_s[h]
            d16 = d.astype(BF16)
            low = jnp.where(off, a_s[h], 0.0).astype(BF16)
            d_s[h] = d - _bdot(_bdot(d16, low).astype(BF16), d16)
        size *= 2

    for h in heads:
        lo = h * GDN_HEAD_K
        uw = _bdot(d_s[h].astype(BF16), rhs_s[:, 2 * lo:2 * lo + 2 * GDN_HEAD_V])
        u_s[:, lo:lo + GDN_HEAD_V] = uw[:, :GDN_HEAD_V]
        w_s[:, lo:lo + GDN_HEAD_V] = uw[:, GDN_HEAD_V:].astype(BF16)

    for h in heads:
        lo = h * GDN_HEAD_K
        s = sst[lo:lo + GDN_HEAD_K, :]
        s16 = s.astype(BF16)
        v_new = u_s[:, lo:lo + GDN_HEAD_V] - _bdot(w_s[:, lo:lo + GDN_HEAD_V], s16)
        v16 = v_new.astype(BF16)
        o_h = _bdot(qd_s[:, lo:lo + GDN_HEAD_K], s16) + _bdot(attn_s[h], v16)
        g_last = _col(glast_s[...], LANE_G + h)
        sst[lo:lo + GDN_HEAD_K, :] = s * jnp.exp(g_last) + lax.dot_general(
            kd_s[:, lo:lo + GDN_HEAD_K], v16, (((0,), (0,)), ((), ())), preferred_element_type=F32)
        ms = jnp.mean(o_h * o_h, axis=-1, keepdims=True)
        o_ref[:, lo:lo + GDN_HEAD_V] = (o_h * lax.rsqrt(ms + RMS_EPS) * nw_ref[...]) * \
            _silu(z_ref[:, lo:lo + GDN_HEAD_V])

    @pl.when(c_idx == pl.num_programs(1) - 1)
    def _():
        st_ref[...] = sst[...]


def _gdn_prefill(proj, bsz, seq, lw):
    cs = GDN_CHUNK
    assert seq % cs == 0
    nc = seq // cs
    row = lambda b, c: b * nc + c
    pcol = lambda width, off: pl.BlockSpec((cs, width), lambda b, c: (row(b, c), off // width))
    full = lambda a: pl.BlockSpec(a.shape, lambda b, c: (0,) * a.ndim)
    weights = [lw['gdn_cw_q'], lw['gdn_cw_k'], lw['gdn_cw_v'], lw['gate_bias'], lw['gate_alog'], lw['gdn_norm_w']]
    o, st = pl.pallas_call(
        _gdn_prefill_kernel,
        grid=(bsz, nc),
        in_specs=[pcol(GDN_DIM, COL_Q), pcol(GDN_DIM, COL_K), pcol(GDN_DIM, COL_V), pcol(GDN_DIM, COL_Z_GDN),
                  pcol(LANES, COL_SMALL)] + [full(w) for w in weights],
        out_specs=[pl.BlockSpec((cs, GDN_DIM), lambda b, c: (row(b, c), 0)),
                   pl.BlockSpec((None, GDN_HEADS * GDN_HEAD_K, GDN_HEAD_V), lambda b, c: (b, 0, 0))],
        out_shape=[jax.ShapeDtypeStruct((bsz * seq, GDN_DIM), F32),
                   jax.ShapeDtypeStruct((bsz, GDN_HEADS * GDN_HEAD_K, GDN_HEAD_V), F32)],
        scratch_shapes=[pltpu.VMEM((HIST_ROWS + cs, GDN_DIM), F32),
                        pltpu.VMEM((HIST_ROWS + cs, GDN_DIM), F32),
                        pltpu.VMEM((HIST_ROWS + cs, GDN_DIM), F32),
                        pltpu.VMEM((GDN_HEADS * GDN_HEAD_K, GDN_HEAD_V), F32),
                        pltpu.VMEM((cs, GDN_DIM), BF16),
                        pltpu.VMEM((cs, GDN_DIM), BF16),
                        pltpu.VMEM((cs, 2 * GDN_DIM), BF16),
                        pltpu.VMEM((GDN_HEADS, cs, cs), F32),
                        pltpu.VMEM((GDN_HEADS, cs, cs), BF16),
                        pltpu.VMEM((GDN_HEADS, cs, cs), BF16),
                        pltpu.VMEM((GDN_HEADS, cs, cs), F32),
                        pltpu.VMEM((cs, GDN_DIM), F32),
                        pltpu.VMEM((cs, GDN_DIM), BF16),
                        pltpu.VMEM((1, LANES), F32)],
        compiler_params=_params("parallel", "arbitrary"),
    )(proj, proj, proj, proj, proj, *weights)
    return o, st


def _softmax_rows(s):
    m = jnp.max(s, axis=-1, keepdims=True)
    e = jnp.exp(s - m)
    return e / jnp.sum(e, axis=-1, keepdims=True)


def _xattn_prefill_kernel(x_ref, nw_ref, wq_ref, mk_ref, mv_ref, wo_ref, o_ref):
    x = x_ref[...]
    ms = jnp.mean(x * x, axis=-1, keepdims=True)
    h = x * lax.rsqrt(ms + RMS_EPS) * nw_ref[...]
    q = jnp.dot(h.astype(BF16), wq_ref[...], preferred_element_type=F32)
    outs = []
    for hd in range(XA_HEADS):
        lo = hd * XA_HEAD_DIM
        s = _mm_nt(q[:, lo:lo + XA_HEAD_DIM], mk_ref[:, lo:lo + XA_HEAD_DIM]) * (XA_HEAD_DIM ** -0.5)
        outs.append(_mm(_softmax_rows(s), mv_ref[:, lo:lo + XA_HEAD_DIM]))
    att = jnp.concatenate(outs, axis=-1)
    o_ref[...] = x + jnp.dot(att.astype(BF16), wo_ref[...], preferred_element_type=F32)


def _xattn_prefill(x, nw, wq, mk, mv, wo, bsz, seq, tq):
    n_mem = mk.shape[1]
    nt = seq // tq
    return pl.pallas_call(
        _xattn_prefill_kernel,
        grid=(bsz, nt),
        in_specs=[pl.BlockSpec((tq, D_MODEL), lambda b, t: (b * nt + t, 0)),
                  pl.BlockSpec((1, D_MODEL), lambda b, t: (0, 0)),
                  pl.BlockSpec((D_MODEL, D_MODEL), lambda b, t: (0, 0)),
                  pl.BlockSpec((None, n_mem, D_MODEL), lambda b, t: (b, 0, 0)),
                  pl.BlockSpec((None, n_mem, D_MODEL), lambda b, t: (b, 0, 0)),
                  pl.BlockSpec((D_MODEL, D_MODEL), lambda b, t: (0, 0))],
        out_specs=pl.BlockSpec((tq, D_MODEL), lambda b, t: (b * nt + t, 0)),
        out_shape=jax.ShapeDtypeStruct(x.shape, F32),
        compiler_params=_params("parallel", "arbitrary"),
    )(x, nw.reshape(1, D_MODEL), wq, mk, mv, wo)


def _xattn_decode_kernel(q_ref, mk_ref, mv_ref, o_ref):
    tb = q_ref.shape[0]
    for i in range(tb):
        q = q_ref[i]
        qh = jnp.concatenate([q[:, hd * XA_HEAD_DIM:(hd + 1) * XA_HEAD_DIM] for hd in range(XA_HEADS)], axis=0)
        s = jnp.sum(mk_ref[i] * qh[None], axis=-1, keepdims=True) * (XA_HEAD_DIM ** -0.5)
        e = jnp.exp(s - jnp.max(s, axis=0, keepdims=True))
        p = e / jnp.sum(e, axis=0, keepdims=True)
        o_ref[i] = jnp.sum(p * mv_ref[i], axis=0)


def _xattn_decode(q, mk_all, mv_all, layer, tb):
    _, nb, n_mem, nh, hd = mk_all.shape
    return pl.pallas_call(
        _xattn_decode_kernel,
        grid=(nb // tb,),
        in_specs=[pl.BlockSpec((tb, 1, D_MODEL), lambda i: (i, 0, 0)),
                  pl.BlockSpec((None, tb, n_mem, nh, hd), lambda i: (layer, i, 0, 0, 0)),
                  pl.BlockSpec((None, tb, n_mem, nh, hd), lambda i: (layer, i, 0, 0, 0))],
        out_specs=pl.BlockSpec((tb, nh, hd), lambda i: (i, 0, 0)),
        out_shape=jax.ShapeDtypeStruct((nb, nh, hd), F32),
        compiler_params=_params("parallel"),
    )(q.reshape(nb, 1, D_MODEL), mk_all, mv_all).reshape(nb, D_MODEL)


def _ffn_prefill_kernel(x_ref, g_ref, gprev_ref, u_ref, cw_ref, cb_ref, wd_ref, o_ref, ext, *, tiles_per_seq):
    tm = g_ref.shape[0]
    starts_seq = pl.program_id(0) % tiles_per_seq == 0
    ext[0:HIST_ROWS, :] = jnp.where(starts_seq, 0.0, gprev_ref[...])
    ext[HIST_ROWS:HIST_ROWS + tm, :] = g_ref[...]
    acc = cb_ref[...]
    for k in range(FFN_CONV):
        start = HIST_ROWS - (FFN_CONV - 1) + k
        acc = acc + ext[start:start + tm, :] * cw_ref[k:k + 1, :]
    hmid = _silu(acc) * u_ref[...]
    o_ref[...] = x_ref[...] + jnp.dot(hmid.astype(BF16), wd_ref[...], preferred_element_type=F32)


def _ffn_prefill(x, gu, cw, cb, wd, seq, tm):
    m = x.shape[0]
    d_ff = wd.shape[0]
    assert seq % tm == 0 and tm % HIST_ROWS == 0
    prev_blocks = tm // HIST_ROWS
    return pl.pallas_call(
        functools.partial(_ffn_prefill_kernel, tiles_per_seq=seq // tm),
        grid=(m // tm,),
        in_specs=[pl.BlockSpec((tm, D_MODEL), lambda i: (i, 0)),
                  pl.BlockSpec((tm, d_ff), lambda i: (i, 0)),
                  pl.BlockSpec((HIST_ROWS, d_ff), lambda i: (jnp.maximum(i * prev_blocks - 1, 0), 0)),
                  pl.BlockSpec((tm, d_ff), lambda i: (i, 1)),
                  pl.BlockSpec((FFN_CONV, d_ff), lambda i: (0, 0)),
                  pl.BlockSpec((1, d_ff), lambda i: (0, 0)),
                  pl.BlockSpec((d_ff, D_MODEL), lambda i: (0, 0))],
        out_specs=pl.BlockSpec((tm, D_MODEL), lambda i: (i, 0)),
        out_shape=jax.ShapeDtypeStruct((m, D_MODEL), F32),
        scratch_shapes=[pltpu.VMEM((HIST_ROWS + tm, d_ff), F32)],
        compiler_params=_params("parallel"),
    )(x, gu, gu, gu, cw, cb.reshape(1, d_ff), wd)


def _ffn_decode_kernel(x_ref, g_ref, h0_ref, h1_ref, u_ref, cw_ref, cb_ref, wd_ref, o_ref):
    acc = cb_ref[...] + h0_ref[...] * cw_ref[0:1, :] + h1_ref[...] * cw_ref[1:2, :] + g_ref[...] * cw_ref[2:3, :]
    hmid = _silu(acc) * u_ref[...]
    o_ref[...] = x_ref[...] + jnp.dot(hmid.astype(BF16), wd_ref[...], preferred_element_type=F32)


def _ffn_decode(x, gu, hist0, hist1, cw, cb, wd):
    m = x.shape[0]
    d_ff = wd.shape[0]
    rows = lambda width, col: pl.BlockSpec((m, width), lambda i: (0, col))
    return pl.pallas_call(
        _ffn_decode_kernel,
        grid=(1,),
        in_specs=[rows(D_MODEL, 0), rows(d_ff, 0), rows(d_ff, 0), rows(d_ff, 0), rows(d_ff, 1),
                  pl.BlockSpec((FFN_CONV, d_ff), lambda i: (0, 0)),
                  pl.BlockSpec((1, d_ff), lambda i: (0, 0)),
                  pl.BlockSpec((d_ff, D_MODEL), lambda i: (0, 0))],
        out_specs=rows(D_MODEL, 0),
        out_shape=jax.ShapeDtypeStruct((m, D_MODEL), F32),
        compiler_params=_params("arbitrary"),
    )(x, gu, hist0, hist1, gu, cw, cb.reshape(1, d_ff), wd)


def _expand_lanes(rows, first_lane, width, n_out, parts):
    k = lax.broadcasted_iota(jnp.int32, (LANES, n_out), 0)
    l = lax.broadcasted_iota(jnp.int32, (LANES, n_out), 1)
    sel = jnp.where(l // width == k - first_lane, 1.0, 0.0)
    return _mm_lhs_split(rows, sel, parts)


def _decode_rows_kernel(proj_ref, sh0, sh1, sh2, gh0, gh1, gh2,
                        scw_ref, scb_ref, gcw_ref, bias_ref, alog_ref,
                        xs_ref, xdt_ref, b_ref, c_ref, sdec_ref,
                        q_ref, k_ref, v_ref, beta_ref, gdec_ref):
    def conv(hists, new, cw, cb):
        acc = new * cw[3:4, :]
        for t, hr in enumerate(hists):
            acc = acc + hr[...] * cw[t:t + 1, :]
        return acc if cb is None else acc + cb

    xbc_new = jnp.concatenate([proj_ref[:, COL_XS:COL_XS + SSD_INNER],
                               proj_ref[:, COL_B:COL_B + SSD_BC],
                               proj_ref[:, COL_C:COL_C + SSD_BC]], axis=-1)
    xbc = _silu(conv((sh0, sh1, sh2), xbc_new, scw_ref, scb_ref[...]))
    sp, sg, log_dec = _gate_rows(proj_ref[:, COL_SMALL:COL_SMALL + LANES], bias_ref[...], alog_ref[...])
    xs = xbc[:, 0:SSD_INNER]
    xs_ref[...] = xs
    xdt_ref[...] = xs * _expand_lanes(sp, LANE_DT, SSD_HEAD_DIM, SSD_INNER, 3)
    b_ref[...] = xbc[:, SSD_INNER:SSD_INNER + SSD_BC]
    c_ref[...] = xbc[:, SSD_INNER + SSD_BC:SSD_INNER + 2 * SSD_BC]
    sdec_ref[...] = jnp.exp(_expand_lanes(log_dec, LANE_DT, SSD_HEAD_DIM, SSD_INNER, 3))

    qkv_new = proj_ref[:, COL_Q:COL_Q + 3 * GDN_DIM]
    qkv = _silu(conv((gh0, gh1, gh2), qkv_new, gcw_ref, None))
    for h in range(GDN_HEADS):
        lo = h * GDN_HEAD_K
        qh = qkv[:, lo:lo + GDN_HEAD_K]
        kh = qkv[:, GDN_DIM + lo:GDN_DIM + lo + GDN_HEAD_K]
        q_ref[:, lo:lo + GDN_HEAD_K] = qh * lax.rsqrt(jnp.sum(qh * qh, axis=-1, keepdims=True) + L2_EPS) * \
            (GDN_HEAD_K ** -0.5)
        k_ref[:, lo:lo + GDN_HEAD_K] = kh * lax.rsqrt(jnp.sum(kh * kh, axis=-1, keepdims=True) + L2_EPS)
    v_ref[...] = qkv[:, 2 * GDN_DIM:3 * GDN_DIM]
    beta_ref[...] = _expand_lanes(sg, LANE_BETA, GDN_HEAD_V, GDN_DIM, 3)
    gdec_ref[...] = jnp.exp(_expand_lanes(log_dec, LANE_G, GDN_HEAD_V, GDN_DIM, 3))


def _decode_rows(proj, ssd_hist, gdn_hist, lw):
    m = proj.shape[0]
    ins = [proj] + [ssd_hist[:, t] for t in range(SSD_CONV - 1)] + [gdn_hist[:, t] for t in range(GDN_CONV - 1)]
    ins += [lw['ssd_conv_w'], lw['ssd_conv_b'], lw['gdn_conv_w'], lw['gate_bias'], lw['gate_alog']]
    widths = [SSD_INNER, SSD_INNER, SSD_BC, SSD_BC, SSD_INNER, GDN_DIM, GDN_DIM, GDN_DIM, GDN_DIM, GDN_DIM]
    return pl.pallas_call(
        _decode_rows_kernel,
        grid=(1,),
        in_specs=[pl.BlockSpec(a.shape, lambda i, nd=a.ndim: (0,) * nd) for a in ins],
        out_specs=[pl.BlockSpec((m, w), lambda i: (0, 0)) for w in widths],
        out_shape=[jax.ShapeDtypeStruct((m, w), F32) for w in widths],
        compiler_params=_params("arbitrary"),
    )(*ins)


def _rows_to_cols(rows, i, parts):
    tb = rows.shape[0]
    r = lax.broadcasted_iota(jnp.int32, (tb, LANES), 0)
    sel = jnp.where(r == i, 1.0, 0.0).astype(BF16)
    acc = None
    for p in _split(rows, parts):
        t = lax.dot_general(p, sel, (((0,), (0,)), ((), ())), preferred_element_type=F32)
        acc = t if acc is None else acc + t
    return acc


def _only_row(rows, i):
    r = lax.broadcasted_iota(jnp.int32, rows.shape, 0)
    return jnp.where(r == i, rows, 0.0)


def _decode_state_kernel(xdt_ref, b_ref, c_ref, sdec_ref, q_ref, k_ref, v_ref, beta_ref, gdec_ref,
                         hs_ref, ss_ref, *rest):
    y_ref, o_ref, hs_out, ss_out = rest[-4:]
    tb = xdt_ref.shape[0]
    xdt = xdt_ref[...]
    bm = b_ref[...]
    cm = c_ref[...]
    sdec = sdec_ref[...]
    qn = q_ref[...]
    kn = k_ref[...]
    vv = v_ref[...]
    beta = beta_ref[...]
    gdec = gdec_ref[...]
    rows_g = SSD_INNER // SSD_GROUPS

    y_acc = jnp.zeros((tb, SSD_INNER), F32)
    ks_acc = jnp.zeros((tb, GDN_DIM), F32)
    for i in range(tb):
        dec_c = _rows_to_cols(sdec, i, 2)
        xdt_c = _rows_to_cols(xdt, i, 1)
        pieces = []
        for g in range(SSD_GROUPS):
            rs = slice(g * rows_g, (g + 1) * rows_g)
            bg = bm[i:i + 1, g * SSD_STATE:(g + 1) * SSD_STATE]
            h_new = hs_ref[i, rs, :] * dec_c[rs, :] + xdt_c[rs, :] * bg
            hs_out[i, rs, :] = h_new
            cg = _only_row(cm[:, g * SSD_STATE:(g + 1) * SSD_STATE], i)
            pieces.append(_mm_nt(cg, h_new))
        y_acc = y_acc + jnp.concatenate(pieces, axis=-1)
        g_c = _rows_to_cols(gdec, i, 2)
        s_dec = ss_ref[i] * g_c
        ss_out[i] = s_dec
        pieces = []
        for h in range(GDN_HEADS):
            rs = slice(h * GDN_HEAD_K, (h + 1) * GDN_HEAD_K)
            pieces.append(_mm(_only_row(kn[:, rs], i), s_dec[rs, :]))
        ks_acc = ks_acc + jnp.concatenate(pieces, axis=-1)
    y_ref[...] = y_acc

    delta = beta * (vv - ks_acc)
    o_acc = jnp.zeros((tb, GDN_DIM), F32)
    for i in range(tb):
        k_c = _rows_to_cols(kn, i, 1)
        pieces = []
        for h in range(GDN_HEADS):
            rs = slice(h * GDN_HEAD_K, (h + 1) * GDN_HEAD_K)
            d_row = delta[i:i + 1, rs]
            s_new = ss_out[i, rs, :] + k_c[rs, :] * d_row
            ss_out[i, rs, :] = s_new
            pieces.append(_mm(_only_row(qn[:, rs], i), s_new))
        o_acc = o_acc + jnp.concatenate(pieces, axis=-1)
    o_ref[...] = o_acc


def _decode_state(rows, hs_all, ss_all, layer, tb, carried):
    xs, xdt, bm, cm, sdec, qn, kn, vv, beta, gdec = rows
    nb = xdt.shape[0]
    row_ins = [xdt, bm, cm, sdec, qn, kn, vv, beta, gdec]
    rspec = lambda a: pl.BlockSpec((tb, a.shape[1]), lambda i: (i, 0))
    sspec = lambda a: pl.BlockSpec((None, tb) + a.shape[2:], lambda i: (layer, i, 0, 0))
    n_in = len(row_ins) + 2
    aliases = {n_in + j: 2 + j for j in range(len(carried))}
    return pl.pallas_call(
        _decode_state_kernel,
        grid=(nb // tb,),
        in_specs=[rspec(a) for a in row_ins] + [sspec(hs_all), sspec(ss_all)] +
                 [pl.BlockSpec(memory_space=pl.ANY) for _ in carried],
        out_specs=[rspec(xdt), rspec(qn), sspec(hs_all), sspec(ss_all)],
        out_shape=[jax.ShapeDtypeStruct(xdt.shape, F32), jax.ShapeDtypeStruct(qn.shape, F32),
                   jax.ShapeDtypeStruct(hs_all.shape, F32), jax.ShapeDtypeStruct(ss_all.shape, F32)],
        input_output_aliases=aliases,
        compiler_params=_params("parallel"),
    )(*row_ins, hs_all, ss_all, *carried)


def _decode_out_kernel(x_ref, y_ref, xs_ref, o_ref, proj_ref, dexp_ref, snw_ref, gnw_ref, w1_ref, w2_ref, out_ref):
    y = y_ref[...] + dexp_ref[...] * xs_ref[...]
    gs = SSD_INNER // SSD_GROUPS
    ys = []
    for g in range(SSD_GROUPS):
        yg = y[:, g * gs:(g + 1) * gs] * _silu(proj_ref[:, COL_Z_SSD + g * gs:COL_Z_SSD + (g + 1) * gs])
        ms = jnp.mean(yg * yg, axis=-1, keepdims=True)
        ys.append(yg * lax.rsqrt(ms + RMS_EPS) * snw_ref[:, g * gs:(g + 1) * gs])
    os = []
    for h in range(GDN_HEADS):
        lo = h * GDN_HEAD_V
        oh = o_ref[:, lo:lo + GDN_HEAD_V]
        ms = jnp.mean(oh * oh, axis=-1, keepdims=True)
        os.append(oh * lax.rsqrt(ms + RMS_EPS) * gnw_ref[...] *
                  _silu(proj_ref[:, COL_Z_GDN + lo:COL_Z_GDN + lo + GDN_HEAD_V]))
    yn = jnp.concatenate(ys, axis=-1)
    on = jnp.concatenate(os, axis=-1)
    out_ref[...] = x_ref[...] + _mm(yn, w1_ref[...]) + _mm(on, w2_ref[...])


def _decode_out(x, y, xs, o, proj, lw):
    ins = [x, y, xs, o, proj, lw['ssd_d_exp'], lw['ssd_norm_w'], lw['gdn_norm_w'], lw['w_out_ssd'], lw['w_out_gdn']]
    return pl.pallas_call(
        _decode_out_kernel,
        grid=(1,),
        in_specs=[pl.BlockSpec(a.shape, lambda i, nd=a.ndim: (0,) * nd) for a in ins],
        out_specs=pl.BlockSpec(x.shape, lambda i: (0, 0)),
        out_shape=jax.ShapeDtypeStruct(x.shape, F32),
        compiler_params=_params("arbitrary"),
    )(*ins)


def _prep_layer(i, p):
    w_in = p['w_in'][i]
    o_zs = 0
    o_xbc = o_zs + SSD_INNER
    o_dt = o_xbc + SSD_INNER + 2 * SSD_BC
    o_qkv = o_dt + SSD_HEADS
    o_zg = o_qkv + 3 * GDN_DIM
    o_b = o_zg + GDN_DIM
    o_a = o_b + GDN_HEADS
    used = SSD_HEADS + 2 * GDN_HEADS
    w_perm = jnp.concatenate([
        w_in[:, o_zs:o_zs + SSD_INNER],
        w_in[:, o_xbc:o_xbc + SSD_INNER],
        w_in[:, o_qkv:o_qkv + 3 * GDN_DIM],
        w_in[:, o_zg:o_zg + GDN_DIM],
        w_in[:, o_xbc + SSD_INNER:o_xbc + SSD_INNER + 2 * SSD_BC],
        w_in[:, o_dt:o_dt + SSD_HEADS],
        w_in[:, o_b:o_b + GDN_HEADS],
        w_in[:, o_a:o_a + GDN_HEADS],
        jnp.zeros((D_MODEL, PROJ_COLS - COL_SMALL - used), F32)], axis=1).astype(BF16)
    lane_pad = jnp.zeros((LANES - used,), F32)
    scw = p['ssd_conv_w'][i]
    scb = p['ssd_conv_b'][i].reshape(1, -1)
    gcw = p['gdn_conv_w'][i]
    return {
        'w_in': w_perm,
        'norm_mix_w': p['norm_mix_w'][i],
        'ssd_conv_w': scw, 'ssd_conv_b': scb, 'gdn_conv_w': gcw,
        'ssd_cw_x': scw[:, :SSD_INNER], 'ssd_cw_b': scw[:, SSD_INNER:SSD_INNER + SSD_BC],
        'ssd_cw_c': scw[:, SSD_INNER + SSD_BC:],
        'ssd_cb_x': scb[:, :SSD_INNER], 'ssd_cb_b': scb[:, SSD_INNER:SSD_INNER + SSD_BC],
        'ssd_cb_c': scb[:, SSD_INNER + SSD_BC:],
        'gdn_cw_q': gcw[:, :GDN_DIM], 'gdn_cw_k': gcw[:, GDN_DIM:2 * GDN_DIM], 'gdn_cw_v': gcw[:, 2 * GDN_DIM:],
        'gate_bias': jnp.concatenate([p['ssd_dt_bias'][i], jnp.zeros((GDN_HEADS,), F32), p['gdn_dt_bias'][i],
                                      lane_pad]).reshape(1, LANES),
        'gate_alog': jnp.concatenate([p['ssd_a_log'][i], jnp.zeros((GDN_HEADS,), F32), p['gdn_a_log'][i],
                                      lane_pad]).reshape(1, LANES),
        'ssd_d_exp': jnp.repeat(p['ssd_d'][i], SSD_HEAD_DIM).reshape(1, SSD_INNER),
        'ssd_norm_w': p['ssd_norm_w'][i].reshape(1, SSD_INNER),
        'gdn_norm_w': p['gdn_norm_w'][i].reshape(1, GDN_HEAD_V),
        'w_out_ssd': p['w_out'][i, :SSD_INNER].astype(BF16),
        'w_out_gdn': p['w_out'][i, SSD_INNER:].astype(BF16),
        'norm_xa_w': p['norm_xa_w'][i], 'norm_mem_w': p['norm_mem_w'][i],
        'xa_wq': p['xa_wq'][i].astype(BF16), 'xa_wo': p['xa_wo'][i].astype(BF16),
        'xa_wkv': jnp.concatenate([p['xa_wk'][i], p['xa_wv'][i]], axis=1).astype(BF16),
        'norm_ffn_w': p['norm_ffn_w'][i],
        'ffn_w_gu': jnp.concatenate([p['ffn_w_gate'][i], p['ffn_w_up'][i]], axis=1).astype(BF16),
        'ffn_conv_w': p['ffn_conv_w'][i], 'ffn_conv_b': p['ffn_conv_b'][i],
        'ffn_w_down': p['ffn_w_down'][i].astype(BF16),
    }


def _unpermute_conv_rows(proj_rows):
    ssd = jnp.concatenate([proj_rows[..., COL_XS:COL_XS + SSD_INNER],
                           proj_rows[..., COL_B:COL_B + SSD_BC],
                           proj_rows[..., COL_C:COL_C + SSD_BC]], axis=-1)
    gdn = proj_rows[..., COL_Q:COL_Q + 3 * GDN_DIM]
    return ssd, gdn


def _tile(m, pref):
    t = min(m, pref)
    while m % t:
        t //= 2
    return t


def kernel(x_prompt, x_sample, mem_prompt, cache_mem_k, cache_mem_v, state_ssd_conv, state_ssd, state_gdn_conv, state_gdn, state_ffn_conv, norm_mix_w, w_in, ssd_conv_w, ssd_conv_b, ssd_dt_bias, ssd_a_log, ssd_d, ssd_norm_w, gdn_conv_w, gdn_dt_bias, gdn_a_log, gdn_norm_w, w_out, norm_xa_w, norm_mem_w, xa_wq, xa_wk, xa_wv, xa_wo, norm_ffn_w, ffn_w_gate, ffn_w_up, ffn_conv_w, ffn_conv_b, ffn_w_down, final_norm_w):
    params = dict(norm_mix_w=norm_mix_w, w_in=w_in, ssd_conv_w=ssd_conv_w, ssd_conv_b=ssd_conv_b,
                  ssd_dt_bias=ssd_dt_bias, ssd_a_log=ssd_a_log, ssd_d=ssd_d, ssd_norm_w=ssd_norm_w,
                  gdn_conv_w=gdn_conv_w, gdn_dt_bias=gdn_dt_bias, gdn_a_log=gdn_a_log, gdn_norm_w=gdn_norm_w,
                  w_out=w_out, norm_xa_w=norm_xa_w, norm_mem_w=norm_mem_w, xa_wq=xa_wq, xa_wk=xa_wk, xa_wv=xa_wv,
                  xa_wo=xa_wo, norm_ffn_w=norm_ffn_w, ffn_w_gate=ffn_w_gate, ffn_w_up=ffn_w_up,
                  ffn_conv_w=ffn_conv_w, ffn_conv_b=ffn_conv_b, ffn_w_down=ffn_w_down)
    depth = w_in.shape[0]
    bsz, seq, _ = x_prompt.shape
    nb = x_sample.shape[0]
    n_mem = mem_prompt.shape[1]
    d_ff = ffn_w_down.shape[1]
    mp = bsz * seq

    xp = x_prompt.reshape(mp, D_MODEL)
    xs = x_sample.reshape(nb, D_MODEL)
    mem = mem_prompt.reshape(bsz * n_mem, D_MODEL)
    tm_p = _tile(mp, 1024)
    tm_mem = _tile(bsz * n_mem, 1024)
    dec_tile = _tile(nb, DEC_TILE)

    hs_all = state_ssd.reshape(depth, nb, SSD_INNER, SSD_STATE)
    ss_all = state_gdn.reshape(depth, nb, GDN_HEADS * GDN_HEAD_K, GDN_HEAD_V)
    mk_all, mv_all = cache_mem_k, cache_mem_v
    new_states = ()

    mkp, mvp = [], []
    p_sc, p_sh, p_gc, p_gs, p_fc = [], [], [], [], []
    s_sc, s_gc, s_fc = [], [], []
    for i in range(depth):
        lw = _prep_layer(i, params)

        kv = _norm_mm(mem, lw['norm_mem_w'], lw['xa_wkv'], tm_mem)
        mk = kv[:, :D_MODEL].reshape(bsz, n_mem, D_MODEL)
        mv = kv[:, D_MODEL:].reshape(bsz, n_mem, D_MODEL)
        mkp.append(mk.reshape(bsz, n_mem, XA_HEADS, XA_HEAD_DIM))
        mvp.append(mv.reshape(bsz, n_mem, XA_HEADS, XA_HEAD_DIM))

        proj = _norm_mm(xp, lw['norm_mix_w'], lw['w_in'], tm_p)
        y, h_new = _ssd_prefill(proj, bsz, seq, lw)
        o, s_new = _gdn_prefill(proj, bsz, seq, lw)
        tail = proj.reshape(bsz, seq, PROJ_COLS)[:, seq - (SSD_CONV - 1):]
        ssd_tail, gdn_tail = _unpermute_conv_rows(tail)
        p_sc.append(ssd_tail)
        p_gc.append(gdn_tail)
        p_sh.append(h_new.reshape(bsz, SSD_HEADS, SSD_HEAD_DIM, SSD_STATE))
        p_gs.append(s_new.reshape(bsz, GDN_HEADS, GDN_HEAD_K, GDN_HEAD_V))
        xp = _mm_res(xp, [y, o], [lw['w_out_ssd'], lw['w_out_gdn']], _tile(mp, 512))
        xp = _xattn_prefill(xp, lw['norm_xa_w'], lw['xa_wq'], mk, mv, lw['xa_wo'], bsz, seq, _tile(seq, 512))
        gu = _norm_mm(xp, lw['norm_ffn_w'], lw['ffn_w_gu'], tm_p)
        p_fc.append(gu.reshape(bsz, seq, 2 * d_ff)[:, seq - (FFN_CONV - 1):, :d_ff])
        xp = _ffn_prefill(xp, gu, lw['ffn_conv_w'], lw['ffn_conv_b'], lw['ffn_w_down'], seq, _tile(seq, 256))

        proj_s = _norm_mm(xs, lw['norm_mix_w'], lw['w_in'], nb)
        ssd_new, gdn_new = _unpermute_conv_rows(proj_s)
        s_sc.append(jnp.concatenate([state_ssd_conv[i][:, 1:], ssd_new[:, None]], axis=1))
        s_gc.append(jnp.concatenate([state_gdn_conv[i][:, 1:], gdn_new[:, None]], axis=1))
        rows = _decode_rows(proj_s, state_ssd_conv[i], state_gdn_conv[i], lw)
        y_s, o_s, hs_new, ss_new = _decode_state(rows, hs_all, ss_all, i, dec_tile, new_states)
        new_states = (hs_new, ss_new)
        xs = _decode_out(xs, y_s, rows[0], o_s, proj_s, lw)
        q_s = _norm_mm(xs, lw['norm_xa_w'], lw['xa_wq'], nb)
        att = _xattn_decode(q_s, mk_all, mv_all, i, _tile(nb, 4))
        xs = _mm_res(xs, [att], [lw['xa_wo']], nb)
        gu_s = _norm_mm(xs, lw['norm_ffn_w'], lw['ffn_w_gu'], nb)
        s_fc.append(jnp.concatenate([state_ffn_conv[i][:, 1:], gu_s[:, None, :d_ff]], axis=1))
        xs = _ffn_decode(xs, gu_s, state_ffn_conv[i][:, 0], state_ffn_conv[i][:, 1],
                         lw['ffn_conv_w'], lw['ffn_conv_b'], lw['ffn_w_down'])

    y_prompt = _rmsnorm(xp, final_norm_w, tm_p).reshape(bsz, seq, D_MODEL)
    y_sample = _rmsnorm(xs, final_norm_w, nb).reshape(nb, 1, D_MODEL)
    return (y_prompt, y_sample, jnp.stack(mkp), jnp.stack(mvp),
            jnp.stack(p_sc), jnp.stack(p_sh), jnp.stack(p_gc), jnp.stack(p_gs), jnp.stack(p_fc),
            jnp.stack(s_sc), new_states[0].reshape(state_ssd.shape), jnp.stack(s_gc),
            new_states[1].reshape(state_gdn.shape), jnp.stack(s_fc))
```

```python
import functools

import jax
import jax.numpy as jnp
from jax import lax
from jax.experimental import pallas as pl
from jax.experimental.pallas import tpu as pltpu

F32 = jnp.float32
BF16 = jnp.bfloat16

D_MODEL = 1024
SSD_HEADS = 16
SSD_HEAD_DIM = 64
SSD_STATE = 128
SSD_GROUPS = 2
SSD_INNER = SSD_HEADS * SSD_HEAD_DIM
SSD_BC = SSD_GROUPS * SSD_STATE
SSD_CONV = 4
GDN_HEADS = 8
GDN_HEAD_K = 128
GDN_HEAD_V = 128
GDN_DIM = GDN_HEADS * GDN_HEAD_V
GDN_CONV = 4
XA_HEADS = 4
XA_HEAD_DIM = D_MODEL // XA_HEADS
FFN_CONV = 3
RMS_EPS = 1e-6
L2_EPS = 1e-6

SUBLANES = 8
LANES = 128
VMEM_LIMIT_BYTES = 56 * 1024 * 1024

COL_Z_SSD = 0
COL_XS = 1024
COL_Q = 2048
COL_K = 3072
COL_V = 4096
COL_Z_GDN = 5120
COL_B = 6144
COL_C = 6400
COL_SMALL = 6656
PROJ_COLS = 7168
LANE_DT = 0
LANE_BETA = 16
LANE_G = 24

SSD_CHUNK = 256
GDN_CHUNK = 128
DEC_TILE = 8


def _params(*sem):
    return pltpu.CompilerParams(dimension_semantics=sem, vmem_limit_bytes=VMEM_LIMIT_BYTES)


def _mm(a, b):
    return jnp.dot(a.astype(BF16), b.astype(BF16), preferred_element_type=F32)


def _mm_nt(a, b):
    return lax.dot_general(a.astype(BF16), b.astype(BF16), (((1,), (1,)), ((), ())),
                           preferred_element_type=F32)


def _split(x, parts):
    out = []
    r = x
    for _ in range(parts - 1):
        p = r.astype(BF16)
        out.append(p)
        r = r - p.astype(F32)
    out.append(r.astype(BF16))
    return out


def _mm_rhs_split(a_exact, b, parts=3):
    a16 = a_exact.astype(BF16)
    acc = None
    for p in _split(b, parts):
        t = jnp.dot(a16, p, preferred_element_type=F32)
        acc = t if acc is None else acc + t
    return acc


def _mm_lhs_split(a, b_exact, parts=3):
    b16 = b_exact.astype(BF16)
    acc = None
    for p in _split(a, parts):
        t = jnp.dot(p, b16, preferred_element_type=F32)
        acc = t if acc is None else acc + t
    return acc


def _silu(x):
    return x / (1.0 + jnp.exp(-x))


def _sigmoid(x):
    return 1.0 / (1.0 + jnp.exp(-x))


def _softplus(x):
    return jnp.maximum(x, 0.0) + jnp.log1p(jnp.exp(-jnp.abs(x)))


def _col(arr, lane):
    idx = lax.broadcasted_iota(jnp.int32, arr.shape, 1)
    return jnp.sum(jnp.where(idx == lane, arr, 0.0), axis=1, keepdims=True)


def _gate_rows(sm, bias, alog):
    sp = _softplus(sm + bias)
    sg = _sigmoid(sm)
    lane = lax.broadcasted_iota(jnp.int32, alog.shape, 1)
    has_decay = (lane < LANE_BETA) | ((lane >= LANE_G) & (lane < LANE_G + GDN_HEADS))
    neg_a = jnp.where(has_decay, -jnp.exp(alog), 0.0)
    return sp, sg, sp * neg_a


def _norm_mm_kernel(x_ref, nw_ref, w_ref, o_ref, xn_ref):
    @pl.when(pl.program_id(1) == 0)
    def _():
        x = x_ref[...]
        ms = jnp.mean(x * x, axis=-1, keepdims=True)
        xn_ref[...] = (x * lax.rsqrt(ms + RMS_EPS) * nw_ref[...]).astype(BF16)

    o_ref[...] = jnp.dot(xn_ref[...], w_ref[...], preferred_element_type=F32)


MAX_COL_TILE_LANES = 11 * LANES


def _col_tile(n):
    return max(t for t in range(LANES, MAX_COL_TILE_LANES + 1, LANES) if n % t == 0)


def _norm_mm(x, nw, w, tm):
    m, k = x.shape
    n = w.shape[1]
    tn = _col_tile(n)
    assert m % tm == 0
    return pl.pallas_call(
        _norm_mm_kernel,
        grid=(m // tm, n // tn),
        in_specs=[pl.BlockSpec((tm, k), lambda i, j: (i, 0)),
                  pl.BlockSpec((1, k), lambda i, j: (0, 0)),
                  pl.BlockSpec((k, tn), lambda i, j: (0, j))],
        out_specs=pl.BlockSpec((tm, tn), lambda i, j: (i, j)),
        out_shape=jax.ShapeDtypeStruct((m, n), F32),
        scratch_shapes=[pltpu.VMEM((tm, k), BF16)],
        compiler_params=_params("parallel", "arbitrary"),
    )(x, nw.reshape(1, k), w)


HIST_ROWS = SUBLANES


def _norm_mm_conv_kernel(x_ref, nw_ref, w_ref, cw_ref, cb_ref, act_ref, o_ref, tail_ref,
                         xn_ref, ext_ref, hist_ref, *, taps, tiles_per_seq, plain_tiles):
    i = pl.program_id(0)
    j = pl.program_id(1)
    tm = x_ref.shape[0]

    @pl.when(j == 0)
    def _():
        x = x_ref[...]
        ms = jnp.mean(x * x, axis=-1, keepdims=True)
        xn_ref[...] = (x * lax.rsqrt(ms + RMS_EPS) * nw_ref[...]).astype(BF16)

    acc = jnp.dot(xn_ref[...], w_ref[...], preferred_element_type=F32)
    tail_ref[...] = acc[tm - HIST_ROWS:tm, :]

    def conv_tile():
        starts_seq = i % tiles_per_seq == 0

        @pl.when(starts_seq)
        def _():
            ext_ref[0:HIST_ROWS, :] = jnp.zeros((HIST_ROWS, ext_ref.shape[1]), F32)

        @pl.when(jnp.logical_not(starts_seq))
        def _():
            ext_ref[0:HIST_ROWS, :] = hist_ref[j]

        ext_ref[HIST_ROWS:HIST_ROWS + tm, :] = acc
        hist_ref[j] = acc[tm - HIST_ROWS:tm, :]
        c = cb_ref[...]
        for k in range(taps):
            start = HIST_ROWS - (taps - 1) + k
            c = c + ext_ref[start:start + tm, :] * cw_ref[k:k + 1, :]
        o_ref[...] = jnp.where(act_ref[...] > 0.0, _silu(c), c)

    if plain_tiles:
        is_plain = functools.reduce(jnp.logical_or, [j == t for t in plain_tiles])

        @pl.when(is_plain)
        def _():
            o_ref[...] = acc

        pl.when(jnp.logical_not(is_plain))(conv_tile)
    else:
        conv_tile()


def _norm_mm_conv(x, nw, w, cw, cb, act, conv_ranges, seq, tm):
    m, k = x.shape
    n = w.shape[1]
    taps = cw.shape[0]
    tn = _col_tile(n)
    assert m % tm == 0 and seq % tm == 0 and tm >= HIST_ROWS
    n_tiles = n // tn
    plain_tiles = tuple(t for t in range(n_tiles)
                        if not any(lo < (t + 1) * tn and t * tn < hi for lo, hi in conv_ranges))
    return pl.pallas_call(
        functools.partial(_norm_mm_conv_kernel, taps=taps, tiles_per_seq=seq // tm, plain_tiles=plain_tiles),
        grid=(m // tm, n_tiles),
        in_specs=[pl.BlockSpec((tm, k), lambda i, j: (i, 0)),
                  pl.BlockSpec((1, k), lambda i, j: (0, 0)),
                  pl.BlockSpec((k, tn), lambda i, j: (0, j)),
                  pl.BlockSpec((taps, tn), lambda i, j: (0, j)),
                  pl.BlockSpec((1, tn), lambda i, j: (0, j)),
                  pl.BlockSpec((1, tn), lambda i, j: (0, j))],
        out_specs=[pl.BlockSpec((tm, tn), lambda i, j: (i, j)),
                   pl.BlockSpec((None, HIST_ROWS, tn), lambda i, j: (i, 0, j))],
        out_shape=[jax.ShapeDtypeStruct((m, n), F32),
                   jax.ShapeDtypeStruct((m // tm, HIST_ROWS, n), F32)],
        scratch_shapes=[pltpu.VMEM((tm, k), BF16),
                        pltpu.VMEM((HIST_ROWS + tm, tn), F32),
                        pltpu.VMEM((n_tiles, HIST_ROWS, tn), F32)],
        compiler_params=_params("arbitrary", "arbitrary"),
    )(x, nw.reshape(1, k), w, cw, cb, act)


def _mm_res_kernel(*refs, n_in):
    res_ref = refs[0]
    a_refs = refs[1:1 + n_in]
    w_refs = refs[1 + n_in:1 + 2 * n_in]
    o_ref = refs[1 + 2 * n_in]
    acc = res_ref[...]
    for a_ref, w_ref in zip(a_refs, w_refs):
        acc = acc + jnp.dot(a_ref[...].astype(BF16), w_ref[...], preferred_element_type=F32)
    o_ref[...] = acc


def _mm_res(res, a_list, w_list, tm):
    m, n = res.shape
    assert m % tm == 0
    n_in = len(a_list)
    in_specs = [pl.BlockSpec((tm, n), lambda i: (i, 0))]
    in_specs += [pl.BlockSpec((tm, a.shape[1]), lambda i: (i, 0)) for a in a_list]
    in_specs += [pl.BlockSpec(w.shape, lambda i: (0, 0)) for w in w_list]
    return pl.pallas_call(
        functools.partial(_mm_res_kernel, n_in=n_in),
        grid=(m // tm,),
        in_specs=in_specs,
        out_specs=pl.BlockSpec((tm, n), lambda i: (i, 0)),
        out_shape=jax.ShapeDtypeStruct((m, n), F32),
        compiler_params=_params("parallel"),
    )(res, *a_list, *w_list)


def _rmsnorm_kernel(x_ref, nw_ref, o_ref):
    x = x_ref[...]
    ms = jnp.mean(x * x, axis=-1, keepdims=True)
    o_ref[...] = x * lax.rsqrt(ms + RMS_EPS) * nw_ref[...]


def _rmsnorm(x, nw, tm):
    m, k = x.shape
    return pl.pallas_call(
        _rmsnorm_kernel,
        grid=(m // tm,),
        in_specs=[pl.BlockSpec((tm, k), lambda i: (i, 0)), pl.BlockSpec((1, k), lambda i: (0, 0))],
        out_specs=pl.BlockSpec((tm, k), lambda i: (i, 0)),
        out_shape=jax.ShapeDtypeStruct((m, k), F32),
        compiler_params=_params("parallel"),
    )(x, nw.reshape(1, k))


def _causal_mask(cs, strict=False):
    r = lax.broadcasted_iota(jnp.int32, (cs, cs), 0)
    c = lax.broadcasted_iota(jnp.int32, (cs, cs), 1)
    return (r > c) if strict else (r >= c)


def _ssd_prefill_kernel(z_ref, xs_ref, b_ref, c_ref, sm_ref,
                        bias_ref, alog_ref, dexp_ref, nw_ref,
                        y_ref, st_ref,
                        hst):
    c_idx = pl.program_id(1)
    first = c_idx == 0
    cs = xs_ref.shape[0]

    @pl.when(first)
    def _():
        hst[...] = jnp.zeros(hst.shape, F32)

    xs = xs_ref[...]
    bm = b_ref[...]
    cm = c_ref[...]

    sp, _, dta = _gate_rows(sm_ref[...], bias_ref[...], alog_ref[...])
    causal = _causal_mask(cs)
    cum = _mm_rhs_split(jnp.where(causal, 1.0, 0.0), dta)
    cum_row = cum.T

    lane_lo = lax.broadcasted_iota(jnp.int32, (1, LANES), 1) < SSD_HEAD_DIM
    row_lo = lax.broadcasted_iota(jnp.int32, (LANES, 1), 0) < SSD_HEAD_DIM
    heads_per_group = SSD_HEADS // SSD_GROUPS
    for g in range(SSD_GROUPS):
        cg = cm[:, g * SSD_STATE:(g + 1) * SSD_STATE]
        bg = bm[:, g * SSD_STATE:(g + 1) * SSD_STATE]
        cb = _mm_nt(cg, bg)
        rows_g = heads_per_group * SSD_HEAD_DIM
        y_state = _mm_nt(cg, hst[g * rows_g:(g + 1) * rows_g, :])
        for hp in range(heads_per_group // 2):
            h0 = g * heads_per_group + 2 * hp
            lo = h0 * SSD_HEAD_DIM
            xs_pair = xs[:, lo:lo + LANES]
            ci = [_col(cum, LANE_DT + h0), _col(cum, LANE_DT + h0 + 1)]
            dti = [_col(sp, LANE_DT + h0), _col(sp, LANE_DT + h0 + 1)]
            xdt_pair = xs_pair * jnp.where(lane_lo, dti[0], dti[1])
            y_pair = None
            for e in range(2):
                cj = cum_row[LANE_DT + h0 + e:LANE_DT + h0 + e + 1, :]
                seg = jnp.exp(jnp.where(causal, ci[e] - cj, -jnp.inf))
                keep = lane_lo if e == 0 else jnp.logical_not(lane_lo)
                t = _mm(cb * seg, jnp.where(keep, xdt_pair, 0.0))
                y_pair = t if y_pair is None else y_pair + t
            y_pair = y_pair + y_state[:, 2 * hp * SSD_HEAD_DIM:2 * hp * SSD_HEAD_DIM + LANES] * \
                jnp.where(lane_lo, jnp.exp(ci[0]), jnp.exp(ci[1]))
            y_pair = y_pair + dexp_ref[:, lo:lo + LANES] * xs_pair
            y_ref[:, lo:lo + LANES] = y_pair
            cl = [ci[0][cs - 1:cs, :], ci[1][cs - 1:cs, :]]
            to_end = jnp.where(lane_lo, jnp.exp(cl[0] - ci[0]), jnp.exp(cl[1] - ci[1]))
            contrib = _mm((xdt_pair * to_end).T, bg)
            dec = jnp.where(row_lo, jnp.exp(cl[0]), jnp.exp(cl[1]))
            hst[lo:lo + LANES, :] = hst[lo:lo + LANES, :] * dec + contrib

    gs = SSD_INNER // SSD_GROUPS
    for g in range(SSD_GROUPS):
        yg = y_ref[:, g * gs:(g + 1) * gs] * _silu(z_ref[:, g * gs:(g + 1) * gs])
        ms = jnp.mean(yg * yg, axis=-1, keepdims=True)
        y_ref[:, g * gs:(g + 1) * gs] = yg * lax.rsqrt(ms + RMS_EPS) * nw_ref[:, g * gs:(g + 1) * gs]

    @pl.when(c_idx == pl.num_programs(1) - 1)
    def _():
        st_ref[...] = hst[...]


def _ssd_prefill(proj, bsz, seq, lw):
    cs = SSD_CHUNK
    assert seq % cs == 0
    nc = seq // cs
    row = lambda b, c: b * nc + c
    pcol = lambda width, off: pl.BlockSpec((cs, width), lambda b, c: (row(b, c), off // width))
    full = lambda a: pl.BlockSpec(a.shape, lambda b, c: (0,) * a.ndim)
    weights = [lw['gate_bias'], lw['gate_alog'], lw['ssd_d_exp'], lw['ssd_norm_w']]
    y, st = pl.pallas_call(
        _ssd_prefill_kernel,
        grid=(bsz, nc),
        in_specs=[pcol(SSD_INNER, COL_Z_SSD), pcol(SSD_INNER, COL_XS), pcol(SSD_BC, COL_B), pcol(SSD_BC, COL_C),
                  pcol(LANES, COL_SMALL)] + [full(w) for w in weights],
        out_specs=[pl.BlockSpec((cs, SSD_INNER), lambda b, c: (row(b, c), 0)),
                   pl.BlockSpec((None, SSD_INNER, SSD_STATE), lambda b, c: (b, 0, 0))],
        out_shape=[jax.ShapeDtypeStruct((bsz * seq, SSD_INNER), F32),
                   jax.ShapeDtypeStruct((bsz, SSD_INNER, SSD_STATE), F32)],
        scratch_shapes=[pltpu.VMEM((SSD_INNER, SSD_STATE), F32)],
        compiler_params=_params("parallel", "arbitrary"),
    )(proj, proj, proj, proj, proj, *weights)
    return y, st


INV_BASE = 16


def _same_block(cs, size):
    r = lax.broadcasted_iota(jnp.int32, (cs, cs), 0) // size
    c = lax.broadcasted_iota(jnp.int32, (cs, cs), 1) // size
    return r == c


def _bdot(a16, b16):
    return jnp.dot(a16, b16, preferred_element_type=F32)


def _gdn_prefill_kernel(q_ref, k_ref, v_ref, z_ref, sm_ref,
                        bias_ref, alog_ref, nw_ref,
                        o_ref, st_ref,
                        sst,
                        qd_s, kd_s, rhs_s, a_s, attn_s, x_s, d_s, u_s, w_s, glast_s):
    c_idx = pl.program_id(1)
    first = c_idx == 0
    cs = q_ref.shape[0]
    heads = range(GDN_HEADS)

    @pl.when(first)
    def _():
        sst[...] = jnp.zeros(sst.shape, F32)

    qc, kc, vc = q_ref, k_ref, v_ref

    _, sg, g_step = _gate_rows(sm_ref[...], bias_ref[...], alog_ref[...])
    causal = _causal_mask(cs)
    strict = _causal_mask(cs, strict=True)
    gc = _mm_rhs_split(jnp.where(causal, 1.0, 0.0), g_step)
    gc_row = gc.T
    glast_s[...] = gc[cs - 1:cs, :]

    for h in heads:
        lo = h * GDN_HEAD_K
        qh = qc[:, lo:lo + GDN_HEAD_K]
        kh = kc[:, lo:lo + GDN_HEAD_K]
        vh = vc[:, lo:lo + GDN_HEAD_V]
        qn = qh * lax.rsqrt(jnp.sum(qh * qh, axis=-1, keepdims=True) + L2_EPS) * (GDN_HEAD_K ** -0.5)
        kn = kh * lax.rsqrt(jnp.sum(kh * kh, axis=-1, keepdims=True) + L2_EPS)
        gi = _col(gc, LANE_G + h)
        bi = _col(sg, LANE_BETA + h)
        gj = gc_row[LANE_G + h:LANE_G + h + 1, :]
        decay = jnp.exp(jnp.where(causal, gi - gj, -jnp.inf))
        egi = jnp.exp(gi)
        kb = kn * bi
        kn16 = kn.astype(BF16)
        a_s[h] = jnp.where(strict, lax.dot_general(kb.astype(BF16), kn16, (((1,), (1,)), ((), ())),
                                                   preferred_element_type=F32) * decay, 0.0)
        attn_s[h] = (lax.dot_general(qn.astype(BF16), kn16, (((1,), (1,)), ((), ())),
                                     preferred_element_type=F32) * decay).astype(BF16)
        qd_s[:, lo:lo + GDN_HEAD_K] = (qn * egi).astype(BF16)
        kd_s[:, lo:lo + GDN_HEAD_K] = (kn * jnp.exp(gi[cs - 1:cs, :] - gi)).astype(BF16)
        rhs_s[:, 2 * lo:2 * lo + GDN_HEAD_V] = (vh * bi).astype(BF16)
        rhs_s[:, 2 * lo + GDN_HEAD_V:2 * lo + 2 * GDN_HEAD_V] = (kb * egi).astype(BF16)

    base = _same_block(cs, INV_BASE)
    eye = jnp.where(causal & jnp.logical_not(strict), 1.0, 0.0)
    for h in heads:
        x = jnp.where(base, -a_s[h], 0.0)
        x_s[h] = x.astype(BF16)
        d_s[h] = eye + x
    span = 2
    while span < INV_BASE:
        for h in heads:
            x16 = x_s[h]
            x2 = _bdot(x16, x16).astype(BF16)
            x_s[h] = x2
            d = d_s[h]
            d_s[h] = d + _bdot(d.astype(BF16), x2)
        span *= 2
    size = INV_BASE
    while size < cs:
        off = _same_block(cs, 2 * size) & jnp.logical_not(_same_block(cs, size))
        for h in heads:
            d = d_s[h]
            d16 = d.astype(BF16)
            low = jnp.where(off, a_s[h], 0.0).astype(BF16)
            d_s[h] = d - _bdot(_bdot(d16, low).astype(BF16), d16)
        size *= 2

    for h in heads:
        lo = h * GDN_HEAD_K
        uw = _bdot(d_s[h].astype(BF16), rhs_s[:, 2 * lo:2 * lo + 2 * GDN_HEAD_V])
        u_s[:, lo:lo + GDN_HEAD_V] = uw[:, :GDN_HEAD_V]
        w_s[:, lo:lo + GDN_HEAD_V] = uw[:, GDN_HEAD_V:].astype(BF16)

    for h in heads:
        lo = h * GDN_HEAD_K
        s = sst[lo:lo + GDN_HEAD_K, :]
        s16 = s.astype(BF16)
        v_new = u_s[:, lo:lo + GDN_HEAD_V] - _bdot(w_s[:, lo:lo + GDN_HEAD_V], s16)
        v16 = v_new.astype(BF16)
        o_h = _bdot(qd_s[:, lo:lo + GDN_HEAD_K], s16) + _bdot(attn_s[h], v16)
        g_last = _col(glast_s[...], LANE_G + h)
        sst[lo:lo + GDN_HEAD_K, :] = s * jnp.exp(g_last) + lax.dot_general(
            kd_s[:, lo:lo + GDN_HEAD_K], v16, (((0,), (0,)), ((), ())), preferred_element_type=F32)
        ms = jnp.mean(o_h * o_h, axis=-1, keepdims=True)
        o_ref[:, lo:lo + GDN_HEAD_V] = (o_h * lax.rsqrt(ms + RMS_EPS) * nw_ref[...]) * \
            _silu(z_ref[:, lo:lo + GDN_HEAD_V])

    @pl.when(c_idx == pl.num_programs(1) - 1)
    def _():
        st_ref[...] = sst[...]


def _gdn_prefill(proj, bsz, seq, lw):
    cs = GDN_CHUNK
    assert seq % cs == 0
    nc = seq // cs
    row = lambda b, c: b * nc + c
    pcol = lambda width, off: pl.BlockSpec((cs, width), lambda b, c: (row(b, c), off // width))
    full = lambda a: pl.BlockSpec(a.shape, lambda b, c: (0,) * a.ndim)
    weights = [lw['gate_bias'], lw['gate_alog'], lw['gdn_norm_w']]
    o, st = pl.pallas_call(
        _gdn_prefill_kernel,
        grid=(bsz, nc),
        in_specs=[pcol(GDN_DIM, COL_Q), pcol(GDN_DIM, COL_K), pcol(GDN_DIM, COL_V), pcol(GDN_DIM, COL_Z_GDN),
                  pcol(LANES, COL_SMALL)] + [full(w) for w in weights],
        out_specs=[pl.BlockSpec((cs, GDN_DIM), lambda b, c: (row(b, c), 0)),
                   pl.BlockSpec((None, GDN_HEADS * GDN_HEAD_K, GDN_HEAD_V), lambda b, c: (b, 0, 0))],
        out_shape=[jax.ShapeDtypeStruct((bsz * seq, GDN_DIM), F32),
                   jax.ShapeDtypeStruct((bsz, GDN_HEADS * GDN_HEAD_K, GDN_HEAD_V), F32)],
        scratch_shapes=[pltpu.VMEM((GDN_HEADS * GDN_HEAD_K, GDN_HEAD_V), F32),
                        pltpu.VMEM((cs, GDN_DIM), BF16),
                        pltpu.VMEM((cs, GDN_DIM), BF16),
                        pltpu.VMEM((cs, 2 * GDN_DIM), BF16),
                        pltpu.VMEM((GDN_HEADS, cs, cs), F32),
                        pltpu.VMEM((GDN_HEADS, cs, cs), BF16),
                        pltpu.VMEM((GDN_HEADS, cs, cs), BF16),
                        pltpu.VMEM((GDN_HEADS, cs, cs), F32),
                        pltpu.VMEM((cs, GDN_DIM), F32),
                        pltpu.VMEM((cs, GDN_DIM), BF16),
                        pltpu.VMEM((1, LANES), F32)],
        compiler_params=_params("parallel", "arbitrary"),
    )(proj, proj, proj, proj, proj, *weights)
    return o, st


def _softmax_rows(s):
    m = jnp.max(s, axis=-1, keepdims=True)
    e = jnp.exp(s - m)
    return e / jnp.sum(e, axis=-1, keepdims=True)


def _xattn_prefill_kernel(x_ref, nw_ref, wq_ref, mk_ref, mv_ref, wo_ref, o_ref):
    x = x_ref[...]
    ms = jnp.mean(x * x, axis=-1, keepdims=True)
    h = x * lax.rsqrt(ms + RMS_EPS) * nw_ref[...]
    q = jnp.dot(h.astype(BF16), wq_ref[...], preferred_element_type=F32)
    outs = []
    for hd in range(XA_HEADS):
        lo = hd * XA_HEAD_DIM
        s = _mm_nt(q[:, lo:lo + XA_HEAD_DIM], mk_ref[:, lo:lo + XA_HEAD_DIM]) * (XA_HEAD_DIM ** -0.5)
        outs.append(_mm(_softmax_rows(s), mv_ref[:, lo:lo + XA_HEAD_DIM]))
    att = jnp.concatenate(outs, axis=-1)
    o_ref[...] = x + jnp.dot(att.astype(BF16), wo_ref[...], preferred_element_type=F32)


def _xattn_prefill(x, nw, wq, mk, mv, wo, bsz, seq, tq):
    n_mem = mk.shape[1]
    nt = seq // tq
    return pl.pallas_call(
        _xattn_prefill_kernel,
        grid=(bsz, nt),
        in_specs=[pl.BlockSpec((tq, D_MODEL), lambda b, t: (b * nt + t, 0)),
                  pl.BlockSpec((1, D_MODEL), lambda b, t: (0, 0)),
                  pl.BlockSpec((D_MODEL, D_MODEL), lambda b, t: (0, 0)),
                  pl.BlockSpec((None, n_mem, D_MODEL), lambda b, t: (b, 0, 0)),
                  pl.BlockSpec((None, n_mem, D_MODEL), lambda b, t: (b, 0, 0)),
                  pl.BlockSpec((D_MODEL, D_MODEL), lambda b, t: (0, 0))],
        out_specs=pl.BlockSpec((tq, D_MODEL), lambda b, t: (b * nt + t, 0)),
        out_shape=jax.ShapeDtypeStruct(x.shape, F32),
        compiler_params=_params("parallel", "arbitrary"),
    )(x, nw.reshape(1, D_MODEL), wq, mk, mv, wo)


def _xattn_decode_kernel(q_ref, mk_ref, mv_ref, o_ref):
    tb = q_ref.shape[0]
    for i in range(tb):
        q = q_ref[i]
        qh = jnp.concatenate([q[:, hd * XA_HEAD_DIM:(hd + 1) * XA_HEAD_DIM] for hd in range(XA_HEADS)], axis=0)
        s = jnp.sum(mk_ref[i] * qh[None], axis=-1, keepdims=True) * (XA_HEAD_DIM ** -0.5)
        e = jnp.exp(s - jnp.max(s, axis=0, keepdims=True))
        p = e / jnp.sum(e, axis=0, keepdims=True)
        o_ref[i] = jnp.sum(p * mv_ref[i], axis=0)


def _xattn_decode(q, mk_all, mv_all, layer, tb):
    _, nb, n_mem, nh, hd = mk_all.shape
    return pl.pallas_call(
        _xattn_decode_kernel,
        grid=(nb // tb,),
        in_specs=[pl.BlockSpec((tb, 1, D_MODEL), lambda i: (i, 0, 0)),
                  pl.BlockSpec((None, tb, n_mem, nh, hd), lambda i: (layer, i, 0, 0, 0)),
                  pl.BlockSpec((None, tb, n_mem, nh, hd), lambda i: (layer, i, 0, 0, 0))],
        out_specs=pl.BlockSpec((tb, nh, hd), lambda i: (i, 0, 0)),
        out_shape=jax.ShapeDtypeStruct((nb, nh, hd), F32),
        compiler_params=_params("parallel"),
    )(q.reshape(nb, 1, D_MODEL), mk_all, mv_all).reshape(nb, D_MODEL)


def _ffn_prefill_kernel(x_ref, g_ref, u_ref, wd_ref, o_ref):
    hmid = g_ref[...] * u_ref[...]
    o_ref[...] = x_ref[...] + jnp.dot(hmid.astype(BF16), wd_ref[...], preferred_element_type=F32)


def _ffn_prefill(x, gu, wd, tm):
    m = x.shape[0]
    d_ff = wd.shape[0]
    assert m % tm == 0
    return pl.pallas_call(
        _ffn_prefill_kernel,
        grid=(m // tm,),
        in_specs=[pl.BlockSpec((tm, D_MODEL), lambda i: (i, 0)),
                  pl.BlockSpec((tm, d_ff), lambda i: (i, 0)),
                  pl.BlockSpec((tm, d_ff), lambda i: (i, 1)),
                  pl.BlockSpec((d_ff, D_MODEL), lambda i: (0, 0))],
        out_specs=pl.BlockSpec((tm, D_MODEL), lambda i: (i, 0)),
        out_shape=jax.ShapeDtypeStruct((m, D_MODEL), F32),
        compiler_params=_params("parallel"),
    )(x, gu, gu, wd)


def _ffn_decode_kernel(x_ref, g_ref, h0_ref, h1_ref, u_ref, cw_ref, cb_ref, wd_ref, o_ref):
    acc = cb_ref[...] + h0_ref[...] * cw_ref[0:1, :] + h1_ref[...] * cw_ref[1:2, :] + g_ref[...] * cw_ref[2:3, :]
    hmid = _silu(acc) * u_ref[...]
    o_ref[...] = x_ref[...] + jnp.dot(hmid.astype(BF16), wd_ref[...], preferred_element_type=F32)


def _ffn_decode(x, gu, hist0, hist1, cw, cb, wd):
    m = x.shape[0]
    d_ff = wd.shape[0]
    rows = lambda width, col: pl.BlockSpec((m, width), lambda i: (0, col))
    return pl.pallas_call(
        _ffn_decode_kernel,
        grid=(1,),
        in_specs=[rows(D_MODEL, 0), rows(d_ff, 0), rows(d_ff, 0), rows(d_ff, 0), rows(d_ff, 1),
                  pl.BlockSpec((FFN_CONV, d_ff), lambda i: (0, 0)),
                  pl.BlockSpec((1, d_ff), lambda i: (0, 0)),
                  pl.BlockSpec((d_ff, D_MODEL), lambda i: (0, 0))],
        out_specs=rows(D_MODEL, 0),
        out_shape=jax.ShapeDtypeStruct((m, D_MODEL), F32),
        compiler_params=_params("arbitrary"),
    )(x, gu, hist0, hist1, gu, cw, cb.reshape(1, d_ff), wd)


def _expand_lanes(rows, first_lane, width, n_out, parts):
    k = lax.broadcasted_iota(jnp.int32, (LANES, n_out), 0)
    l = lax.broadcasted_iota(jnp.int32, (LANES, n_out), 1)
    sel = jnp.where(l // width == k - first_lane, 1.0, 0.0)
    return _mm_lhs_split(rows, sel, parts)


def _decode_rows_kernel(proj_ref, sh0, sh1, sh2, gh0, gh1, gh2,
                        scw_ref, scb_ref, gcw_ref, bias_ref, alog_ref,
                        xs_ref, xdt_ref, b_ref, c_ref, sdec_ref,
                        q_ref, k_ref, v_ref, beta_ref, gdec_ref):
    def conv(hists, new, cw, cb):
        acc = new * cw[3:4, :]
        for t, hr in enumerate(hists):
            acc = acc + hr[...] * cw[t:t + 1, :]
        return acc if cb is None else acc + cb

    xbc_new = jnp.concatenate([proj_ref[:, COL_XS:COL_XS + SSD_INNER],
                               proj_ref[:, COL_B:COL_B + SSD_BC],
                               proj_ref[:, COL_C:COL_C + SSD_BC]], axis=-1)
    xbc = _silu(conv((sh0, sh1, sh2), xbc_new, scw_ref, scb_ref[...]))
    sp, sg, log_dec = _gate_rows(proj_ref[:, COL_SMALL:COL_SMALL + LANES], bias_ref[...], alog_ref[...])
    xs = xbc[:, 0:SSD_INNER]
    xs_ref[...] = xs
    xdt_ref[...] = xs * _expand_lanes(sp, LANE_DT, SSD_HEAD_DIM, SSD_INNER, 3)
    b_ref[...] = xbc[:, SSD_INNER:SSD_INNER + SSD_BC]
    c_ref[...] = xbc[:, SSD_INNER + SSD_BC:SSD_INNER + 2 * SSD_BC]
    sdec_ref[...] = jnp.exp(_expand_lanes(log_dec, LANE_DT, SSD_HEAD_DIM, SSD_INNER, 3))

    qkv_new = proj_ref[:, COL_Q:COL_Q + 3 * GDN_DIM]
    qkv = _silu(conv((gh0, gh1, gh2), qkv_new, gcw_ref, None))
    for h in range(GDN_HEADS):
        lo = h * GDN_HEAD_K
        qh = qkv[:, lo:lo + GDN_HEAD_K]
        kh = qkv[:, GDN_DIM + lo:GDN_DIM + lo + GDN_HEAD_K]
        q_ref[:, lo:lo + GDN_HEAD_K] = qh * lax.rsqrt(jnp.sum(qh * qh, axis=-1, keepdims=True) + L2_EPS) * \
            (GDN_HEAD_K ** -0.5)
        k_ref[:, lo:lo + GDN_HEAD_K] = kh * lax.rsqrt(jnp.sum(kh * kh, axis=-1, keepdims=True) + L2_EPS)
    v_ref[...] = qkv[:, 2 * GDN_DIM:3 * GDN_DIM]
    beta_ref[...] = _expand_lanes(sg, LANE_BETA, GDN_HEAD_V, GDN_DIM, 3)
    gdec_ref[...] = jnp.exp(_expand_lanes(log_dec, LANE_G, GDN_HEAD_V, GDN_DIM, 3))


def _decode_rows(proj, ssd_hist, gdn_hist, lw):
    m = proj.shape[0]
    ins = [proj] + [ssd_hist[:, t] for t in range(SSD_CONV - 1)] + [gdn_hist[:, t] for t in range(GDN_CONV - 1)]
    ins += [lw['ssd_conv_w'], lw['ssd_conv_b'], lw['gdn_conv_w'], lw['gate_bias'], lw['gate_alog']]
    widths = [SSD_INNER, SSD_INNER, SSD_BC, SSD_BC, SSD_INNER, GDN_DIM, GDN_DIM, GDN_DIM, GDN_DIM, GDN_DIM]
    return pl.pallas_call(
        _decode_rows_kernel,
        grid=(1,),
        in_specs=[pl.BlockSpec(a.shape, lambda i, nd=a.ndim: (0,) * nd) for a in ins],
        out_specs=[pl.BlockSpec((m, w), lambda i: (0, 0)) for w in widths],
        out_shape=[jax.ShapeDtypeStruct((m, w), F32) for w in widths],
        compiler_params=_params("arbitrary"),
    )(*ins)


def _rows_to_cols(rows, i, parts):
    tb = rows.shape[0]
    r = lax.broadcasted_iota(jnp.int32, (tb, LANES), 0)
    sel = jnp.where(r == i, 1.0, 0.0).astype(BF16)
    acc = None
    for p in _split(rows, parts):
        t = lax.dot_general(p, sel, (((0,), (0,)), ((), ())), preferred_element_type=F32)
        acc = t if acc is None else acc + t
    return acc


def _only_row(rows, i):
    r = lax.broadcasted_iota(jnp.int32, rows.shape, 0)
    return jnp.where(r == i, rows, 0.0)


def _decode_state_kernel(xdt_ref, b_ref, c_ref, sdec_ref, q_ref, k_ref, v_ref, beta_ref, gdec_ref,
                         hs_ref, ss_ref, *rest):
    y_ref, o_ref, hs_out, ss_out = rest[-4:]
    tb = xdt_ref.shape[0]
    xdt = xdt_ref[...]
    bm = b_ref[...]
    cm = c_ref[...]
    sdec = sdec_ref[...]
    qn = q_ref[...]
    kn = k_ref[...]
    vv = v_ref[...]
    beta = beta_ref[...]
    gdec = gdec_ref[...]
    rows_g = SSD_INNER // SSD_GROUPS

    y_acc = jnp.zeros((tb, SSD_INNER), F32)
    ks_acc = jnp.zeros((tb, GDN_DIM), F32)
    for i in range(tb):
        dec_c = _rows_to_cols(sdec, i, 2)
        xdt_c = _rows_to_cols(xdt, i, 1)
        pieces = []
        for g in range(SSD_GROUPS):
            rs = slice(g * rows_g, (g + 1) * rows_g)
            bg = bm[i:i + 1, g * SSD_STATE:(g + 1) * SSD_STATE]
            h_new = hs_ref[i, rs, :] * dec_c[rs, :] + xdt_c[rs, :] * bg
            hs_out[i, rs, :] = h_new
            cg = _only_row(cm[:, g * SSD_STATE:(g + 1) * SSD_STATE], i)
            pieces.append(_mm_nt(cg, h_new))
        y_acc = y_acc + jnp.concatenate(pieces, axis=-1)
        g_c = _rows_to_cols(gdec, i, 2)
        s_dec = ss_ref[i] * g_c
        ss_out[i] = s_dec
        pieces = []
        for h in range(GDN_HEADS):
            rs = slice(h * GDN_HEAD_K, (h + 1) * GDN_HEAD_K)
            pieces.append(_mm(_only_row(kn[:, rs], i), s_dec[rs, :]))
        ks_acc = ks_acc + jnp.concatenate(pieces, axis=-1)
    y_ref[...] = y_acc

    delta = beta * (vv - ks_acc)
    o_acc = jnp.zeros((tb, GDN_DIM), F32)
    for i in range(tb):
        k_c = _rows_to_cols(kn, i, 1)
        pieces = []
        for h in range(GDN_HEADS):
            rs = slice(h * GDN_HEAD_K, (h + 1) * GDN_HEAD_K)
            d_row = delta[i:i + 1, rs]
            s_new = ss_out[i, rs, :] + k_c[rs, :] * d_row
            ss_out[i, rs, :] = s_new
            pieces.append(_mm(_only_row(qn[:, rs], i), s_new))
        o_acc = o_acc + jnp.concatenate(pieces, axis=-1)
    o_ref[...] = o_acc


def _decode_state(rows, hs_all, ss_all, layer, tb, carried):
    xs, xdt, bm, cm, sdec, qn, kn, vv, beta, gdec = rows
    nb = xdt.shape[0]
    row_ins = [xdt, bm, cm, sdec, qn, kn, vv, beta, gdec]
    rspec = lambda a: pl.BlockSpec((tb, a.shape[1]), lambda i: (i, 0))
    sspec = lambda a: pl.BlockSpec((None, tb) + a.shape[2:], lambda i: (layer, i, 0, 0))
    n_in = len(row_ins) + 2
    aliases = {n_in + j: 2 + j for j in range(len(carried))}
    return pl.pallas_call(
        _decode_state_kernel,
        grid=(nb // tb,),
        in_specs=[rspec(a) for a in row_ins] + [sspec(hs_all), sspec(ss_all)] +
                 [pl.BlockSpec(memory_space=pl.ANY) for _ in carried],
        out_specs=[rspec(xdt), rspec(qn), sspec(hs_all), sspec(ss_all)],
        out_shape=[jax.ShapeDtypeStruct(xdt.shape, F32), jax.ShapeDtypeStruct(qn.shape, F32),
                   jax.ShapeDtypeStruct(hs_all.shape, F32), jax.ShapeDtypeStruct(ss_all.shape, F32)],
        input_output_aliases=aliases,
        compiler_params=_params("parallel"),
    )(*row_ins, hs_all, ss_all, *carried)


def _decode_out_kernel(x_ref, y_ref, xs_ref, o_ref, proj_ref, dexp_ref, snw_ref, gnw_ref, w1_ref, w2_ref, out_ref):
    y = y_ref[...] + dexp_ref[...] * xs_ref[...]
    gs = SSD_INNER // SSD_GROUPS
    ys = []
    for g in range(SSD_GROUPS):
        yg = y[:, g * gs:(g + 1) * gs] * _silu(proj_ref[:, COL_Z_SSD + g * gs:COL_Z_SSD + (g + 1) * gs])
        ms = jnp.mean(yg * yg, axis=-1, keepdims=True)
        ys.append(yg * lax.rsqrt(ms + RMS_EPS) * snw_ref[:, g * gs:(g + 1) * gs])
    os = []
    for h in range(GDN_HEADS):
        lo = h * GDN_HEAD_V
        oh = o_ref[:, lo:lo + GDN_HEAD_V]
        ms = jnp.mean(oh * oh, axis=-1, keepdims=True)
        os.append(oh * lax.rsqrt(ms + RMS_EPS) * gnw_ref[...] *
                  _silu(proj_ref[:, COL_Z_GDN + lo:COL_Z_GDN + lo + GDN_HEAD_V]))
    yn = jnp.concatenate(ys, axis=-1)
    on = jnp.concatenate(os, axis=-1)
    out_ref[...] = x_ref[...] + _mm(yn, w1_ref[...]) + _mm(on, w2_ref[...])


def _decode_out(x, y, xs, o, proj, lw):
    ins = [x, y, xs, o, proj, lw['ssd_d_exp'], lw['ssd_norm_w'], lw['gdn_norm_w'], lw['w_out_ssd'], lw['w_out_gdn']]
    return pl.pallas_call(
        _decode_out_kernel,
        grid=(1,),
        in_specs=[pl.BlockSpec(a.shape, lambda i, nd=a.ndim: (0,) * nd) for a in ins],
        out_specs=pl.BlockSpec(x.shape, lambda i: (0, 0)),
        out_shape=jax.ShapeDtypeStruct(x.shape, F32),
        compiler_params=_params("arbitrary"),
    )(*ins)


def _identity_taps(taps, n):
    return jnp.concatenate([jnp.zeros((taps - 1, n), F32), jnp.ones((1, n), F32)], axis=0)


def _prep_layer(i, p):
    w_in = p['w_in'][i]
    o_zs = 0
    o_xbc = o_zs + SSD_INNER
    o_dt = o_xbc + SSD_INNER + 2 * SSD_BC
    o_qkv = o_dt + SSD_HEADS
    o_zg = o_qkv + 3 * GDN_DIM
    o_b = o_zg + GDN_DIM
    o_a = o_b + GDN_HEADS
    used = SSD_HEADS + 2 * GDN_HEADS
    w_perm = jnp.concatenate([
        w_in[:, o_zs:o_zs + SSD_INNER],
        w_in[:, o_xbc:o_xbc + SSD_INNER],
        w_in[:, o_qkv:o_qkv + 3 * GDN_DIM],
        w_in[:, o_zg:o_zg + GDN_DIM],
        w_in[:, o_xbc + SSD_INNER:o_xbc + SSD_INNER + 2 * SSD_BC],
        w_in[:, o_dt:o_dt + SSD_HEADS],
        w_in[:, o_b:o_b + GDN_HEADS],
        w_in[:, o_a:o_a + GDN_HEADS],
        jnp.zeros((D_MODEL, PROJ_COLS - COL_SMALL - used), F32)], axis=1).astype(BF16)
    lane_pad = jnp.zeros((LANES - used,), F32)
    scw = p['ssd_conv_w'][i]
    scb = p['ssd_conv_b'][i].reshape(1, -1)
    gcw = p['gdn_conv_w'][i]
    return {
        'w_in': w_perm,
        'norm_mix_w': p['norm_mix_w'][i],
        'ssd_conv_w': scw, 'ssd_conv_b': scb, 'gdn_conv_w': gcw,
        'in_cw': jnp.concatenate([_identity_taps(SSD_CONV, SSD_INNER), scw[:, :SSD_INNER], gcw,
                                  _identity_taps(GDN_CONV, GDN_DIM), scw[:, SSD_INNER:],
                                  _identity_taps(SSD_CONV, PROJ_COLS - COL_SMALL)], axis=1),
        'in_cb': jnp.concatenate([jnp.zeros((1, COL_XS), F32), scb[:, :SSD_INNER],
                                  jnp.zeros((1, COL_B - COL_Q), F32), scb[:, SSD_INNER:],
                                  jnp.zeros((1, PROJ_COLS - COL_SMALL), F32)], axis=1),
        'in_act': jnp.concatenate([jnp.zeros((1, COL_XS), F32), jnp.ones((1, COL_Z_GDN - COL_XS), F32),
                                   jnp.zeros((1, COL_B - COL_Z_GDN), F32), jnp.ones((1, COL_SMALL - COL_B), F32),
                                   jnp.zeros((1, PROJ_COLS - COL_SMALL), F32)], axis=1),
        'gu_cw': jnp.concatenate([p['ffn_conv_w'][i], _identity_taps(FFN_CONV, p['ffn_conv_w'].shape[2])], axis=1),
        'gu_cb': jnp.concatenate([p['ffn_conv_b'][i], jnp.zeros_like(p['ffn_conv_b'][i])]).reshape(1, -1),
        'gu_act': jnp.concatenate([jnp.ones_like(p['ffn_conv_b'][i]),
                                   jnp.zeros_like(p['ffn_conv_b'][i])]).reshape(1, -1),
        'gate_bias': jnp.concatenate([p['ssd_dt_bias'][i], jnp.zeros((GDN_HEADS,), F32), p['gdn_dt_bias'][i],
                                      lane_pad]).reshape(1, LANES),
        'gate_alog': jnp.concatenate([p['ssd_a_log'][i], jnp.zeros((GDN_HEADS,), F32), p['gdn_a_log'][i],
                                      lane_pad]).reshape(1, LANES),
        'ssd_d_exp': jnp.repeat(p['ssd_d'][i], SSD_HEAD_DIM).reshape(1, SSD_INNER),
        'ssd_norm_w': p['ssd_norm_w'][i].reshape(1, SSD_INNER),
        'gdn_norm_w': p['gdn_norm_w'][i].reshape(1, GDN_HEAD_V),
        'w_out_ssd': p['w_out'][i, :SSD_INNER].astype(BF16),
        'w_out_gdn': p['w_out'][i, SSD_INNER:].astype(BF16),
        'norm_xa_w': p['norm_xa_w'][i], 'norm_mem_w': p['norm_mem_w'][i],
        'xa_wq': p['xa_wq'][i].astype(BF16), 'xa_wo': p['xa_wo'][i].astype(BF16),
        'xa_wkv': jnp.concatenate([p['xa_wk'][i], p['xa_wv'][i]], axis=1).astype(BF16),
        'norm_ffn_w': p['norm_ffn_w'][i],
        'ffn_w_gu': jnp.concatenate([p['ffn_w_gate'][i], p['ffn_w_up'][i]], axis=1).astype(BF16),
        'ffn_conv_w': p['ffn_conv_w'][i], 'ffn_conv_b': p['ffn_conv_b'][i],
        'ffn_w_down': p['ffn_w_down'][i].astype(BF16),
    }


def _unpermute_conv_rows(proj_rows):
    ssd = jnp.concatenate([proj_rows[..., COL_XS:COL_XS + SSD_INNER],
                           proj_rows[..., COL_B:COL_B + SSD_BC],
                           proj_rows[..., COL_C:COL_C + SSD_BC]], axis=-1)
    gdn = proj_rows[..., COL_Q:COL_Q + 3 * GDN_DIM]
    return ssd, gdn


def _tile(m, pref):
    t = min(m, pref)
    while m % t:
        t //= 2
    return t


def kernel(x_prompt, x_sample, mem_prompt, cache_mem_k, cache_mem_v, state_ssd_conv, state_ssd, state_gdn_conv, state_gdn, state_ffn_conv, norm_mix_w, w_in, ssd_conv_w, ssd_conv_b, ssd_dt_bias, ssd_a_log, ssd_d, ssd_norm_w, gdn_conv_w, gdn_dt_bias, gdn_a_log, gdn_norm_w, w_out, norm_xa_w, norm_mem_w, xa_wq, xa_wk, xa_wv, xa_wo, norm_ffn_w, ffn_w_gate, ffn_w_up, ffn_conv_w, ffn_conv_b, ffn_w_down, final_norm_w):
    params = dict(norm_mix_w=norm_mix_w, w_in=w_in, ssd_conv_w=ssd_conv_w, ssd_conv_b=ssd_conv_b,
                  ssd_dt_bias=ssd_dt_bias, ssd_a_log=ssd_a_log, ssd_d=ssd_d, ssd_norm_w=ssd_norm_w,
                  gdn_conv_w=gdn_conv_w, gdn_dt_bias=gdn_dt_bias, gdn_a_log=gdn_a_log, gdn_norm_w=gdn_norm_w,
                  w_out=w_out, norm_xa_w=norm_xa_w, norm_mem_w=norm_mem_w, xa_wq=xa_wq, xa_wk=xa_wk, xa_wv=xa_wv,
                  xa_wo=xa_wo, norm_ffn_w=norm_ffn_w, ffn_w_gate=ffn_w_gate, ffn_w_up=ffn_w_up,
                  ffn_conv_w=ffn_conv_w, ffn_conv_b=ffn_conv_b, ffn_w_down=ffn_w_down)
    depth = w_in.shape[0]
    bsz, seq, _ = x_prompt.shape
    nb = x_sample.shape[0]
    n_mem = mem_prompt.shape[1]
    d_ff = ffn_w_down.shape[1]
    mp = bsz * seq

    xp = x_prompt.reshape(mp, D_MODEL)
    xs = x_sample.reshape(nb, D_MODEL)
    mem = mem_prompt.reshape(bsz * n_mem, D_MODEL)
    tm_p = _tile(seq, 1024)
    tm_mem = _tile(bsz * n_mem, 1024)
    dec_tile = _tile(nb, DEC_TILE)

    hs_all = state_ssd.reshape(depth, nb, SSD_INNER, SSD_STATE)
    ss_all = state_gdn.reshape(depth, nb, GDN_HEADS * GDN_HEAD_K, GDN_HEAD_V)
    mk_all, mv_all = cache_mem_k, cache_mem_v
    new_states = ()

    mkp, mvp = [], []
    p_sc, p_sh, p_gc, p_gs, p_fc = [], [], [], [], []
    s_sc, s_gc, s_fc = [], [], []
    for i in range(depth):
        lw = _prep_layer(i, params)

        kv = _norm_mm(mem, lw['norm_mem_w'], lw['xa_wkv'], tm_mem)
        mk = kv[:, :D_MODEL].reshape(bsz, n_mem, D_MODEL)
        mv = kv[:, D_MODEL:].reshape(bsz, n_mem, D_MODEL)
        mkp.append(mk.reshape(bsz, n_mem, XA_HEADS, XA_HEAD_DIM))
        mvp.append(mv.reshape(bsz, n_mem, XA_HEADS, XA_HEAD_DIM))

        proj, tails = _norm_mm_conv(xp, lw['norm_mix_w'], lw['w_in'], lw['in_cw'], lw['in_cb'], lw['in_act'],
                                    [(COL_XS, COL_Z_GDN), (COL_B, COL_SMALL)], seq, tm_p)
        y, h_new = _ssd_prefill(proj, bsz, seq, lw)
        o, s_new = _gdn_prefill(proj, bsz, seq, lw)
        seq_tails = tails.reshape(bsz, seq // tm_p, HIST_ROWS, PROJ_COLS)[:, -1]
        ssd_tail, gdn_tail = _unpermute_conv_rows(seq_tails[:, HIST_ROWS - (SSD_CONV - 1):])
        p_sc.append(ssd_tail)
        p_gc.append(gdn_tail)
        p_sh.append(h_new.reshape(bsz, SSD_HEADS, SSD_HEAD_DIM, SSD_STATE))
        p_gs.append(s_new.reshape(bsz, GDN_HEADS, GDN_HEAD_K, GDN_HEAD_V))
        xp = _mm_res(xp, [y, o], [lw['w_out_ssd'], lw['w_out_gdn']], _tile(mp, 512))
        xp = _xattn_prefill(xp, lw['norm_xa_w'], lw['xa_wq'], mk, mv, lw['xa_wo'], bsz, seq, _tile(seq, 512))
        gu, gu_tails = _norm_mm_conv(xp, lw['norm_ffn_w'], lw['ffn_w_gu'], lw['gu_cw'], lw['gu_cb'], lw['gu_act'],
                                     [(0, d_ff)], seq, tm_p)
        p_fc.append(gu_tails.reshape(bsz, seq // tm_p, HIST_ROWS, 2 * d_ff)[:, -1, HIST_ROWS - (FFN_CONV - 1):, :d_ff])
        xp = _ffn_prefill(xp, gu, lw['ffn_w_down'], _tile(mp, 512))

        proj_s = _norm_mm(xs, lw['norm_mix_w'], lw['w_in'], nb)
        ssd_new, gdn_new = _unpermute_conv_rows(proj_s)
        s_sc.append(jnp.concatenate([state_ssd_conv[i][:, 1:], ssd_new[:, None]], axis=1))
        s_gc.append(jnp.concatenate([state_gdn_conv[i][:, 1:], gdn_new[:, None]], axis=1))
        rows = _decode_rows(proj_s, state_ssd_conv[i], state_gdn_conv[i], lw)
        y_s, o_s, hs_new, ss_new = _decode_state(rows, hs_all, ss_all, i, dec_tile, new_states)
        new_states = (hs_new, ss_new)
        xs = _decode_out(xs, y_s, rows[0], o_s, proj_s, lw)
        q_s = _norm_mm(xs, lw['norm_xa_w'], lw['xa_wq'], nb)
        att = _xattn_decode(q_s, mk_all, mv_all, i, _tile(nb, 4))
        xs = _mm_res(xs, [att], [lw['xa_wo']], nb)
        gu_s = _norm_mm(xs, lw['norm_ffn_w'], lw['ffn_w_gu'], nb)
        s_fc.append(jnp.concatenate([state_ffn_conv[i][:, 1:], gu_s[:, None, :d_ff]], axis=1))
        xs = _ffn_decode(xs, gu_s, state_ffn_conv[i][:, 0], state_ffn_conv[i][:, 1],
                         lw['ffn_conv_w'], lw['ffn_conv_b'], lw['ffn_w_down'])

    y_prompt = _rmsnorm(xp, final_norm_w, tm_p).reshape(bsz, seq, D_MODEL)
    y_sample = _rmsnorm(xs, final_norm_w, nb).reshape(nb, 1, D_MODEL)
    return (y_prompt, y_sample, jnp.stack(mkp), jnp.stack(mvp),
            jnp.stack(p_sc), jnp.stack(p_sh), jnp.stack(p_gc), jnp.stack(p_gs), jnp.stack(p_fc),
            jnp.stack(s_sc), new_states[0].reshape(state_ssd.shape), jnp.stack(s_gc),
            new_states[1].reshape(state_gdn.shape), jnp.stack(s_fc))
```

```python
import functools

import jax
import jax.numpy as jnp
from jax import lax
from jax.experimental import pallas as pl
from jax.experimental.pallas import tpu as pltpu

F32 = jnp.float32
BF16 = jnp.bfloat16

D_MODEL = 1024
SSD_HEADS = 16
SSD_HEAD_DIM = 64
SSD_STATE = 128
SSD_GROUPS = 2
SSD_INNER = SSD_HEADS * SSD_HEAD_DIM
SSD_BC = SSD_GROUPS * SSD_STATE
SSD_CONV = 4
GDN_HEADS = 8
GDN_HEAD_K = 128
GDN_HEAD_V = 128
GDN_DIM = GDN_HEADS * GDN_HEAD_V
GDN_CONV = 4
XA_HEADS = 4
XA_HEAD_DIM = D_MODEL // XA_HEADS
FFN_CONV = 3
RMS_EPS = 1e-6
L2_EPS = 1e-6

SUBLANES = 8
LANES = 128
MXU_COLS = 256
VMEM_LIMIT_BYTES = 56 * 1024 * 1024

COL_Z_SSD = 0
COL_XS = 1024
COL_Q = 2048
COL_K = 3072
COL_V = 4096
COL_Z_GDN = 5120
COL_B = 6144
COL_C = 6400
COL_SMALL = 6656
PROJ_COLS = 7168
LANE_DT = 0
LANE_BETA = 16
LANE_G = 24

SSD_CHUNK = 256
GDN_CHUNK = 128
DEC_TILE = 8


def _params(*sem):
    return pltpu.CompilerParams(dimension_semantics=sem, vmem_limit_bytes=VMEM_LIMIT_BYTES)


def _mm(a, b):
    return jnp.dot(a.astype(BF16), b.astype(BF16), preferred_element_type=F32)


def _mm_nt(a, b):
    return lax.dot_general(a.astype(BF16), b.astype(BF16), (((1,), (1,)), ((), ())),
                           preferred_element_type=F32)


def _split(x, parts):
    out = []
    r = x
    for _ in range(parts - 1):
        p = r.astype(BF16)
        out.append(p)
        r = r - p.astype(F32)
    out.append(r.astype(BF16))
    return out


def _mm_rhs_split(a_exact, b, parts=3):
    a16 = a_exact.astype(BF16)
    acc = None
    for p in _split(b, parts):
        t = jnp.dot(a16, p, preferred_element_type=F32)
        acc = t if acc is None else acc + t
    return acc


def _mm_lhs_split(a, b_exact, parts=3):
    b16 = b_exact.astype(BF16)
    acc = None
    for p in _split(a, parts):
        t = jnp.dot(p, b16, preferred_element_type=F32)
        acc = t if acc is None else acc + t
    return acc


def _silu(x):
    return x / (1.0 + jnp.exp(-x))


def _sigmoid(x):
    return 1.0 / (1.0 + jnp.exp(-x))


def _softplus(x):
    return jnp.maximum(x, 0.0) + jnp.log1p(jnp.exp(-jnp.abs(x)))


def _col(arr, lane):
    idx = lax.broadcasted_iota(jnp.int32, arr.shape, 1)
    return jnp.sum(jnp.where(idx == lane, arr, 0.0), axis=1, keepdims=True)


def _gate_rows(sm, bias, alog):
    sp = _softplus(sm + bias)
    sg = _sigmoid(sm)
    lane = lax.broadcasted_iota(jnp.int32, alog.shape, 1)
    has_decay = (lane < LANE_BETA) | ((lane >= LANE_G) & (lane < LANE_G + GDN_HEADS))
    neg_a = jnp.where(has_decay, -jnp.exp(alog), 0.0)
    return sp, sg, sp * neg_a


def _norm_mm_kernel(x_ref, nw_ref, w_ref, o_ref, xn_ref):
    @pl.when(pl.program_id(1) == 0)
    def _():
        x = x_ref[...]
        ms = jnp.mean(x * x, axis=-1, keepdims=True)
        xn_ref[...] = (x * lax.rsqrt(ms + RMS_EPS) * nw_ref[...]).astype(BF16)

    o_ref[...] = jnp.dot(xn_ref[...], w_ref[...], preferred_element_type=F32)


MAX_COL_TILE_LANES = 11 * LANES


def _col_tile(n):
    return max(t for t in range(LANES, MAX_COL_TILE_LANES + 1, LANES) if n % t == 0)


def _norm_mm(x, nw, w, tm):
    m, k = x.shape
    n = w.shape[1]
    tn = _col_tile(n)
    assert m % tm == 0
    return pl.pallas_call(
        _norm_mm_kernel,
        grid=(m // tm, n // tn),
        in_specs=[pl.BlockSpec((tm, k), lambda i, j: (i, 0)),
                  pl.BlockSpec((1, k), lambda i, j: (0, 0)),
                  pl.BlockSpec((k, tn), lambda i, j: (0, j))],
        out_specs=pl.BlockSpec((tm, tn), lambda i, j: (i, j)),
        out_shape=jax.ShapeDtypeStruct((m, n), F32),
        scratch_shapes=[pltpu.VMEM((tm, k), BF16)],
        compiler_params=_params("parallel", "arbitrary"),
    )(x, nw.reshape(1, k), w)


HIST_ROWS = SUBLANES


CONV_BUFS = 4


def _norm_mm_conv_kernel(x_ref, nw_ref, w_ref, cw_ref, cb_ref, o_ref, tail_ref, ext_ref, hist_ref,
                         *, taps, tiles_per_seq, conv_ranges):
    i = pl.program_id(0)
    tm = x_ref.shape[0]
    n = w_ref.shape[1]

    @pl.when(i == 0)
    def _():
        hist_ref[...] = jnp.zeros(hist_ref.shape, F32)

    x = x_ref[...]
    ms = jnp.mean(x * x, axis=-1, keepdims=True)
    xn = (x * lax.rsqrt(ms + RMS_EPS) * nw_ref[...]).astype(BF16)
    starts_seq = i % tiles_per_seq == 0
    for ci, lo in enumerate(range(0, n, MXU_COLS)):
        cols = slice(lo, min(lo + MXU_COLS, n))
        acc = jnp.dot(xn, w_ref[:, cols], preferred_element_type=F32)
        tail_ref[:, cols] = acc[tm - HIST_ROWS:tm, :]
        if not any(a <= cols.start and cols.stop <= b for a, b in conv_ranges):
            o_ref[:, cols] = acc
            continue
        buf = ext_ref.at[ci % CONV_BUFS]
        width = cols.stop - cols.start
        buf[0:HIST_ROWS, 0:width] = jnp.where(starts_seq, 0.0, hist_ref[:, cols])
        buf[HIST_ROWS:HIST_ROWS + tm, 0:width] = acc
        hist_ref[:, cols] = acc[tm - HIST_ROWS:tm, :]
        c = cb_ref[:, cols]
        for k in range(taps):
            start = HIST_ROWS - (taps - 1) + k
            c = c + buf[start:start + tm, 0:width] * cw_ref[k:k + 1, cols]
        o_ref[:, cols] = _silu(c)


def _norm_mm_conv(x, nw, w, cw, cb, conv_ranges, seq, tm):
    m, k = x.shape
    n = w.shape[1]
    taps = cw.shape[0]
    assert m % tm == 0 and seq % tm == 0 and tm >= HIST_ROWS
    assert all(a % MXU_COLS == 0 and b % MXU_COLS == 0 for a, b in conv_ranges)
    whole = lambda a: pl.BlockSpec(a.shape, lambda i: (0, 0), pipeline_mode=pl.Buffered(1))
    nw2 = nw.reshape(1, k)
    return pl.pallas_call(
        functools.partial(_norm_mm_conv_kernel, taps=taps, tiles_per_seq=seq // tm,
                          conv_ranges=tuple(conv_ranges)),
        grid=(m // tm,),
        in_specs=[pl.BlockSpec((tm, k), lambda i: (i, 0)), whole(nw2), whole(w), whole(cw), whole(cb)],
        out_specs=[pl.BlockSpec((tm, n), lambda i: (i, 0)),
                   pl.BlockSpec((None, HIST_ROWS, n), lambda i: (i, 0, 0))],
        out_shape=[jax.ShapeDtypeStruct((m, n), F32),
                   jax.ShapeDtypeStruct((m // tm, HIST_ROWS, n), F32)],
        scratch_shapes=[pltpu.VMEM((CONV_BUFS, HIST_ROWS + tm, MXU_COLS), F32),
                        pltpu.VMEM((HIST_ROWS, n), F32)],
        compiler_params=_params("arbitrary"),
    )(x, nw2, w, cw, cb)


def _mm_res_kernel(*refs, n_in):
    res_ref = refs[0]
    a_refs = refs[1:1 + n_in]
    w_refs = refs[1 + n_in:1 + 2 * n_in]
    o_ref = refs[1 + 2 * n_in]
    acc = res_ref[...]
    for a_ref, w_ref in zip(a_refs, w_refs):
        acc = acc + jnp.dot(a_ref[...].astype(BF16), w_ref[...], preferred_element_type=F32)
    o_ref[...] = acc


def _mm_res(res, a_list, w_list, tm):
    m, n = res.shape
    assert m % tm == 0
    n_in = len(a_list)
    in_specs = [pl.BlockSpec((tm, n), lambda i: (i, 0))]
    in_specs += [pl.BlockSpec((tm, a.shape[1]), lambda i: (i, 0)) for a in a_list]
    in_specs += [pl.BlockSpec(w.shape, lambda i: (0, 0)) for w in w_list]
    return pl.pallas_call(
        functools.partial(_mm_res_kernel, n_in=n_in),
        grid=(m // tm,),
        in_specs=in_specs,
        out_specs=pl.BlockSpec((tm, n), lambda i: (i, 0)),
        out_shape=jax.ShapeDtypeStruct((m, n), F32),
        compiler_params=_params("parallel"),
    )(res, *a_list, *w_list)


def _rmsnorm_kernel(x_ref, nw_ref, o_ref):
    x = x_ref[...]
    ms = jnp.mean(x * x, axis=-1, keepdims=True)
    o_ref[...] = x * lax.rsqrt(ms + RMS_EPS) * nw_ref[...]


def _rmsnorm(x, nw, tm):
    m, k = x.shape
    return pl.pallas_call(
        _rmsnorm_kernel,
        grid=(m // tm,),
        in_specs=[pl.BlockSpec((tm, k), lambda i: (i, 0)), pl.BlockSpec((1, k), lambda i: (0, 0))],
        out_specs=pl.BlockSpec((tm, k), lambda i: (i, 0)),
        out_shape=jax.ShapeDtypeStruct((m, k), F32),
        compiler_params=_params("parallel"),
    )(x, nw.reshape(1, k))


def _causal_mask(cs, strict=False):
    r = lax.broadcasted_iota(jnp.int32, (cs, cs), 0)
    c = lax.broadcasted_iota(jnp.int32, (cs, cs), 1)
    return (r > c) if strict else (r >= c)


def _ssd_prefill_kernel(z_ref, xs_ref, b_ref, c_ref, sm_ref,
                        bias_ref, alog_ref, dexp_ref, nw_ref,
                        y_ref, st_ref,
                        hst):
    c_idx = pl.program_id(1)
    first = c_idx == 0
    cs = xs_ref.shape[0]

    @pl.when(first)
    def _():
        hst[...] = jnp.zeros(hst.shape, F32)

    xs = xs_ref[...]
    bm = b_ref[...]
    cm = c_ref[...]

    sp, _, dta = _gate_rows(sm_ref[...], bias_ref[...], alog_ref[...])
    causal = _causal_mask(cs)
    cum = _mm_rhs_split(jnp.where(causal, 1.0, 0.0), dta)
    cum_row = cum.T

    lane_lo = lax.broadcasted_iota(jnp.int32, (1, LANES), 1) < SSD_HEAD_DIM
    row_lo = lax.broadcasted_iota(jnp.int32, (LANES, 1), 0) < SSD_HEAD_DIM
    heads_per_group = SSD_HEADS // SSD_GROUPS
    for g in range(SSD_GROUPS):
        cg = cm[:, g * SSD_STATE:(g + 1) * SSD_STATE]
        bg = bm[:, g * SSD_STATE:(g + 1) * SSD_STATE]
        cb = _mm_nt(cg, bg)
        rows_g = heads_per_group * SSD_HEAD_DIM
        y_state = _mm_nt(cg, hst[g * rows_g:(g + 1) * rows_g, :])
        for hp in range(heads_per_group // 2):
            h0 = g * heads_per_group + 2 * hp
            lo = h0 * SSD_HEAD_DIM
            xs_pair = xs[:, lo:lo + LANES]
            ci = [_col(cum, LANE_DT + h0), _col(cum, LANE_DT + h0 + 1)]
            dti = [_col(sp, LANE_DT + h0), _col(sp, LANE_DT + h0 + 1)]
            xdt_pair = xs_pair * jnp.where(lane_lo, dti[0], dti[1])
            y_pair = None
            for e in range(2):
                cj = cum_row[LANE_DT + h0 + e:LANE_DT + h0 + e + 1, :]
                seg = jnp.exp(jnp.where(causal, ci[e] - cj, -jnp.inf))
                keep = lane_lo if e == 0 else jnp.logical_not(lane_lo)
                t = _mm(cb * seg, jnp.where(keep, xdt_pair, 0.0))
                y_pair = t if y_pair is None else y_pair + t
            y_pair = y_pair + y_state[:, 2 * hp * SSD_HEAD_DIM:2 * hp * SSD_HEAD_DIM + LANES] * \
                jnp.where(lane_lo, jnp.exp(ci[0]), jnp.exp(ci[1]))
            y_pair = y_pair + dexp_ref[:, lo:lo + LANES] * xs_pair
            y_ref[:, lo:lo + LANES] = y_pair
            cl = [ci[0][cs - 1:cs, :], ci[1][cs - 1:cs, :]]
            to_end = jnp.where(lane_lo, jnp.exp(cl[0] - ci[0]), jnp.exp(cl[1] - ci[1]))
            contrib = _mm((xdt_pair * to_end).T, bg)
            dec = jnp.where(row_lo, jnp.exp(cl[0]), jnp.exp(cl[1]))
            hst[lo:lo + LANES, :] = hst[lo:lo + LANES, :] * dec + contrib

    gs = SSD_INNER // SSD_GROUPS
    for g in range(SSD_GROUPS):
        yg = y_ref[:, g * gs:(g + 1) * gs] * _silu(z_ref[:, g * gs:(g + 1) * gs])
        ms = jnp.mean(yg * yg, axis=-1, keepdims=True)
        y_ref[:, g * gs:(g + 1) * gs] = yg * lax.rsqrt(ms + RMS_EPS) * nw_ref[:, g * gs:(g + 1) * gs]

    @pl.when(c_idx == pl.num_programs(1) - 1)
    def _():
        st_ref[...] = hst[...]


def _ssd_prefill(proj, bsz, seq, lw):
    cs = SSD_CHUNK
    assert seq % cs == 0
    nc = seq // cs
    row = lambda b, c: b * nc + c
    pcol = lambda width, off: pl.BlockSpec((cs, width), lambda b, c: (row(b, c), off // width))
    full = lambda a: pl.BlockSpec(a.shape, lambda b, c: (0,) * a.ndim)
    weights = [lw['gate_bias'], lw['gate_alog'], lw['ssd_d_exp'], lw['ssd_norm_w']]
    y, st = pl.pallas_call(
        _ssd_prefill_kernel,
        grid=(bsz, nc),
        in_specs=[pcol(SSD_INNER, COL_Z_SSD), pcol(SSD_INNER, COL_XS), pcol(SSD_BC, COL_B), pcol(SSD_BC, COL_C),
                  pcol(LANES, COL_SMALL)] + [full(w) for w in weights],
        out_specs=[pl.BlockSpec((cs, SSD_INNER), lambda b, c: (row(b, c), 0)),
                   pl.BlockSpec((None, SSD_INNER, SSD_STATE), lambda b, c: (b, 0, 0))],
        out_shape=[jax.ShapeDtypeStruct((bsz * seq, SSD_INNER), F32),
                   jax.ShapeDtypeStruct((bsz, SSD_INNER, SSD_STATE), F32)],
        scratch_shapes=[pltpu.VMEM((SSD_INNER, SSD_STATE), F32)],
        compiler_params=_params("parallel", "arbitrary"),
    )(proj, proj, proj, proj, proj, *weights)
    return y, st


INV_BASE = 16


def _same_block(cs, size):
    r = lax.broadcasted_iota(jnp.int32, (cs, cs), 0) // size
    c = lax.broadcasted_iota(jnp.int32, (cs, cs), 1) // size
    return r == c


def _bdot(a16, b16):
    return jnp.dot(a16, b16, preferred_element_type=F32)


def _gdn_prefill_kernel(q_ref, k_ref, v_ref, z_ref, sm_ref,
                        bias_ref, alog_ref, nw_ref,
                        o_ref, st_ref,
                        sst,
                        qd_s, kd_s, rhs_s, a_s, attn_s, x_s, d_s, u_s, w_s, glast_s):
    c_idx = pl.program_id(1)
    first = c_idx == 0
    cs = q_ref.shape[0]
    heads = range(GDN_HEADS)

    @pl.when(first)
    def _():
        sst[...] = jnp.zeros(sst.shape, F32)

    qc, kc, vc = q_ref, k_ref, v_ref

    _, sg, g_step = _gate_rows(sm_ref[...], bias_ref[...], alog_ref[...])
    causal = _causal_mask(cs)
    strict = _causal_mask(cs, strict=True)
    gc = _mm_rhs_split(jnp.where(causal, 1.0, 0.0), g_step)
    gc_row = gc.T
    glast_s[...] = gc[cs - 1:cs, :]

    for h in heads:
        lo = h * GDN_HEAD_K
        qh = qc[:, lo:lo + GDN_HEAD_K]
        kh = kc[:, lo:lo + GDN_HEAD_K]
        vh = vc[:, lo:lo + GDN_HEAD_V]
        qn = qh * lax.rsqrt(jnp.sum(qh * qh, axis=-1, keepdims=True) + L2_EPS) * (GDN_HEAD_K ** -0.5)
        kn = kh * lax.rsqrt(jnp.sum(kh * kh, axis=-1, keepdims=True) + L2_EPS)
        gi = _col(gc, LANE_G + h)
        bi = _col(sg, LANE_BETA + h)
        gj = gc_row[LANE_G + h:LANE_G + h + 1, :]
        decay = jnp.exp(jnp.where(causal, gi - gj, -jnp.inf))
        egi = jnp.exp(gi)
        kb = kn * bi
        kn16 = kn.astype(BF16)
        a_s[h] = jnp.where(strict, lax.dot_general(kb.astype(BF16), kn16, (((1,), (1,)), ((), ())),
                                                   preferred_element_type=F32) * decay, 0.0)
        attn_s[h] = (lax.dot_general(qn.astype(BF16), kn16, (((1,), (1,)), ((), ())),
                                     preferred_element_type=F32) * decay).astype(BF16)
        qd_s[:, lo:lo + GDN_HEAD_K] = (qn * egi).astype(BF16)
        kd_s[:, lo:lo + GDN_HEAD_K] = (kn * jnp.exp(gi[cs - 1:cs, :] - gi)).astype(BF16)
        rhs_s[:, 2 * lo:2 * lo + GDN_HEAD_V] = (vh * bi).astype(BF16)
        rhs_s[:, 2 * lo + GDN_HEAD_V:2 * lo + 2 * GDN_HEAD_V] = (kb * egi).astype(BF16)

    base = _same_block(cs, INV_BASE)
    eye = jnp.where(causal & jnp.logical_not(strict), 1.0, 0.0)
    for h in heads:
        x = jnp.where(base, -a_s[h], 0.0)
        x_s[h] = x.astype(BF16)
        d_s[h] = eye + x
    span = 2
    while span < INV_BASE:
        for h in heads:
            x16 = x_s[h]
            x2 = _bdot(x16, x16).astype(BF16)
            x_s[h] = x2
            d = d_s[h]
            d_s[h] = d + _bdot(d.astype(BF16), x2)
        span *= 2
    size = INV_BASE
    while size < cs:
        off = _same_block(cs, 2 * size) & jnp.logical_not(_same_block(cs, size))
        for h in heads:
            d = d_s[h]
            d16 = d.astype(BF16)
            low = jnp.where(off, a_s[h], 0.0).astype(BF16)
            d_s[h] = d - _bdot(_bdot(d16, low).astype(BF16), d16)
        size *= 2

    for h in heads:
        lo = h * GDN_HEAD_K
        uw = _bdot(d_s[h].astype(BF16), rhs_s[:, 2 * lo:2 * lo + 2 * GDN_HEAD_V])
        u_s[:, lo:lo + GDN_HEAD_V] = uw[:, :GDN_HEAD_V]
        w_s[:, lo:lo + GDN_HEAD_V] = uw[:, GDN_HEAD_V:].astype(BF16)

    for h in heads:
        lo = h * GDN_HEAD_K
        s = sst[lo:lo + GDN_HEAD_K, :]
        s16 = s.astype(BF16)
        v_new = u_s[:, lo:lo + GDN_HEAD_V] - _bdot(w_s[:, lo:lo + GDN_HEAD_V], s16)
        v16 = v_new.astype(BF16)
        o_h = _bdot(qd_s[:, lo:lo + GDN_HEAD_K], s16) + _bdot(attn_s[h], v16)
        g_last = _col(glast_s[...], LANE_G + h)
        sst[lo:lo + GDN_HEAD_K, :] = s * jnp.exp(g_last) + lax.dot_general(
            kd_s[:, lo:lo + GDN_HEAD_K], v16, (((0,), (0,)), ((), ())), preferred_element_type=F32)
        ms = jnp.mean(o_h * o_h, axis=-1, keepdims=True)
        o_ref[:, lo:lo + GDN_HEAD_V] = (o_h * lax.rsqrt(ms + RMS_EPS) * nw_ref[...]) * \
            _silu(z_ref[:, lo:lo + GDN_HEAD_V])

    @pl.when(c_idx == pl.num_programs(1) - 1)
    def _():
        st_ref[...] = sst[...]


def _gdn_prefill(proj, bsz, seq, lw):
    cs = GDN_CHUNK
    assert seq % cs == 0
    nc = seq // cs
    row = lambda b, c: b * nc + c
    pcol = lambda width, off: pl.BlockSpec((cs, width), lambda b, c: (row(b, c), off // width))
    full = lambda a: pl.BlockSpec(a.shape, lambda b, c: (0,) * a.ndim)
    weights = [lw['gate_bias'], lw['gate_alog'], lw['gdn_norm_w']]
    o, st = pl.pallas_call(
        _gdn_prefill_kernel,
        grid=(bsz, nc),
        in_specs=[pcol(GDN_DIM, COL_Q), pcol(GDN_DIM, COL_K), pcol(GDN_DIM, COL_V), pcol(GDN_DIM, COL_Z_GDN),
                  pcol(LANES, COL_SMALL)] + [full(w) for w in weights],
        out_specs=[pl.BlockSpec((cs, GDN_DIM), lambda b, c: (row(b, c), 0)),
                   pl.BlockSpec((None, GDN_HEADS * GDN_HEAD_K, GDN_HEAD_V), lambda b, c: (b, 0, 0))],
        out_shape=[jax.ShapeDtypeStruct((bsz * seq, GDN_DIM), F32),
                   jax.ShapeDtypeStruct((bsz, GDN_HEADS * GDN_HEAD_K, GDN_HEAD_V), F32)],
        scratch_shapes=[pltpu.VMEM((GDN_HEADS * GDN_HEAD_K, GDN_HEAD_V), F32),
                        pltpu.VMEM((cs, GDN_DIM), BF16),
                        pltpu.VMEM((cs, GDN_DIM), BF16),
                        pltpu.VMEM((cs, 2 * GDN_DIM), BF16),
                        pltpu.VMEM((GDN_HEADS, cs, cs), F32),
                        pltpu.VMEM((GDN_HEADS, cs, cs), BF16),
                        pltpu.VMEM((GDN_HEADS, cs, cs), BF16),
                        pltpu.VMEM((GDN_HEADS, cs, cs), F32),
                        pltpu.VMEM((cs, GDN_DIM), F32),
                        pltpu.VMEM((cs, GDN_DIM), BF16),
                        pltpu.VMEM((1, LANES), F32)],
        compiler_params=_params("parallel", "arbitrary"),
    )(proj, proj, proj, proj, proj, *weights)
    return o, st


def _softmax_rows(s):
    m = jnp.max(s, axis=-1, keepdims=True)
    e = jnp.exp(s - m)
    return e / jnp.sum(e, axis=-1, keepdims=True)


def _xattn_prefill_kernel(x_ref, nw_ref, wq_ref, mk_ref, mv_ref, wo_ref, o_ref):
    x = x_ref[...]
    ms = jnp.mean(x * x, axis=-1, keepdims=True)
    h = x * lax.rsqrt(ms + RMS_EPS) * nw_ref[...]
    q = jnp.dot(h.astype(BF16), wq_ref[...], preferred_element_type=F32)
    outs = []
    for hd in range(XA_HEADS):
        lo = hd * XA_HEAD_DIM
        s = _mm_nt(q[:, lo:lo + XA_HEAD_DIM], mk_ref[:, lo:lo + XA_HEAD_DIM]) * (XA_HEAD_DIM ** -0.5)
        outs.append(_mm(_softmax_rows(s), mv_ref[:, lo:lo + XA_HEAD_DIM]))
    att = jnp.concatenate(outs, axis=-1)
    o_ref[...] = x + jnp.dot(att.astype(BF16), wo_ref[...], preferred_element_type=F32)


def _xattn_prefill(x, nw, wq, mk, mv, wo, bsz, seq, tq):
    n_mem = mk.shape[1]
    nt = seq // tq
    return pl.pallas_call(
        _xattn_prefill_kernel,
        grid=(bsz, nt),
        in_specs=[pl.BlockSpec((tq, D_MODEL), lambda b, t: (b * nt + t, 0)),
                  pl.BlockSpec((1, D_MODEL), lambda b, t: (0, 0)),
                  pl.BlockSpec((D_MODEL, D_MODEL), lambda b, t: (0, 0)),
                  pl.BlockSpec((None, n_mem, D_MODEL), lambda b, t: (b, 0, 0)),
                  pl.BlockSpec((None, n_mem, D_MODEL), lambda b, t: (b, 0, 0)),
                  pl.BlockSpec((D_MODEL, D_MODEL), lambda b, t: (0, 0))],
        out_specs=pl.BlockSpec((tq, D_MODEL), lambda b, t: (b * nt + t, 0)),
        out_shape=jax.ShapeDtypeStruct(x.shape, F32),
        compiler_params=_params("parallel", "arbitrary"),
    )(x, nw.reshape(1, D_MODEL), wq, mk, mv, wo)


def _xattn_decode_kernel(q_ref, mk_ref, mv_ref, o_ref):
    tb = q_ref.shape[0]
    for i in range(tb):
        q = q_ref[i]
        qh = jnp.concatenate([q[:, hd * XA_HEAD_DIM:(hd + 1) * XA_HEAD_DIM] for hd in range(XA_HEADS)], axis=0)
        s = jnp.sum(mk_ref[i] * qh[None], axis=-1, keepdims=True) * (XA_HEAD_DIM ** -0.5)
        e = jnp.exp(s - jnp.max(s, axis=0, keepdims=True))
        p = e / jnp.sum(e, axis=0, keepdims=True)
        o_ref[i] = jnp.sum(p * mv_ref[i], axis=0)


def _xattn_decode(q, mk_all, mv_all, layer, tb):
    _, nb, n_mem, nh, hd = mk_all.shape
    return pl.pallas_call(
        _xattn_decode_kernel,
        grid=(nb // tb,),
        in_specs=[pl.BlockSpec((tb, 1, D_MODEL), lambda i: (i, 0, 0)),
                  pl.BlockSpec((None, tb, n_mem, nh, hd), lambda i: (layer, i, 0, 0, 0)),
                  pl.BlockSpec((None, tb, n_mem, nh, hd), lambda i: (layer, i, 0, 0, 0))],
        out_specs=pl.BlockSpec((tb, nh, hd), lambda i: (i, 0, 0)),
        out_shape=jax.ShapeDtypeStruct((nb, nh, hd), F32),
        compiler_params=_params("parallel"),
    )(q.reshape(nb, 1, D_MODEL), mk_all, mv_all).reshape(nb, D_MODEL)


def _ffn_prefill_kernel(x_ref, g_ref, u_ref, wd_ref, o_ref):
    hmid = g_ref[...] * u_ref[...]
    o_ref[...] = x_ref[...] + jnp.dot(hmid.astype(BF16), wd_ref[...], preferred_element_type=F32)


def _ffn_prefill(x, gu, wd, tm):
    m = x.shape[0]
    d_ff = wd.shape[0]
    assert m % tm == 0
    return pl.pallas_call(
        _ffn_prefill_kernel,
        grid=(m // tm,),
        in_specs=[pl.BlockSpec((tm, D_MODEL), lambda i: (i, 0)),
                  pl.BlockSpec((tm, d_ff), lambda i: (i, 0)),
                  pl.BlockSpec((tm, d_ff), lambda i: (i, 1)),
                  pl.BlockSpec((d_ff, D_MODEL), lambda i: (0, 0))],
        out_specs=pl.BlockSpec((tm, D_MODEL), lambda i: (i, 0)),
        out_shape=jax.ShapeDtypeStruct((m, D_MODEL), F32),
        compiler_params=_params("parallel"),
    )(x, gu, gu, wd)


def _ffn_decode_kernel(x_ref, g_ref, h0_ref, h1_ref, u_ref, cw_ref, cb_ref, wd_ref, o_ref):
    acc = cb_ref[...] + h0_ref[...] * cw_ref[0:1, :] + h1_ref[...] * cw_ref[1:2, :] + g_ref[...] * cw_ref[2:3, :]
    hmid = _silu(acc) * u_ref[...]
    o_ref[...] = x_ref[...] + jnp.dot(hmid.astype(BF16), wd_ref[...], preferred_element_type=F32)


def _ffn_decode(x, gu, hist0, hist1, cw, cb, wd):
    m = x.shape[0]
    d_ff = wd.shape[0]
    rows = lambda width, col: pl.BlockSpec((m, width), lambda i: (0, col))
    return pl.pallas_call(
        _ffn_decode_kernel,
        grid=(1,),
        in_specs=[rows(D_MODEL, 0), rows(d_ff, 0), rows(d_ff, 0), rows(d_ff, 0), rows(d_ff, 1),
                  pl.BlockSpec((FFN_CONV, d_ff), lambda i: (0, 0)),
                  pl.BlockSpec((1, d_ff), lambda i: (0, 0)),
                  pl.BlockSpec((d_ff, D_MODEL), lambda i: (0, 0))],
        out_specs=rows(D_MODEL, 0),
        out_shape=jax.ShapeDtypeStruct((m, D_MODEL), F32),
        compiler_params=_params("arbitrary"),
    )(x, gu, hist0, hist1, gu, cw, cb.reshape(1, d_ff), wd)


def _expand_lanes(rows, first_lane, width, n_out, parts):
    k = lax.broadcasted_iota(jnp.int32, (LANES, n_out), 0)
    l = lax.broadcasted_iota(jnp.int32, (LANES, n_out), 1)
    sel = jnp.where(l // width == k - first_lane, 1.0, 0.0)
    return _mm_lhs_split(rows, sel, parts)


def _decode_rows_kernel(proj_ref, sh0, sh1, sh2, gh0, gh1, gh2,
                        scw_ref, scb_ref, gcw_ref, bias_ref, alog_ref,
                        xs_ref, xdt_ref, b_ref, c_ref, sdec_ref,
                        q_ref, k_ref, v_ref, beta_ref, gdec_ref):
    def conv(hists, new, cw, cb):
        acc = new * cw[3:4, :]
        for t, hr in enumerate(hists):
            acc = acc + hr[...] * cw[t:t + 1, :]
        return acc if cb is None else acc + cb

    xbc_new = jnp.concatenate([proj_ref[:, COL_XS:COL_XS + SSD_INNER],
                               proj_ref[:, COL_B:COL_B + SSD_BC],
                               proj_ref[:, COL_C:COL_C + SSD_BC]], axis=-1)
    xbc = _silu(conv((sh0, sh1, sh2), xbc_new, scw_ref, scb_ref[...]))
    sp, sg, log_dec = _gate_rows(proj_ref[:, COL_SMALL:COL_SMALL + LANES], bias_ref[...], alog_ref[...])
    xs = xbc[:, 0:SSD_INNER]
    xs_ref[...] = xs
    xdt_ref[...] = xs * _expand_lanes(sp, LANE_DT, SSD_HEAD_DIM, SSD_INNER, 3)
    b_ref[...] = xbc[:, SSD_INNER:SSD_INNER + SSD_BC]
    c_ref[...] = xbc[:, SSD_INNER + SSD_BC:SSD_INNER + 2 * SSD_BC]
    sdec_ref[...] = jnp.exp(_expand_lanes(log_dec, LANE_DT, SSD_HEAD_DIM, SSD_INNER, 3))

    qkv_new = proj_ref[:, COL_Q:COL_Q + 3 * GDN_DIM]
    qkv = _silu(conv((gh0, gh1, gh2), qkv_new, gcw_ref, None))
    for h in range(GDN_HEADS):
        lo = h * GDN_HEAD_K
        qh = qkv[:, lo:lo + GDN_HEAD_K]
        kh = qkv[:, GDN_DIM + lo:GDN_DIM + lo + GDN_HEAD_K]
        q_ref[:, lo:lo + GDN_HEAD_K] = qh * lax.rsqrt(jnp.sum(qh * qh, axis=-1, keepdims=True) + L2_EPS) * \
            (GDN_HEAD_K ** -0.5)
        k_ref[:, lo:lo + GDN_HEAD_K] = kh * lax.rsqrt(jnp.sum(kh * kh, axis=-1, keepdims=True) + L2_EPS)
    v_ref[...] = qkv[:, 2 * GDN_DIM:3 * GDN_DIM]
    beta_ref[...] = _expand_lanes(sg, LANE_BETA, GDN_HEAD_V, GDN_DIM, 3)
    gdec_ref[...] = jnp.exp(_expand_lanes(log_dec, LANE_G, GDN_HEAD_V, GDN_DIM, 3))


def _decode_rows(proj, ssd_hist, gdn_hist, lw):
    m = proj.shape[0]
    ins = [proj] + [ssd_hist[:, t] for t in range(SSD_CONV - 1)] + [gdn_hist[:, t] for t in range(GDN_CONV - 1)]
    ins += [lw['ssd_conv_w'], lw['ssd_conv_b'], lw['gdn_conv_w'], lw['gate_bias'], lw['gate_alog']]
    widths = [SSD_INNER, SSD_INNER, SSD_BC, SSD_BC, SSD_INNER, GDN_DIM, GDN_DIM, GDN_DIM, GDN_DIM, GDN_DIM]
    return pl.pallas_call(
        _decode_rows_kernel,
        grid=(1,),
        in_specs=[pl.BlockSpec(a.shape, lambda i, nd=a.ndim: (0,) * nd) for a in ins],
        out_specs=[pl.BlockSpec((m, w), lambda i: (0, 0)) for w in widths],
        out_shape=[jax.ShapeDtypeStruct((m, w), F32) for w in widths],
        compiler_params=_params("arbitrary"),
    )(*ins)


def _rows_to_cols(rows, i, parts):
    tb = rows.shape[0]
    r = lax.broadcasted_iota(jnp.int32, (tb, LANES), 0)
    sel = jnp.where(r == i, 1.0, 0.0).astype(BF16)
    acc = None
    for p in _split(rows, parts):
        t = lax.dot_general(p, sel, (((0,), (0,)), ((), ())), preferred_element_type=F32)
        acc = t if acc is None else acc + t
    return acc


def _only_row(rows, i):
    r = lax.broadcasted_iota(jnp.int32, rows.shape, 0)
    return jnp.where(r == i, rows, 0.0)


def _decode_state_kernel(xdt_ref, b_ref, c_ref, sdec_ref, q_ref, k_ref, v_ref, beta_ref, gdec_ref,
                         hs_ref, ss_ref, *rest):
    y_ref, o_ref, hs_out, ss_out = rest[-4:]
    tb = xdt_ref.shape[0]
    xdt = xdt_ref[...]
    bm = b_ref[...]
    cm = c_ref[...]
    sdec = sdec_ref[...]
    qn = q_ref[...]
    kn = k_ref[...]
    vv = v_ref[...]
    beta = beta_ref[...]
    gdec = gdec_ref[...]
    rows_g = SSD_INNER // SSD_GROUPS

    y_acc = jnp.zeros((tb, SSD_INNER), F32)
    ks_acc = jnp.zeros((tb, GDN_DIM), F32)
    for i in range(tb):
        dec_c = _rows_to_cols(sdec, i, 2)
        xdt_c = _rows_to_cols(xdt, i, 1)
        pieces = []
        for g in range(SSD_GROUPS):
            rs = slice(g * rows_g, (g + 1) * rows_g)
            bg = bm[i:i + 1, g * SSD_STATE:(g + 1) * SSD_STATE]
            h_new = hs_ref[i, rs, :] * dec_c[rs, :] + xdt_c[rs, :] * bg
            hs_out[i, rs, :] = h_new
            cg = _only_row(cm[:, g * SSD_STATE:(g + 1) * SSD_STATE], i)
            pieces.append(_mm_nt(cg, h_new))
        y_acc = y_acc + jnp.concatenate(pieces, axis=-1)
        g_c = _rows_to_cols(gdec, i, 2)
        s_dec = ss_ref[i] * g_c
        ss_out[i] = s_dec
        pieces = []
        for h in range(GDN_HEADS):
            rs = slice(h * GDN_HEAD_K, (h + 1) * GDN_HEAD_K)
            pieces.append(_mm(_only_row(kn[:, rs], i), s_dec[rs, :]))
        ks_acc = ks_acc + jnp.concatenate(pieces, axis=-1)
    y_ref[...] = y_acc

    delta = beta * (vv - ks_acc)
    o_acc = jnp.zeros((tb, GDN_DIM), F32)
    for i in range(tb):
        k_c = _rows_to_cols(kn, i, 1)
        pieces = []
        for h in range(GDN_HEADS):
            rs = slice(h * GDN_HEAD_K, (h + 1) * GDN_HEAD_K)
            d_row = delta[i:i + 1, rs]
            s_new = ss_out[i, rs, :] + k_c[rs, :] * d_row
            ss_out[i, rs, :] = s_new
            pieces.append(_mm(_only_row(qn[:, rs], i), s_new))
        o_acc = o_acc + jnp.concatenate(pieces, axis=-1)
    o_ref[...] = o_acc


def _decode_state(rows, hs_all, ss_all, layer, tb, carried):
    xs, xdt, bm, cm, sdec, qn, kn, vv, beta, gdec = rows
    nb = xdt.shape[0]
    row_ins = [xdt, bm, cm, sdec, qn, kn, vv, beta, gdec]
    rspec = lambda a: pl.BlockSpec((tb, a.shape[1]), lambda i: (i, 0))
    sspec = lambda a: pl.BlockSpec((None, tb) + a.shape[2:], lambda i: (layer, i, 0, 0))
    n_in = len(row_ins) + 2
    aliases = {n_in + j: 2 + j for j in range(len(carried))}
    return pl.pallas_call(
        _decode_state_kernel,
        grid=(nb // tb,),
        in_specs=[rspec(a) for a in row_ins] + [sspec(hs_all), sspec(ss_all)] +
                 [pl.BlockSpec(memory_space=pl.ANY) for _ in carried],
        out_specs=[rspec(xdt), rspec(qn), sspec(hs_all), sspec(ss_all)],
        out_shape=[jax.ShapeDtypeStruct(xdt.shape, F32), jax.ShapeDtypeStruct(qn.shape, F32),
                   jax.ShapeDtypeStruct(hs_all.shape, F32), jax.ShapeDtypeStruct(ss_all.shape, F32)],
        input_output_aliases=aliases,
        compiler_params=_params("parallel"),
    )(*row_ins, hs_all, ss_all, *carried)


def _decode_out_kernel(x_ref, y_ref, xs_ref, o_ref, proj_ref, dexp_ref, snw_ref, gnw_ref, w1_ref, w2_ref, out_ref):
    y = y_ref[...] + dexp_ref[...] * xs_ref[...]
    gs = SSD_INNER // SSD_GROUPS
    ys = []
    for g in range(SSD_GROUPS):
        yg = y[:, g * gs:(g + 1) * gs] * _silu(proj_ref[:, COL_Z_SSD + g * gs:COL_Z_SSD + (g + 1) * gs])
        ms = jnp.mean(yg * yg, axis=-1, keepdims=True)
        ys.append(yg * lax.rsqrt(ms + RMS_EPS) * snw_ref[:, g * gs:(g + 1) * gs])
    os = []
    for h in range(GDN_HEADS):
        lo = h * GDN_HEAD_V
        oh = o_ref[:, lo:lo + GDN_HEAD_V]
        ms = jnp.mean(oh * oh, axis=-1, keepdims=True)
        os.append(oh * lax.rsqrt(ms + RMS_EPS) * gnw_ref[...] *
                  _silu(proj_ref[:, COL_Z_GDN + lo:COL_Z_GDN + lo + GDN_HEAD_V]))
    yn = jnp.concatenate(ys, axis=-1)
    on = jnp.concatenate(os, axis=-1)
    out_ref[...] = x_ref[...] + _mm(yn, w1_ref[...]) + _mm(on, w2_ref[...])


def _decode_out(x, y, xs, o, proj, lw):
    ins = [x, y, xs, o, proj, lw['ssd_d_exp'], lw['ssd_norm_w'], lw['gdn_norm_w'], lw['w_out_ssd'], lw['w_out_gdn']]
    return pl.pallas_call(
        _decode_out_kernel,
        grid=(1,),
        in_specs=[pl.BlockSpec(a.shape, lambda i, nd=a.ndim: (0,) * nd) for a in ins],
        out_specs=pl.BlockSpec(x.shape, lambda i: (0, 0)),
        out_shape=jax.ShapeDtypeStruct(x.shape, F32),
        compiler_params=_params("arbitrary"),
    )(*ins)


def _identity_taps(taps, n):
    return jnp.concatenate([jnp.zeros((taps - 1, n), F32), jnp.ones((1, n), F32)], axis=0)


def _prep_layer(i, p):
    w_in = p['w_in'][i]
    o_zs = 0
    o_xbc = o_zs + SSD_INNER
    o_dt = o_xbc + SSD_INNER + 2 * SSD_BC
    o_qkv = o_dt + SSD_HEADS
    o_zg = o_qkv + 3 * GDN_DIM
    o_b = o_zg + GDN_DIM
    o_a = o_b + GDN_HEADS
    used = SSD_HEADS + 2 * GDN_HEADS
    w_perm = jnp.concatenate([
        w_in[:, o_zs:o_zs + SSD_INNER],
        w_in[:, o_xbc:o_xbc + SSD_INNER],
        w_in[:, o_qkv:o_qkv + 3 * GDN_DIM],
        w_in[:, o_zg:o_zg + GDN_DIM],
        w_in[:, o_xbc + SSD_INNER:o_xbc + SSD_INNER + 2 * SSD_BC],
        w_in[:, o_dt:o_dt + SSD_HEADS],
        w_in[:, o_b:o_b + GDN_HEADS],
        w_in[:, o_a:o_a + GDN_HEADS],
        jnp.zeros((D_MODEL, PROJ_COLS - COL_SMALL - used), F32)], axis=1).astype(BF16)
    lane_pad = jnp.zeros((LANES - used,), F32)
    scw = p['ssd_conv_w'][i]
    scb = p['ssd_conv_b'][i].reshape(1, -1)
    gcw = p['gdn_conv_w'][i]
    return {
        'w_in': w_perm,
        'norm_mix_w': p['norm_mix_w'][i],
        'ssd_conv_w': scw, 'ssd_conv_b': scb, 'gdn_conv_w': gcw,
        'in_cw': jnp.concatenate([_identity_taps(SSD_CONV, SSD_INNER), scw[:, :SSD_INNER], gcw,
                                  _identity_taps(GDN_CONV, GDN_DIM), scw[:, SSD_INNER:],
                                  _identity_taps(SSD_CONV, PROJ_COLS - COL_SMALL)], axis=1),
        'in_cb': jnp.concatenate([jnp.zeros((1, COL_XS), F32), scb[:, :SSD_INNER],
                                  jnp.zeros((1, COL_B - COL_Q), F32), scb[:, SSD_INNER:],
                                  jnp.zeros((1, PROJ_COLS - COL_SMALL), F32)], axis=1),
        'gu_cw': jnp.concatenate([p['ffn_conv_w'][i], _identity_taps(FFN_CONV, p['ffn_conv_w'].shape[2])], axis=1),
        'gu_cb': jnp.concatenate([p['ffn_conv_b'][i], jnp.zeros_like(p['ffn_conv_b'][i])]).reshape(1, -1),
        'gate_bias': jnp.concatenate([p['ssd_dt_bias'][i], jnp.zeros((GDN_HEADS,), F32), p['gdn_dt_bias'][i],
                                      lane_pad]).reshape(1, LANES),
        'gate_alog': jnp.concatenate([p['ssd_a_log'][i], jnp.zeros((GDN_HEADS,), F32), p['gdn_a_log'][i],
                                      lane_pad]).reshape(1, LANES),
        'ssd_d_exp': jnp.repeat(p['ssd_d'][i], SSD_HEAD_DIM).reshape(1, SSD_INNER),
        'ssd_norm_w': p['ssd_norm_w'][i].reshape(1, SSD_INNER),
        'gdn_norm_w': p['gdn_norm_w'][i].reshape(1, GDN_HEAD_V),
        'w_out_ssd': p['w_out'][i, :SSD_INNER].astype(BF16),
        'w_out_gdn': p['w_out'][i, SSD_INNER:].astype(BF16),
        'norm_xa_w': p['norm_xa_w'][i], 'norm_mem_w': p['norm_mem_w'][i],
        'xa_wq': p['xa_wq'][i].astype(BF16), 'xa_wo': p['xa_wo'][i].astype(BF16),
        'xa_wkv': jnp.concatenate([p['xa_wk'][i], p['xa_wv'][i]], axis=1).astype(BF16),
        'norm_ffn_w': p['norm_ffn_w'][i],
        'ffn_w_gu': jnp.concatenate([p['ffn_w_gate'][i], p['ffn_w_up'][i]], axis=1).astype(BF16),
        'ffn_conv_w': p['ffn_conv_w'][i], 'ffn_conv_b': p['ffn_conv_b'][i],
        'ffn_w_down': p['ffn_w_down'][i].astype(BF16),
    }


def _unpermute_conv_rows(proj_rows):
    ssd = jnp.concatenate([proj_rows[..., COL_XS:COL_XS + SSD_INNER],
                           proj_rows[..., COL_B:COL_B + SSD_BC],
                           proj_rows[..., COL_C:COL_C + SSD_BC]], axis=-1)
    gdn = proj_rows[..., COL_Q:COL_Q + 3 * GDN_DIM]
    return ssd, gdn


def _tile(m, pref):
    t = min(m, pref)
    while m % t:
        t //= 2
    return t


def kernel(x_prompt, x_sample, mem_prompt, cache_mem_k, cache_mem_v, state_ssd_conv, state_ssd, state_gdn_conv, state_gdn, state_ffn_conv, norm_mix_w, w_in, ssd_conv_w, ssd_conv_b, ssd_dt_bias, ssd_a_log, ssd_d, ssd_norm_w, gdn_conv_w, gdn_dt_bias, gdn_a_log, gdn_norm_w, w_out, norm_xa_w, norm_mem_w, xa_wq, xa_wk, xa_wv, xa_wo, norm_ffn_w, ffn_w_gate, ffn_w_up, ffn_conv_w, ffn_conv_b, ffn_w_down, final_norm_w):
    params = dict(norm_mix_w=norm_mix_w, w_in=w_in, ssd_conv_w=ssd_conv_w, ssd_conv_b=ssd_conv_b,
                  ssd_dt_bias=ssd_dt_bias, ssd_a_log=ssd_a_log, ssd_d=ssd_d, ssd_norm_w=ssd_norm_w,
                  gdn_conv_w=gdn_conv_w, gdn_dt_bias=gdn_dt_bias, gdn_a_log=gdn_a_log, gdn_norm_w=gdn_norm_w,
                  w_out=w_out, norm_xa_w=norm_xa_w, norm_mem_w=norm_mem_w, xa_wq=xa_wq, xa_wk=xa_wk, xa_wv=xa_wv,
                  xa_wo=xa_wo, norm_ffn_w=norm_ffn_w, ffn_w_gate=ffn_w_gate, ffn_w_up=ffn_w_up,
                  ffn_conv_w=ffn_conv_w, ffn_conv_b=ffn_conv_b, ffn_w_down=ffn_w_down)
    depth = w_in.shape[0]
    bsz, seq, _ = x_prompt.shape
    nb = x_sample.shape[0]
    n_mem = mem_prompt.shape[1]
    d_ff = ffn_w_down.shape[1]
    mp = bsz * seq

    xp = x_prompt.reshape(mp, D_MODEL)
    xs = x_sample.reshape(nb, D_MODEL)
    mem = mem_prompt.reshape(bsz * n_mem, D_MODEL)
    tm_p = _tile(seq, 1024)
    tm_c = _tile(seq, 256)
    tm_mem = _tile(bsz * n_mem, 1024)
    dec_tile = _tile(nb, DEC_TILE)

    hs_all = state_ssd.reshape(depth, nb, SSD_INNER, SSD_STATE)
    ss_all = state_gdn.reshape(depth, nb, GDN_HEADS * GDN_HEAD_K, GDN_HEAD_V)
    mk_all, mv_all = cache_mem_k, cache_mem_v
    new_states = ()

    mkp, mvp = [], []
    p_sc, p_sh, p_gc, p_gs, p_fc = [], [], [], [], []
    s_sc, s_gc, s_fc = [], [], []
    for i in range(depth):
        lw = _prep_layer(i, params)

        kv = _norm_mm(mem, lw['norm_mem_w'], lw['xa_wkv'], tm_mem)
        mk = kv[:, :D_MODEL].reshape(bsz, n_mem, D_MODEL)
        mv = kv[:, D_MODEL:].reshape(bsz, n_mem, D_MODEL)
        mkp.append(mk.reshape(bsz, n_mem, XA_HEADS, XA_HEAD_DIM))
        mvp.append(mv.reshape(bsz, n_mem, XA_HEADS, XA_HEAD_DIM))

        proj, tails = _norm_mm_conv(xp, lw['norm_mix_w'], lw['w_in'], lw['in_cw'], lw['in_cb'],
                                    [(COL_XS, COL_Z_GDN), (COL_B, COL_SMALL)], seq, tm_c)
        y, h_new = _ssd_prefill(proj, bsz, seq, lw)
        o, s_new = _gdn_prefill(proj, bsz, seq, lw)
        seq_tails = tails.reshape(bsz, seq // tm_c, HIST_ROWS, PROJ_COLS)[:, -1]
        ssd_tail, gdn_tail = _unpermute_conv_rows(seq_tails[:, HIST_ROWS - (SSD_CONV - 1):])
        p_sc.append(ssd_tail)
        p_gc.append(gdn_tail)
        p_sh.append(h_new.reshape(bsz, SSD_HEADS, SSD_HEAD_DIM, SSD_STATE))
        p_gs.append(s_new.reshape(bsz, GDN_HEADS, GDN_HEAD_K, GDN_HEAD_V))
        xp = _mm_res(xp, [y, o], [lw['w_out_ssd'], lw['w_out_gdn']], _tile(mp, 512))
        xp = _xattn_prefill(xp, lw['norm_xa_w'], lw['xa_wq'], mk, mv, lw['xa_wo'], bsz, seq, _tile(seq, 512))
        gu, gu_tails = _norm_mm_conv(xp, lw['norm_ffn_w'], lw['ffn_w_gu'], lw['gu_cw'], lw['gu_cb'],
                                     [(0, d_ff)], seq, tm_c)
        p_fc.append(gu_tails.reshape(bsz, seq // tm_c, HIST_ROWS, 2 * d_ff)[:, -1, HIST_ROWS - (FFN_CONV - 1):, :d_ff])
        xp = _ffn_prefill(xp, gu, lw['ffn_w_down'], _tile(mp, 512))

        proj_s = _norm_mm(xs, lw['norm_mix_w'], lw['w_in'], nb)
        ssd_new, gdn_new = _unpermute_conv_rows(proj_s)
        s_sc.append(jnp.concatenate([state_ssd_conv[i][:, 1:], ssd_new[:, None]], axis=1))
        s_gc.append(jnp.concatenate([state_gdn_conv[i][:, 1:], gdn_new[:, None]], axis=1))
        rows = _decode_rows(proj_s, state_ssd_conv[i], state_gdn_conv[i], lw)
        y_s, o_s, hs_new, ss_new = _decode_state(rows, hs_all, ss_all, i, dec_tile, new_states)
        new_states = (hs_new, ss_new)
        xs = _decode_out(xs, y_s, rows[0], o_s, proj_s, lw)
        q_s = _norm_mm(xs, lw['norm_xa_w'], lw['xa_wq'], nb)
        att = _xattn_decode(q_s, mk_all, mv_all, i, _tile(nb, 4))
        xs = _mm_res(xs, [att], [lw['xa_wo']], nb)
        gu_s = _norm_mm(xs, lw['norm_ffn_w'], lw['ffn_w_gu'], nb)
        s_fc.append(jnp.concatenate([state_ffn_conv[i][:, 1:], gu_s[:, None, :d_ff]], axis=1))
        xs = _ffn_decode(xs, gu_s, state_ffn_conv[i][:, 0], state_ffn_conv[i][:, 1],
                         lw['ffn_conv_w'], lw['ffn_conv_b'], lw['ffn_w_down'])

    y_prompt = _rmsnorm(xp, final_norm_w, tm_p).reshape(bsz, seq, D_MODEL)
    y_sample = _rmsnorm(xs, final_norm_w, nb).reshape(nb, 1, D_MODEL)
    return (y_prompt, y_sample, jnp.stack(mkp), jnp.stack(mvp),
            jnp.stack(p_sc), jnp.stack(p_sh), jnp.stack(p_gc), jnp.stack(p_gs), jnp.stack(p_fc),
            jnp.stack(s_sc), new_states[0].reshape(state_ssd.shape), jnp.stack(s_gc),
            new_states[1].reshape(state_gdn.shape), jnp.stack(s_fc))
```

```python
import functools

import jax
import jax.numpy as jnp
from jax import lax
from jax.experimental import pallas as pl
from jax.experimental.pallas import tpu as pltpu

F32 = jnp.float32
BF16 = jnp.bfloat16

D_MODEL = 1024
SSD_HEADS = 16
SSD_HEAD_DIM = 64
SSD_STATE = 128
SSD_GROUPS = 2
SSD_INNER = SSD_HEADS * SSD_HEAD_DIM
SSD_BC = SSD_GROUPS * SSD_STATE
SSD_CONV = 4
GDN_HEADS = 8
GDN_HEAD_K = 128
GDN_HEAD_V = 128
GDN_DIM = GDN_HEADS * GDN_HEAD_V
GDN_CONV = 4
XA_HEADS = 4
XA_HEAD_DIM = D_MODEL // XA_HEADS
FFN_CONV = 3
RMS_EPS = 1e-6
L2_EPS = 1e-6

SUBLANES = 8
LANES = 128
MXU_COLS = 256
VMEM_LIMIT_BYTES = 56 * 1024 * 1024

COL_Z_SSD = 0
COL_XS = 1024
COL_Q = 2048
COL_K = 3072
COL_V = 4096
COL_Z_GDN = 5120
COL_B = 6144
COL_C = 6400
COL_SMALL = 6656
PROJ_COLS = 7168
LANE_DT = 0
LANE_BETA = 16
LANE_G = 24

SSD_CHUNK = 256
GDN_CHUNK = 128
DEC_TILE = 8


def _params(*sem):
    return pltpu.CompilerParams(dimension_semantics=sem, vmem_limit_bytes=VMEM_LIMIT_BYTES)


def _mm(a, b):
    return jnp.dot(a.astype(BF16), b.astype(BF16), preferred_element_type=F32)


def _mm_nt(a, b):
    return lax.dot_general(a.astype(BF16), b.astype(BF16), (((1,), (1,)), ((), ())),
                           preferred_element_type=F32)


def _split(x, parts):
    out = []
    r = x
    for _ in range(parts - 1):
        p = r.astype(BF16)
        out.append(p)
        r = r - p.astype(F32)
    out.append(r.astype(BF16))
    return out


def _mm_rhs_split(a_exact, b, parts=3):
    a16 = a_exact.astype(BF16)
    acc = None
    for p in _split(b, parts):
        t = jnp.dot(a16, p, preferred_element_type=F32)
        acc = t if acc is None else acc + t
    return acc


def _mm_lhs_split(a, b_exact, parts=3):
    b16 = b_exact.astype(BF16)
    acc = None
    for p in _split(a, parts):
        t = jnp.dot(p, b16, preferred_element_type=F32)
        acc = t if acc is None else acc + t
    return acc


def _silu(x):
    return x / (1.0 + jnp.exp(-x))


def _sigmoid(x):
    return 1.0 / (1.0 + jnp.exp(-x))


def _softplus(x):
    return jnp.maximum(x, 0.0) + jnp.log1p(jnp.exp(-jnp.abs(x)))


def _col(arr, lane):
    idx = lax.broadcasted_iota(jnp.int32, arr.shape, 1)
    return jnp.sum(jnp.where(idx == lane, arr, 0.0), axis=1, keepdims=True)


def _gate_rows(sm, bias, alog):
    sp = _softplus(sm + bias)
    sg = _sigmoid(sm)
    lane = lax.broadcasted_iota(jnp.int32, alog.shape, 1)
    has_decay = (lane < LANE_BETA) | ((lane >= LANE_G) & (lane < LANE_G + GDN_HEADS))
    neg_a = jnp.where(has_decay, -jnp.exp(alog), 0.0)
    return sp, sg, sp * neg_a


def _norm_mm_kernel(x_ref, nw_ref, w_ref, o_ref, xn_ref):
    @pl.when(pl.program_id(1) == 0)
    def _():
        x = x_ref[...]
        ms = jnp.mean(x * x, axis=-1, keepdims=True)
        xn_ref[...] = (x * lax.rsqrt(ms + RMS_EPS) * nw_ref[...]).astype(BF16)

    o_ref[...] = jnp.dot(xn_ref[...], w_ref[...], preferred_element_type=F32)


MAX_COL_TILE_LANES = 11 * LANES


def _col_tile(n):
    return max(t for t in range(LANES, MAX_COL_TILE_LANES + 1, LANES) if n % t == 0)


def _norm_mm(x, nw, w, tm):
    m, k = x.shape
    n = w.shape[1]
    tn = _col_tile(n)
    assert m % tm == 0
    return pl.pallas_call(
        _norm_mm_kernel,
        grid=(m // tm, n // tn),
        in_specs=[pl.BlockSpec((tm, k), lambda i, j: (i, 0)),
                  pl.BlockSpec((1, k), lambda i, j: (0, 0)),
                  pl.BlockSpec((k, tn), lambda i, j: (0, j))],
        out_specs=pl.BlockSpec((tm, tn), lambda i, j: (i, j)),
        out_shape=jax.ShapeDtypeStruct((m, n), F32),
        scratch_shapes=[pltpu.VMEM((tm, k), BF16)],
        compiler_params=_params("parallel", "arbitrary"),
    )(x, nw.reshape(1, k), w)


HIST_ROWS = SUBLANES


CONV_BUFS = 4


def _norm_mm_conv_kernel(x_ref, nw_ref, w_ref, cw_ref, cb_ref, o_ref, tail_ref, ext_ref, hist_ref,
                         *, taps, tiles_per_seq, conv_ranges):
    i = pl.program_id(0)
    tm = x_ref.shape[0]
    n = w_ref.shape[1]

    @pl.when(i == 0)
    def _():
        hist_ref[...] = jnp.zeros(hist_ref.shape, F32)

    x = x_ref[...]
    ms = jnp.mean(x * x, axis=-1, keepdims=True)
    xn = (x * lax.rsqrt(ms + RMS_EPS) * nw_ref[...]).astype(BF16)
    starts_seq = i % tiles_per_seq == 0
    for ci, lo in enumerate(range(0, n, MXU_COLS)):
        cols = slice(lo, min(lo + MXU_COLS, n))
        acc = jnp.dot(xn, w_ref[:, cols], preferred_element_type=F32)
        tail_ref[:, cols] = acc[tm - HIST_ROWS:tm, :]
        if not any(a <= cols.start and cols.stop <= b for a, b in conv_ranges):
            o_ref[:, cols] = acc
            continue
        buf = ext_ref.at[ci % CONV_BUFS]
        width = cols.stop - cols.start
        buf[0:HIST_ROWS, 0:width] = jnp.where(starts_seq, 0.0, hist_ref[:, cols])
        buf[HIST_ROWS:HIST_ROWS + tm, 0:width] = acc
        hist_ref[:, cols] = acc[tm - HIST_ROWS:tm, :]
        c = cb_ref[:, cols]
        for k in range(taps):
            start = HIST_ROWS - (taps - 1) + k
            c = c + buf[start:start + tm, 0:width] * cw_ref[k:k + 1, cols]
        o_ref[:, cols] = _silu(c)


def _norm_mm_conv(x, nw, w, cw, cb, conv_ranges, seq, tm):
    m, k = x.shape
    n = w.shape[1]
    taps = cw.shape[0]
    assert m % tm == 0 and seq % tm == 0 and tm >= HIST_ROWS
    assert all(a % MXU_COLS == 0 and b % MXU_COLS == 0 for a, b in conv_ranges)
    whole = lambda a: pl.BlockSpec(a.shape, lambda i: (0, 0), pipeline_mode=pl.Buffered(1))
    nw2 = nw.reshape(1, k)
    return pl.pallas_call(
        functools.partial(_norm_mm_conv_kernel, taps=taps, tiles_per_seq=seq // tm,
                          conv_ranges=tuple(conv_ranges)),
        grid=(m // tm,),
        in_specs=[pl.BlockSpec((tm, k), lambda i: (i, 0)), whole(nw2), whole(w), whole(cw), whole(cb)],
        out_specs=[pl.BlockSpec((tm, n), lambda i: (i, 0)),
                   pl.BlockSpec((None, HIST_ROWS, n), lambda i: (i, 0, 0))],
        out_shape=[jax.ShapeDtypeStruct((m, n), F32),
                   jax.ShapeDtypeStruct((m // tm, HIST_ROWS, n), F32)],
        scratch_shapes=[pltpu.VMEM((CONV_BUFS, HIST_ROWS + tm, MXU_COLS), F32),
                        pltpu.VMEM((HIST_ROWS, n), F32)],
        compiler_params=_params("arbitrary"),
    )(x, nw2, w, cw, cb)


def _mm_res_kernel(*refs, n_in):
    res_ref = refs[0]
    a_refs = refs[1:1 + n_in]
    w_refs = refs[1 + n_in:1 + 2 * n_in]
    o_ref = refs[1 + 2 * n_in]
    acc = res_ref[...]
    for a_ref, w_ref in zip(a_refs, w_refs):
        acc = acc + jnp.dot(a_ref[...].astype(BF16), w_ref[...], preferred_element_type=F32)
    o_ref[...] = acc


def _mm_res(res, a_list, w_list, tm):
    m, n = res.shape
    assert m % tm == 0
    n_in = len(a_list)
    in_specs = [pl.BlockSpec((tm, n), lambda i: (i, 0))]
    in_specs += [pl.BlockSpec((tm, a.shape[1]), lambda i: (i, 0)) for a in a_list]
    in_specs += [pl.BlockSpec(w.shape, lambda i: (0, 0)) for w in w_list]
    return pl.pallas_call(
        functools.partial(_mm_res_kernel, n_in=n_in),
        grid=(m // tm,),
        in_specs=in_specs,
        out_specs=pl.BlockSpec((tm, n), lambda i: (i, 0)),
        out_shape=jax.ShapeDtypeStruct((m, n), F32),
        compiler_params=_params("parallel"),
    )(res, *a_list, *w_list)


def _rmsnorm_kernel(x_ref, nw_ref, o_ref):
    x = x_ref[...]
    ms = jnp.mean(x * x, axis=-1, keepdims=True)
    o_ref[...] = x * lax.rsqrt(ms + RMS_EPS) * nw_ref[...]


def _rmsnorm(x, nw, tm):
    m, k = x.shape
    return pl.pallas_call(
        _rmsnorm_kernel,
        grid=(m // tm,),
        in_specs=[pl.BlockSpec((tm, k), lambda i: (i, 0)), pl.BlockSpec((1, k), lambda i: (0, 0))],
        out_specs=pl.BlockSpec((tm, k), lambda i: (i, 0)),
        out_shape=jax.ShapeDtypeStruct((m, k), F32),
        compiler_params=_params("parallel"),
    )(x, nw.reshape(1, k))


def _causal_mask(cs, strict=False):
    r = lax.broadcasted_iota(jnp.int32, (cs, cs), 0)
    c = lax.broadcasted_iota(jnp.int32, (cs, cs), 1)
    return (r > c) if strict else (r >= c)


def _ssd_prefill_kernel(z_ref, xs_ref, b_ref, c_ref, sm_ref,
                        bias_ref, alog_ref, dexp_ref, nw_ref,
                        y_ref, st_ref,
                        hst, y_s):
    c_idx = pl.program_id(1)
    first = c_idx == 0
    cs = xs_ref.shape[0]

    @pl.when(first)
    def _():
        hst[...] = jnp.zeros(hst.shape, F32)

    xs = xs_ref[...]
    bm = b_ref[...]
    cm = c_ref[...]

    sp, _, dta = _gate_rows(sm_ref[...], bias_ref[...], alog_ref[...])
    causal = _causal_mask(cs)
    cum = _mm_rhs_split(jnp.where(causal, 1.0, 0.0), dta)
    cum_row = cum.T

    lane_lo = lax.broadcasted_iota(jnp.int32, (1, LANES), 1) < SSD_HEAD_DIM
    row_lo = lax.broadcasted_iota(jnp.int32, (LANES, 1), 0) < SSD_HEAD_DIM
    heads_per_group = SSD_HEADS // SSD_GROUPS
    for g in range(SSD_GROUPS):
        cg = cm[:, g * SSD_STATE:(g + 1) * SSD_STATE]
        bg = bm[:, g * SSD_STATE:(g + 1) * SSD_STATE]
        cb = _mm_nt(cg, bg)
        rows_g = heads_per_group * SSD_HEAD_DIM
        y_state = _mm_nt(cg, hst[g * rows_g:(g + 1) * rows_g, :])
        for hp in range(heads_per_group // 2):
            h0 = g * heads_per_group + 2 * hp
            lo = h0 * SSD_HEAD_DIM
            xs_pair = xs[:, lo:lo + LANES]
            ci = [_col(cum, LANE_DT + h0), _col(cum, LANE_DT + h0 + 1)]
            dti = [_col(sp, LANE_DT + h0), _col(sp, LANE_DT + h0 + 1)]
            xdt_pair = xs_pair * jnp.where(lane_lo, dti[0], dti[1])
            y_pair = None
            for e in range(2):
                cj = cum_row[LANE_DT + h0 + e:LANE_DT + h0 + e + 1, :]
                seg = jnp.exp(jnp.where(causal, ci[e] - cj, -jnp.inf))
                keep = lane_lo if e == 0 else jnp.logical_not(lane_lo)
                t = _mm(cb * seg, jnp.where(keep, xdt_pair, 0.0))
                y_pair = t if y_pair is None else y_pair + t
            y_pair = y_pair + y_state[:, 2 * hp * SSD_HEAD_DIM:2 * hp * SSD_HEAD_DIM + LANES] * \
                jnp.where(lane_lo, jnp.exp(ci[0]), jnp.exp(ci[1]))
            y_pair = y_pair + dexp_ref[:, lo:lo + LANES] * xs_pair
            y_s[:, lo:lo + LANES] = y_pair
            cl = [ci[0][cs - 1:cs, :], ci[1][cs - 1:cs, :]]
            to_end = jnp.where(lane_lo, jnp.exp(cl[0] - ci[0]), jnp.exp(cl[1] - ci[1]))
            contrib = _mm((xdt_pair * to_end).T, bg)
            dec = jnp.where(row_lo, jnp.exp(cl[0]), jnp.exp(cl[1]))
            hst[lo:lo + LANES, :] = hst[lo:lo + LANES, :] * dec + contrib

    gs = SSD_INNER // SSD_GROUPS
    for g in range(SSD_GROUPS):
        yg = y_s[:, g * gs:(g + 1) * gs] * _silu(z_ref[:, g * gs:(g + 1) * gs])
        ms = jnp.mean(yg * yg, axis=-1, keepdims=True)
        y_ref[:, g * gs:(g + 1) * gs] = (yg * lax.rsqrt(ms + RMS_EPS) * nw_ref[:, g * gs:(g + 1) * gs]).astype(BF16)

    @pl.when(c_idx == pl.num_programs(1) - 1)
    def _():
        st_ref[...] = hst[...]


def _ssd_prefill(proj, bsz, seq, lw):
    cs = SSD_CHUNK
    assert seq % cs == 0
    nc = seq // cs
    row = lambda b, c: b * nc + c
    pcol = lambda width, off: pl.BlockSpec((cs, width), lambda b, c: (row(b, c), off // width))
    full = lambda a: pl.BlockSpec(a.shape, lambda b, c: (0,) * a.ndim)
    weights = [lw['gate_bias'], lw['gate_alog'], lw['ssd_d_exp'], lw['ssd_norm_w']]
    y, st = pl.pallas_call(
        _ssd_prefill_kernel,
        grid=(bsz, nc),
        in_specs=[pcol(SSD_INNER, COL_Z_SSD), pcol(SSD_INNER, COL_XS), pcol(SSD_BC, COL_B), pcol(SSD_BC, COL_C),
                  pcol(LANES, COL_SMALL)] + [full(w) for w in weights],
        out_specs=[pl.BlockSpec((cs, SSD_INNER), lambda b, c: (row(b, c), 0)),
                   pl.BlockSpec((None, SSD_INNER, SSD_STATE), lambda b, c: (b, 0, 0))],
        out_shape=[jax.ShapeDtypeStruct((bsz * seq, SSD_INNER), BF16),
                   jax.ShapeDtypeStruct((bsz, SSD_INNER, SSD_STATE), F32)],
        scratch_shapes=[pltpu.VMEM((SSD_INNER, SSD_STATE), F32),
                        pltpu.VMEM((cs, SSD_INNER), F32)],
        compiler_params=_params("parallel", "arbitrary"),
    )(proj, proj, proj, proj, proj, *weights)
    return y, st


INV_BASE = 16


def _same_block(cs, size):
    r = lax.broadcasted_iota(jnp.int32, (cs, cs), 0) // size
    c = lax.broadcasted_iota(jnp.int32, (cs, cs), 1) // size
    return r == c


def _bdot(a16, b16):
    return jnp.dot(a16, b16, preferred_element_type=F32)


def _gdn_prefill_kernel(q_ref, k_ref, v_ref, z_ref, sm_ref,
                        bias_ref, alog_ref, nw_ref,
                        o_ref, st_ref,
                        sst,
                        qd_s, kd_s, rhs_s, a_s, attn_s, x_s, d_s, u_s, w_s, glast_s):
    c_idx = pl.program_id(1)
    first = c_idx == 0
    cs = q_ref.shape[0]
    heads = range(GDN_HEADS)

    @pl.when(first)
    def _():
        sst[...] = jnp.zeros(sst.shape, F32)

    qc, kc, vc = q_ref, k_ref, v_ref

    _, sg, g_step = _gate_rows(sm_ref[...], bias_ref[...], alog_ref[...])
    causal = _causal_mask(cs)
    strict = _causal_mask(cs, strict=True)
    gc = _mm_rhs_split(jnp.where(causal, 1.0, 0.0), g_step)
    gc_row = gc.T
    glast_s[...] = gc[cs - 1:cs, :]

    for h in heads:
        lo = h * GDN_HEAD_K
        qh = qc[:, lo:lo + GDN_HEAD_K]
        kh = kc[:, lo:lo + GDN_HEAD_K]
        vh = vc[:, lo:lo + GDN_HEAD_V]
        qn = qh * lax.rsqrt(jnp.sum(qh * qh, axis=-1, keepdims=True) + L2_EPS) * (GDN_HEAD_K ** -0.5)
        kn = kh * lax.rsqrt(jnp.sum(kh * kh, axis=-1, keepdims=True) + L2_EPS)
        gi = _col(gc, LANE_G + h)
        bi = _col(sg, LANE_BETA + h)
        gj = gc_row[LANE_G + h:LANE_G + h + 1, :]
        decay = jnp.exp(jnp.where(causal, gi - gj, -jnp.inf))
        egi = jnp.exp(gi)
        kb = kn * bi
        kn16 = kn.astype(BF16)
        a_s[h] = jnp.where(strict, lax.dot_general(kb.astype(BF16), kn16, (((1,), (1,)), ((), ())),
                                                   preferred_element_type=F32) * decay, 0.0)
        attn_s[h] = (lax.dot_general(qn.astype(BF16), kn16, (((1,), (1,)), ((), ())),
                                     preferred_element_type=F32) * decay).astype(BF16)
        qd_s[:, lo:lo + GDN_HEAD_K] = (qn * egi).astype(BF16)
        kd_s[:, lo:lo + GDN_HEAD_K] = (kn * jnp.exp(gi[cs - 1:cs, :] - gi)).astype(BF16)
        rhs_s[:, 2 * lo:2 * lo + GDN_HEAD_V] = (vh * bi).astype(BF16)
        rhs_s[:, 2 * lo + GDN_HEAD_V:2 * lo + 2 * GDN_HEAD_V] = (kb * egi).astype(BF16)

    base = _same_block(cs, INV_BASE)
    eye = jnp.where(causal & jnp.logical_not(strict), 1.0, 0.0)
    for h in heads:
        x = jnp.where(base, -a_s[h], 0.0)
        x_s[h] = x.astype(BF16)
        d_s[h] = eye + x
    span = 2
    while span < INV_BASE:
        for h in heads:
            x16 = x_s[h]
            x2 = _bdot(x16, x16).astype(BF16)
            x_s[h] = x2
            d = d_s[h]
            d_s[h] = d + _bdot(d.astype(BF16), x2)
        span *= 2
    size = INV_BASE
    while size < cs:
        off = _same_block(cs, 2 * size) & jnp.logical_not(_same_block(cs, size))
        for h in heads:
            d = d_s[h]
            d16 = d.astype(BF16)
            low = jnp.where(off, a_s[h], 0.0).astype(BF16)
            d_s[h] = d - _bdot(_bdot(d16, low).astype(BF16), d16)
        size *= 2

    for h in heads:
        lo = h * GDN_HEAD_K
        uw = _bdot(d_s[h].astype(BF16), rhs_s[:, 2 * lo:2 * lo + 2 * GDN_HEAD_V])
        u_s[:, lo:lo + GDN_HEAD_V] = uw[:, :GDN_HEAD_V]
        w_s[:, lo:lo + GDN_HEAD_V] = uw[:, GDN_HEAD_V:].astype(BF16)

    for h in heads:
        lo = h * GDN_HEAD_K
        s = sst[lo:lo + GDN_HEAD_K, :]
        s16 = s.astype(BF16)
        v_new = u_s[:, lo:lo + GDN_HEAD_V] - _bdot(w_s[:, lo:lo + GDN_HEAD_V], s16)
        v16 = v_new.astype(BF16)
        o_h = _bdot(qd_s[:, lo:lo + GDN_HEAD_K], s16) + _bdot(attn_s[h], v16)
        g_last = _col(glast_s[...], LANE_G + h)
        sst[lo:lo + GDN_HEAD_K, :] = s * jnp.exp(g_last) + lax.dot_general(
            kd_s[:, lo:lo + GDN_HEAD_K], v16, (((0,), (0,)), ((), ())), preferred_element_type=F32)
        ms = jnp.mean(o_h * o_h, axis=-1, keepdims=True)
        o_ref[:, lo:lo + GDN_HEAD_V] = ((o_h * lax.rsqrt(ms + RMS_EPS) * nw_ref[...]) *
                                        _silu(z_ref[:, lo:lo + GDN_HEAD_V])).astype(BF16)

    @pl.when(c_idx == pl.num_programs(1) - 1)
    def _():
        st_ref[...] = sst[...]


def _gdn_prefill(proj, bsz, seq, lw):
    cs = GDN_CHUNK
    assert seq % cs == 0
    nc = seq // cs
    row = lambda b, c: b * nc + c
    pcol = lambda width, off: pl.BlockSpec((cs, width), lambda b, c: (row(b, c), off // width))
    full = lambda a: pl.BlockSpec(a.shape, lambda b, c: (0,) * a.ndim)
    weights = [lw['gate_bias'], lw['gate_alog'], lw['gdn_norm_w']]
    o, st = pl.pallas_call(
        _gdn_prefill_kernel,
        grid=(bsz, nc),
        in_specs=[pcol(GDN_DIM, COL_Q), pcol(GDN_DIM, COL_K), pcol(GDN_DIM, COL_V), pcol(GDN_DIM, COL_Z_GDN),
                  pcol(LANES, COL_SMALL)] + [full(w) for w in weights],
        out_specs=[pl.BlockSpec((cs, GDN_DIM), lambda b, c: (row(b, c), 0)),
                   pl.BlockSpec((None, GDN_HEADS * GDN_HEAD_K, GDN_HEAD_V), lambda b, c: (b, 0, 0))],
        out_shape=[jax.ShapeDtypeStruct((bsz * seq, GDN_DIM), BF16),
                   jax.ShapeDtypeStruct((bsz, GDN_HEADS * GDN_HEAD_K, GDN_HEAD_V), F32)],
        scratch_shapes=[pltpu.VMEM((GDN_HEADS * GDN_HEAD_K, GDN_HEAD_V), F32),
                        pltpu.VMEM((cs, GDN_DIM), BF16),
                        pltpu.VMEM((cs, GDN_DIM), BF16),
                        pltpu.VMEM((cs, 2 * GDN_DIM), BF16),
                        pltpu.VMEM((GDN_HEADS, cs, cs), F32),
                        pltpu.VMEM((GDN_HEADS, cs, cs), BF16),
                        pltpu.VMEM((GDN_HEADS, cs, cs), BF16),
                        pltpu.VMEM((GDN_HEADS, cs, cs), F32),
                        pltpu.VMEM((cs, GDN_DIM), F32),
                        pltpu.VMEM((cs, GDN_DIM), BF16),
                        pltpu.VMEM((1, LANES), F32)],
        compiler_params=_params("parallel", "arbitrary"),
    )(proj, proj, proj, proj, proj, *weights)
    return o, st


def _softmax_rows(s):
    m = jnp.max(s, axis=-1, keepdims=True)
    e = jnp.exp(s - m)
    return e / jnp.sum(e, axis=-1, keepdims=True)


def _xattn_prefill_kernel(x_ref, nw_ref, wq_ref, mk_ref, mv_ref, wo_ref, o_ref):
    x = x_ref[...]
    ms = jnp.mean(x * x, axis=-1, keepdims=True)
    h = x * lax.rsqrt(ms + RMS_EPS) * nw_ref[...]
    q = jnp.dot(h.astype(BF16), wq_ref[...], preferred_element_type=F32)
    outs = []
    for hd in range(XA_HEADS):
        lo = hd * XA_HEAD_DIM
        s = _mm_nt(q[:, lo:lo + XA_HEAD_DIM], mk_ref[:, lo:lo + XA_HEAD_DIM]) * (XA_HEAD_DIM ** -0.5)
        outs.append(_mm(_softmax_rows(s), mv_ref[:, lo:lo + XA_HEAD_DIM]))
    att = jnp.concatenate(outs, axis=-1)
    o_ref[...] = x + jnp.dot(att.astype(BF16), wo_ref[...], preferred_element_type=F32)


def _xattn_prefill(x, nw, wq, mk, mv, wo, bsz, seq, tq):
    n_mem = mk.shape[1]
    nt = seq // tq
    return pl.pallas_call(
        _xattn_prefill_kernel,
        grid=(bsz, nt),
        in_specs=[pl.BlockSpec((tq, D_MODEL), lambda b, t: (b * nt + t, 0)),
                  pl.BlockSpec((1, D_MODEL), lambda b, t: (0, 0)),
                  pl.BlockSpec((D_MODEL, D_MODEL), lambda b, t: (0, 0)),
                  pl.BlockSpec((None, n_mem, D_MODEL), lambda b, t: (b, 0, 0)),
                  pl.BlockSpec((None, n_mem, D_MODEL), lambda b, t: (b, 0, 0)),
                  pl.BlockSpec((D_MODEL, D_MODEL), lambda b, t: (0, 0))],
        out_specs=pl.BlockSpec((tq, D_MODEL), lambda b, t: (b * nt + t, 0)),
        out_shape=jax.ShapeDtypeStruct(x.shape, F32),
        compiler_params=_params("parallel", "arbitrary"),
    )(x, nw.reshape(1, D_MODEL), wq, mk, mv, wo)


def _xattn_decode_kernel(q_ref, mk_ref, mv_ref, o_ref):
    tb = q_ref.shape[0]
    for i in range(tb):
        q = q_ref[i]
        qh = jnp.concatenate([q[:, hd * XA_HEAD_DIM:(hd + 1) * XA_HEAD_DIM] for hd in range(XA_HEADS)], axis=0)
        s = jnp.sum(mk_ref[i] * qh[None], axis=-1, keepdims=True) * (XA_HEAD_DIM ** -0.5)
        e = jnp.exp(s - jnp.max(s, axis=0, keepdims=True))
        p = e / jnp.sum(e, axis=0, keepdims=True)
        o_ref[i] = jnp.sum(p * mv_ref[i], axis=0)


def _xattn_decode(q, mk_all, mv_all, layer, tb):
    _, nb, n_mem, nh, hd = mk_all.shape
    return pl.pallas_call(
        _xattn_decode_kernel,
        grid=(nb // tb,),
        in_specs=[pl.BlockSpec((tb, 1, D_MODEL), lambda i: (i, 0, 0)),
                  pl.BlockSpec((None, tb, n_mem, nh, hd), lambda i: (layer, i, 0, 0, 0)),
                  pl.BlockSpec((None, tb, n_mem, nh, hd), lambda i: (layer, i, 0, 0, 0))],
        out_specs=pl.BlockSpec((tb, nh, hd), lambda i: (i, 0, 0)),
        out_shape=jax.ShapeDtypeStruct((nb, nh, hd), F32),
        compiler_params=_params("parallel"),
    )(q.reshape(nb, 1, D_MODEL), mk_all, mv_all).reshape(nb, D_MODEL)


def _ffn_prefill_kernel(x_ref, nw_ref, wgu_ref, cw_ref, cb_ref, wd_ref, o_ref, tail_ref,
                        ext_ref, hist_ref, hm_ref, *, tiles_per_seq):
    i = pl.program_id(0)
    tm = x_ref.shape[0]
    d_ff = wd_ref.shape[0]

    @pl.when(i == 0)
    def _():
        hist_ref[...] = jnp.zeros(hist_ref.shape, F32)

    x = x_ref[...]
    ms = jnp.mean(x * x, axis=-1, keepdims=True)
    xn = (x * lax.rsqrt(ms + RMS_EPS) * nw_ref[...]).astype(BF16)
    starts_seq = i % tiles_per_seq == 0
    for ci, lo in enumerate(range(0, d_ff, MXU_COLS)):
        cols = slice(lo, min(lo + MXU_COLS, d_ff))
        width = cols.stop - cols.start
        gate = jnp.dot(xn, wgu_ref[:, cols], preferred_element_type=F32)
        up = jnp.dot(xn, wgu_ref[:, d_ff + cols.start:d_ff + cols.stop], preferred_element_type=F32)
        tail_ref[:, cols] = gate[tm - HIST_ROWS:tm, :]
        buf = ext_ref.at[ci % CONV_BUFS]
        buf[0:HIST_ROWS, 0:width] = jnp.where(starts_seq, 0.0, hist_ref[:, cols])
        buf[HIST_ROWS:HIST_ROWS + tm, 0:width] = gate
        hist_ref[:, cols] = gate[tm - HIST_ROWS:tm, :]
        c = cb_ref[:, cols]
        for k in range(FFN_CONV):
            start = HIST_ROWS - (FFN_CONV - 1) + k
            c = c + buf[start:start + tm, 0:width] * cw_ref[k:k + 1, cols]
        hm_ref[:, cols] = (_silu(c) * up).astype(BF16)
    o_ref[...] = x + jnp.dot(hm_ref[...], wd_ref[...], preferred_element_type=F32)


def _ffn_prefill(x, nw, wgu, cw, cb, wd, seq, tm):
    m = x.shape[0]
    d_ff = wd.shape[0]
    assert m % tm == 0 and seq % tm == 0 and tm >= HIST_ROWS
    whole = lambda a: pl.BlockSpec(a.shape, lambda i: (0, 0), pipeline_mode=pl.Buffered(1))
    ins = [nw.reshape(1, D_MODEL), wgu, cw, cb.reshape(1, d_ff), wd]
    return pl.pallas_call(
        functools.partial(_ffn_prefill_kernel, tiles_per_seq=seq // tm),
        grid=(m // tm,),
        in_specs=[pl.BlockSpec((tm, D_MODEL), lambda i: (i, 0))] + [whole(a) for a in ins],
        out_specs=[pl.BlockSpec((tm, D_MODEL), lambda i: (i, 0)),
                   pl.BlockSpec((None, HIST_ROWS, d_ff), lambda i: (i, 0, 0))],
        out_shape=[jax.ShapeDtypeStruct((m, D_MODEL), F32),
                   jax.ShapeDtypeStruct((m // tm, HIST_ROWS, d_ff), F32)],
        scratch_shapes=[pltpu.VMEM((CONV_BUFS, HIST_ROWS + tm, MXU_COLS), F32),
                        pltpu.VMEM((HIST_ROWS, d_ff), F32),
                        pltpu.VMEM((tm, d_ff), BF16)],
        compiler_params=_params("arbitrary"),
    )(x, *ins)


def _ffn_decode_kernel(x_ref, g_ref, h0_ref, h1_ref, u_ref, cw_ref, cb_ref, wd_ref, o_ref):
    acc = cb_ref[...] + h0_ref[...] * cw_ref[0:1, :] + h1_ref[...] * cw_ref[1:2, :] + g_ref[...] * cw_ref[2:3, :]
    hmid = _silu(acc) * u_ref[...]
    o_ref[...] = x_ref[...] + jnp.dot(hmid.astype(BF16), wd_ref[...], preferred_element_type=F32)


def _ffn_decode(x, gu, hist0, hist1, cw, cb, wd):
    m = x.shape[0]
    d_ff = wd.shape[0]
    rows = lambda width, col: pl.BlockSpec((m, width), lambda i: (0, col))
    return pl.pallas_call(
        _ffn_decode_kernel,
        grid=(1,),
        in_specs=[rows(D_MODEL, 0), rows(d_ff, 0), rows(d_ff, 0), rows(d_ff, 0), rows(d_ff, 1),
                  pl.BlockSpec((FFN_CONV, d_ff), lambda i: (0, 0)),
                  pl.BlockSpec((1, d_ff), lambda i: (0, 0)),
                  pl.BlockSpec((d_ff, D_MODEL), lambda i: (0, 0))],
        out_specs=rows(D_MODEL, 0),
        out_shape=jax.ShapeDtypeStruct((m, D_MODEL), F32),
        compiler_params=_params("arbitrary"),
    )(x, gu, hist0, hist1, gu, cw, cb.reshape(1, d_ff), wd)


def _expand_lanes(rows, first_lane, width, n_out, parts):
    k = lax.broadcasted_iota(jnp.int32, (LANES, n_out), 0)
    l = lax.broadcasted_iota(jnp.int32, (LANES, n_out), 1)
    sel = jnp.where(l // width == k - first_lane, 1.0, 0.0)
    return _mm_lhs_split(rows, sel, parts)


def _decode_rows_kernel(proj_ref, sh0, sh1, sh2, gh0, gh1, gh2,
                        scw_ref, scb_ref, gcw_ref, bias_ref, alog_ref,
                        xs_ref, xdt_ref, b_ref, c_ref, sdec_ref,
                        q_ref, k_ref, v_ref, beta_ref, gdec_ref):
    def conv(hists, new, cw, cb):
        acc = new * cw[3:4, :]
        for t, hr in enumerate(hists):
            acc = acc + hr[...] * cw[t:t + 1, :]
        return acc if cb is None else acc + cb

    xbc_new = jnp.concatenate([proj_ref[:, COL_XS:COL_XS + SSD_INNER],
                               proj_ref[:, COL_B:COL_B + SSD_BC],
                               proj_ref[:, COL_C:COL_C + SSD_BC]], axis=-1)
    xbc = _silu(conv((sh0, sh1, sh2), xbc_new, scw_ref, scb_ref[...]))
    sp, sg, log_dec = _gate_rows(proj_ref[:, COL_SMALL:COL_SMALL + LANES], bias_ref[...], alog_ref[...])
    xs = xbc[:, 0:SSD_INNER]
    xs_ref[...] = xs
    xdt_ref[...] = xs * _expand_lanes(sp, LANE_DT, SSD_HEAD_DIM, SSD_INNER, 3)
    b_ref[...] = xbc[:, SSD_INNER:SSD_INNER + SSD_BC]
    c_ref[...] = xbc[:, SSD_INNER + SSD_BC:SSD_INNER + 2 * SSD_BC]
    sdec_ref[...] = jnp.exp(_expand_lanes(log_dec, LANE_DT, SSD_HEAD_DIM, SSD_INNER, 3))

    qkv_new = proj_ref[:, COL_Q:COL_Q + 3 * GDN_DIM]
    qkv = _silu(conv((gh0, gh1, gh2), qkv_new, gcw_ref, None))
    for h in range(GDN_HEADS):
        lo = h * GDN_HEAD_K
        qh = qkv[:, lo:lo + GDN_HEAD_K]
        kh = qkv[:, GDN_DIM + lo:GDN_DIM + lo + GDN_HEAD_K]
        q_ref[:, lo:lo + GDN_HEAD_K] = qh * lax.rsqrt(jnp.sum(qh * qh, axis=-1, keepdims=True) + L2_EPS) * \
            (GDN_HEAD_K ** -0.5)
        k_ref[:, lo:lo + GDN_HEAD_K] = kh * lax.rsqrt(jnp.sum(kh * kh, axis=-1, keepdims=True) + L2_EPS)
    v_ref[...] = qkv[:, 2 * GDN_DIM:3 * GDN_DIM]
    beta_ref[...] = _expand_lanes(sg, LANE_BETA, GDN_HEAD_V, GDN_DIM, 3)
    gdec_ref[...] = jnp.exp(_expand_lanes(log_dec, LANE_G, GDN_HEAD_V, GDN_DIM, 3))


def _decode_rows(proj, ssd_hist, gdn_hist, lw):
    m = proj.shape[0]
    ins = [proj] + [ssd_hist[:, t] for t in range(SSD_CONV - 1)] + [gdn_hist[:, t] for t in range(GDN_CONV - 1)]
    ins += [lw['ssd_conv_w'], lw['ssd_conv_b'], lw['gdn_conv_w'], lw['gate_bias'], lw['gate_alog']]
    widths = [SSD_INNER, SSD_INNER, SSD_BC, SSD_BC, SSD_INNER, GDN_DIM, GDN_DIM, GDN_DIM, GDN_DIM, GDN_DIM]
    return pl.pallas_call(
        _decode_rows_kernel,
        grid=(1,),
        in_specs=[pl.BlockSpec(a.shape, lambda i, nd=a.ndim: (0,) * nd) for a in ins],
        out_specs=[pl.BlockSpec((m, w), lambda i: (0, 0)) for w in widths],
        out_shape=[jax.ShapeDtypeStruct((m, w), F32) for w in widths],
        compiler_params=_params("arbitrary"),
    )(*ins)


def _rows_to_cols(rows, i, parts):
    tb = rows.shape[0]
    r = lax.broadcasted_iota(jnp.int32, (tb, LANES), 0)
    sel = jnp.where(r == i, 1.0, 0.0).astype(BF16)
    acc = None
    for p in _split(rows, parts):
        t = lax.dot_general(p, sel, (((0,), (0,)), ((), ())), preferred_element_type=F32)
        acc = t if acc is None else acc + t
    return acc


def _only_row(rows, i):
    r = lax.broadcasted_iota(jnp.int32, rows.shape, 0)
    return jnp.where(r == i, rows, 0.0)


def _decode_state_kernel(xdt_ref, b_ref, c_ref, sdec_ref, q_ref, k_ref, v_ref, beta_ref, gdec_ref,
                         hs_ref, ss_ref, *rest):
    y_ref, o_ref, hs_out, ss_out = rest[-4:]
    tb = xdt_ref.shape[0]
    xdt = xdt_ref[...]
    bm = b_ref[...]
    cm = c_ref[...]
    sdec = sdec_ref[...]
    qn = q_ref[...]
    kn = k_ref[...]
    vv = v_ref[...]
    beta = beta_ref[...]
    gdec = gdec_ref[...]
    rows_g = SSD_INNER // SSD_GROUPS

    y_acc = jnp.zeros((tb, SSD_INNER), F32)
    ks_acc = jnp.zeros((tb, GDN_DIM), F32)
    for i in range(tb):
        dec_c = _rows_to_cols(sdec, i, 2)
        xdt_c = _rows_to_cols(xdt, i, 1)
        pieces = []
        for g in range(SSD_GROUPS):
            rs = slice(g * rows_g, (g + 1) * rows_g)
            bg = bm[i:i + 1, g * SSD_STATE:(g + 1) * SSD_STATE]
            h_new = hs_ref[i, rs, :] * dec_c[rs, :] + xdt_c[rs, :] * bg
            hs_out[i, rs, :] = h_new
            cg = _only_row(cm[:, g * SSD_STATE:(g + 1) * SSD_STATE], i)
            pieces.append(_mm_nt(cg, h_new))
        y_acc = y_acc + jnp.concatenate(pieces, axis=-1)
        g_c = _rows_to_cols(gdec, i, 2)
        s_dec = ss_ref[i] * g_c
        ss_out[i] = s_dec
        pieces = []
        for h in range(GDN_HEADS):
            rs = slice(h * GDN_HEAD_K, (h + 1) * GDN_HEAD_K)
            pieces.append(_mm(_only_row(kn[:, rs], i), s_dec[rs, :]))
        ks_acc = ks_acc + jnp.concatenate(pieces, axis=-1)
    y_ref[...] = y_acc

    delta = beta * (vv - ks_acc)
    o_acc = jnp.zeros((tb, GDN_DIM), F32)
    for i in range(tb):
        k_c = _rows_to_cols(kn, i, 1)
        pieces = []
        for h in range(GDN_HEADS):
            rs = slice(h * GDN_HEAD_K, (h + 1) * GDN_HEAD_K)
            d_row = delta[i:i + 1, rs]
            s_new = ss_out[i, rs, :] + k_c[rs, :] * d_row
            ss_out[i, rs, :] = s_new
            pieces.append(_mm(_only_row(qn[:, rs], i), s_new))
        o_acc = o_acc + jnp.concatenate(pieces, axis=-1)
    o_ref[...] = o_acc


def _decode_state(rows, hs_all, ss_all, layer, tb, carried):
    xs, xdt, bm, cm, sdec, qn, kn, vv, beta, gdec = rows
    nb = xdt.shape[0]
    row_ins = [xdt, bm, cm, sdec, qn, kn, vv, beta, gdec]
    rspec = lambda a: pl.BlockSpec((tb, a.shape[1]), lambda i: (i, 0))
    sspec = lambda a: pl.BlockSpec((None, tb) + a.shape[2:], lambda i: (layer, i, 0, 0))
    n_in = len(row_ins) + 2
    aliases = {n_in + j: 2 + j for j in range(len(carried))}
    return pl.pallas_call(
        _decode_state_kernel,
        grid=(nb // tb,),
        in_specs=[rspec(a) for a in row_ins] + [sspec(hs_all), sspec(ss_all)] +
                 [pl.BlockSpec(memory_space=pl.ANY) for _ in carried],
        out_specs=[rspec(xdt), rspec(qn), sspec(hs_all), sspec(ss_all)],
        out_shape=[jax.ShapeDtypeStruct(xdt.shape, F32), jax.ShapeDtypeStruct(qn.shape, F32),
                   jax.ShapeDtypeStruct(hs_all.shape, F32), jax.ShapeDtypeStruct(ss_all.shape, F32)],
        input_output_aliases=aliases,
        compiler_params=_params("parallel"),
    )(*row_ins, hs_all, ss_all, *carried)


def _decode_out_kernel(x_ref, y_ref, xs_ref, o_ref, proj_ref, dexp_ref, snw_ref, gnw_ref, w1_ref, w2_ref, out_ref):
    y = y_ref[...] + dexp_ref[...] * xs_ref[...]
    gs = SSD_INNER // SSD_GROUPS
    ys = []
    for g in range(SSD_GROUPS):
        yg = y[:, g * gs:(g + 1) * gs] * _silu(proj_ref[:, COL_Z_SSD + g * gs:COL_Z_SSD + (g + 1) * gs])
        ms = jnp.mean(yg * yg, axis=-1, keepdims=True)
        ys.append(yg * lax.rsqrt(ms + RMS_EPS) * snw_ref[:, g * gs:(g + 1) * gs])
    os = []
    for h in range(GDN_HEADS):
        lo = h * GDN_HEAD_V
        oh = o_ref[:, lo:lo + GDN_HEAD_V]
        ms = jnp.mean(oh * oh, axis=-1, keepdims=True)
        os.append(oh * lax.rsqrt(ms + RMS_EPS) * gnw_ref[...] *
                  _silu(proj_ref[:, COL_Z_GDN + lo:COL_Z_GDN + lo + GDN_HEAD_V]))
    yn = jnp.concatenate(ys, axis=-1)
    on = jnp.concatenate(os, axis=-1)
    out_ref[...] = x_ref[...] + _mm(yn, w1_ref[...]) + _mm(on, w2_ref[...])


def _decode_out(x, y, xs, o, proj, lw):
    ins = [x, y, xs, o, proj, lw['ssd_d_exp'], lw['ssd_norm_w'], lw['gdn_norm_w'], lw['w_out_ssd'], lw['w_out_gdn']]
    return pl.pallas_call(
        _decode_out_kernel,
        grid=(1,),
        in_specs=[pl.BlockSpec(a.shape, lambda i, nd=a.ndim: (0,) * nd) for a in ins],
        out_specs=pl.BlockSpec(x.shape, lambda i: (0, 0)),
        out_shape=jax.ShapeDtypeStruct(x.shape, F32),
        compiler_params=_params("arbitrary"),
    )(*ins)


def _identity_taps(taps, n):
    return jnp.concatenate([jnp.zeros((taps - 1, n), F32), jnp.ones((1, n), F32)], axis=0)


def _prep_layer(i, p):
    w_in = p['w_in'][i]
    o_zs = 0
    o_xbc = o_zs + SSD_INNER
    o_dt = o_xbc + SSD_INNER + 2 * SSD_BC
    o_qkv = o_dt + SSD_HEADS
    o_zg = o_qkv + 3 * GDN_DIM
    o_b = o_zg + GDN_DIM
    o_a = o_b + GDN_HEADS
    used = SSD_HEADS + 2 * GDN_HEADS
    w_perm = jnp.concatenate([
        w_in[:, o_zs:o_zs + SSD_INNER],
        w_in[:, o_xbc:o_xbc + SSD_INNER],
        w_in[:, o_qkv:o_qkv + 3 * GDN_DIM],
        w_in[:, o_zg:o_zg + GDN_DIM],
        w_in[:, o_xbc + SSD_INNER:o_xbc + SSD_INNER + 2 * SSD_BC],
        w_in[:, o_dt:o_dt + SSD_HEADS],
        w_in[:, o_b:o_b + GDN_HEADS],
        w_in[:, o_a:o_a + GDN_HEADS],
        jnp.zeros((D_MODEL, PROJ_COLS - COL_SMALL - used), F32)], axis=1).astype(BF16)
    lane_pad = jnp.zeros((LANES - used,), F32)
    scw = p['ssd_conv_w'][i]
    scb = p['ssd_conv_b'][i].reshape(1, -1)
    gcw = p['gdn_conv_w'][i]
    return {
        'w_in': w_perm,
        'norm_mix_w': p['norm_mix_w'][i],
        'ssd_conv_w': scw, 'ssd_conv_b': scb, 'gdn_conv_w': gcw,
        'in_cw': jnp.concatenate([_identity_taps(SSD_CONV, SSD_INNER), scw[:, :SSD_INNER], gcw,
                                  _identity_taps(GDN_CONV, GDN_DIM), scw[:, SSD_INNER:],
                                  _identity_taps(SSD_CONV, PROJ_COLS - COL_SMALL)], axis=1),
        'in_cb': jnp.concatenate([jnp.zeros((1, COL_XS), F32), scb[:, :SSD_INNER],
                                  jnp.zeros((1, COL_B - COL_Q), F32), scb[:, SSD_INNER:],
                                  jnp.zeros((1, PROJ_COLS - COL_SMALL), F32)], axis=1),
        'gate_bias': jnp.concatenate([p['ssd_dt_bias'][i], jnp.zeros((GDN_HEADS,), F32), p['gdn_dt_bias'][i],
                                      lane_pad]).reshape(1, LANES),
        'gate_alog': jnp.concatenate([p['ssd_a_log'][i], jnp.zeros((GDN_HEADS,), F32), p['gdn_a_log'][i],
                                      lane_pad]).reshape(1, LANES),
        'ssd_d_exp': jnp.repeat(p['ssd_d'][i], SSD_HEAD_DIM).reshape(1, SSD_INNER),
        'ssd_norm_w': p['ssd_norm_w'][i].reshape(1, SSD_INNER),
        'gdn_norm_w': p['gdn_norm_w'][i].reshape(1, GDN_HEAD_V),
        'w_out_ssd': p['w_out'][i, :SSD_INNER].astype(BF16),
        'w_out_gdn': p['w_out'][i, SSD_INNER:].astype(BF16),
        'norm_xa_w': p['norm_xa_w'][i], 'norm_mem_w': p['norm_mem_w'][i],
        'xa_wq': p['xa_wq'][i].astype(BF16), 'xa_wo': p['xa_wo'][i].astype(BF16),
        'xa_wkv': jnp.concatenate([p['xa_wk'][i], p['xa_wv'][i]], axis=1).astype(BF16),
        'norm_ffn_w': p['norm_ffn_w'][i],
        'ffn_w_gu': jnp.concatenate([p['ffn_w_gate'][i], p['ffn_w_up'][i]], axis=1).astype(BF16),
        'ffn_conv_w': p['ffn_conv_w'][i], 'ffn_conv_b': p['ffn_conv_b'][i],
        'ffn_w_down': p['ffn_w_down'][i].astype(BF16),
    }


def _unpermute_conv_rows(proj_rows):
    ssd = jnp.concatenate([proj_rows[..., COL_XS:COL_XS + SSD_INNER],
                           proj_rows[..., COL_B:COL_B + SSD_BC],
                           proj_rows[..., COL_C:COL_C + SSD_BC]], axis=-1)
    gdn = proj_rows[..., COL_Q:COL_Q + 3 * GDN_DIM]
    return ssd, gdn


def _tile(m, pref):
    t = min(m, pref)
    while m % t:
        t //= 2
    return t


def kernel(x_prompt, x_sample, mem_prompt, cache_mem_k, cache_mem_v, state_ssd_conv, state_ssd, state_gdn_conv, state_gdn, state_ffn_conv, norm_mix_w, w_in, ssd_conv_w, ssd_conv_b, ssd_dt_bias, ssd_a_log, ssd_d, ssd_norm_w, gdn_conv_w, gdn_dt_bias, gdn_a_log, gdn_norm_w, w_out, norm_xa_w, norm_mem_w, xa_wq, xa_wk, xa_wv, xa_wo, norm_ffn_w, ffn_w_gate, ffn_w_up, ffn_conv_w, ffn_conv_b, ffn_w_down, final_norm_w):
    params = dict(norm_mix_w=norm_mix_w, w_in=w_in, ssd_conv_w=ssd_conv_w, ssd_conv_b=ssd_conv_b,
                  ssd_dt_bias=ssd_dt_bias, ssd_a_log=ssd_a_log, ssd_d=ssd_d, ssd_norm_w=ssd_norm_w,
                  gdn_conv_w=gdn_conv_w, gdn_dt_bias=gdn_dt_bias, gdn_a_log=gdn_a_log, gdn_norm_w=gdn_norm_w,
                  w_out=w_out, norm_xa_w=norm_xa_w, norm_mem_w=norm_mem_w, xa_wq=xa_wq, xa_wk=xa_wk, xa_wv=xa_wv,
                  xa_wo=xa_wo, norm_ffn_w=norm_ffn_w, ffn_w_gate=ffn_w_gate, ffn_w_up=ffn_w_up,
                  ffn_conv_w=ffn_conv_w, ffn_conv_b=ffn_conv_b, ffn_w_down=ffn_w_down)
    depth = w_in.shape[0]
    bsz, seq, _ = x_prompt.shape
    nb = x_sample.shape[0]
    n_mem = mem_prompt.shape[1]
    d_ff = ffn_w_down.shape[1]
    mp = bsz * seq

    xp = x_prompt.reshape(mp, D_MODEL)
    xs = x_sample.reshape(nb, D_MODEL)
    mem = mem_prompt.reshape(bsz * n_mem, D_MODEL)
    tm_p = _tile(seq, 1024)
    tm_c = _tile(seq, 256)
    tm_mem = _tile(bsz * n_mem, 1024)
    dec_tile = _tile(nb, DEC_TILE)

    hs_all = state_ssd.reshape(depth, nb, SSD_INNER, SSD_STATE)
    ss_all = state_gdn.reshape(depth, nb, GDN_HEADS * GDN_HEAD_K, GDN_HEAD_V)
    mk_all, mv_all = cache_mem_k, cache_mem_v
    new_states = ()

    mkp, mvp = [], []
    p_sc, p_sh, p_gc, p_gs, p_fc = [], [], [], [], []
    s_sc, s_gc, s_fc = [], [], []
    for i in range(depth):
        lw = _prep_layer(i, params)

        kv = _norm_mm(mem, lw['norm_mem_w'], lw['xa_wkv'], tm_mem)
        mk = kv[:, :D_MODEL].reshape(bsz, n_mem, D_MODEL)
        mv = kv[:, D_MODEL:].reshape(bsz, n_mem, D_MODEL)
        mkp.append(mk.reshape(bsz, n_mem, XA_HEADS, XA_HEAD_DIM))
        mvp.append(mv.reshape(bsz, n_mem, XA_HEADS, XA_HEAD_DIM))

        proj, tails = _norm_mm_conv(xp, lw['norm_mix_w'], lw['w_in'], lw['in_cw'], lw['in_cb'],
                                    [(COL_XS, COL_Z_GDN), (COL_B, COL_SMALL)], seq, tm_c)
        y, h_new = _ssd_prefill(proj, bsz, seq, lw)
        o, s_new = _gdn_prefill(proj, bsz, seq, lw)
        seq_tails = tails.reshape(bsz, seq // tm_c, HIST_ROWS, PROJ_COLS)[:, -1]
        ssd_tail, gdn_tail = _unpermute_conv_rows(seq_tails[:, HIST_ROWS - (SSD_CONV - 1):])
        p_sc.append(ssd_tail)
        p_gc.append(gdn_tail)
        p_sh.append(h_new.reshape(bsz, SSD_HEADS, SSD_HEAD_DIM, SSD_STATE))
        p_gs.append(s_new.reshape(bsz, GDN_HEADS, GDN_HEAD_K, GDN_HEAD_V))
        xp = _mm_res(xp, [y, o], [lw['w_out_ssd'], lw['w_out_gdn']], _tile(mp, 512))
        xp = _xattn_prefill(xp, lw['norm_xa_w'], lw['xa_wq'], mk, mv, lw['xa_wo'], bsz, seq, _tile(seq, 512))
        xp, gate_tails = _ffn_prefill(xp, lw['norm_ffn_w'], lw['ffn_w_gu'], lw['ffn_conv_w'], lw['ffn_conv_b'],
                                      lw['ffn_w_down'], seq, tm_c)
        p_fc.append(gate_tails.reshape(bsz, seq // tm_c, HIST_ROWS, d_ff)[:, -1, HIST_ROWS - (FFN_CONV - 1):])

        proj_s = _norm_mm(xs, lw['norm_mix_w'], lw['w_in'], nb)
        ssd_new, gdn_new = _unpermute_conv_rows(proj_s)
        s_sc.append(jnp.concatenate([state_ssd_conv[i][:, 1:], ssd_new[:, None]], axis=1))
        s_gc.append(jnp.concatenate([state_gdn_conv[i][:, 1:], gdn_new[:, None]], axis=1))
        rows = _decode_rows(proj_s, state_ssd_conv[i], state_gdn_conv[i], lw)
        y_s, o_s, hs_new, ss_new = _decode_state(rows, hs_all, ss_all, i, dec_tile, new_states)
        new_states = (hs_new, ss_new)
        xs = _decode_out(xs, y_s, rows[0], o_s, proj_s, lw)
        q_s = _norm_mm(xs, lw['norm_xa_w'], lw['xa_wq'], nb)
        att = _xattn_decode(q_s, mk_all, mv_all, i, _tile(nb, 4))
        xs = _mm_res(xs, [att], [lw['xa_wo']], nb)
        gu_s = _norm_mm(xs, lw['norm_ffn_w'], lw['ffn_w_gu'], nb)
        s_fc.append(jnp.concatenate([state_ffn_conv[i][:, 1:], gu_s[:, None, :d_ff]], axis=1))
        xs = _ffn_decode(xs, gu_s, state_ffn_conv[i][:, 0], state_ffn_conv[i][:, 1],
                         lw['ffn_conv_w'], lw['ffn_conv_b'], lw['ffn_w_down'])

    y_prompt = _rmsnorm(xp, final_norm_w, tm_p).reshape(bsz, seq, D_MODEL)
    y_sample = _rmsnorm(xs, final_norm_w, nb).reshape(nb, 1, D_MODEL)
    return (y_prompt, y_sample, jnp.stack(mkp), jnp.stack(mvp),
            jnp.stack(p_sc), jnp.stack(p_sh), jnp.stack(p_gc), jnp.stack(p_gs), jnp.stack(p_fc),
            jnp.stack(s_sc), new_states[0].reshape(state_ssd.shape), jnp.stack(s_gc),
            new_states[1].reshape(state_gdn.shape), jnp.stack(s_fc))
```

```python
import functools

import jax
import jax.numpy as jnp
from jax import lax
from jax.experimental import pallas as pl
from jax.experimental.pallas import tpu as pltpu

F32 = jnp.float32
BF16 = jnp.bfloat16

D_MODEL = 1024
SSD_HEADS = 16
SSD_HEAD_DIM = 64
SSD_STATE = 128
SSD_GROUPS = 2
SSD_INNER = SSD_HEADS * SSD_HEAD_DIM
SSD_BC = SSD_GROUPS * SSD_STATE
SSD_CONV = 4
GDN_HEADS = 8
GDN_HEAD_K = 128
GDN_HEAD_V = 128
GDN_DIM = GDN_HEADS * GDN_HEAD_V
GDN_CONV = 4
XA_HEADS = 4
XA_HEAD_DIM = D_MODEL // XA_HEADS
FFN_CONV = 3
RMS_EPS = 1e-6
L2_EPS = 1e-6

SUBLANES = 8
LANES = 128
MXU_COLS = 256
VMEM_LIMIT_BYTES = 56 * 1024 * 1024

COL_Z_SSD = 0
COL_XS = 1024
COL_Q = 2048
COL_K = 3072
COL_V = 4096
COL_Z_GDN = 5120
COL_B = 6144
COL_C = 6400
COL_SMALL = 6656
PROJ_COLS = 7168
LANE_DT = 0
LANE_BETA = 16
LANE_G = 24

SSD_CHUNK = 256
GDN_CHUNK = 128
DEC_TILE = 8


def _params(*sem):
    return pltpu.CompilerParams(dimension_semantics=sem, vmem_limit_bytes=VMEM_LIMIT_BYTES)


def _mm(a, b):
    return jnp.dot(a.astype(BF16), b.astype(BF16), preferred_element_type=F32)


def _mm_nt(a, b):
    return lax.dot_general(a.astype(BF16), b.astype(BF16), (((1,), (1,)), ((), ())),
                           preferred_element_type=F32)


def _split(x, parts):
    out = []
    r = x
    for _ in range(parts - 1):
        p = r.astype(BF16)
        out.append(p)
        r = r - p.astype(F32)
    out.append(r.astype(BF16))
    return out


def _mm_rhs_split(a_exact, b, parts=3):
    a16 = a_exact.astype(BF16)
    acc = None
    for p in _split(b, parts):
        t = jnp.dot(a16, p, preferred_element_type=F32)
        acc = t if acc is None else acc + t
    return acc


def _mm_lhs_split(a, b_exact, parts=3):
    b16 = b_exact.astype(BF16)
    acc = None
    for p in _split(a, parts):
        t = jnp.dot(p, b16, preferred_element_type=F32)
        acc = t if acc is None else acc + t
    return acc


def _silu(x):
    return x / (1.0 + jnp.exp(-x))


def _sigmoid(x):
    return 1.0 / (1.0 + jnp.exp(-x))


def _softplus(x):
    return jnp.maximum(x, 0.0) + jnp.log1p(jnp.exp(-jnp.abs(x)))


def _col(arr, lane):
    idx = lax.broadcasted_iota(jnp.int32, arr.shape, 1)
    return jnp.sum(jnp.where(idx == lane, arr, 0.0), axis=1, keepdims=True)


def _gate_rows(sm, bias, alog):
    sp = _softplus(sm + bias)
    sg = _sigmoid(sm)
    lane = lax.broadcasted_iota(jnp.int32, alog.shape, 1)
    has_decay = (lane < LANE_BETA) | ((lane >= LANE_G) & (lane < LANE_G + GDN_HEADS))
    neg_a = jnp.where(has_decay, -jnp.exp(alog), 0.0)
    return sp, sg, sp * neg_a


def _norm_mm_kernel(x_ref, nw_ref, w_ref, o_ref, xn_ref):
    @pl.when(pl.program_id(1) == 0)
    def _():
        x = x_ref[...]
        ms = jnp.mean(x * x, axis=-1, keepdims=True)
        xn_ref[...] = (x * lax.rsqrt(ms + RMS_EPS) * nw_ref[...]).astype(BF16)

    o_ref[...] = jnp.dot(xn_ref[...], w_ref[...], preferred_element_type=F32)


MAX_COL_TILE_LANES = 11 * LANES


def _col_tile(n):
    return max(t for t in range(LANES, MAX_COL_TILE_LANES + 1, LANES) if n % t == 0)


def _norm_mm(x, nw, w, tm):
    m, k = x.shape
    n = w.shape[1]
    tn = _col_tile(n)
    assert m % tm == 0
    return pl.pallas_call(
        _norm_mm_kernel,
        grid=(m // tm, n // tn),
        in_specs=[pl.BlockSpec((tm, k), lambda i, j: (i, 0)),
                  pl.BlockSpec((1, k), lambda i, j: (0, 0)),
                  pl.BlockSpec((k, tn), lambda i, j: (0, j))],
        out_specs=pl.BlockSpec((tm, tn), lambda i, j: (i, j)),
        out_shape=jax.ShapeDtypeStruct((m, n), F32),
        scratch_shapes=[pltpu.VMEM((tm, k), BF16)],
        compiler_params=_params("parallel", "arbitrary"),
    )(x, nw.reshape(1, k), w)


HIST_ROWS = SUBLANES


CONV_BUFS = 4


def _norm_mm_conv_kernel(x_ref, nw_ref, w_ref, cw_ref, cb_ref, o_ref, tail_ref, ext_ref, hist_ref,
                         *, taps, tiles_per_seq, conv_ranges):
    i = pl.program_id(0)
    tm = x_ref.shape[0]
    n = w_ref.shape[1]

    @pl.when(i == 0)
    def _():
        hist_ref[...] = jnp.zeros(hist_ref.shape, F32)

    x = x_ref[...]
    ms = jnp.mean(x * x, axis=-1, keepdims=True)
    xn = (x * lax.rsqrt(ms + RMS_EPS) * nw_ref[...]).astype(BF16)
    starts_seq = i % tiles_per_seq == 0
    for ci, lo in enumerate(range(0, n, MXU_COLS)):
        cols = slice(lo, min(lo + MXU_COLS, n))
        acc = jnp.dot(xn, w_ref[:, cols], preferred_element_type=F32)
        tail_ref[:, cols] = acc[tm - HIST_ROWS:tm, :]
        if not any(a <= cols.start and cols.stop <= b for a, b in conv_ranges):
            o_ref[:, cols] = acc
            continue
        buf = ext_ref.at[ci % CONV_BUFS]
        width = cols.stop - cols.start
        buf[0:HIST_ROWS, 0:width] = jnp.where(starts_seq, 0.0, hist_ref[:, cols])
        buf[HIST_ROWS:HIST_ROWS + tm, 0:width] = acc
        hist_ref[:, cols] = acc[tm - HIST_ROWS:tm, :]
        c = cb_ref[:, cols]
        for k in range(taps):
            start = HIST_ROWS - (taps - 1) + k
            c = c + buf[start:start + tm, 0:width] * cw_ref[k:k + 1, cols]
        o_ref[:, cols] = _silu(c)


def _norm_mm_conv(x, nw, w, cw, cb, conv_ranges, seq, tm):
    m, k = x.shape
    n = w.shape[1]
    taps = cw.shape[0]
    assert m % tm == 0 and seq % tm == 0 and tm >= HIST_ROWS
    assert all(a % MXU_COLS == 0 and b % MXU_COLS == 0 for a, b in conv_ranges)
    whole = lambda a: pl.BlockSpec(a.shape, lambda i: (0, 0), pipeline_mode=pl.Buffered(1))
    nw2 = nw.reshape(1, k)
    return pl.pallas_call(
        functools.partial(_norm_mm_conv_kernel, taps=taps, tiles_per_seq=seq // tm,
                          conv_ranges=tuple(conv_ranges)),
        grid=(m // tm,),
        in_specs=[pl.BlockSpec((tm, k), lambda i: (i, 0)), whole(nw2), whole(w), whole(cw), whole(cb)],
        out_specs=[pl.BlockSpec((tm, n), lambda i: (i, 0)),
                   pl.BlockSpec((None, HIST_ROWS, n), lambda i: (i, 0, 0))],
        out_shape=[jax.ShapeDtypeStruct((m, n), F32),
                   jax.ShapeDtypeStruct((m // tm, HIST_ROWS, n), F32)],
        scratch_shapes=[pltpu.VMEM((CONV_BUFS, HIST_ROWS + tm, MXU_COLS), F32),
                        pltpu.VMEM((HIST_ROWS, n), F32)],
        compiler_params=_params("arbitrary"),
    )(x, nw2, w, cw, cb)


def _mm_res_kernel(*refs, n_in):
    res_ref = refs[0]
    a_refs = refs[1:1 + n_in]
    w_refs = refs[1 + n_in:1 + 2 * n_in]
    o_ref = refs[1 + 2 * n_in]
    acc = res_ref[...]
    for a_ref, w_ref in zip(a_refs, w_refs):
        acc = acc + jnp.dot(a_ref[...].astype(BF16), w_ref[...], preferred_element_type=F32)
    o_ref[...] = acc


def _mm_res(res, a_list, w_list, tm):
    m, n = res.shape
    assert m % tm == 0
    n_in = len(a_list)
    in_specs = [pl.BlockSpec((tm, n), lambda i: (i, 0))]
    in_specs += [pl.BlockSpec((tm, a.shape[1]), lambda i: (i, 0)) for a in a_list]
    in_specs += [pl.BlockSpec(w.shape, lambda i: (0, 0)) for w in w_list]
    return pl.pallas_call(
        functools.partial(_mm_res_kernel, n_in=n_in),
        grid=(m // tm,),
        in_specs=in_specs,
        out_specs=pl.BlockSpec((tm, n), lambda i: (i, 0)),
        out_shape=jax.ShapeDtypeStruct((m, n), F32),
        compiler_params=_params("parallel"),
    )(res, *a_list, *w_list)


def _rmsnorm_kernel(x_ref, nw_ref, o_ref):
    x = x_ref[...]
    ms = jnp.mean(x * x, axis=-1, keepdims=True)
    o_ref[...] = x * lax.rsqrt(ms + RMS_EPS) * nw_ref[...]


def _rmsnorm(x, nw, tm):
    m, k = x.shape
    return pl.pallas_call(
        _rmsnorm_kernel,
        grid=(m // tm,),
        in_specs=[pl.BlockSpec((tm, k), lambda i: (i, 0)), pl.BlockSpec((1, k), lambda i: (0, 0))],
        out_specs=pl.BlockSpec((tm, k), lambda i: (i, 0)),
        out_shape=jax.ShapeDtypeStruct((m, k), F32),
        compiler_params=_params("parallel"),
    )(x, nw.reshape(1, k))


def _causal_mask(cs, strict=False):
    r = lax.broadcasted_iota(jnp.int32, (cs, cs), 0)
    c = lax.broadcasted_iota(jnp.int32, (cs, cs), 1)
    return (r > c) if strict else (r >= c)


def _ssd_prefill_kernel(z_ref, xs_ref, b_ref, c_ref, sm_ref,
                        bias_ref, alog_ref, dexp_ref, nw_ref,
                        y_ref, st_ref,
                        hst, y_s):
    c_idx = pl.program_id(1)
    first = c_idx == 0
    cs = xs_ref.shape[0]

    @pl.when(first)
    def _():
        hst[...] = jnp.zeros(hst.shape, F32)

    xs = xs_ref[...]
    bm = b_ref[...]
    cm = c_ref[...]

    sp, _, dta = _gate_rows(sm_ref[...], bias_ref[...], alog_ref[...])
    causal = _causal_mask(cs)
    cum = _mm_rhs_split(jnp.where(causal, 1.0, 0.0), dta)
    cum_row = cum.T

    lane_lo = lax.broadcasted_iota(jnp.int32, (1, LANES), 1) < SSD_HEAD_DIM
    row_lo = lax.broadcasted_iota(jnp.int32, (LANES, 1), 0) < SSD_HEAD_DIM
    heads_per_group = SSD_HEADS // SSD_GROUPS
    for g in range(SSD_GROUPS):
        cg = cm[:, g * SSD_STATE:(g + 1) * SSD_STATE]
        bg = bm[:, g * SSD_STATE:(g + 1) * SSD_STATE]
        cb = _mm_nt(cg, bg)
        rows_g = heads_per_group * SSD_HEAD_DIM
        y_state = _mm_nt(cg, hst[g * rows_g:(g + 1) * rows_g, :])
        for hp in range(heads_per_group // 2):
            h0 = g * heads_per_group + 2 * hp
            lo = h0 * SSD_HEAD_DIM
            xs_pair = xs[:, lo:lo + LANES]
            ci = [_col(cum, LANE_DT + h0), _col(cum, LANE_DT + h0 + 1)]
            dti = [_col(sp, LANE_DT + h0), _col(sp, LANE_DT + h0 + 1)]
            xdt_pair = xs_pair * jnp.where(lane_lo, dti[0], dti[1])
            y_pair = None
            for e in range(2):
                cj = cum_row[LANE_DT + h0 + e:LANE_DT + h0 + e + 1, :]
                seg = jnp.exp(jnp.where(causal, ci[e] - cj, -jnp.inf))
                keep = lane_lo if e == 0 else jnp.logical_not(lane_lo)
                t = _mm(cb * seg, jnp.where(keep, xdt_pair, 0.0))
                y_pair = t if y_pair is None else y_pair + t
            y_pair = y_pair + y_state[:, 2 * hp * SSD_HEAD_DIM:2 * hp * SSD_HEAD_DIM + LANES] * \
                jnp.where(lane_lo, jnp.exp(ci[0]), jnp.exp(ci[1]))
            y_pair = y_pair + dexp_ref[:, lo:lo + LANES] * xs_pair
            y_s[:, lo:lo + LANES] = y_pair
            cl = [ci[0][cs - 1:cs, :], ci[1][cs - 1:cs, :]]
            to_end = jnp.where(lane_lo, jnp.exp(cl[0] - ci[0]), jnp.exp(cl[1] - ci[1]))
            contrib = _mm((xdt_pair * to_end).T, bg)
            dec = jnp.where(row_lo, jnp.exp(cl[0]), jnp.exp(cl[1]))
            hst[lo:lo + LANES, :] = hst[lo:lo + LANES, :] * dec + contrib

    gs = SSD_INNER // SSD_GROUPS
    for g in range(SSD_GROUPS):
        yg = y_s[:, g * gs:(g + 1) * gs] * _silu(z_ref[:, g * gs:(g + 1) * gs])
        ms = jnp.mean(yg * yg, axis=-1, keepdims=True)
        y_ref[:, g * gs:(g + 1) * gs] = (yg * lax.rsqrt(ms + RMS_EPS) * nw_ref[:, g * gs:(g + 1) * gs]).astype(BF16)

    @pl.when(c_idx == pl.num_programs(1) - 1)
    def _():
        st_ref[...] = hst[...]


def _ssd_prefill(proj, bsz, seq, lw):
    cs = SSD_CHUNK
    assert seq % cs == 0
    nc = seq // cs
    row = lambda b, c: b * nc + c
    pcol = lambda width, off: pl.BlockSpec((cs, width), lambda b, c: (row(b, c), off // width))
    full = lambda a: pl.BlockSpec(a.shape, lambda b, c: (0,) * a.ndim)
    weights = [lw['gate_bias'], lw['gate_alog'], lw['ssd_d_exp'], lw['ssd_norm_w']]
    y, st = pl.pallas_call(
        _ssd_prefill_kernel,
        grid=(bsz, nc),
        in_specs=[pcol(SSD_INNER, COL_Z_SSD), pcol(SSD_INNER, COL_XS), pcol(SSD_BC, COL_B), pcol(SSD_BC, COL_C),
                  pcol(LANES, COL_SMALL)] + [full(w) for w in weights],
        out_specs=[pl.BlockSpec((cs, SSD_INNER), lambda b, c: (row(b, c), 0)),
                   pl.BlockSpec((None, SSD_INNER, SSD_STATE), lambda b, c: (b, 0, 0))],
        out_shape=[jax.ShapeDtypeStruct((bsz * seq, SSD_INNER), BF16),
                   jax.ShapeDtypeStruct((bsz, SSD_INNER, SSD_STATE), F32)],
        scratch_shapes=[pltpu.VMEM((SSD_INNER, SSD_STATE), F32),
                        pltpu.VMEM((cs, SSD_INNER), F32)],
        compiler_params=_params("parallel", "arbitrary"),
    )(proj, proj, proj, proj, proj, *weights)
    return y, st


INV_BASE = 16


def _same_block(cs, size):
    r = lax.broadcasted_iota(jnp.int32, (cs, cs), 0) // size
    c = lax.broadcasted_iota(jnp.int32, (cs, cs), 1) // size
    return r == c


def _bdot(a16, b16):
    return jnp.dot(a16, b16, preferred_element_type=F32)


def _gdn_prefill_kernel(q_ref, k_ref, v_ref, z_ref, sm_ref,
                        bias_ref, alog_ref, nw_ref,
                        o_ref, st_ref,
                        sst,
                        qd_s, kd_s, rhs_s, a_s, attn_s, x_s, d_s, u_s, w_s, glast_s, *, n_chunks):
    c_idx = pl.program_id(1)
    nseq, cs = q_ref.shape[0], q_ref.shape[1]
    n_units = nseq * GDN_HEADS
    heads = [(s, h, s * GDN_HEADS + h) for s in range(nseq) for h in range(GDN_HEADS)]
    qc, kc, vc = q_ref, k_ref, v_ref
    causal = _causal_mask(cs)
    strict = _causal_mask(cs, strict=True)

    def stage_a(par):
        sg, gc, gc_row = [], [], []
        for s in range(nseq):
            _, sg_s, g_step = _gate_rows(sm_ref[s], bias_ref[...], alog_ref[...])
            gc_s = _mm_rhs_split(jnp.where(causal, 1.0, 0.0), g_step)
            sg.append(sg_s)
            gc.append(gc_s)
            gc_row.append(gc_s.T)
            glast_s[par * nseq + s] = gc_s[cs - 1:cs, :]
        for s, h, u in heads:
            _stage_a_unit(par, s, h, u, sg, gc, gc_row)

    def _stage_a_unit(par, s, h, u, sg, gc, gc_row):
        u = par * n_units + u
        ps = par * nseq + s
        lo = h * GDN_HEAD_K
        qh = qc[s, :, lo:lo + GDN_HEAD_K]
        kh = kc[s, :, lo:lo + GDN_HEAD_K]
        vh = vc[s, :, lo:lo + GDN_HEAD_V]
        qn = qh * lax.rsqrt(jnp.sum(qh * qh, axis=-1, keepdims=True) + L2_EPS) * (GDN_HEAD_K ** -0.5)
        kn = kh * lax.rsqrt(jnp.sum(kh * kh, axis=-1, keepdims=True) + L2_EPS)
        gi = _col(gc[s], LANE_G + h)
        bi = _col(sg[s], LANE_BETA + h)
        gj = gc_row[s][LANE_G + h:LANE_G + h + 1, :]
        decay = jnp.exp(jnp.where(causal, gi - gj, -jnp.inf))
        egi = jnp.exp(gi)
        kb = kn * bi
        kn16 = kn.astype(BF16)
        a_s[u] = jnp.where(strict, lax.dot_general(kb.astype(BF16), kn16, (((1,), (1,)), ((), ())),
                                                   preferred_element_type=F32) * decay, 0.0)
        attn_s[u] = (lax.dot_general(qn.astype(BF16), kn16, (((1,), (1,)), ((), ())),
                                     preferred_element_type=F32) * decay).astype(BF16)
        qd_s[ps, :, lo:lo + GDN_HEAD_K] = (qn * egi).astype(BF16)
        kd_s[ps, :, lo:lo + GDN_HEAD_K] = (kn * jnp.exp(gi[cs - 1:cs, :] - gi)).astype(BF16)
        rhs_s[ps, :, 2 * lo:2 * lo + GDN_HEAD_V] = (vh * bi).astype(BF16)
        rhs_s[ps, :, 2 * lo + GDN_HEAD_V:2 * lo + 2 * GDN_HEAD_V] = (kb * egi).astype(BF16)

    def stages_bcd(par):
        base = _same_block(cs, INV_BASE)
        eye = jnp.where(causal & jnp.logical_not(strict), 1.0, 0.0)
        for _, _, u in heads:
            x = jnp.where(base, -a_s[par * n_units + u], 0.0)
            x_s[u] = x.astype(BF16)
            d_s[u] = eye + x
        span = 2
        while span < INV_BASE:
            for _, _, u in heads:
                x16 = x_s[u]
                x2 = _bdot(x16, x16).astype(BF16)
                x_s[u] = x2
                d = d_s[u]
                d_s[u] = d + _bdot(d.astype(BF16), x2)
            span *= 2
        size = INV_BASE
        while size < cs:
            off = _same_block(cs, 2 * size) & jnp.logical_not(_same_block(cs, size))
            for _, _, u in heads:
                d = d_s[u]
                d16 = d.astype(BF16)
                low = jnp.where(off, a_s[par * n_units + u], 0.0).astype(BF16)
                d_s[u] = d - _bdot(_bdot(d16, low).astype(BF16), d16)
            size *= 2

        for s, h, u in heads:
            lo = h * GDN_HEAD_K
            uw = _bdot(d_s[u].astype(BF16), rhs_s[par * nseq + s, :, 2 * lo:2 * lo + 2 * GDN_HEAD_V])
            u_s[s, :, lo:lo + GDN_HEAD_V] = uw[:, :GDN_HEAD_V]
            w_s[s, :, lo:lo + GDN_HEAD_V] = uw[:, GDN_HEAD_V:].astype(BF16)

        for s, h, u in heads:
            lo = h * GDN_HEAD_K
            ps = par * nseq + s
            st = sst[s, lo:lo + GDN_HEAD_K, :]
            s16 = st.astype(BF16)
            v_new = u_s[s, :, lo:lo + GDN_HEAD_V] - _bdot(w_s[s, :, lo:lo + GDN_HEAD_V], s16)
            v16 = v_new.astype(BF16)
            o_h = _bdot(qd_s[ps, :, lo:lo + GDN_HEAD_K], s16) + _bdot(attn_s[par * n_units + u], v16)
            g_last = _col(glast_s[ps], LANE_G + h)
            sst[s, lo:lo + GDN_HEAD_K, :] = st * jnp.exp(g_last) + lax.dot_general(
                kd_s[ps, :, lo:lo + GDN_HEAD_K], v16, (((0,), (0,)), ((), ())), preferred_element_type=F32)
            ms = jnp.mean(o_h * o_h, axis=-1, keepdims=True)
            o_ref[s, :, lo:lo + GDN_HEAD_V] = ((o_h * lax.rsqrt(ms + RMS_EPS) * nw_ref[...]) *
                                               _silu(z_ref[s, :, lo:lo + GDN_HEAD_V])).astype(BF16)

    @pl.when(c_idx == 0)
    def _():
        sst[...] = jnp.zeros(sst.shape, F32)
        stage_a(0)

    middle = (c_idx > 0) & (c_idx < n_chunks)
    for par in range(2):
        @pl.when(middle & (c_idx % 2 == par))
        def _(par=par):
            stages_bcd(1 - par)
            stage_a(par)

    @pl.when(c_idx == n_chunks)
    def _():
        stages_bcd((n_chunks - 1) % 2)
        st_ref[...] = sst[...]


GDN_SEQS_PER_STEP = 2


def _gdn_prefill(proj, bsz, seq, lw):
    cs = GDN_CHUNK
    nseq = GDN_SEQS_PER_STEP if bsz % GDN_SEQS_PER_STEP == 0 else 1
    assert seq % cs == 0
    nc = seq // cs
    proj3 = proj.reshape(bsz, seq, PROJ_COLS)
    cur = lambda c: jnp.minimum(c, nc - 1)
    prev = lambda c: jnp.maximum(c - 1, 0)
    pcol = lambda width, off, ch: pl.BlockSpec((nseq, cs, width), lambda b, c: (b, ch(c), off // width))
    full = lambda a: pl.BlockSpec(a.shape, lambda b, c: (0,) * a.ndim)
    weights = [lw['gate_bias'], lw['gate_alog'], lw['gdn_norm_w']]
    units = nseq * GDN_HEADS
    o, st = pl.pallas_call(
        functools.partial(_gdn_prefill_kernel, n_chunks=nc),
        grid=(bsz // nseq, nc + 1),
        in_specs=[pcol(GDN_DIM, COL_Q, cur), pcol(GDN_DIM, COL_K, cur), pcol(GDN_DIM, COL_V, cur),
                  pcol(GDN_DIM, COL_Z_GDN, prev), pcol(LANES, COL_SMALL, cur)] + [full(w) for w in weights],
        out_specs=[pl.BlockSpec((nseq, cs, GDN_DIM), lambda b, c: (b, prev(c), 0)),
                   pl.BlockSpec((nseq, GDN_HEADS * GDN_HEAD_K, GDN_HEAD_V), lambda b, c: (b, 0, 0))],
        out_shape=[jax.ShapeDtypeStruct((bsz, seq, GDN_DIM), BF16),
                   jax.ShapeDtypeStruct((bsz, GDN_HEADS * GDN_HEAD_K, GDN_HEAD_V), F32)],
        scratch_shapes=[pltpu.VMEM((nseq, GDN_HEADS * GDN_HEAD_K, GDN_HEAD_V), F32),
                        pltpu.VMEM((2 * nseq, cs, GDN_DIM), BF16),
                        pltpu.VMEM((2 * nseq, cs, GDN_DIM), BF16),
                        pltpu.VMEM((2 * nseq, cs, 2 * GDN_DIM), BF16),
                        pltpu.VMEM((2 * units, cs, cs), F32),
                        pltpu.VMEM((2 * units, cs, cs), BF16),
                        pltpu.VMEM((units, cs, cs), BF16),
                        pltpu.VMEM((units, cs, cs), F32),
                        pltpu.VMEM((nseq, cs, GDN_DIM), F32),
                        pltpu.VMEM((nseq, cs, GDN_DIM), BF16),
                        pltpu.VMEM((2 * nseq, 1, LANES), F32)],
        compiler_params=_params("parallel", "arbitrary"),
    )(proj3, proj3, proj3, proj3, proj3, *weights)
    return o.reshape(bsz * seq, GDN_DIM), st


def _softmax_rows(s):
    m = jnp.max(s, axis=-1, keepdims=True)
    e = jnp.exp(s - m)
    return e / jnp.sum(e, axis=-1, keepdims=True)


def _xattn_prefill_kernel(x_ref, nw_ref, wq_ref, mk_ref, mv_ref, wo_ref, o_ref):
    x = x_ref[...]
    ms = jnp.mean(x * x, axis=-1, keepdims=True)
    h = x * lax.rsqrt(ms + RMS_EPS) * nw_ref[...]
    q = jnp.dot(h.astype(BF16), wq_ref[...], preferred_element_type=F32)
    outs = []
    for hd in range(XA_HEADS):
        lo = hd * XA_HEAD_DIM
        s = _mm_nt(q[:, lo:lo + XA_HEAD_DIM], mk_ref[:, lo:lo + XA_HEAD_DIM]) * (XA_HEAD_DIM ** -0.5)
        outs.append(_mm(_softmax_rows(s), mv_ref[:, lo:lo + XA_HEAD_DIM]))
    att = jnp.concatenate(outs, axis=-1)
    o_ref[...] = x + jnp.dot(att.astype(BF16), wo_ref[...], preferred_element_type=F32)


def _xattn_prefill(x, nw, wq, mk, mv, wo, bsz, seq, tq):
    n_mem = mk.shape[1]
    nt = seq // tq
    return pl.pallas_call(
        _xattn_prefill_kernel,
        grid=(bsz, nt),
        in_specs=[pl.BlockSpec((tq, D_MODEL), lambda b, t: (b * nt + t, 0)),
                  pl.BlockSpec((1, D_MODEL), lambda b, t: (0, 0)),
                  pl.BlockSpec((D_MODEL, D_MODEL), lambda b, t: (0, 0)),
                  pl.BlockSpec((None, n_mem, D_MODEL), lambda b, t: (b, 0, 0)),
                  pl.BlockSpec((None, n_mem, D_MODEL), lambda b, t: (b, 0, 0)),
                  pl.BlockSpec((D_MODEL, D_MODEL), lambda b, t: (0, 0))],
        out_specs=pl.BlockSpec((tq, D_MODEL), lambda b, t: (b * nt + t, 0)),
        out_shape=jax.ShapeDtypeStruct(x.shape, F32),
        compiler_params=_params("parallel", "arbitrary"),
    )(x, nw.reshape(1, D_MODEL), wq, mk, mv, wo)


def _xattn_decode_kernel(q_ref, mk_ref, mv_ref, o_ref):
    tb = q_ref.shape[0]
    for i in range(tb):
        q = q_ref[i]
        qh = jnp.concatenate([q[:, hd * XA_HEAD_DIM:(hd + 1) * XA_HEAD_DIM] for hd in range(XA_HEADS)], axis=0)
        s = jnp.sum(mk_ref[i] * qh[None], axis=-1, keepdims=True) * (XA_HEAD_DIM ** -0.5)
        e = jnp.exp(s - jnp.max(s, axis=0, keepdims=True))
        p = e / jnp.sum(e, axis=0, keepdims=True)
        o_ref[i] = jnp.sum(p * mv_ref[i], axis=0)


def _xattn_decode(q, mk_all, mv_all, layer, tb):
    _, nb, n_mem, nh, hd = mk_all.shape
    return pl.pallas_call(
        _xattn_decode_kernel,
        grid=(nb // tb,),
        in_specs=[pl.BlockSpec((tb, 1, D_MODEL), lambda i: (i, 0, 0)),
                  pl.BlockSpec((None, tb, n_mem, nh, hd), lambda i: (layer, i, 0, 0, 0)),
                  pl.BlockSpec((None, tb, n_mem, nh, hd), lambda i: (layer, i, 0, 0, 0))],
        out_specs=pl.BlockSpec((tb, nh, hd), lambda i: (i, 0, 0)),
        out_shape=jax.ShapeDtypeStruct((nb, nh, hd), F32),
        compiler_params=_params("parallel"),
    )(q.reshape(nb, 1, D_MODEL), mk_all, mv_all).reshape(nb, D_MODEL)


def _ffn_prefill_kernel(x_ref, nw_ref, wgu_ref, cw_ref, cb_ref, wd_ref, o_ref, tail_ref,
                        ext_ref, hist_ref, hm_ref, *, tiles_per_seq):
    i = pl.program_id(0)
    tm = x_ref.shape[0]
    d_ff = wd_ref.shape[0]

    @pl.when(i == 0)
    def _():
        hist_ref[...] = jnp.zeros(hist_ref.shape, F32)

    x = x_ref[...]
    ms = jnp.mean(x * x, axis=-1, keepdims=True)
    xn = (x * lax.rsqrt(ms + RMS_EPS) * nw_ref[...]).astype(BF16)
    starts_seq = i % tiles_per_seq == 0
    for ci, lo in enumerate(range(0, d_ff, MXU_COLS)):
        cols = slice(lo, min(lo + MXU_COLS, d_ff))
        width = cols.stop - cols.start
        gate = jnp.dot(xn, wgu_ref[:, cols], preferred_element_type=F32)
        up = jnp.dot(xn, wgu_ref[:, d_ff + cols.start:d_ff + cols.stop], preferred_element_type=F32)
        tail_ref[:, cols] = gate[tm - HIST_ROWS:tm, :]
        buf = ext_ref.at[ci % CONV_BUFS]
        buf[0:HIST_ROWS, 0:width] = jnp.where(starts_seq, 0.0, hist_ref[:, cols])
        buf[HIST_ROWS:HIST_ROWS + tm, 0:width] = gate
        hist_ref[:, cols] = gate[tm - HIST_ROWS:tm, :]
        c = cb_ref[:, cols]
        for k in range(FFN_CONV):
            start = HIST_ROWS - (FFN_CONV - 1) + k
            c = c + buf[start:start + tm, 0:width] * cw_ref[k:k + 1, cols]
        hm_ref[:, cols] = (_silu(c) * up).astype(BF16)
    o_ref[...] = x + jnp.dot(hm_ref[...], wd_ref[...], preferred_element_type=F32)


def _ffn_prefill(x, nw, wgu, cw, cb, wd, seq, tm):
    m = x.shape[0]
    d_ff = wd.shape[0]
    assert m % tm == 0 and seq % tm == 0 and tm >= HIST_ROWS
    whole = lambda a: pl.BlockSpec(a.shape, lambda i: (0, 0), pipeline_mode=pl.Buffered(1))
    ins = [nw.reshape(1, D_MODEL), wgu, cw, cb.reshape(1, d_ff), wd]
    return pl.pallas_call(
        functools.partial(_ffn_prefill_kernel, tiles_per_seq=seq // tm),
        grid=(m // tm,),
        in_specs=[pl.BlockSpec((tm, D_MODEL), lambda i: (i, 0))] + [whole(a) for a in ins],
        out_specs=[pl.BlockSpec((tm, D_MODEL), lambda i: (i, 0)),
                   pl.BlockSpec((None, HIST_ROWS, d_ff), lambda i: (i, 0, 0))],
        out_shape=[jax.ShapeDtypeStruct((m, D_MODEL), F32),
                   jax.ShapeDtypeStruct((m // tm, HIST_ROWS, d_ff), F32)],
        scratch_shapes=[pltpu.VMEM((CONV_BUFS, HIST_ROWS + tm, MXU_COLS), F32),
                        pltpu.VMEM((HIST_ROWS, d_ff), F32),
                        pltpu.VMEM((tm, d_ff), BF16)],
        compiler_params=_params("arbitrary"),
    )(x, *ins)


def _ffn_decode_kernel(x_ref, g_ref, h0_ref, h1_ref, u_ref, cw_ref, cb_ref, wd_ref, o_ref):
    acc = cb_ref[...] + h0_ref[...] * cw_ref[0:1, :] + h1_ref[...] * cw_ref[1:2, :] + g_ref[...] * cw_ref[2:3, :]
    hmid = _silu(acc) * u_ref[...]
    o_ref[...] = x_ref[...] + jnp.dot(hmid.astype(BF16), wd_ref[...], preferred_element_type=F32)


def _ffn_decode(x, gu, hist0, hist1, cw, cb, wd):
    m = x.shape[0]
    d_ff = wd.shape[0]
    rows = lambda width, col: pl.BlockSpec((m, width), lambda i: (0, col))
    return pl.pallas_call(
        _ffn_decode_kernel,
        grid=(1,),
        in_specs=[rows(D_MODEL, 0), rows(d_ff, 0), rows(d_ff, 0), rows(d_ff, 0), rows(d_ff, 1),
                  pl.BlockSpec((FFN_CONV, d_ff), lambda i: (0, 0)),
                  pl.BlockSpec((1, d_ff), lambda i: (0, 0)),
                  pl.BlockSpec((d_ff, D_MODEL), lambda i: (0, 0))],
        out_specs=rows(D_MODEL, 0),
        out_shape=jax.ShapeDtypeStruct((m, D_MODEL), F32),
        compiler_params=_params("arbitrary"),
    )(x, gu, hist0, hist1, gu, cw, cb.reshape(1, d_ff), wd)


def _expand_lanes(rows, first_lane, width, n_out, parts):
    k = lax.broadcasted_iota(jnp.int32, (LANES, n_out), 0)
    l = lax.broadcasted_iota(jnp.int32, (LANES, n_out), 1)
    sel = jnp.where(l // width == k - first_lane, 1.0, 0.0)
    return _mm_lhs_split(rows, sel, parts)


def _decode_rows_kernel(proj_ref, sh0, sh1, sh2, gh0, gh1, gh2,
                        scw_ref, scb_ref, gcw_ref, bias_ref, alog_ref,
                        xs_ref, xdt_ref, b_ref, c_ref, sdec_ref,
                        q_ref, k_ref, v_ref, beta_ref, gdec_ref):
    def conv(hists, new, cw, cb):
        acc = new * cw[3:4, :]
        for t, hr in enumerate(hists):
            acc = acc + hr[...] * cw[t:t + 1, :]
        return acc if cb is None else acc + cb

    xbc_new = jnp.concatenate([proj_ref[:, COL_XS:COL_XS + SSD_INNER],
                               proj_ref[:, COL_B:COL_B + SSD_BC],
                               proj_ref[:, COL_C:COL_C + SSD_BC]], axis=-1)
    xbc = _silu(conv((sh0, sh1, sh2), xbc_new, scw_ref, scb_ref[...]))
    sp, sg, log_dec = _gate_rows(proj_ref[:, COL_SMALL:COL_SMALL + LANES], bias_ref[...], alog_ref[...])
    xs = xbc[:, 0:SSD_INNER]
    xs_ref[...] = xs
    xdt_ref[...] = xs * _expand_lanes(sp, LANE_DT, SSD_HEAD_DIM, SSD_INNER, 3)
    b_ref[...] = xbc[:, SSD_INNER:SSD_INNER + SSD_BC]
    c_ref[...] = xbc[:, SSD_INNER + SSD_BC:SSD_INNER + 2 * SSD_BC]
    sdec_ref[...] = jnp.exp(_expand_lanes(log_dec, LANE_DT, SSD_HEAD_DIM, SSD_INNER, 3))

    qkv_new = proj_ref[:, COL_Q:COL_Q + 3 * GDN_DIM]
    qkv = _silu(conv((gh0, gh1, gh2), qkv_new, gcw_ref, None))
    for h in range(GDN_HEADS):
        lo = h * GDN_HEAD_K
        qh = qkv[:, lo:lo + GDN_HEAD_K]
        kh = qkv[:, GDN_DIM + lo:GDN_DIM + lo + GDN_HEAD_K]
        q_ref[:, lo:lo + GDN_HEAD_K] = qh * lax.rsqrt(jnp.sum(qh * qh, axis=-1, keepdims=True) + L2_EPS) * \
            (GDN_HEAD_K ** -0.5)
        k_ref[:, lo:lo + GDN_HEAD_K] = kh * lax.rsqrt(jnp.sum(kh * kh, axis=-1, keepdims=True) + L2_EPS)
    v_ref[...] = qkv[:, 2 * GDN_DIM:3 * GDN_DIM]
    beta_ref[...] = _expand_lanes(sg, LANE_BETA, GDN_HEAD_V, GDN_DIM, 3)
    gdec_ref[...] = jnp.exp(_expand_lanes(log_dec, LANE_G, GDN_HEAD_V, GDN_DIM, 3))


def _decode_rows(proj, ssd_hist, gdn_hist, lw):
    m = proj.shape[0]
    ins = [proj] + [ssd_hist[:, t] for t in range(SSD_CONV - 1)] + [gdn_hist[:, t] for t in range(GDN_CONV - 1)]
    ins += [lw['ssd_conv_w'], lw['ssd_conv_b'], lw['gdn_conv_w'], lw['gate_bias'], lw['gate_alog']]
    widths = [SSD_INNER, SSD_INNER, SSD_BC, SSD_BC, SSD_INNER, GDN_DIM, GDN_DIM, GDN_DIM, GDN_DIM, GDN_DIM]
    return pl.pallas_call(
        _decode_rows_kernel,
        grid=(1,),
        in_specs=[pl.BlockSpec(a.shape, lambda i, nd=a.ndim: (0,) * nd) for a in ins],
        out_specs=[pl.BlockSpec((m, w), lambda i: (0, 0)) for w in widths],
        out_shape=[jax.ShapeDtypeStruct((m, w), F32) for w in widths],
        compiler_params=_params("arbitrary"),
    )(*ins)


def _rows_to_cols(rows, i, parts):
    tb = rows.shape[0]
    r = lax.broadcasted_iota(jnp.int32, (tb, LANES), 0)
    sel = jnp.where(r == i, 1.0, 0.0).astype(BF16)
    acc = None
    for p in _split(rows, parts):
        t = lax.dot_general(p, sel, (((0,), (0,)), ((), ())), preferred_element_type=F32)
        acc = t if acc is None else acc + t
    return acc


def _only_row(rows, i):
    r = lax.broadcasted_iota(jnp.int32, rows.shape, 0)
    return jnp.where(r == i, rows, 0.0)


def _decode_state_kernel(xdt_ref, b_ref, c_ref, sdec_ref, q_ref, k_ref, v_ref, beta_ref, gdec_ref,
                         hs_ref, ss_ref, *rest):
    y_ref, o_ref, hs_out, ss_out = rest[-4:]
    tb = xdt_ref.shape[0]
    xdt = xdt_ref[...]
    bm = b_ref[...]
    cm = c_ref[...]
    sdec = sdec_ref[...]
    qn = q_ref[...]
    kn = k_ref[...]
    vv = v_ref[...]
    beta = beta_ref[...]
    gdec = gdec_ref[...]
    rows_g = SSD_INNER // SSD_GROUPS

    y_acc = jnp.zeros((tb, SSD_INNER), F32)
    ks_acc = jnp.zeros((tb, GDN_DIM), F32)
    for i in range(tb):
        dec_c = _rows_to_cols(sdec, i, 2)
        xdt_c = _rows_to_cols(xdt, i, 1)
        pieces = []
        for g in range(SSD_GROUPS):
            rs = slice(g * rows_g, (g + 1) * rows_g)
            bg = bm[i:i + 1, g * SSD_STATE:(g + 1) * SSD_STATE]
            h_new = hs_ref[i, rs, :] * dec_c[rs, :] + xdt_c[rs, :] * bg
            hs_out[i, rs, :] = h_new
            cg = _only_row(cm[:, g * SSD_STATE:(g + 1) * SSD_STATE], i)
            pieces.append(_mm_nt(cg, h_new))
        y_acc = y_acc + jnp.concatenate(pieces, axis=-1)
        g_c = _rows_to_cols(gdec, i, 2)
        s_dec = ss_ref[i] * g_c
        ss_out[i] = s_dec
        pieces = []
        for h in range(GDN_HEADS):
            rs = slice(h * GDN_HEAD_K, (h + 1) * GDN_HEAD_K)
            pieces.append(_mm(_only_row(kn[:, rs], i), s_dec[rs, :]))
        ks_acc = ks_acc + jnp.concatenate(pieces, axis=-1)
    y_ref[...] = y_acc

    delta = beta * (vv - ks_acc)
    o_acc = jnp.zeros((tb, GDN_DIM), F32)
    for i in range(tb):
        k_c = _rows_to_cols(kn, i, 1)
        pieces = []
        for h in range(GDN_HEADS):
            rs = slice(h * GDN_HEAD_K, (h + 1) * GDN_HEAD_K)
            d_row = delta[i:i + 1, rs]
            s_new = ss_out[i, rs, :] + k_c[rs, :] * d_row
            ss_out[i, rs, :] = s_new
            pieces.append(_mm(_only_row(qn[:, rs], i), s_new))
        o_acc = o_acc + jnp.concatenate(pieces, axis=-1)
    o_ref[...] = o_acc


def _decode_state(rows, hs_all, ss_all, layer, tb, carried):
    xs, xdt, bm, cm, sdec, qn, kn, vv, beta, gdec = rows
    nb = xdt.shape[0]
    row_ins = [xdt, bm, cm, sdec, qn, kn, vv, beta, gdec]
    rspec = lambda a: pl.BlockSpec((tb, a.shape[1]), lambda i: (i, 0))
    sspec = lambda a: pl.BlockSpec((None, tb) + a.shape[2:], lambda i: (layer, i, 0, 0))
    n_in = len(row_ins) + 2
    aliases = {n_in + j: 2 + j for j in range(len(carried))}
    return pl.pallas_call(
        _decode_state_kernel,
        grid=(nb // tb,),
        in_specs=[rspec(a) for a in row_ins] + [sspec(hs_all), sspec(ss_all)] +
                 [pl.BlockSpec(memory_space=pl.ANY) for _ in carried],
        out_specs=[rspec(xdt), rspec(qn), sspec(hs_all), sspec(ss_all)],
        out_shape=[jax.ShapeDtypeStruct(xdt.shape, F32), jax.ShapeDtypeStruct(qn.shape, F32),
                   jax.ShapeDtypeStruct(hs_all.shape, F32), jax.ShapeDtypeStruct(ss_all.shape, F32)],
        input_output_aliases=aliases,
        compiler_params=_params("parallel"),
    )(*row_ins, hs_all, ss_all, *carried)


def _decode_out_kernel(x_ref, y_ref, xs_ref, o_ref, proj_ref, dexp_ref, snw_ref, gnw_ref, w1_ref, w2_ref, out_ref):
    y = y_ref[...] + dexp_ref[...] * xs_ref[...]
    gs = SSD_INNER // SSD_GROUPS
    ys = []
    for g in range(SSD_GROUPS):
        yg = y[:, g * gs:(g + 1) * gs] * _silu(proj_ref[:, COL_Z_SSD + g * gs:COL_Z_SSD + (g + 1) * gs])
        ms = jnp.mean(yg * yg, axis=-1, keepdims=True)
        ys.append(yg * lax.rsqrt(ms + RMS_EPS) * snw_ref[:, g * gs:(g + 1) * gs])
    os = []
    for h in range(GDN_HEADS):
        lo = h * GDN_HEAD_V
        oh = o_ref[:, lo:lo + GDN_HEAD_V]
        ms = jnp.mean(oh * oh, axis=-1, keepdims=True)
        os.append(oh * lax.rsqrt(ms + RMS_EPS) * gnw_ref[...] *
                  _silu(proj_ref[:, COL_Z_GDN + lo:COL_Z_GDN + lo + GDN_HEAD_V]))
    yn = jnp.concatenate(ys, axis=-1)
    on = jnp.concatenate(os, axis=-1)
    out_ref[...] = x_ref[...] + _mm(yn, w1_ref[...]) + _mm(on, w2_ref[...])


def _decode_out(x, y, xs, o, proj, lw):
    ins = [x, y, xs, o, proj, lw['ssd_d_exp'], lw['ssd_norm_w'], lw['gdn_norm_w'], lw['w_out_ssd'], lw['w_out_gdn']]
    return pl.pallas_call(
        _decode_out_kernel,
        grid=(1,),
        in_specs=[pl.BlockSpec(a.shape, lambda i, nd=a.ndim: (0,) * nd) for a in ins],
        out_specs=pl.BlockSpec(x.shape, lambda i: (0, 0)),
        out_shape=jax.ShapeDtypeStruct(x.shape, F32),
        compiler_params=_params("arbitrary"),
    )(*ins)


def _identity_taps(taps, n):
    return jnp.concatenate([jnp.zeros((taps - 1, n), F32), jnp.ones((1, n), F32)], axis=0)


def _prep_layer(i, p):
    w_in = p['w_in'][i]
    o_zs = 0
    o_xbc = o_zs + SSD_INNER
    o_dt = o_xbc + SSD_INNER + 2 * SSD_BC
    o_qkv = o_dt + SSD_HEADS
    o_zg = o_qkv + 3 * GDN_DIM
    o_b = o_zg + GDN_DIM
    o_a = o_b + GDN_HEADS
    used = SSD_HEADS + 2 * GDN_HEADS
    w_perm = jnp.concatenate([
        w_in[:, o_zs:o_zs + SSD_INNER],
        w_in[:, o_xbc:o_xbc + SSD_INNER],
        w_in[:, o_qkv:o_qkv + 3 * GDN_DIM],
        w_in[:, o_zg:o_zg + GDN_DIM],
        w_in[:, o_xbc + SSD_INNER:o_xbc + SSD_INNER + 2 * SSD_BC],
        w_in[:, o_dt:o_dt + SSD_HEADS],
        w_in[:, o_b:o_b + GDN_HEADS],
        w_in[:, o_a:o_a + GDN_HEADS],
        jnp.zeros((D_MODEL, PROJ_COLS - COL_SMALL - used), F32)], axis=1).astype(BF16)
    lane_pad = jnp.zeros((LANES - used,), F32)
    scw = p['ssd_conv_w'][i]
    scb = p['ssd_conv_b'][i].reshape(1, -1)
    gcw = p['gdn_conv_w'][i]
    return {
        'w_in': w_perm,
        'norm_mix_w': p['norm_mix_w'][i],
        'ssd_conv_w': scw, 'ssd_conv_b': scb, 'gdn_conv_w': gcw,
        'in_cw': jnp.concatenate([_identity_taps(SSD_CONV, SSD_INNER), scw[:, :SSD_INNER], gcw,
                                  _identity_taps(GDN_CONV, GDN_DIM), scw[:, SSD_INNER:],
                                  _identity_taps(SSD_CONV, PROJ_COLS - COL_SMALL)], axis=1),
        'in_cb': jnp.concatenate([jnp.zeros((1, COL_XS), F32), scb[:, :SSD_INNER],
                                  jnp.zeros((1, COL_B - COL_Q), F32), scb[:, SSD_INNER:],
                                  jnp.zeros((1, PROJ_COLS - COL_SMALL), F32)], axis=1),
        'gate_bias': jnp.concatenate([p['ssd_dt_bias'][i], jnp.zeros((GDN_HEADS,), F32), p['gdn_dt_bias'][i],
                                      lane_pad]).reshape(1, LANES),
        'gate_alog': jnp.concatenate([p['ssd_a_log'][i], jnp.zeros((GDN_HEADS,), F32), p['gdn_a_log'][i],
                                      lane_pad]).reshape(1, LANES),
        'ssd_d_exp': jnp.repeat(p['ssd_d'][i], SSD_HEAD_DIM).reshape(1, SSD_INNER),
        'ssd_norm_w': p['ssd_norm_w'][i].reshape(1, SSD_INNER),
        'gdn_norm_w': p['gdn_norm_w'][i].reshape(1, GDN_HEAD_V),
        'w_out_ssd': p['w_out'][i, :SSD_INNER].astype(BF16),
        'w_out_gdn': p['w_out'][i, SSD_INNER:].astype(BF16),
        'norm_xa_w': p['norm_xa_w'][i], 'norm_mem_w': p['norm_mem_w'][i],
        'xa_wq': p['xa_wq'][i].astype(BF16), 'xa_wo': p['xa_wo'][i].astype(BF16),
        'xa_wkv': jnp.concatenate([p['xa_wk'][i], p['xa_wv'][i]], axis=1).astype(BF16),
        'norm_ffn_w': p['norm_ffn_w'][i],
        'ffn_w_gu': jnp.concatenate([p['ffn_w_gate'][i], p['ffn_w_up'][i]], axis=1).astype(BF16),
        'ffn_conv_w': p['ffn_conv_w'][i], 'ffn_conv_b': p['ffn_conv_b'][i],
        'ffn_w_down': p['ffn_w_down'][i].astype(BF16),
    }


def _unpermute_conv_rows(proj_rows):
    ssd = jnp.concatenate([proj_rows[..., COL_XS:COL_XS + SSD_INNER],
                           proj_rows[..., COL_B:COL_B + SSD_BC],
                           proj_rows[..., COL_C:COL_C + SSD_BC]], axis=-1)
    gdn = proj_rows[..., COL_Q:COL_Q + 3 * GDN_DIM]
    return ssd, gdn


def _tile(m, pref):
    t = min(m, pref)
    while m % t:
        t //= 2
    return t


def kernel(x_prompt, x_sample, mem_prompt, cache_mem_k, cache_mem_v, state_ssd_conv, state_ssd, state_gdn_conv, state_gdn, state_ffn_conv, norm_mix_w, w_in, ssd_conv_w, ssd_conv_b, ssd_dt_bias, ssd_a_log, ssd_d, ssd_norm_w, gdn_conv_w, gdn_dt_bias, gdn_a_log, gdn_norm_w, w_out, norm_xa_w, norm_mem_w, xa_wq, xa_wk, xa_wv, xa_wo, norm_ffn_w, ffn_w_gate, ffn_w_up, ffn_conv_w, ffn_conv_b, ffn_w_down, final_norm_w):
    params = dict(norm_mix_w=norm_mix_w, w_in=w_in, ssd_conv_w=ssd_conv_w, ssd_conv_b=ssd_conv_b,
                  ssd_dt_bias=ssd_dt_bias, ssd_a_log=ssd_a_log, ssd_d=ssd_d, ssd_norm_w=ssd_norm_w,
                  gdn_conv_w=gdn_conv_w, gdn_dt_bias=gdn_dt_bias, gdn_a_log=gdn_a_log, gdn_norm_w=gdn_norm_w,
                  w_out=w_out, norm_xa_w=norm_xa_w, norm_mem_w=norm_mem_w, xa_wq=xa_wq, xa_wk=xa_wk, xa_wv=xa_wv,
                  xa_wo=xa_wo, norm_ffn_w=norm_ffn_w, ffn_w_gate=ffn_w_gate, ffn_w_up=ffn_w_up,
                  ffn_conv_w=ffn_conv_w, ffn_conv_b=ffn_conv_b, ffn_w_down=ffn_w_down)
    depth = w_in.shape[0]
    bsz, seq, _ = x_prompt.shape
    nb = x_sample.shape[0]
    n_mem = mem_prompt.shape[1]
    d_ff = ffn_w_down.shape[1]
    mp = bsz * seq

    xp = x_prompt.reshape(mp, D_MODEL)
    xs = x_sample.reshape(nb, D_MODEL)
    mem = mem_prompt.reshape(bsz * n_mem, D_MODEL)
    tm_p = _tile(seq, 1024)
    tm_c = _tile(seq, 256)
    tm_mem = _tile(bsz * n_mem, 1024)
    dec_tile = _tile(nb, DEC_TILE)

    hs_all = state_ssd.reshape(depth, nb, SSD_INNER, SSD_STATE)
    ss_all = state_gdn.reshape(depth, nb, GDN_HEADS * GDN_HEAD_K, GDN_HEAD_V)
    mk_all, mv_all = cache_mem_k, cache_mem_v
    new_states = ()

    mkp, mvp = [], []
    p_sc, p_sh, p_gc, p_gs, p_fc = [], [], [], [], []
    s_sc, s_gc, s_fc = [], [], []
    for i in range(depth):
        lw = _prep_layer(i, params)

        kv = _norm_mm(mem, lw['norm_mem_w'], lw['xa_wkv'], tm_mem)
        mk = kv[:, :D_MODEL].reshape(bsz, n_mem, D_MODEL)
        mv = kv[:, D_MODEL:].reshape(bsz, n_mem, D_MODEL)
        mkp.append(mk.reshape(bsz, n_mem, XA_HEADS, XA_HEAD_DIM))
        mvp.append(mv.reshape(bsz, n_mem, XA_HEADS, XA_HEAD_DIM))

        proj, tails = _norm_mm_conv(xp, lw['norm_mix_w'], lw['w_in'], lw['in_cw'], lw['in_cb'],
                                    [(COL_XS, COL_Z_GDN), (COL_B, COL_SMALL)], seq, tm_c)
        y, h_new = _ssd_prefill(proj, bsz, seq, lw)
        o, s_new = _gdn_prefill(proj, bsz, seq, lw)
        seq_tails = tails.reshape(bsz, seq // tm_c, HIST_ROWS, PROJ_COLS)[:, -1]
        ssd_tail, gdn_tail = _unpermute_conv_rows(seq_tails[:, HIST_ROWS - (SSD_CONV - 1):])
        p_sc.append(ssd_tail)
        p_gc.append(gdn_tail)
        p_sh.append(h_new.reshape(bsz, SSD_HEADS, SSD_HEAD_DIM, SSD_STATE))
        p_gs.append(s_new.reshape(bsz, GDN_HEADS, GDN_HEAD_K, GDN_HEAD_V))
        xp = _mm_res(xp, [y, o], [lw['w_out_ssd'], lw['w_out_gdn']], _tile(mp, 512))
        xp = _xattn_prefill(xp, lw['norm_xa_w'], lw['xa_wq'], mk, mv, lw['xa_wo'], bsz, seq, _tile(seq, 512))
        xp, gate_tails = _ffn_prefill(xp, lw['norm_ffn_w'], lw['ffn_w_gu'], lw['ffn_conv_w'], lw['ffn_conv_b'],
                                      lw['ffn_w_down'], seq, tm_c)
        p_fc.append(gate_tails.reshape(bsz, seq // tm_c, HIST_ROWS, d_ff)[:, -1, HIST_ROWS - (FFN_CONV - 1):])

        proj_s = _norm_mm(xs, lw['norm_mix_w'], lw['w_in'], nb)
        ssd_new, gdn_new = _unpermute_conv_rows(proj_s)
        s_sc.append(jnp.concatenate([state_ssd_conv[i][:, 1:], ssd_new[:, None]], axis=1))
        s_gc.append(jnp.concatenate([state_gdn_conv[i][:, 1:], gdn_new[:, None]], axis=1))
        rows = _decode_rows(proj_s, state_ssd_conv[i], state_gdn_conv[i], lw)
        y_s, o_s, hs_new, ss_new = _decode_state(rows, hs_all, ss_all, i, dec_tile, new_states)
        new_states = (hs_new, ss_new)
        xs = _decode_out(xs, y_s, rows[0], o_s, proj_s, lw)
        q_s = _norm_mm(xs, lw['norm_xa_w'], lw['xa_wq'], nb)
        att = _xattn_decode(q_s, mk_all, mv_all, i, _tile(nb, 4))
        xs = _mm_res(xs, [att], [lw['xa_wo']], nb)
        gu_s = _norm_mm(xs, lw['norm_ffn_w'], lw['ffn_w_gu'], nb)
        s_fc.append(jnp.concatenate([state_ffn_conv[i][:, 1:], gu_s[:, None, :d_ff]], axis=1))
        xs = _ffn_decode(xs, gu_s, state_ffn_conv[i][:, 0], state_ffn_conv[i][:, 1],
                         lw['ffn_conv_w'], lw['ffn_conv_b'], lw['ffn_w_down'])

    y_prompt = _rmsnorm(xp, final_norm_w, tm_p).reshape(bsz, seq, D_MODEL)
    y_sample = _rmsnorm(xs, final_norm_w, nb).reshape(nb, 1, D_MODEL)
    return (y_prompt, y_sample, jnp.stack(mkp), jnp.stack(mvp),
            jnp.stack(p_sc), jnp.stack(p_sh), jnp.stack(p_gc), jnp.stack(p_gs), jnp.stack(p_fc),
            jnp.stack(s_sc), new_states[0].reshape(state_ssd.shape), jnp.stack(s_gc),
            new_states[1].reshape(state_gdn.shape), jnp.stack(s_fc))
```

```python
import functools

import jax
import jax.numpy as jnp
from jax import lax
from jax.experimental import pallas as pl
from jax.experimental.pallas import tpu as pltpu

F32 = jnp.float32
BF16 = jnp.bfloat16

D_MODEL = 1024
SSD_HEADS = 16
SSD_HEAD_DIM = 64
SSD_STATE = 128
SSD_GROUPS = 2
SSD_INNER = SSD_HEADS * SSD_HEAD_DIM
SSD_BC = SSD_GROUPS * SSD_STATE
SSD_CONV = 4
GDN_HEADS = 8
GDN_HEAD_K = 128
GDN_HEAD_V = 128
GDN_DIM = GDN_HEADS * GDN_HEAD_V
GDN_CONV = 4
XA_HEADS = 4
XA_HEAD_DIM = D_MODEL // XA_HEADS
FFN_CONV = 3
RMS_EPS = 1e-6
L2_EPS = 1e-6

SUBLANES = 8
LANES = 128
MXU_COLS = 256
VMEM_LIMIT_BYTES = 56 * 1024 * 1024

COL_Z_SSD = 0
COL_XS = 1024
COL_Q = 2048
COL_K = 3072
COL_V = 4096
COL_Z_GDN = 5120
COL_B = 6144
COL_C = 6400
COL_SMALL = 6656
PROJ_COLS = 7168
LANE_DT = 0
LANE_BETA = 16
LANE_G = 24

MIXER_CHUNK = 128
DEC_TILE = 8


def _params(*sem):
    return pltpu.CompilerParams(dimension_semantics=sem, vmem_limit_bytes=VMEM_LIMIT_BYTES)


def _mm(a, b):
    return jnp.dot(a.astype(BF16), b.astype(BF16), preferred_element_type=F32)


def _mm_nt(a, b):
    return lax.dot_general(a.astype(BF16), b.astype(BF16), (((1,), (1,)), ((), ())),
                           preferred_element_type=F32)


def _split(x, parts):
    out = []
    r = x
    for _ in range(parts - 1):
        p = r.astype(BF16)
        out.append(p)
        r = r - p.astype(F32)
    out.append(r.astype(BF16))
    return out


def _mm_rhs_split(a_exact, b, parts=3):
    a16 = a_exact.astype(BF16)
    acc = None
    for p in _split(b, parts):
        t = jnp.dot(a16, p, preferred_element_type=F32)
        acc = t if acc is None else acc + t
    return acc


def _mm_lhs_split(a, b_exact, parts=3):
    b16 = b_exact.astype(BF16)
    acc = None
    for p in _split(a, parts):
        t = jnp.dot(p, b16, preferred_element_type=F32)
        acc = t if acc is None else acc + t
    return acc


def _silu(x):
    return x / (1.0 + jnp.exp(-x))


def _sigmoid(x):
    return 1.0 / (1.0 + jnp.exp(-x))


def _softplus(x):
    return jnp.maximum(x, 0.0) + jnp.log1p(jnp.exp(-jnp.abs(x)))


def _col(arr, lane):
    idx = lax.broadcasted_iota(jnp.int32, arr.shape, 1)
    return jnp.sum(jnp.where(idx == lane, arr, 0.0), axis=1, keepdims=True)


def _gate_rows(sm, bias, alog):
    sp = _softplus(sm + bias)
    sg = _sigmoid(sm)
    lane = lax.broadcasted_iota(jnp.int32, alog.shape, 1)
    has_decay = (lane < LANE_BETA) | ((lane >= LANE_G) & (lane < LANE_G + GDN_HEADS))
    neg_a = jnp.where(has_decay, -jnp.exp(alog), 0.0)
    return sp, sg, sp * neg_a


def _norm_mm_kernel(x_ref, nw_ref, w_ref, o_ref, xn_ref):
    @pl.when(pl.program_id(1) == 0)
    def _():
        x = x_ref[...]
        ms = jnp.mean(x * x, axis=-1, keepdims=True)
        xn_ref[...] = (x * lax.rsqrt(ms + RMS_EPS) * nw_ref[...]).astype(BF16)

    o_ref[...] = jnp.dot(xn_ref[...], w_ref[...], preferred_element_type=F32)


MAX_COL_TILE_LANES = 11 * LANES


def _col_tile(n):
    return max(t for t in range(LANES, MAX_COL_TILE_LANES + 1, LANES) if n % t == 0)


def _norm_mm(x, nw, w, tm):
    m, k = x.shape
    n = w.shape[1]
    tn = _col_tile(n)
    assert m % tm == 0
    return pl.pallas_call(
        _norm_mm_kernel,
        grid=(m // tm, n // tn),
        in_specs=[pl.BlockSpec((tm, k), lambda i, j: (i, 0)),
                  pl.BlockSpec((1, k), lambda i, j: (0, 0)),
                  pl.BlockSpec((k, tn), lambda i, j: (0, j))],
        out_specs=pl.BlockSpec((tm, tn), lambda i, j: (i, j)),
        out_shape=jax.ShapeDtypeStruct((m, n), F32),
        scratch_shapes=[pltpu.VMEM((tm, k), BF16)],
        compiler_params=_params("parallel", "arbitrary"),
    )(x, nw.reshape(1, k), w)


HIST_ROWS = SUBLANES


CONV_BUFS = 4


def _norm_mm_conv_kernel(x_ref, nw_ref, w_ref, cw_ref, cb_ref, o_ref, tail_ref, ext_ref, hist_ref,
                         *, taps, tiles_per_seq, conv_ranges):
    i = pl.program_id(0)
    tm = x_ref.shape[0]
    n = w_ref.shape[1]

    @pl.when(i == 0)
    def _():
        hist_ref[...] = jnp.zeros(hist_ref.shape, F32)

    x = x_ref[...]
    ms = jnp.mean(x * x, axis=-1, keepdims=True)
    xn = (x * lax.rsqrt(ms + RMS_EPS) * nw_ref[...]).astype(BF16)
    starts_seq = i % tiles_per_seq == 0
    for ci, lo in enumerate(range(0, n, MXU_COLS)):
        cols = slice(lo, min(lo + MXU_COLS, n))
        acc = jnp.dot(xn, w_ref[:, cols], preferred_element_type=F32)
        tail_ref[:, cols] = acc[tm - HIST_ROWS:tm, :]
        if not any(a <= cols.start and cols.stop <= b for a, b in conv_ranges):
            o_ref[:, cols] = acc
            continue
        buf = ext_ref.at[ci % CONV_BUFS]
        width = cols.stop - cols.start
        buf[0:HIST_ROWS, 0:width] = jnp.where(starts_seq, 0.0, hist_ref[:, cols])
        buf[HIST_ROWS:HIST_ROWS + tm, 0:width] = acc
        hist_ref[:, cols] = acc[tm - HIST_ROWS:tm, :]
        c = cb_ref[:, cols]
        for k in range(taps):
            start = HIST_ROWS - (taps - 1) + k
            c = c + buf[start:start + tm, 0:width] * cw_ref[k:k + 1, cols]
        o_ref[:, cols] = _silu(c)


def _norm_mm_conv(x, nw, w, cw, cb, conv_ranges, seq, tm):
    m, k = x.shape
    n = w.shape[1]
    taps = cw.shape[0]
    assert m % tm == 0 and seq % tm == 0 and tm >= HIST_ROWS
    assert all(a % MXU_COLS == 0 and b % MXU_COLS == 0 for a, b in conv_ranges)
    whole = lambda a: pl.BlockSpec(a.shape, lambda i: (0, 0), pipeline_mode=pl.Buffered(1))
    nw2 = nw.reshape(1, k)
    return pl.pallas_call(
        functools.partial(_norm_mm_conv_kernel, taps=taps, tiles_per_seq=seq // tm,
                          conv_ranges=tuple(conv_ranges)),
        grid=(m // tm,),
        in_specs=[pl.BlockSpec((tm, k), lambda i: (i, 0)), whole(nw2), whole(w), whole(cw), whole(cb)],
        out_specs=[pl.BlockSpec((tm, n), lambda i: (i, 0)),
                   pl.BlockSpec((None, HIST_ROWS, n), lambda i: (i, 0, 0))],
        out_shape=[jax.ShapeDtypeStruct((m, n), F32),
                   jax.ShapeDtypeStruct((m // tm, HIST_ROWS, n), F32)],
        scratch_shapes=[pltpu.VMEM((CONV_BUFS, HIST_ROWS + tm, MXU_COLS), F32),
                        pltpu.VMEM((HIST_ROWS, n), F32)],
        compiler_params=_params("arbitrary"),
    )(x, nw2, w, cw, cb)


def _mm_res_kernel(*refs, n_in):
    res_ref = refs[0]
    a_refs = refs[1:1 + n_in]
    w_refs = refs[1 + n_in:1 + 2 * n_in]
    o_ref = refs[1 + 2 * n_in]
    acc = res_ref[...]
    for a_ref, w_ref in zip(a_refs, w_refs):
        acc = acc + jnp.dot(a_ref[...].astype(BF16), w_ref[...], preferred_element_type=F32)
    o_ref[...] = acc


def _mm_res(res, a_list, w_list, tm):
    m, n = res.shape
    assert m % tm == 0
    n_in = len(a_list)
    in_specs = [pl.BlockSpec((tm, n), lambda i: (i, 0))]
    in_specs += [pl.BlockSpec((tm, a.shape[1]), lambda i: (i, 0)) for a in a_list]
    in_specs += [pl.BlockSpec(w.shape, lambda i: (0, 0)) for w in w_list]
    return pl.pallas_call(
        functools.partial(_mm_res_kernel, n_in=n_in),
        grid=(m // tm,),
        in_specs=in_specs,
        out_specs=pl.BlockSpec((tm, n), lambda i: (i, 0)),
        out_shape=jax.ShapeDtypeStruct((m, n), F32),
        compiler_params=_params("parallel"),
    )(res, *a_list, *w_list)


def _rmsnorm_kernel(x_ref, nw_ref, o_ref):
    x = x_ref[...]
    ms = jnp.mean(x * x, axis=-1, keepdims=True)
    o_ref[...] = x * lax.rsqrt(ms + RMS_EPS) * nw_ref[...]


def _rmsnorm(x, nw, tm):
    m, k = x.shape
    return pl.pallas_call(
        _rmsnorm_kernel,
        grid=(m // tm,),
        in_specs=[pl.BlockSpec((tm, k), lambda i: (i, 0)), pl.BlockSpec((1, k), lambda i: (0, 0))],
        out_specs=pl.BlockSpec((tm, k), lambda i: (i, 0)),
        out_shape=jax.ShapeDtypeStruct((m, k), F32),
        compiler_params=_params("parallel"),
    )(x, nw.reshape(1, k))


def _causal_mask(cs, strict=False):
    r = lax.broadcasted_iota(jnp.int32, (cs, cs), 0)
    c = lax.broadcasted_iota(jnp.int32, (cs, cs), 1)
    return (r > c) if strict else (r >= c)


def _ssd_chunk(z_ref, xs_ref, b_ref, c_ref, sm_ref, bias_ref, alog_ref, dexp_ref, nw_ref, y_ref, hst, y_s):
    cs = xs_ref.shape[0]
    xs = xs_ref[...]
    bm = b_ref[...]
    cm = c_ref[...]

    sp, _, dta = _gate_rows(sm_ref[...], bias_ref[...], alog_ref[...])
    causal = _causal_mask(cs)
    cum = _mm_rhs_split(jnp.where(causal, 1.0, 0.0), dta)
    cum_row = cum.T

    lane_lo = lax.broadcasted_iota(jnp.int32, (1, LANES), 1) < SSD_HEAD_DIM
    row_lo = lax.broadcasted_iota(jnp.int32, (LANES, 1), 0) < SSD_HEAD_DIM
    heads_per_group = SSD_HEADS // SSD_GROUPS
    for g in range(SSD_GROUPS):
        cg = cm[:, g * SSD_STATE:(g + 1) * SSD_STATE]
        bg = bm[:, g * SSD_STATE:(g + 1) * SSD_STATE]
        cb = _mm_nt(cg, bg)
        rows_g = heads_per_group * SSD_HEAD_DIM
        y_state = _mm_nt(cg, hst[g * rows_g:(g + 1) * rows_g, :])
        for hp in range(heads_per_group // 2):
            h0 = g * heads_per_group + 2 * hp
            lo = h0 * SSD_HEAD_DIM
            xs_pair = xs[:, lo:lo + LANES]
            ci = [_col(cum, LANE_DT + h0), _col(cum, LANE_DT + h0 + 1)]
            dti = [_col(sp, LANE_DT + h0), _col(sp, LANE_DT + h0 + 1)]
            xdt_pair = xs_pair * jnp.where(lane_lo, dti[0], dti[1])
            y_pair = None
            for e in range(2):
                cj = cum_row[LANE_DT + h0 + e:LANE_DT + h0 + e + 1, :]
                seg = jnp.exp(jnp.where(causal, ci[e] - cj, -jnp.inf))
                keep = lane_lo if e == 0 else jnp.logical_not(lane_lo)
                t = _mm(cb * seg, jnp.where(keep, xdt_pair, 0.0))
                y_pair = t if y_pair is None else y_pair + t
            y_pair = y_pair + y_state[:, 2 * hp * SSD_HEAD_DIM:2 * hp * SSD_HEAD_DIM + LANES] * \
                jnp.where(lane_lo, jnp.exp(ci[0]), jnp.exp(ci[1]))
            y_pair = y_pair + dexp_ref[:, lo:lo + LANES] * xs_pair
            y_s[:, lo:lo + LANES] = y_pair
            cl = [ci[0][cs - 1:cs, :], ci[1][cs - 1:cs, :]]
            to_end = jnp.where(lane_lo, jnp.exp(cl[0] - ci[0]), jnp.exp(cl[1] - ci[1]))
            contrib = _mm((xdt_pair * to_end).T, bg)
            dec = jnp.where(row_lo, jnp.exp(cl[0]), jnp.exp(cl[1]))
            hst[lo:lo + LANES, :] = hst[lo:lo + LANES, :] * dec + contrib

    gs = SSD_INNER // SSD_GROUPS
    for g in range(SSD_GROUPS):
        yg = y_s[:, g * gs:(g + 1) * gs] * _silu(z_ref[:, g * gs:(g + 1) * gs])
        ms = jnp.mean(yg * yg, axis=-1, keepdims=True)
        y_ref[:, g * gs:(g + 1) * gs] = (yg * lax.rsqrt(ms + RMS_EPS) * nw_ref[:, g * gs:(g + 1) * gs]).astype(BF16)


INV_BASE = 16


def _same_block(cs, size):
    r = lax.broadcasted_iota(jnp.int32, (cs, cs), 0) // size
    c = lax.broadcasted_iota(jnp.int32, (cs, cs), 1) // size
    return r == c


def _bdot(a16, b16):
    return jnp.dot(a16, b16, preferred_element_type=F32)


def _mixer_prefill_kernel(q_ref, k_ref, v_ref, z_ref, sm_ref,
                          zs_ref, xs_ref, b_ref, c_ref,
                          bias_ref, alog_ref, nw_ref, dexp_ref, snw_ref,
                          o_ref, st_ref, y_ref, hst_ref,
                          sst, hst, y_s,
                          qd_s, kd_s, rhs_s, a_s, attn_s, x_s, d_s, u_s, w_s, glast_s, *, n_chunks):
    c_idx = pl.program_id(1)
    nseq, cs = q_ref.shape[0], q_ref.shape[1]
    n_units = nseq * GDN_HEADS
    heads = [(s, h, s * GDN_HEADS + h) for s in range(nseq) for h in range(GDN_HEADS)]
    qc, kc, vc = q_ref, k_ref, v_ref
    causal = _causal_mask(cs)
    strict = _causal_mask(cs, strict=True)

    def stage_a(par):
        sg, gc, gc_row = [], [], []
        for s in range(nseq):
            _, sg_s, g_step = _gate_rows(sm_ref[s], bias_ref[...], alog_ref[...])
            gc_s = _mm_rhs_split(jnp.where(causal, 1.0, 0.0), g_step)
            sg.append(sg_s)
            gc.append(gc_s)
            gc_row.append(gc_s.T)
            glast_s[par * nseq + s] = gc_s[cs - 1:cs, :]
        for s, h, u in heads:
            _stage_a_unit(par, s, h, u, sg, gc, gc_row)

    def _stage_a_unit(par, s, h, u, sg, gc, gc_row):
        u = par * n_units + u
        ps = par * nseq + s
        lo = h * GDN_HEAD_K
        qh = qc[s, :, lo:lo + GDN_HEAD_K]
        kh = kc[s, :, lo:lo + GDN_HEAD_K]
        vh = vc[s, :, lo:lo + GDN_HEAD_V]
        qn = qh * lax.rsqrt(jnp.sum(qh * qh, axis=-1, keepdims=True) + L2_EPS) * (GDN_HEAD_K ** -0.5)
        kn = kh * lax.rsqrt(jnp.sum(kh * kh, axis=-1, keepdims=True) + L2_EPS)
        gi = _col(gc[s], LANE_G + h)
        bi = _col(sg[s], LANE_BETA + h)
        gj = gc_row[s][LANE_G + h:LANE_G + h + 1, :]
        decay = jnp.exp(jnp.where(causal, gi - gj, -jnp.inf))
        egi = jnp.exp(gi)
        kb = kn * bi
        kn16 = kn.astype(BF16)
        a_s[u] = jnp.where(strict, lax.dot_general(kb.astype(BF16), kn16, (((1,), (1,)), ((), ())),
                                                   preferred_element_type=F32) * decay, 0.0)
        attn_s[u] = (lax.dot_general(qn.astype(BF16), kn16, (((1,), (1,)), ((), ())),
                                     preferred_element_type=F32) * decay).astype(BF16)
        qd_s[ps, :, lo:lo + GDN_HEAD_K] = (qn * egi).astype(BF16)
        kd_s[ps, :, lo:lo + GDN_HEAD_K] = (kn * jnp.exp(gi[cs - 1:cs, :] - gi)).astype(BF16)
        rhs_s[ps, :, 2 * lo:2 * lo + GDN_HEAD_V] = (vh * bi).astype(BF16)
        rhs_s[ps, :, 2 * lo + GDN_HEAD_V:2 * lo + 2 * GDN_HEAD_V] = (kb * egi).astype(BF16)

    def stages_bcd(par):
        base = _same_block(cs, INV_BASE)
        eye = jnp.where(causal & jnp.logical_not(strict), 1.0, 0.0)
        for _, _, u in heads:
            x = jnp.where(base, -a_s[par * n_units + u], 0.0)
            x_s[u] = x.astype(BF16)
            d_s[u] = eye + x
        span = 2
        while span < INV_BASE:
            for _, _, u in heads:
                x16 = x_s[u]
                x2 = _bdot(x16, x16).astype(BF16)
                x_s[u] = x2
                d = d_s[u]
                d_s[u] = d + _bdot(d.astype(BF16), x2)
            span *= 2
        size = INV_BASE
        while size < cs:
            off = _same_block(cs, 2 * size) & jnp.logical_not(_same_block(cs, size))
            for _, _, u in heads:
                d = d_s[u]
                d16 = d.astype(BF16)
                low = jnp.where(off, a_s[par * n_units + u], 0.0).astype(BF16)
                d_s[u] = d - _bdot(_bdot(d16, low).astype(BF16), d16)
            size *= 2

        for s, h, u in heads:
            lo = h * GDN_HEAD_K
            uw = _bdot(d_s[u].astype(BF16), rhs_s[par * nseq + s, :, 2 * lo:2 * lo + 2 * GDN_HEAD_V])
            u_s[s, :, lo:lo + GDN_HEAD_V] = uw[:, :GDN_HEAD_V]
            w_s[s, :, lo:lo + GDN_HEAD_V] = uw[:, GDN_HEAD_V:].astype(BF16)

        for s, h, u in heads:
            lo = h * GDN_HEAD_K
            ps = par * nseq + s
            st = sst[s, lo:lo + GDN_HEAD_K, :]
            s16 = st.astype(BF16)
            v_new = u_s[s, :, lo:lo + GDN_HEAD_V] - _bdot(w_s[s, :, lo:lo + GDN_HEAD_V], s16)
            v16 = v_new.astype(BF16)
            o_h = _bdot(qd_s[ps, :, lo:lo + GDN_HEAD_K], s16) + _bdot(attn_s[par * n_units + u], v16)
            g_last = _col(glast_s[ps], LANE_G + h)
            sst[s, lo:lo + GDN_HEAD_K, :] = st * jnp.exp(g_last) + lax.dot_general(
                kd_s[ps, :, lo:lo + GDN_HEAD_K], v16, (((0,), (0,)), ((), ())), preferred_element_type=F32)
            ms = jnp.mean(o_h * o_h, axis=-1, keepdims=True)
            o_ref[s, :, lo:lo + GDN_HEAD_V] = ((o_h * lax.rsqrt(ms + RMS_EPS) * nw_ref[...]) *
                                               _silu(z_ref[s, :, lo:lo + GDN_HEAD_V])).astype(BF16)

    def ssd_chunks():
        for s in range(nseq):
            _ssd_chunk(zs_ref.at[s], xs_ref.at[s], b_ref.at[s], c_ref.at[s], sm_ref.at[s], bias_ref, alog_ref,
                       dexp_ref, snw_ref, y_ref.at[s], hst.at[s], y_s.at[s])

    @pl.when(c_idx == 0)
    def _():
        sst[...] = jnp.zeros(sst.shape, F32)
        hst[...] = jnp.zeros(hst.shape, F32)
        stage_a(0)
        ssd_chunks()

    middle = (c_idx > 0) & (c_idx < n_chunks)
    for par in range(2):
        @pl.when(middle & (c_idx % 2 == par))
        def _(par=par):
            stages_bcd(1 - par)
            stage_a(par)
            ssd_chunks()

    @pl.when(c_idx == n_chunks)
    def _():
        stages_bcd((n_chunks - 1) % 2)
        st_ref[...] = sst[...]
        hst_ref[...] = hst[...]


GDN_SEQS_PER_STEP = 2


def _mixer_prefill(proj, bsz, seq, lw):
    cs = MIXER_CHUNK
    nseq = GDN_SEQS_PER_STEP if bsz % GDN_SEQS_PER_STEP == 0 else 1
    assert seq % cs == 0
    nc = seq // cs
    proj3 = proj.reshape(bsz, seq, PROJ_COLS)
    cur = lambda c: jnp.minimum(c, nc - 1)
    prev = lambda c: jnp.maximum(c - 1, 0)
    pcol = lambda width, off, ch: pl.BlockSpec((nseq, cs, width), lambda b, c: (b, ch(c), off // width))
    full = lambda a: pl.BlockSpec(a.shape, lambda b, c: (0,) * a.ndim)
    weights = [lw['gate_bias'], lw['gate_alog'], lw['gdn_norm_w'], lw['ssd_d_exp'], lw['ssd_norm_w']]
    units = nseq * GDN_HEADS
    o, st, y, hst = pl.pallas_call(
        functools.partial(_mixer_prefill_kernel, n_chunks=nc),
        grid=(bsz // nseq, nc + 1),
        in_specs=[pcol(GDN_DIM, COL_Q, cur), pcol(GDN_DIM, COL_K, cur), pcol(GDN_DIM, COL_V, cur),
                  pcol(GDN_DIM, COL_Z_GDN, prev), pcol(LANES, COL_SMALL, cur),
                  pcol(SSD_INNER, COL_Z_SSD, cur), pcol(SSD_INNER, COL_XS, cur), pcol(SSD_BC, COL_B, cur),
                  pcol(SSD_BC, COL_C, cur)] + [full(w) for w in weights],
        out_specs=[pl.BlockSpec((nseq, cs, GDN_DIM), lambda b, c: (b, prev(c), 0)),
                   pl.BlockSpec((nseq, GDN_HEADS * GDN_HEAD_K, GDN_HEAD_V), lambda b, c: (b, 0, 0)),
                   pl.BlockSpec((nseq, cs, SSD_INNER), lambda b, c: (b, cur(c), 0)),
                   pl.BlockSpec((nseq, SSD_INNER, SSD_STATE), lambda b, c: (b, 0, 0))],
        out_shape=[jax.ShapeDtypeStruct((bsz, seq, GDN_DIM), BF16),
                   jax.ShapeDtypeStruct((bsz, GDN_HEADS * GDN_HEAD_K, GDN_HEAD_V), F32),
                   jax.ShapeDtypeStruct((bsz, seq, SSD_INNER), BF16),
                   jax.ShapeDtypeStruct((bsz, SSD_INNER, SSD_STATE), F32)],
        scratch_shapes=[pltpu.VMEM((nseq, GDN_HEADS * GDN_HEAD_K, GDN_HEAD_V), F32),
                        pltpu.VMEM((nseq, SSD_INNER, SSD_STATE), F32),
                        pltpu.VMEM((nseq, cs, SSD_INNER), F32),
                        pltpu.VMEM((2 * nseq, cs, GDN_DIM), BF16),
                        pltpu.VMEM((2 * nseq, cs, GDN_DIM), BF16),
                        pltpu.VMEM((2 * nseq, cs, 2 * GDN_DIM), BF16),
                        pltpu.VMEM((2 * units, cs, cs), F32),
                        pltpu.VMEM((2 * units, cs, cs), BF16),
                        pltpu.VMEM((units, cs, cs), BF16),
                        pltpu.VMEM((units, cs, cs), F32),
                        pltpu.VMEM((nseq, cs, GDN_DIM), F32),
                        pltpu.VMEM((nseq, cs, GDN_DIM), BF16),
                        pltpu.VMEM((2 * nseq, 1, LANES), F32)],
        compiler_params=_params("parallel", "arbitrary"),
    )(*([proj3] * 9), *weights)
    return y.reshape(bsz * seq, SSD_INNER), hst, o.reshape(bsz * seq, GDN_DIM), st


def _softmax_rows(s):
    m = jnp.max(s, axis=-1, keepdims=True)
    e = jnp.exp(s - m)
    return e / jnp.sum(e, axis=-1, keepdims=True)


def _xattn_prefill_kernel(x_ref, nw_ref, wq_ref, mk_ref, mv_ref, wo_ref, o_ref):
    x = x_ref[...]
    ms = jnp.mean(x * x, axis=-1, keepdims=True)
    h = x * lax.rsqrt(ms + RMS_EPS) * nw_ref[...]
    q = jnp.dot(h.astype(BF16), wq_ref[...], preferred_element_type=F32)
    outs = []
    for hd in range(XA_HEADS):
        lo = hd * XA_HEAD_DIM
        s = _mm_nt(q[:, lo:lo + XA_HEAD_DIM], mk_ref[:, lo:lo + XA_HEAD_DIM]) * (XA_HEAD_DIM ** -0.5)
        outs.append(_mm(_softmax_rows(s), mv_ref[:, lo:lo + XA_HEAD_DIM]))
    att = jnp.concatenate(outs, axis=-1)
    o_ref[...] = x + jnp.dot(att.astype(BF16), wo_ref[...], preferred_element_type=F32)


def _xattn_prefill(x, nw, wq, mk, mv, wo, bsz, seq, tq):
    n_mem = mk.shape[1]
    nt = seq // tq
    return pl.pallas_call(
        _xattn_prefill_kernel,
        grid=(bsz, nt),
        in_specs=[pl.BlockSpec((tq, D_MODEL), lambda b, t: (b * nt + t, 0)),
                  pl.BlockSpec((1, D_MODEL), lambda b, t: (0, 0)),
                  pl.BlockSpec((D_MODEL, D_MODEL), lambda b, t: (0, 0)),
                  pl.BlockSpec((None, n_mem, D_MODEL), lambda b, t: (b, 0, 0)),
                  pl.BlockSpec((None, n_mem, D_MODEL), lambda b, t: (b, 0, 0)),
                  pl.BlockSpec((D_MODEL, D_MODEL), lambda b, t: (0, 0))],
        out_specs=pl.BlockSpec((tq, D_MODEL), lambda b, t: (b * nt + t, 0)),
        out_shape=jax.ShapeDtypeStruct(x.shape, F32),
        compiler_params=_params("parallel", "arbitrary"),
    )(x, nw.reshape(1, D_MODEL), wq, mk, mv, wo)


def _xattn_decode_kernel(q_ref, mk_ref, mv_ref, o_ref):
    tb = q_ref.shape[0]
    for i in range(tb):
        q = q_ref[i]
        qh = jnp.concatenate([q[:, hd * XA_HEAD_DIM:(hd + 1) * XA_HEAD_DIM] for hd in range(XA_HEADS)], axis=0)
        s = jnp.sum(mk_ref[i] * qh[None], axis=-1, keepdims=True) * (XA_HEAD_DIM ** -0.5)
        e = jnp.exp(s - jnp.max(s, axis=0, keepdims=True))
        p = e / jnp.sum(e, axis=0, keepdims=True)
        o_ref[i] = jnp.sum(p * mv_ref[i], axis=0)


def _xattn_decode(q, mk_all, mv_all, layer, tb):
    _, nb, n_mem, nh, hd = mk_all.shape
    return pl.pallas_call(
        _xattn_decode_kernel,
        grid=(nb // tb,),
        in_specs=[pl.BlockSpec((tb, 1, D_MODEL), lambda i: (i, 0, 0)),
                  pl.BlockSpec((None, tb, n_mem, nh, hd), lambda i: (layer, i, 0, 0, 0)),
                  pl.BlockSpec((None, tb, n_mem, nh, hd), lambda i: (layer, i, 0, 0, 0))],
        out_specs=pl.BlockSpec((tb, nh, hd), lambda i: (i, 0, 0)),
        out_shape=jax.ShapeDtypeStruct((nb, nh, hd), F32),
        compiler_params=_params("parallel"),
    )(q.reshape(nb, 1, D_MODEL), mk_all, mv_all).reshape(nb, D_MODEL)


def _ffn_prefill_kernel(x_ref, nw_ref, wgu_ref, cw_ref, cb_ref, wd_ref, o_ref, tail_ref,
                        ext_ref, hist_ref, hm_ref, *, tiles_per_seq):
    i = pl.program_id(0)
    tm = x_ref.shape[0]
    d_ff = wd_ref.shape[0]

    @pl.when(i == 0)
    def _():
        hist_ref[...] = jnp.zeros(hist_ref.shape, F32)

    x = x_ref[...]
    ms = jnp.mean(x * x, axis=-1, keepdims=True)
    xn = (x * lax.rsqrt(ms + RMS_EPS) * nw_ref[...]).astype(BF16)
    starts_seq = i % tiles_per_seq == 0
    for ci, lo in enumerate(range(0, d_ff, MXU_COLS)):
        cols = slice(lo, min(lo + MXU_COLS, d_ff))
        width = cols.stop - cols.start
        gate = jnp.dot(xn, wgu_ref[:, cols], preferred_element_type=F32)
        up = jnp.dot(xn, wgu_ref[:, d_ff + cols.start:d_ff + cols.stop], preferred_element_type=F32)
        tail_ref[:, cols] = gate[tm - HIST_ROWS:tm, :]
        buf = ext_ref.at[ci % CONV_BUFS]
        buf[0:HIST_ROWS, 0:width] = jnp.where(starts_seq, 0.0, hist_ref[:, cols])
        buf[HIST_ROWS:HIST_ROWS + tm, 0:width] = gate
        hist_ref[:, cols] = gate[tm - HIST_ROWS:tm, :]
        c = cb_ref[:, cols]
        for k in range(FFN_CONV):
            start = HIST_ROWS - (FFN_CONV - 1) + k
            c = c + buf[start:start + tm, 0:width] * cw_ref[k:k + 1, cols]
        hm_ref[:, cols] = (_silu(c) * up).astype(BF16)
    o_ref[...] = x + jnp.dot(hm_ref[...], wd_ref[...], preferred_element_type=F32)


def _ffn_prefill(x, nw, wgu, cw, cb, wd, seq, tm):
    m = x.shape[0]
    d_ff = wd.shape[0]
    assert m % tm == 0 and seq % tm == 0 and tm >= HIST_ROWS
    whole = lambda a: pl.BlockSpec(a.shape, lambda i: (0, 0), pipeline_mode=pl.Buffered(1))
    ins = [nw.reshape(1, D_MODEL), wgu, cw, cb.reshape(1, d_ff), wd]
    return pl.pallas_call(
        functools.partial(_ffn_prefill_kernel, tiles_per_seq=seq // tm),
        grid=(m // tm,),
        in_specs=[pl.BlockSpec((tm, D_MODEL), lambda i: (i, 0))] + [whole(a) for a in ins],
        out_specs=[pl.BlockSpec((tm, D_MODEL), lambda i: (i, 0)),
                   pl.BlockSpec((None, HIST_ROWS, d_ff), lambda i: (i, 0, 0))],
        out_shape=[jax.ShapeDtypeStruct((m, D_MODEL), F32),
                   jax.ShapeDtypeStruct((m // tm, HIST_ROWS, d_ff), F32)],
        scratch_shapes=[pltpu.VMEM((CONV_BUFS, HIST_ROWS + tm, MXU_COLS), F32),
                        pltpu.VMEM((HIST_ROWS, d_ff), F32),
                        pltpu.VMEM((tm, d_ff), BF16)],
        compiler_params=_params("arbitrary"),
    )(x, *ins)


def _ffn_decode_kernel(x_ref, g_ref, h0_ref, h1_ref, u_ref, cw_ref, cb_ref, wd_ref, o_ref):
    acc = cb_ref[...] + h0_ref[...] * cw_ref[0:1, :] + h1_ref[...] * cw_ref[1:2, :] + g_ref[...] * cw_ref[2:3, :]
    hmid = _silu(acc) * u_ref[...]
    o_ref[...] = x_ref[...] + jnp.dot(hmid.astype(BF16), wd_ref[...], preferred_element_type=F32)


def _ffn_decode(x, gu, hist0, hist1, cw, cb, wd):
    m = x.shape[0]
    d_ff = wd.shape[0]
    rows = lambda width, col: pl.BlockSpec((m, width), lambda i: (0, col))
    return pl.pallas_call(
        _ffn_decode_kernel,
        grid=(1,),
        in_specs=[rows(D_MODEL, 0), rows(d_ff, 0), rows(d_ff, 0), rows(d_ff, 0), rows(d_ff, 1),
                  pl.BlockSpec((FFN_CONV, d_ff), lambda i: (0, 0)),
                  pl.BlockSpec((1, d_ff), lambda i: (0, 0)),
                  pl.BlockSpec((d_ff, D_MODEL), lambda i: (0, 0))],
        out_specs=rows(D_MODEL, 0),
        out_shape=jax.ShapeDtypeStruct((m, D_MODEL), F32),
        compiler_params=_params("arbitrary"),
    )(x, gu, hist0, hist1, gu, cw, cb.reshape(1, d_ff), wd)


def _expand_lanes(rows, first_lane, width, n_out, parts):
    k = lax.broadcasted_iota(jnp.int32, (LANES, n_out), 0)
    l = lax.broadcasted_iota(jnp.int32, (LANES, n_out), 1)
    sel = jnp.where(l // width == k - first_lane, 1.0, 0.0)
    return _mm_lhs_split(rows, sel, parts)


def _decode_rows_kernel(proj_ref, sh0, sh1, sh2, gh0, gh1, gh2,
                        scw_ref, scb_ref, gcw_ref, bias_ref, alog_ref,
                        xs_ref, xdt_ref, b_ref, c_ref, sdec_ref,
                        q_ref, k_ref, v_ref, beta_ref, gdec_ref):
    def conv(hists, new, cw, cb):
        acc = new * cw[3:4, :]
        for t, hr in enumerate(hists):
            acc = acc + hr[...] * cw[t:t + 1, :]
        return acc if cb is None else acc + cb

    xbc_new = jnp.concatenate([proj_ref[:, COL_XS:COL_XS + SSD_INNER],
                               proj_ref[:, COL_B:COL_B + SSD_BC],
                               proj_ref[:, COL_C:COL_C + SSD_BC]], axis=-1)
    xbc = _silu(conv((sh0, sh1, sh2), xbc_new, scw_ref, scb_ref[...]))
    sp, sg, log_dec = _gate_rows(proj_ref[:, COL_SMALL:COL_SMALL + LANES], bias_ref[...], alog_ref[...])
    xs = xbc[:, 0:SSD_INNER]
    xs_ref[...] = xs
    xdt_ref[...] = xs * _expand_lanes(sp, LANE_DT, SSD_HEAD_DIM, SSD_INNER, 3)
    b_ref[...] = xbc[:, SSD_INNER:SSD_INNER + SSD_BC]
    c_ref[...] = xbc[:, SSD_INNER + SSD_BC:SSD_INNER + 2 * SSD_BC]
    sdec_ref[...] = jnp.exp(_expand_lanes(log_dec, LANE_DT, SSD_HEAD_DIM, SSD_INNER, 3))

    qkv_new = proj_ref[:, COL_Q:COL_Q + 3 * GDN_DIM]
    qkv = _silu(conv((gh0, gh1, gh2), qkv_new, gcw_ref, None))
    for h in range(GDN_HEADS):
        lo = h * GDN_HEAD_K
        qh = qkv[:, lo:lo + GDN_HEAD_K]
        kh = qkv[:, GDN_DIM + lo:GDN_DIM + lo + GDN_HEAD_K]
        q_ref[:, lo:lo + GDN_HEAD_K] = qh * lax.rsqrt(jnp.sum(qh * qh, axis=-1, keepdims=True) + L2_EPS) * \
            (GDN_HEAD_K ** -0.5)
        k_ref[:, lo:lo + GDN_HEAD_K] = kh * lax.rsqrt(jnp.sum(kh * kh, axis=-1, keepdims=True) + L2_EPS)
    v_ref[...] = qkv[:, 2 * GDN_DIM:3 * GDN_DIM]
    beta_ref[...] = _expand_lanes(sg, LANE_BETA, GDN_HEAD_V, GDN_DIM, 3)
    gdec_ref[...] = jnp.exp(_expand_lanes(log_dec, LANE_G, GDN_HEAD_V, GDN_DIM, 3))


def _decode_rows(proj, ssd_hist, gdn_hist, lw):
    m = proj.shape[0]
    ins = [proj] + [ssd_hist[:, t] for t in range(SSD_CONV - 1)] + [gdn_hist[:, t] for t in range(GDN_CONV - 1)]
    ins += [lw['ssd_conv_w'], lw['ssd_conv_b'], lw['gdn_conv_w'], lw['gate_bias'], lw['gate_alog']]
    widths = [SSD_INNER, SSD_INNER, SSD_BC, SSD_BC, SSD_INNER, GDN_DIM, GDN_DIM, GDN_DIM, GDN_DIM, GDN_DIM]
    return pl.pallas_call(
        _decode_rows_kernel,
        grid=(1,),
        in_specs=[pl.BlockSpec(a.shape, lambda i, nd=a.ndim: (0,) * nd) for a in ins],
        out_specs=[pl.BlockSpec((m, w), lambda i: (0, 0)) for w in widths],
        out_shape=[jax.ShapeDtypeStruct((m, w), F32) for w in widths],
        compiler_params=_params("arbitrary"),
    )(*ins)


def _rows_to_cols(rows, i, parts):
    tb = rows.shape[0]
    r = lax.broadcasted_iota(jnp.int32, (tb, LANES), 0)
    sel = jnp.where(r == i, 1.0, 0.0).astype(BF16)
    acc = None
    for p in _split(rows, parts):
        t = lax.dot_general(p, sel, (((0,), (0,)), ((), ())), preferred_element_type=F32)
        acc = t if acc is None else acc + t
    return acc


def _only_row(rows, i):
    r = lax.broadcasted_iota(jnp.int32, rows.shape, 0)
    return jnp.where(r == i, rows, 0.0)


def _decode_state_kernel(xdt_ref, b_ref, c_ref, sdec_ref, q_ref, k_ref, v_ref, beta_ref, gdec_ref,
                         hs_ref, ss_ref, *rest):
    y_ref, o_ref, hs_out, ss_out = rest[-4:]
    tb = xdt_ref.shape[0]
    xdt = xdt_ref[...]
    bm = b_ref[...]
    cm = c_ref[...]
    sdec = sdec_ref[...]
    qn = q_ref[...]
    kn = k_ref[...]
    vv = v_ref[...]
    beta = beta_ref[...]
    gdec = gdec_ref[...]
    rows_g = SSD_INNER // SSD_GROUPS

    y_acc = jnp.zeros((tb, SSD_INNER), F32)
    ks_acc = jnp.zeros((tb, GDN_DIM), F32)
    for i in range(tb):
        dec_c = _rows_to_cols(sdec, i, 2)
        xdt_c = _rows_to_cols(xdt, i, 1)
        pieces = []
        for g in range(SSD_GROUPS):
            rs = slice(g * rows_g, (g + 1) * rows_g)
            bg = bm[i:i + 1, g * SSD_STATE:(g + 1) * SSD_STATE]
            h_new = hs_ref[i, rs, :] * dec_c[rs, :] + xdt_c[rs, :] * bg
            hs_out[i, rs, :] = h_new
            cg = _only_row(cm[:, g * SSD_STATE:(g + 1) * SSD_STATE], i)
            pieces.append(_mm_nt(cg, h_new))
        y_acc = y_acc + jnp.concatenate(pieces, axis=-1)
        g_c = _rows_to_cols(gdec, i, 2)
        s_dec = ss_ref[i] * g_c
        ss_out[i] = s_dec
        pieces = []
        for h in range(GDN_HEADS):
            rs = slice(h * GDN_HEAD_K, (h + 1) * GDN_HEAD_K)
            pieces.append(_mm(_only_row(kn[:, rs], i), s_dec[rs, :]))
        ks_acc = ks_acc + jnp.concatenate(pieces, axis=-1)
    y_ref[...] = y_acc

    delta = beta * (vv - ks_acc)
    o_acc = jnp.zeros((tb, GDN_DIM), F32)
    for i in range(tb):
        k_c = _rows_to_cols(kn, i, 1)
        pieces = []
        for h in range(GDN_HEADS):
            rs = slice(h * GDN_HEAD_K, (h + 1) * GDN_HEAD_K)
            d_row = delta[i:i + 1, rs]
            s_new = ss_out[i, rs, :] + k_c[rs, :] * d_row
            ss_out[i, rs, :] = s_new
            pieces.append(_mm(_only_row(qn[:, rs], i), s_new))
        o_acc = o_acc + jnp.concatenate(pieces, axis=-1)
    o_ref[...] = o_acc


def _decode_state(rows, hs_all, ss_all, layer, tb, carried):
    xs, xdt, bm, cm, sdec, qn, kn, vv, beta, gdec = rows
    nb = xdt.shape[0]
    row_ins = [xdt, bm, cm, sdec, qn, kn, vv, beta, gdec]
    rspec = lambda a: pl.BlockSpec((tb, a.shape[1]), lambda i: (i, 0))
    sspec = lambda a: pl.BlockSpec((None, tb) + a.shape[2:], lambda i: (layer, i, 0, 0))
    n_in = len(row_ins) + 2
    aliases = {n_in + j: 2 + j for j in range(len(carried))}
    return pl.pallas_call(
        _decode_state_kernel,
        grid=(nb // tb,),
        in_specs=[rspec(a) for a in row_ins] + [sspec(hs_all), sspec(ss_all)] +
                 [pl.BlockSpec(memory_space=pl.ANY) for _ in carried],
        out_specs=[rspec(xdt), rspec(qn), sspec(hs_all), sspec(ss_all)],
        out_shape=[jax.ShapeDtypeStruct(xdt.shape, F32), jax.ShapeDtypeStruct(qn.shape, F32),
                   jax.ShapeDtypeStruct(hs_all.shape, F32), jax.ShapeDtypeStruct(ss_all.shape, F32)],
        input_output_aliases=aliases,
        compiler_params=_params("parallel"),
    )(*row_ins, hs_all, ss_all, *carried)


def _decode_out_kernel(x_ref, y_ref, xs_ref, o_ref, proj_ref, dexp_ref, snw_ref, gnw_ref, w1_ref, w2_ref, out_ref):
    y = y_ref[...] + dexp_ref[...] * xs_ref[...]
    gs = SSD_INNER // SSD_GROUPS
    ys = []
    for g in range(SSD_GROUPS):
        yg = y[:, g * gs:(g + 1) * gs] * _silu(proj_ref[:, COL_Z_SSD + g * gs:COL_Z_SSD + (g + 1) * gs])
        ms = jnp.mean(yg * yg, axis=-1, keepdims=True)
        ys.append(yg * lax.rsqrt(ms + RMS_EPS) * snw_ref[:, g * gs:(g + 1) * gs])
    os = []
    for h in range(GDN_HEADS):
        lo = h * GDN_HEAD_V
        oh = o_ref[:, lo:lo + GDN_HEAD_V]
        ms = jnp.mean(oh * oh, axis=-1, keepdims=True)
        os.append(oh * lax.rsqrt(ms + RMS_EPS) * gnw_ref[...] *
                  _silu(proj_ref[:, COL_Z_GDN + lo:COL_Z_GDN + lo + GDN_HEAD_V]))
    yn = jnp.concatenate(ys, axis=-1)
    on = jnp.concatenate(os, axis=-1)
    out_ref[...] = x_ref[...] + _mm(yn, w1_ref[...]) + _mm(on, w2_ref[...])


def _decode_out(x, y, xs, o, proj, lw):
    ins = [x, y, xs, o, proj, lw['ssd_d_exp'], lw['ssd_norm_w'], lw['gdn_norm_w'], lw['w_out_ssd'], lw['w_out_gdn']]
    return pl.pallas_call(
        _decode_out_kernel,
        grid=(1,),
        in_specs=[pl.BlockSpec(a.shape, lambda i, nd=a.ndim: (0,) * nd) for a in ins],
        out_specs=pl.BlockSpec(x.shape, lambda i: (0, 0)),
        out_shape=jax.ShapeDtypeStruct(x.shape, F32),
        compiler_params=_params("arbitrary"),
    )(*ins)


def _identity_taps(taps, n):
    return jnp.concatenate([jnp.zeros((taps - 1, n), F32), jnp.ones((1, n), F32)], axis=0)


def _prep_layer(i, p):
    w_in = p['w_in'][i]
    o_zs = 0
    o_xbc = o_zs + SSD_INNER
    o_dt = o_xbc + SSD_INNER + 2 * SSD_BC
    o_qkv = o_dt + SSD_HEADS
    o_zg = o_qkv + 3 * GDN_DIM
    o_b = o_zg + GDN_DIM
    o_a = o_b + GDN_HEADS
    used = SSD_HEADS + 2 * GDN_HEADS
    w_perm = jnp.concatenate([
        w_in[:, o_zs:o_zs + SSD_INNER],
        w_in[:, o_xbc:o_xbc + SSD_INNER],
        w_in[:, o_qkv:o_qkv + 3 * GDN_DIM],
        w_in[:, o_zg:o_zg + GDN_DIM],
        w_in[:, o_xbc + SSD_INNER:o_xbc + SSD_INNER + 2 * SSD_BC],
        w_in[:, o_dt:o_dt + SSD_HEADS],
        w_in[:, o_b:o_b + GDN_HEADS],
        w_in[:, o_a:o_a + GDN_HEADS],
        jnp.zeros((D_MODEL, PROJ_COLS - COL_SMALL - used), F32)], axis=1).astype(BF16)
    lane_pad = jnp.zeros((LANES - used,), F32)
    scw = p['ssd_conv_w'][i]
    scb = p['ssd_conv_b'][i].reshape(1, -1)
    gcw = p['gdn_conv_w'][i]
    return {
        'w_in': w_perm,
        'norm_mix_w': p['norm_mix_w'][i],
        'ssd_conv_w': scw, 'ssd_conv_b': scb, 'gdn_conv_w': gcw,
        'in_cw': jnp.concatenate([_identity_taps(SSD_CONV, SSD_INNER), scw[:, :SSD_INNER], gcw,
                                  _identity_taps(GDN_CONV, GDN_DIM), scw[:, SSD_INNER:],
                                  _identity_taps(SSD_CONV, PROJ_COLS - COL_SMALL)], axis=1),
        'in_cb': jnp.concatenate([jnp.zeros((1, COL_XS), F32), scb[:, :SSD_INNER],
                                  jnp.zeros((1, COL_B - COL_Q), F32), scb[:, SSD_INNER:],
                                  jnp.zeros((1, PROJ_COLS - COL_SMALL), F32)], axis=1),
        'gate_bias': jnp.concatenate([p['ssd_dt_bias'][i], jnp.zeros((GDN_HEADS,), F32), p['gdn_dt_bias'][i],
                                      lane_pad]).reshape(1, LANES),
        'gate_alog': jnp.concatenate([p['ssd_a_log'][i], jnp.zeros((GDN_HEADS,), F32), p['gdn_a_log'][i],
                                      lane_pad]).reshape(1, LANES),
        'ssd_d_exp': jnp.repeat(p['ssd_d'][i], SSD_HEAD_DIM).reshape(1, SSD_INNER),
        'ssd_norm_w': p['ssd_norm_w'][i].reshape(1, SSD_INNER),
        'gdn_norm_w': p['gdn_norm_w'][i].reshape(1, GDN_HEAD_V),
        'w_out_ssd': p['w_out'][i, :SSD_INNER].astype(BF16),
        'w_out_gdn': p['w_out'][i, SSD_INNER:].astype(BF16),
        'norm_xa_w': p['norm_xa_w'][i], 'norm_mem_w': p['norm_mem_w'][i],
        'xa_wq': p['xa_wq'][i].astype(BF16), 'xa_wo': p['xa_wo'][i].astype(BF16),
        'xa_wkv': jnp.concatenate([p['xa_wk'][i], p['xa_wv'][i]], axis=1).astype(BF16),
        'norm_ffn_w': p['norm_ffn_w'][i],
        'ffn_w_gu': jnp.concatenate([p['ffn_w_gate'][i], p['ffn_w_up'][i]], axis=1).astype(BF16),
        'ffn_conv_w': p['ffn_conv_w'][i], 'ffn_conv_b': p['ffn_conv_b'][i],
        'ffn_w_down': p['ffn_w_down'][i].astype(BF16),
    }


def _unpermute_conv_rows(proj_rows):
    ssd = jnp.concatenate([proj_rows[..., COL_XS:COL_XS + SSD_INNER],
                           proj_rows[..., COL_B:COL_B + SSD_BC],
                           proj_rows[..., COL_C:COL_C + SSD_BC]], axis=-1)
    gdn = proj_rows[..., COL_Q:COL_Q + 3 * GDN_DIM]
    return ssd, gdn


def _tile(m, pref):
    t = min(m, pref)
    while m % t:
        t //= 2
    return t


def kernel(x_prompt, x_sample, mem_prompt, cache_mem_k, cache_mem_v, state_ssd_conv, state_ssd, state_gdn_conv, state_gdn, state_ffn_conv, norm_mix_w, w_in, ssd_conv_w, ssd_conv_b, ssd_dt_bias, ssd_a_log, ssd_d, ssd_norm_w, gdn_conv_w, gdn_dt_bias, gdn_a_log, gdn_norm_w, w_out, norm_xa_w, norm_mem_w, xa_wq, xa_wk, xa_wv, xa_wo, norm_ffn_w, ffn_w_gate, ffn_w_up, ffn_conv_w, ffn_conv_b, ffn_w_down, final_norm_w):
    params = dict(norm_mix_w=norm_mix_w, w_in=w_in, ssd_conv_w=ssd_conv_w, ssd_conv_b=ssd_conv_b,
                  ssd_dt_bias=ssd_dt_bias, ssd_a_log=ssd_a_log, ssd_d=ssd_d, ssd_norm_w=ssd_norm_w,
                  gdn_conv_w=gdn_conv_w, gdn_dt_bias=gdn_dt_bias, gdn_a_log=gdn_a_log, gdn_norm_w=gdn_norm_w,
                  w_out=w_out, norm_xa_w=norm_xa_w, norm_mem_w=norm_mem_w, xa_wq=xa_wq, xa_wk=xa_wk, xa_wv=xa_wv,
                  xa_wo=xa_wo, norm_ffn_w=norm_ffn_w, ffn_w_gate=ffn_w_gate, ffn_w_up=ffn_w_up,
                  ffn_conv_w=ffn_conv_w, ffn_conv_b=ffn_conv_b, ffn_w_down=ffn_w_down)
    depth = w_in.shape[0]
    bsz, seq, _ = x_prompt.shape
    nb = x_sample.shape[0]
    n_mem = mem_prompt.shape[1]
    d_ff = ffn_w_down.shape[1]
    mp = bsz * seq

    xp = x_prompt.reshape(mp, D_MODEL)
    xs = x_sample.reshape(nb, D_MODEL)
    mem = mem_prompt.reshape(bsz * n_mem, D_MODEL)
    tm_p = _tile(seq, 1024)
    tm_c = _tile(seq, 256)
    tm_mem = _tile(bsz * n_mem, 1024)
    dec_tile = _tile(nb, DEC_TILE)

    hs_all = state_ssd.reshape(depth, nb, SSD_INNER, SSD_STATE)
    ss_all = state_gdn.reshape(depth, nb, GDN_HEADS * GDN_HEAD_K, GDN_HEAD_V)
    mk_all, mv_all = cache_mem_k, cache_mem_v
    new_states = ()

    mkp, mvp = [], []
    p_sc, p_sh, p_gc, p_gs, p_fc = [], [], [], [], []
    s_sc, s_gc, s_fc = [], [], []
    for i in range(depth):
        lw = _prep_layer(i, params)

        kv = _norm_mm(mem, lw['norm_mem_w'], lw['xa_wkv'], tm_mem)
        mk = kv[:, :D_MODEL].reshape(bsz, n_mem, D_MODEL)
        mv = kv[:, D_MODEL:].reshape(bsz, n_mem, D_MODEL)
        mkp.append(mk.reshape(bsz, n_mem, XA_HEADS, XA_HEAD_DIM))
        mvp.append(mv.reshape(bsz, n_mem, XA_HEADS, XA_HEAD_DIM))

        proj, tails = _norm_mm_conv(xp, lw['norm_mix_w'], lw['w_in'], lw['in_cw'], lw['in_cb'],
                                    [(COL_XS, COL_Z_GDN), (COL_B, COL_SMALL)], seq, tm_c)
        y, h_new, o, s_new = _mixer_prefill(proj, bsz, seq, lw)
        seq_tails = tails.reshape(bsz, seq // tm_c, HIST_ROWS, PROJ_COLS)[:, -1]
        ssd_tail, gdn_tail = _unpermute_conv_rows(seq_tails[:, HIST_ROWS - (SSD_CONV - 1):])
        p_sc.append(ssd_tail)
        p_gc.append(gdn_tail)
        p_sh.append(h_new.reshape(bsz, SSD_HEADS, SSD_HEAD_DIM, SSD_STATE))
        p_gs.append(s_new.reshape(bsz, GDN_HEADS, GDN_HEAD_K, GDN_HEAD_V))
        xp = _mm_res(xp, [y, o], [lw['w_out_ssd'], lw['w_out_gdn']], _tile(mp, 512))
        xp = _xattn_prefill(xp, lw['norm_xa_w'], lw['xa_wq'], mk, mv, lw['xa_wo'], bsz, seq, _tile(seq, 512))
        xp, gate_tails = _ffn_prefill(xp, lw['norm_ffn_w'], lw['ffn_w_gu'], lw['ffn_conv_w'], lw['ffn_conv_b'],
                                      lw['ffn_w_down'], seq, tm_c)
        p_fc.append(gate_tails.reshape(bsz, seq // tm_c, HIST_ROWS, d_ff)[:, -1, HIST_ROWS - (FFN_CONV - 1):])

        proj_s = _norm_mm(xs, lw['norm_mix_w'], lw['w_in'], nb)
        ssd_new, gdn_new = _unpermute_conv_rows(proj_s)
        s_sc.append(jnp.concatenate([state_ssd_conv[i][:, 1:], ssd_new[:, None]], axis=1))
        s_gc.append(jnp.concatenate([state_gdn_conv[i][:, 1:], gdn_new[:, None]], axis=1))
        rows = _decode_rows(proj_s, state_ssd_conv[i], state_gdn_conv[i], lw)
        y_s, o_s, hs_new, ss_new = _decode_state(rows, hs_all, ss_all, i, dec_tile, new_states)
        new_states = (hs_new, ss_new)
        xs = _decode_out(xs, y_s, rows[0], o_s, proj_s, lw)
        q_s = _norm_mm(xs, lw['norm_xa_w'], lw['xa_wq'], nb)
        att = _xattn_decode(q_s, mk_all, mv_all, i, _tile(nb, 4))
        xs = _mm_res(xs, [att], [lw['xa_wo']], nb)
        gu_s = _norm_mm(xs, lw['norm_ffn_w'], lw['ffn_w_gu'], nb)
        s_fc.append(jnp.concatenate([state_ffn_conv[i][:, 1:], gu_s[:, None, :d_ff]], axis=1))
        xs = _ffn_decode(xs, gu_s, state_ffn_conv[i][:, 0], state_ffn_conv[i][:, 1],
                         lw['ffn_conv_w'], lw['ffn_conv_b'], lw['ffn_w_down'])

    y_prompt = _rmsnorm(xp, final_norm_w, tm_p).reshape(bsz, seq, D_MODEL)
    y_sample = _rmsnorm(xs, final_norm_w, nb).reshape(nb, 1, D_MODEL)
    return (y_prompt, y_sample, jnp.stack(mkp), jnp.stack(mvp),
            jnp.stack(p_sc), jnp.stack(p_sh), jnp.stack(p_gc), jnp.stack(p_gs), jnp.stack(p_fc),
            jnp.stack(s_sc), new_states[0].reshape(state_ssd.shape), jnp.stack(s_gc),
            new_states[1].reshape(state_gdn.shape), jnp.stack(s_fc))
```

```python
import functools

import jax
import jax.numpy as jnp
from jax import lax
from jax.experimental import pallas as pl
from jax.experimental.pallas import tpu as pltpu

F32 = jnp.float32
BF16 = jnp.bfloat16

D_MODEL = 1024
SSD_HEADS = 16
SSD_HEAD_DIM = 64
SSD_STATE = 128
SSD_GROUPS = 2
SSD_INNER = SSD_HEADS * SSD_HEAD_DIM
SSD_BC = SSD_GROUPS * SSD_STATE
SSD_CONV = 4
GDN_HEADS = 8
GDN_HEAD_K = 128
GDN_HEAD_V = 128
GDN_DIM = GDN_HEADS * GDN_HEAD_V
GDN_CONV = 4
XA_HEADS = 4
XA_HEAD_DIM = D_MODEL // XA_HEADS
FFN_CONV = 3
RMS_EPS = 1e-6
L2_EPS = 1e-6

SUBLANES = 8
LANES = 128
MXU_COLS = 256
VMEM_LIMIT_BYTES = 56 * 1024 * 1024

COL_Z_SSD = 0
COL_XS = 1024
COL_Q = 2048
COL_K = 3072
COL_V = 4096
COL_Z_GDN = 5120
COL_B = 6144
COL_C = 6400
COL_SMALL = 6656
PROJ_COLS = 7168
LANE_DT = 0
LANE_BETA = 16
LANE_G = 24

MIXER_CHUNK = 128
DEC_TILE = 8


def _params(*sem):
    return pltpu.CompilerParams(dimension_semantics=sem, vmem_limit_bytes=VMEM_LIMIT_BYTES)


def _mm(a, b):
    return jnp.dot(a.astype(BF16), b.astype(BF16), preferred_element_type=F32)


def _mm_nt(a, b):
    return lax.dot_general(a.astype(BF16), b.astype(BF16), (((1,), (1,)), ((), ())),
                           preferred_element_type=F32)


def _split(x, parts):
    out = []
    r = x
    for _ in range(parts - 1):
        p = r.astype(BF16)
        out.append(p)
        r = r - p.astype(F32)
    out.append(r.astype(BF16))
    return out


def _mm_rhs_split(a_exact, b, parts=3):
    a16 = a_exact.astype(BF16)
    acc = None
    for p in _split(b, parts):
        t = jnp.dot(a16, p, preferred_element_type=F32)
        acc = t if acc is None else acc + t
    return acc


def _mm_lhs_split(a, b_exact, parts=3):
    b16 = b_exact.astype(BF16)
    acc = None
    for p in _split(a, parts):
        t = jnp.dot(p, b16, preferred_element_type=F32)
        acc = t if acc is None else acc + t
    return acc


def _silu(x):
    h = 0.5 * x
    return h + h * jnp.tanh(h)


def _sigmoid(x):
    return 1.0 / (1.0 + jnp.exp(-x))


def _softplus(x):
    return jnp.maximum(x, 0.0) + jnp.log1p(jnp.exp(-jnp.abs(x)))


def _col(arr, lane):
    idx = lax.broadcasted_iota(jnp.int32, arr.shape, 1)
    return jnp.sum(jnp.where(idx == lane, arr, 0.0), axis=1, keepdims=True)


def _gate_rows(sm, bias, alog):
    sp = _softplus(sm + bias)
    sg = _sigmoid(sm)
    lane = lax.broadcasted_iota(jnp.int32, alog.shape, 1)
    has_decay = (lane < LANE_BETA) | ((lane >= LANE_G) & (lane < LANE_G + GDN_HEADS))
    neg_a = jnp.where(has_decay, -jnp.exp(alog), 0.0)
    return sp, sg, sp * neg_a


def _norm_mm_kernel(x_ref, nw_ref, w_ref, o_ref, xn_ref):
    @pl.when(pl.program_id(1) == 0)
    def _():
        x = x_ref[...]
        ms = jnp.mean(x * x, axis=-1, keepdims=True)
        xn_ref[...] = (x * lax.rsqrt(ms + RMS_EPS) * nw_ref[...]).astype(BF16)

    o_ref[...] = jnp.dot(xn_ref[...], w_ref[...], preferred_element_type=F32)


MAX_COL_TILE_LANES = 11 * LANES


def _col_tile(n):
    return max(t for t in range(LANES, MAX_COL_TILE_LANES + 1, LANES) if n % t == 0)


def _norm_mm(x, nw, w, tm):
    m, k = x.shape
    n = w.shape[1]
    tn = _col_tile(n)
    assert m % tm == 0
    return pl.pallas_call(
        _norm_mm_kernel,
        grid=(m // tm, n // tn),
        in_specs=[pl.BlockSpec((tm, k), lambda i, j: (i, 0)),
                  pl.BlockSpec((1, k), lambda i, j: (0, 0)),
                  pl.BlockSpec((k, tn), lambda i, j: (0, j))],
        out_specs=pl.BlockSpec((tm, tn), lambda i, j: (i, j)),
        out_shape=jax.ShapeDtypeStruct((m, n), F32),
        scratch_shapes=[pltpu.VMEM((tm, k), BF16)],
        compiler_params=_params("parallel", "arbitrary"),
    )(x, nw.reshape(1, k), w)


HIST_ROWS = SUBLANES


CONV_BUFS = 4


def _norm_mm_conv_kernel(x_ref, nw_ref, w_ref, cw_ref, cb_ref, o_ref, tail_ref, ext_ref, hist_ref,
                         *, taps, tiles_per_seq, conv_ranges):
    i = pl.program_id(0)
    tm = x_ref.shape[0]
    n = w_ref.shape[1]

    @pl.when(i == 0)
    def _():
        hist_ref[...] = jnp.zeros(hist_ref.shape, F32)

    x = x_ref[...]
    ms = jnp.mean(x * x, axis=-1, keepdims=True)
    xn = (x * lax.rsqrt(ms + RMS_EPS) * nw_ref[...]).astype(BF16)
    starts_seq = i % tiles_per_seq == 0
    for ci, lo in enumerate(range(0, n, MXU_COLS)):
        cols = slice(lo, min(lo + MXU_COLS, n))
        acc = jnp.dot(xn, w_ref[:, cols], preferred_element_type=F32)
        tail_ref[:, cols] = acc[tm - HIST_ROWS:tm, :]
        if not any(a <= cols.start and cols.stop <= b for a, b in conv_ranges):
            o_ref[:, cols] = acc
            continue
        buf = ext_ref.at[ci % CONV_BUFS]
        width = cols.stop - cols.start
        buf[0:HIST_ROWS, 0:width] = jnp.where(starts_seq, 0.0, hist_ref[:, cols])
        buf[HIST_ROWS:HIST_ROWS + tm, 0:width] = acc
        hist_ref[:, cols] = acc[tm - HIST_ROWS:tm, :]
        o_ref[:, cols] = _silu(_causal_taps(buf[:, 0:width], cw_ref, cb_ref[:, cols], cols, taps, tm))


def _causal_taps(ext, cw_ref, bias, cols, taps, tm):
    rows = slice(HIST_ROWS, HIST_ROWS + tm)
    w = [cw_ref[k:k + 1, cols] for k in range(taps)]
    if taps == 4:
        lag1 = pltpu.roll(ext, 1, axis=0)
        older = pltpu.roll(ext * w[1] + lag1 * w[0], 2, axis=0)
        return bias + ext[rows, :] * w[3] + lag1[rows, :] * w[2] + older[rows, :]
    out = bias + ext[rows, :] * w[taps - 1]
    for lag in range(1, taps):
        out = out + pltpu.roll(ext, lag, axis=0)[rows, :] * w[taps - 1 - lag]
    return out


def _norm_mm_conv(x, nw, w, cw, cb, conv_ranges, seq, tm):
    m, k = x.shape
    n = w.shape[1]
    taps = cw.shape[0]
    assert m % tm == 0 and seq % tm == 0 and tm >= HIST_ROWS
    assert all(a % MXU_COLS == 0 and b % MXU_COLS == 0 for a, b in conv_ranges)
    whole = lambda a: pl.BlockSpec(a.shape, lambda i: (0, 0), pipeline_mode=pl.Buffered(1))
    nw2 = nw.reshape(1, k)
    return pl.pallas_call(
        functools.partial(_norm_mm_conv_kernel, taps=taps, tiles_per_seq=seq // tm,
                          conv_ranges=tuple(conv_ranges)),
        grid=(m // tm,),
        in_specs=[pl.BlockSpec((tm, k), lambda i: (i, 0)), whole(nw2), whole(w), whole(cw), whole(cb)],
        out_specs=[pl.BlockSpec((tm, n), lambda i: (i, 0)),
                   pl.BlockSpec((None, HIST_ROWS, n), lambda i: (i, 0, 0))],
        out_shape=[jax.ShapeDtypeStruct((m, n), F32),
                   jax.ShapeDtypeStruct((m // tm, HIST_ROWS, n), F32)],
        scratch_shapes=[pltpu.VMEM((CONV_BUFS, HIST_ROWS + tm, MXU_COLS), F32),
                        pltpu.VMEM((HIST_ROWS, n), F32)],
        compiler_params=_params("arbitrary"),
    )(x, nw2, w, cw, cb)


def _mm_res_kernel(*refs, n_in):
    res_ref = refs[0]
    a_refs = refs[1:1 + n_in]
    w_refs = refs[1 + n_in:1 + 2 * n_in]
    o_ref = refs[1 + 2 * n_in]
    acc = res_ref[...]
    for a_ref, w_ref in zip(a_refs, w_refs):
        acc = acc + jnp.dot(a_ref[...].astype(BF16), w_ref[...], preferred_element_type=F32)
    o_ref[...] = acc


def _mm_res(res, a_list, w_list, tm):
    m, n = res.shape
    assert m % tm == 0
    n_in = len(a_list)
    in_specs = [pl.BlockSpec((tm, n), lambda i: (i, 0))]
    in_specs += [pl.BlockSpec((tm, a.shape[1]), lambda i: (i, 0)) for a in a_list]
    in_specs += [pl.BlockSpec(w.shape, lambda i: (0, 0)) for w in w_list]
    return pl.pallas_call(
        functools.partial(_mm_res_kernel, n_in=n_in),
        grid=(m // tm,),
        in_specs=in_specs,
        out_specs=pl.BlockSpec((tm, n), lambda i: (i, 0)),
        out_shape=jax.ShapeDtypeStruct((m, n), F32),
        compiler_params=_params("parallel"),
    )(res, *a_list, *w_list)


def _rmsnorm_kernel(x_ref, nw_ref, o_ref):
    x = x_ref[...]
    ms = jnp.mean(x * x, axis=-1, keepdims=True)
    o_ref[...] = x * lax.rsqrt(ms + RMS_EPS) * nw_ref[...]


def _rmsnorm(x, nw, tm):
    m, k = x.shape
    return pl.pallas_call(
        _rmsnorm_kernel,
        grid=(m // tm,),
        in_specs=[pl.BlockSpec((tm, k), lambda i: (i, 0)), pl.BlockSpec((1, k), lambda i: (0, 0))],
        out_specs=pl.BlockSpec((tm, k), lambda i: (i, 0)),
        out_shape=jax.ShapeDtypeStruct((m, k), F32),
        compiler_params=_params("parallel"),
    )(x, nw.reshape(1, k))


def _causal_mask(cs, strict=False):
    r = lax.broadcasted_iota(jnp.int32, (cs, cs), 0)
    c = lax.broadcasted_iota(jnp.int32, (cs, cs), 1)
    return (r > c) if strict else (r >= c)


def _ssd_chunk(z_ref, xs_ref, b_ref, c_ref, sm_ref, bias_ref, alog_ref, dexp_ref, nw_ref, y_ref, hst, y_s):
    cs = xs_ref.shape[0]
    xs = xs_ref[...]
    bm = b_ref[...]
    cm = c_ref[...]

    sp, _, dta = _gate_rows(sm_ref[...], bias_ref[...], alog_ref[...])
    causal = _causal_mask(cs)
    cum = _mm_rhs_split(jnp.where(causal, 1.0, 0.0), dta)
    cum_row = cum.T

    lane_lo = lax.broadcasted_iota(jnp.int32, (1, LANES), 1) < SSD_HEAD_DIM
    row_lo = lax.broadcasted_iota(jnp.int32, (LANES, 1), 0) < SSD_HEAD_DIM
    heads_per_group = SSD_HEADS // SSD_GROUPS
    for g in range(SSD_GROUPS):
        cg = cm[:, g * SSD_STATE:(g + 1) * SSD_STATE]
        bg = bm[:, g * SSD_STATE:(g + 1) * SSD_STATE]
        cb = _mm_nt(cg, bg)
        rows_g = heads_per_group * SSD_HEAD_DIM
        y_state = _mm_nt(cg, hst[g * rows_g:(g + 1) * rows_g, :])
        for hp in range(heads_per_group // 2):
            h0 = g * heads_per_group + 2 * hp
            lo = h0 * SSD_HEAD_DIM
            xs_pair = xs[:, lo:lo + LANES]
            ci = [_col(cum, LANE_DT + h0), _col(cum, LANE_DT + h0 + 1)]
            dti = [_col(sp, LANE_DT + h0), _col(sp, LANE_DT + h0 + 1)]
            xdt_pair = xs_pair * jnp.where(lane_lo, dti[0], dti[1])
            y_pair = None
            for e in range(2):
                cj = cum_row[LANE_DT + h0 + e:LANE_DT + h0 + e + 1, :]
                seg = jnp.exp(jnp.where(causal, ci[e] - cj, -jnp.inf))
                keep = lane_lo if e == 0 else jnp.logical_not(lane_lo)
                t = _mm(cb * seg, jnp.where(keep, xdt_pair, 0.0))
                y_pair = t if y_pair is None else y_pair + t
            y_pair = y_pair + y_state[:, 2 * hp * SSD_HEAD_DIM:2 * hp * SSD_HEAD_DIM + LANES] * \
                jnp.where(lane_lo, jnp.exp(ci[0]), jnp.exp(ci[1]))
            y_pair = y_pair + dexp_ref[:, lo:lo + LANES] * xs_pair
            y_s[:, lo:lo + LANES] = y_pair
            cl = [ci[0][cs - 1:cs, :], ci[1][cs - 1:cs, :]]
            to_end = jnp.where(lane_lo, jnp.exp(cl[0] - ci[0]), jnp.exp(cl[1] - ci[1]))
            contrib = _mm((xdt_pair * to_end).T, bg)
            dec = jnp.where(row_lo, jnp.exp(cl[0]), jnp.exp(cl[1]))
            hst[lo:lo + LANES, :] = hst[lo:lo + LANES, :] * dec + contrib

    gs = SSD_INNER // SSD_GROUPS
    for g in range(SSD_GROUPS):
        yg = y_s[:, g * gs:(g + 1) * gs] * _silu(z_ref[:, g * gs:(g + 1) * gs])
        ms = jnp.mean(yg * yg, axis=-1, keepdims=True)
        y_ref[:, g * gs:(g + 1) * gs] = (yg * lax.rsqrt(ms + RMS_EPS) * nw_ref[:, g * gs:(g + 1) * gs]).astype(BF16)


INV_BASE = 16


def _same_block(cs, size):
    r = lax.broadcasted_iota(jnp.int32, (cs, cs), 0) // size
    c = lax.broadcasted_iota(jnp.int32, (cs, cs), 1) // size
    return r == c


def _bdot(a16, b16):
    return jnp.dot(a16, b16, preferred_element_type=F32)


def _mixer_prefill_kernel(q_ref, k_ref, v_ref, z_ref, sm_ref,
                          zs_ref, xs_ref, b_ref, c_ref,
                          bias_ref, alog_ref, nw_ref, dexp_ref, snw_ref,
                          o_ref, st_ref, y_ref, hst_ref,
                          sst, hst, y_s,
                          qd_s, kd_s, rhs_s, a_s, attn_s, x_s, d_s, u_s, w_s, glast_s, *, n_chunks):
    c_idx = pl.program_id(1)
    nseq, cs = q_ref.shape[0], q_ref.shape[1]
    n_units = nseq * GDN_HEADS
    heads = [(s, h, s * GDN_HEADS + h) for s in range(nseq) for h in range(GDN_HEADS)]
    qc, kc, vc = q_ref, k_ref, v_ref
    causal = _causal_mask(cs)
    strict = _causal_mask(cs, strict=True)

    def stage_a(par):
        sg, gc, gc_row = [], [], []
        for s in range(nseq):
            _, sg_s, g_step = _gate_rows(sm_ref[s], bias_ref[...], alog_ref[...])
            gc_s = _mm_rhs_split(jnp.where(causal, 1.0, 0.0), g_step)
            sg.append(sg_s)
            gc.append(gc_s)
            gc_row.append(gc_s.T)
            glast_s[par * nseq + s] = gc_s[cs - 1:cs, :]
        for s, h, u in heads:
            _stage_a_unit(par, s, h, u, sg, gc, gc_row)

    def _stage_a_unit(par, s, h, u, sg, gc, gc_row):
        u = par * n_units + u
        ps = par * nseq + s
        lo = h * GDN_HEAD_K
        qh = qc[s, :, lo:lo + GDN_HEAD_K]
        kh = kc[s, :, lo:lo + GDN_HEAD_K]
        vh = vc[s, :, lo:lo + GDN_HEAD_V]
        qn = qh * lax.rsqrt(jnp.sum(qh * qh, axis=-1, keepdims=True) + L2_EPS) * (GDN_HEAD_K ** -0.5)
        kn = kh * lax.rsqrt(jnp.sum(kh * kh, axis=-1, keepdims=True) + L2_EPS)
        gi = _col(gc[s], LANE_G + h)
        bi = _col(sg[s], LANE_BETA + h)
        gj = gc_row[s][LANE_G + h:LANE_G + h + 1, :]
        decay = jnp.exp(jnp.where(causal, gi - gj, -jnp.inf))
        egi = jnp.exp(gi)
        kb = kn * bi
        kn16 = kn.astype(BF16)
        a_s[u] = jnp.where(strict, lax.dot_general(kb.astype(BF16), kn16, (((1,), (1,)), ((), ())),
                                                   preferred_element_type=F32) * decay, 0.0)
        attn_s[u] = (lax.dot_general(qn.astype(BF16), kn16, (((1,), (1,)), ((), ())),
                                     preferred_element_type=F32) * decay).astype(BF16)
        qd_s[ps, :, lo:lo + GDN_HEAD_K] = (qn * egi).astype(BF16)
        kd_s[ps, :, lo:lo + GDN_HEAD_K] = (kn * jnp.exp(gi[cs - 1:cs, :] - gi)).astype(BF16)
        rhs_s[ps, :, 2 * lo:2 * lo + GDN_HEAD_V] = (vh * bi).astype(BF16)
        rhs_s[ps, :, 2 * lo + GDN_HEAD_V:2 * lo + 2 * GDN_HEAD_V] = (kb * egi).astype(BF16)

    def stages_bcd(par):
        base = _same_block(cs, INV_BASE)
        eye = jnp.where(causal & jnp.logical_not(strict), 1.0, 0.0)
        for _, _, u in heads:
            x = jnp.where(base, -a_s[par * n_units + u], 0.0)
            x_s[u] = x.astype(BF16)
            d_s[u] = eye + x
        span = 2
        while span < INV_BASE:
            for _, _, u in heads:
                x16 = x_s[u]
                x2 = _bdot(x16, x16).astype(BF16)
                x_s[u] = x2
                d = d_s[u]
                d_s[u] = d + _bdot(d.astype(BF16), x2)
            span *= 2
        size = INV_BASE
        while size < cs:
            off = _same_block(cs, 2 * size) & jnp.logical_not(_same_block(cs, size))
            for _, _, u in heads:
                d = d_s[u]
                d16 = d.astype(BF16)
                low = jnp.where(off, a_s[par * n_units + u], 0.0).astype(BF16)
                d_s[u] = d - _bdot(_bdot(d16, low).astype(BF16), d16)
            size *= 2

        for s, h, u in heads:
            lo = h * GDN_HEAD_K
            uw = _bdot(d_s[u].astype(BF16), rhs_s[par * nseq + s, :, 2 * lo:2 * lo + 2 * GDN_HEAD_V])
            u_s[s, :, lo:lo + GDN_HEAD_V] = uw[:, :GDN_HEAD_V]
            w_s[s, :, lo:lo + GDN_HEAD_V] = uw[:, GDN_HEAD_V:].astype(BF16)

        for s, h, u in heads:
            lo = h * GDN_HEAD_K
            ps = par * nseq + s
            st = sst[s, lo:lo + GDN_HEAD_K, :]
            s16 = st.astype(BF16)
            v_new = u_s[s, :, lo:lo + GDN_HEAD_V] - _bdot(w_s[s, :, lo:lo + GDN_HEAD_V], s16)
            v16 = v_new.astype(BF16)
            o_h = _bdot(qd_s[ps, :, lo:lo + GDN_HEAD_K], s16) + _bdot(attn_s[par * n_units + u], v16)
            g_last = _col(glast_s[ps], LANE_G + h)
            sst[s, lo:lo + GDN_HEAD_K, :] = st * jnp.exp(g_last) + lax.dot_general(
                kd_s[ps, :, lo:lo + GDN_HEAD_K], v16, (((0,), (0,)), ((), ())), preferred_element_type=F32)
            ms = jnp.mean(o_h * o_h, axis=-1, keepdims=True)
            o_ref[s, :, lo:lo + GDN_HEAD_V] = ((o_h * lax.rsqrt(ms + RMS_EPS) * nw_ref[...]) *
                                               _silu(z_ref[s, :, lo:lo + GDN_HEAD_V])).astype(BF16)

    def ssd_chunks():
        for s in range(nseq):
            _ssd_chunk(zs_ref.at[s], xs_ref.at[s], b_ref.at[s], c_ref.at[s], sm_ref.at[s], bias_ref, alog_ref,
                       dexp_ref, snw_ref, y_ref.at[s], hst.at[s], y_s.at[s])

    @pl.when(c_idx == 0)
    def _():
        sst[...] = jnp.zeros(sst.shape, F32)
        hst[...] = jnp.zeros(hst.shape, F32)
        stage_a(0)
        ssd_chunks()

    middle = (c_idx > 0) & (c_idx < n_chunks)
    for par in range(2):
        @pl.when(middle & (c_idx % 2 == par))
        def _(par=par):
            stages_bcd(1 - par)
            stage_a(par)
            ssd_chunks()

    @pl.when(c_idx == n_chunks)
    def _():
        stages_bcd((n_chunks - 1) % 2)
        st_ref[...] = sst[...]
        hst_ref[...] = hst[...]


GDN_SEQS_PER_STEP = 2


def _mixer_prefill(proj, bsz, seq, lw):
    cs = MIXER_CHUNK
    nseq = GDN_SEQS_PER_STEP if bsz % GDN_SEQS_PER_STEP == 0 else 1
    assert seq % cs == 0
    nc = seq // cs
    proj3 = proj.reshape(bsz, seq, PROJ_COLS)
    cur = lambda c: jnp.minimum(c, nc - 1)
    prev = lambda c: jnp.maximum(c - 1, 0)
    pcol = lambda width, off, ch: pl.BlockSpec((nseq, cs, width), lambda b, c: (b, ch(c), off // width))
    full = lambda a: pl.BlockSpec(a.shape, lambda b, c: (0,) * a.ndim)
    weights = [lw['gate_bias'], lw['gate_alog'], lw['gdn_norm_w'], lw['ssd_d_exp'], lw['ssd_norm_w']]
    units = nseq * GDN_HEADS
    o, st, y, hst = pl.pallas_call(
        functools.partial(_mixer_prefill_kernel, n_chunks=nc),
        grid=(bsz // nseq, nc + 1),
        in_specs=[pcol(GDN_DIM, COL_Q, cur), pcol(GDN_DIM, COL_K, cur), pcol(GDN_DIM, COL_V, cur),
                  pcol(GDN_DIM, COL_Z_GDN, prev), pcol(LANES, COL_SMALL, cur),
                  pcol(SSD_INNER, COL_Z_SSD, cur), pcol(SSD_INNER, COL_XS, cur), pcol(SSD_BC, COL_B, cur),
                  pcol(SSD_BC, COL_C, cur)] + [full(w) for w in weights],
        out_specs=[pl.BlockSpec((nseq, cs, GDN_DIM), lambda b, c: (b, prev(c), 0)),
                   pl.BlockSpec((nseq, GDN_HEADS * GDN_HEAD_K, GDN_HEAD_V), lambda b, c: (b, 0, 0)),
                   pl.BlockSpec((nseq, cs, SSD_INNER), lambda b, c: (b, cur(c), 0)),
                   pl.BlockSpec((nseq, SSD_INNER, SSD_STATE), lambda b, c: (b, 0, 0))],
        out_shape=[jax.ShapeDtypeStruct((bsz, seq, GDN_DIM), BF16),
                   jax.ShapeDtypeStruct((bsz, GDN_HEADS * GDN_HEAD_K, GDN_HEAD_V), F32),
                   jax.ShapeDtypeStruct((bsz, seq, SSD_INNER), BF16),
                   jax.ShapeDtypeStruct((bsz, SSD_INNER, SSD_STATE), F32)],
        scratch_shapes=[pltpu.VMEM((nseq, GDN_HEADS * GDN_HEAD_K, GDN_HEAD_V), F32),
                        pltpu.VMEM((nseq, SSD_INNER, SSD_STATE), F32),
                        pltpu.VMEM((nseq, cs, SSD_INNER), F32),
                        pltpu.VMEM((2 * nseq, cs, GDN_DIM), BF16),
                        pltpu.VMEM((2 * nseq, cs, GDN_DIM), BF16),
                        pltpu.VMEM((2 * nseq, cs, 2 * GDN_DIM), BF16),
                        pltpu.VMEM((2 * units, cs, cs), F32),
                        pltpu.VMEM((2 * units, cs, cs), BF16),
                        pltpu.VMEM((units, cs, cs), BF16),
                        pltpu.VMEM((units, cs, cs), F32),
                        pltpu.VMEM((nseq, cs, GDN_DIM), F32),
                        pltpu.VMEM((nseq, cs, GDN_DIM), BF16),
                        pltpu.VMEM((2 * nseq, 1, LANES), F32)],
        compiler_params=_params("parallel", "arbitrary"),
    )(*([proj3] * 9), *weights)
    return y.reshape(bsz * seq, SSD_INNER), hst, o.reshape(bsz * seq, GDN_DIM), st


def _softmax_rows(s):
    m = jnp.max(s, axis=-1, keepdims=True)
    e = jnp.exp(s - m)
    return e / jnp.sum(e, axis=-1, keepdims=True)


def _xattn_prefill_kernel(x_ref, nw_ref, wq_ref, mk_ref, mv_ref, wo_ref, o_ref):
    x = x_ref[...]
    ms = jnp.mean(x * x, axis=-1, keepdims=True)
    h = x * lax.rsqrt(ms + RMS_EPS) * nw_ref[...]
    q = jnp.dot(h.astype(BF16), wq_ref[...], preferred_element_type=F32)
    outs = []
    for hd in range(XA_HEADS):
        lo = hd * XA_HEAD_DIM
        s = _mm_nt(q[:, lo:lo + XA_HEAD_DIM], mk_ref[:, lo:lo + XA_HEAD_DIM]) * (XA_HEAD_DIM ** -0.5)
        outs.append(_mm(_softmax_rows(s), mv_ref[:, lo:lo + XA_HEAD_DIM]))
    att = jnp.concatenate(outs, axis=-1)
    o_ref[...] = x + jnp.dot(att.astype(BF16), wo_ref[...], preferred_element_type=F32)


def _xattn_prefill(x, nw, wq, mk, mv, wo, bsz, seq, tq):
    n_mem = mk.shape[1]
    nt = seq // tq
    return pl.pallas_call(
        _xattn_prefill_kernel,
        grid=(bsz, nt),
        in_specs=[pl.BlockSpec((tq, D_MODEL), lambda b, t: (b * nt + t, 0)),
                  pl.BlockSpec((1, D_MODEL), lambda b, t: (0, 0)),
                  pl.BlockSpec((D_MODEL, D_MODEL), lambda b, t: (0, 0)),
                  pl.BlockSpec((None, n_mem, D_MODEL), lambda b, t: (b, 0, 0)),
                  pl.BlockSpec((None, n_mem, D_MODEL), lambda b, t: (b, 0, 0)),
                  pl.BlockSpec((D_MODEL, D_MODEL), lambda b, t: (0, 0))],
        out_specs=pl.BlockSpec((tq, D_MODEL), lambda b, t: (b * nt + t, 0)),
        out_shape=jax.ShapeDtypeStruct(x.shape, F32),
        compiler_params=_params("parallel", "arbitrary"),
    )(x, nw.reshape(1, D_MODEL), wq, mk, mv, wo)


def _xattn_decode_kernel(q_ref, mk_ref, mv_ref, o_ref):
    tb = q_ref.shape[0]
    for i in range(tb):
        q = q_ref[i]
        qh = jnp.concatenate([q[:, hd * XA_HEAD_DIM:(hd + 1) * XA_HEAD_DIM] for hd in range(XA_HEADS)], axis=0)
        s = jnp.sum(mk_ref[i] * qh[None], axis=-1, keepdims=True) * (XA_HEAD_DIM ** -0.5)
        e = jnp.exp(s - jnp.max(s, axis=0, keepdims=True))
        p = e / jnp.sum(e, axis=0, keepdims=True)
        o_ref[i] = jnp.sum(p * mv_ref[i], axis=0)


def _xattn_decode(q, mk_all, mv_all, layer, tb):
    _, nb, n_mem, nh, hd = mk_all.shape
    return pl.pallas_call(
        _xattn_decode_kernel,
        grid=(nb // tb,),
        in_specs=[pl.BlockSpec((tb, 1, D_MODEL), lambda i: (i, 0, 0)),
                  pl.BlockSpec((None, tb, n_mem, nh, hd), lambda i: (layer, i, 0, 0, 0)),
                  pl.BlockSpec((None, tb, n_mem, nh, hd), lambda i: (layer, i, 0, 0, 0))],
        out_specs=pl.BlockSpec((tb, nh, hd), lambda i: (i, 0, 0)),
        out_shape=jax.ShapeDtypeStruct((nb, nh, hd), F32),
        compiler_params=_params("parallel"),
    )(q.reshape(nb, 1, D_MODEL), mk_all, mv_all).reshape(nb, D_MODEL)


def _ffn_prefill_kernel(x_ref, nw_ref, wgu_ref, cw_ref, cb_ref, wd_ref, o_ref, tail_ref,
                        ext_ref, hist_ref, hm_ref, *, tiles_per_seq):
    i = pl.program_id(0)
    tm = x_ref.shape[0]
    d_ff = wd_ref.shape[0]

    @pl.when(i == 0)
    def _():
        hist_ref[...] = jnp.zeros(hist_ref.shape, F32)

    x = x_ref[...]
    ms = jnp.mean(x * x, axis=-1, keepdims=True)
    xn = (x * lax.rsqrt(ms + RMS_EPS) * nw_ref[...]).astype(BF16)
    starts_seq = i % tiles_per_seq == 0
    for ci, lo in enumerate(range(0, d_ff, MXU_COLS)):
        cols = slice(lo, min(lo + MXU_COLS, d_ff))
        width = cols.stop - cols.start
        gate = jnp.dot(xn, wgu_ref[:, cols], preferred_element_type=F32)
        up = jnp.dot(xn, wgu_ref[:, d_ff + cols.start:d_ff + cols.stop], preferred_element_type=F32)
        tail_ref[:, cols] = gate[tm - HIST_ROWS:tm, :]
        buf = ext_ref.at[ci % CONV_BUFS]
        buf[0:HIST_ROWS, 0:width] = jnp.where(starts_seq, 0.0, hist_ref[:, cols])
        buf[HIST_ROWS:HIST_ROWS + tm, 0:width] = gate
        hist_ref[:, cols] = gate[tm - HIST_ROWS:tm, :]
        c = _causal_taps(buf[:, 0:width], cw_ref, cb_ref[:, cols], cols, FFN_CONV, tm)
        hm_ref[:, cols] = (_silu(c) * up).astype(BF16)
    o_ref[...] = x + jnp.dot(hm_ref[...], wd_ref[...], preferred_element_type=F32)


def _ffn_prefill(x, nw, wgu, cw, cb, wd, seq, tm):
    m = x.shape[0]
    d_ff = wd.shape[0]
    assert m % tm == 0 and seq % tm == 0 and tm >= HIST_ROWS
    whole = lambda a: pl.BlockSpec(a.shape, lambda i: (0, 0), pipeline_mode=pl.Buffered(1))
    ins = [nw.reshape(1, D_MODEL), wgu, cw, cb.reshape(1, d_ff), wd]
    return pl.pallas_call(
        functools.partial(_ffn_prefill_kernel, tiles_per_seq=seq // tm),
        grid=(m // tm,),
        in_specs=[pl.BlockSpec((tm, D_MODEL), lambda i: (i, 0))] + [whole(a) for a in ins],
        out_specs=[pl.BlockSpec((tm, D_MODEL), lambda i: (i, 0)),
                   pl.BlockSpec((None, HIST_ROWS, d_ff), lambda i: (i, 0, 0))],
        out_shape=[jax.ShapeDtypeStruct((m, D_MODEL), F32),
                   jax.ShapeDtypeStruct((m // tm, HIST_ROWS, d_ff), F32)],
        scratch_shapes=[pltpu.VMEM((CONV_BUFS, HIST_ROWS + tm, MXU_COLS), F32),
                        pltpu.VMEM((HIST_ROWS, d_ff), F32),
                        pltpu.VMEM((tm, d_ff), BF16)],
        compiler_params=_params("arbitrary"),
    )(x, *ins)


def _ffn_decode_kernel(x_ref, g_ref, h0_ref, h1_ref, u_ref, cw_ref, cb_ref, wd_ref, o_ref):
    acc = cb_ref[...] + h0_ref[...] * cw_ref[0:1, :] + h1_ref[...] * cw_ref[1:2, :] + g_ref[...] * cw_ref[2:3, :]
    hmid = _silu(acc) * u_ref[...]
    o_ref[...] = x_ref[...] + jnp.dot(hmid.astype(BF16), wd_ref[...], preferred_element_type=F32)


def _ffn_decode(x, gu, hist0, hist1, cw, cb, wd):
    m = x.shape[0]
    d_ff = wd.shape[0]
    rows = lambda width, col: pl.BlockSpec((m, width), lambda i: (0, col))
    return pl.pallas_call(
        _ffn_decode_kernel,
        grid=(1,),
        in_specs=[rows(D_MODEL, 0), rows(d_ff, 0), rows(d_ff, 0), rows(d_ff, 0), rows(d_ff, 1),
                  pl.BlockSpec((FFN_CONV, d_ff), lambda i: (0, 0)),
                  pl.BlockSpec((1, d_ff), lambda i: (0, 0)),
                  pl.BlockSpec((d_ff, D_MODEL), lambda i: (0, 0))],
        out_specs=rows(D_MODEL, 0),
        out_shape=jax.ShapeDtypeStruct((m, D_MODEL), F32),
        compiler_params=_params("arbitrary"),
    )(x, gu, hist0, hist1, gu, cw, cb.reshape(1, d_ff), wd)


def _expand_lanes(rows, first_lane, width, n_out, parts):
    k = lax.broadcasted_iota(jnp.int32, (LANES, n_out), 0)
    l = lax.broadcasted_iota(jnp.int32, (LANES, n_out), 1)
    sel = jnp.where(l // width == k - first_lane, 1.0, 0.0)
    return _mm_lhs_split(rows, sel, parts)


def _decode_rows_kernel(proj_ref, sh0, sh1, sh2, gh0, gh1, gh2,
                        scw_ref, scb_ref, gcw_ref, bias_ref, alog_ref,
                        xs_ref, xdt_ref, b_ref, c_ref, sdec_ref,
                        q_ref, k_ref, v_ref, beta_ref, gdec_ref):
    def conv(hists, new, cw, cb):
        acc = new * cw[3:4, :]
        for t, hr in enumerate(hists):
            acc = acc + hr[...] * cw[t:t + 1, :]
        return acc if cb is None else acc + cb

    xbc_new = jnp.concatenate([proj_ref[:, COL_XS:COL_XS + SSD_INNER],
                               proj_ref[:, COL_B:COL_B + SSD_BC],
                               proj_ref[:, COL_C:COL_C + SSD_BC]], axis=-1)
    xbc = _silu(conv((sh0, sh1, sh2), xbc_new, scw_ref, scb_ref[...]))
    sp, sg, log_dec = _gate_rows(proj_ref[:, COL_SMALL:COL_SMALL + LANES], bias_ref[...], alog_ref[...])
    xs = xbc[:, 0:SSD_INNER]
    xs_ref[...] = xs
    xdt_ref[...] = xs * _expand_lanes(sp, LANE_DT, SSD_HEAD_DIM, SSD_INNER, 3)
    b_ref[...] = xbc[:, SSD_INNER:SSD_INNER + SSD_BC]
    c_ref[...] = xbc[:, SSD_INNER + SSD_BC:SSD_INNER + 2 * SSD_BC]
    sdec_ref[...] = jnp.exp(_expand_lanes(log_dec, LANE_DT, SSD_HEAD_DIM, SSD_INNER, 3))

    qkv_new = proj_ref[:, COL_Q:COL_Q + 3 * GDN_DIM]
    qkv = _silu(conv((gh0, gh1, gh2), qkv_new, gcw_ref, None))
    for h in range(GDN_HEADS):
        lo = h * GDN_HEAD_K
        qh = qkv[:, lo:lo + GDN_HEAD_K]
        kh = qkv[:, GDN_DIM + lo:GDN_DIM + lo + GDN_HEAD_K]
        q_ref[:, lo:lo + GDN_HEAD_K] = qh * lax.rsqrt(jnp.sum(qh * qh, axis=-1, keepdims=True) + L2_EPS) * \
            (GDN_HEAD_K ** -0.5)
        k_ref[:, lo:lo + GDN_HEAD_K] = kh * lax.rsqrt(jnp.sum(kh * kh, axis=-1, keepdims=True) + L2_EPS)
    v_ref[...] = qkv[:, 2 * GDN_DIM:3 * GDN_DIM]
    beta_ref[...] = _expand_lanes(sg, LANE_BETA, GDN_HEAD_V, GDN_DIM, 3)
    gdec_ref[...] = jnp.exp(_expand_lanes(log_dec, LANE_G, GDN_HEAD_V, GDN_DIM, 3))


def _decode_rows(proj, ssd_hist, gdn_hist, lw):
    m = proj.shape[0]
    ins = [proj] + [ssd_hist[:, t] for t in range(SSD_CONV - 1)] + [gdn_hist[:, t] for t in range(GDN_CONV - 1)]
    ins += [lw['ssd_conv_w'], lw['ssd_conv_b'], lw['gdn_conv_w'], lw['gate_bias'], lw['gate_alog']]
    widths = [SSD_INNER, SSD_INNER, SSD_BC, SSD_BC, SSD_INNER, GDN_DIM, GDN_DIM, GDN_DIM, GDN_DIM, GDN_DIM]
    return pl.pallas_call(
        _decode_rows_kernel,
        grid=(1,),
        in_specs=[pl.BlockSpec(a.shape, lambda i, nd=a.ndim: (0,) * nd) for a in ins],
        out_specs=[pl.BlockSpec((m, w), lambda i: (0, 0)) for w in widths],
        out_shape=[jax.ShapeDtypeStruct((m, w), F32) for w in widths],
        compiler_params=_params("arbitrary"),
    )(*ins)


def _rows_to_cols(rows, i, parts):
    tb = rows.shape[0]
    r = lax.broadcasted_iota(jnp.int32, (tb, LANES), 0)
    sel = jnp.where(r == i, 1.0, 0.0).astype(BF16)
    acc = None
    for p in _split(rows, parts):
        t = lax.dot_general(p, sel, (((0,), (0,)), ((), ())), preferred_element_type=F32)
        acc = t if acc is None else acc + t
    return acc


def _only_row(rows, i):
    r = lax.broadcasted_iota(jnp.int32, rows.shape, 0)
    return jnp.where(r == i, rows, 0.0)


def _decode_state_kernel(xdt_ref, b_ref, c_ref, sdec_ref, q_ref, k_ref, v_ref, beta_ref, gdec_ref,
                         hs_ref, ss_ref, *rest):
    y_ref, o_ref, hs_out, ss_out = rest[-4:]
    tb = xdt_ref.shape[0]
    xdt = xdt_ref[...]
    bm = b_ref[...]
    cm = c_ref[...]
    sdec = sdec_ref[...]
    qn = q_ref[...]
    kn = k_ref[...]
    vv = v_ref[...]
    beta = beta_ref[...]
    gdec = gdec_ref[...]
    rows_g = SSD_INNER // SSD_GROUPS

    y_acc = jnp.zeros((tb, SSD_INNER), F32)
    ks_acc = jnp.zeros((tb, GDN_DIM), F32)
    for i in range(tb):
        dec_c = _rows_to_cols(sdec, i, 2)
        xdt_c = _rows_to_cols(xdt, i, 1)
        pieces = []
        for g in range(SSD_GROUPS):
            rs = slice(g * rows_g, (g + 1) * rows_g)
            bg = bm[i:i + 1, g * SSD_STATE:(g + 1) * SSD_STATE]
            h_new = hs_ref[i, rs, :] * dec_c[rs, :] + xdt_c[rs, :] * bg
            hs_out[i, rs, :] = h_new
            cg = _only_row(cm[:, g * SSD_STATE:(g + 1) * SSD_STATE], i)
            pieces.append(_mm_nt(cg, h_new))
        y_acc = y_acc + jnp.concatenate(pieces, axis=-1)
        g_c = _rows_to_cols(gdec, i, 2)
        s_dec = ss_ref[i] * g_c
        ss_out[i] = s_dec
        pieces = []
        for h in range(GDN_HEADS):
            rs = slice(h * GDN_HEAD_K, (h + 1) * GDN_HEAD_K)
            pieces.append(_mm(_only_row(kn[:, rs], i), s_dec[rs, :]))
        ks_acc = ks_acc + jnp.concatenate(pieces, axis=-1)
    y_ref[...] = y_acc

    delta = beta * (vv - ks_acc)
    o_acc = jnp.zeros((tb, GDN_DIM), F32)
    for i in range(tb):
        k_c = _rows_to_cols(kn, i, 1)
        pieces = []
        for h in range(GDN_HEADS):
            rs = slice(h * GDN_HEAD_K, (h + 1) * GDN_HEAD_K)
            d_row = delta[i:i + 1, rs]
            s_new = ss_out[i, rs, :] + k_c[rs, :] * d_row
            ss_out[i, rs, :] = s_new
            pieces.append(_mm(_only_row(qn[:, rs], i), s_new))
        o_acc = o_acc + jnp.concatenate(pieces, axis=-1)
    o_ref[...] = o_acc


def _decode_state(rows, hs_all, ss_all, layer, tb, carried):
    xs, xdt, bm, cm, sdec, qn, kn, vv, beta, gdec = rows
    nb = xdt.shape[0]
    row_ins = [xdt, bm, cm, sdec, qn, kn, vv, beta, gdec]
    rspec = lambda a: pl.BlockSpec((tb, a.shape[1]), lambda i: (i, 0))
    sspec = lambda a: pl.BlockSpec((None, tb) + a.shape[2:], lambda i: (layer, i, 0, 0))
    n_in = len(row_ins) + 2
    aliases = {n_in + j: 2 + j for j in range(len(carried))}
    return pl.pallas_call(
        _decode_state_kernel,
        grid=(nb // tb,),
        in_specs=[rspec(a) for a in row_ins] + [sspec(hs_all), sspec(ss_all)] +
                 [pl.BlockSpec(memory_space=pl.ANY) for _ in carried],
        out_specs=[rspec(xdt), rspec(qn), sspec(hs_all), sspec(ss_all)],
        out_shape=[jax.ShapeDtypeStruct(xdt.shape, F32), jax.ShapeDtypeStruct(qn.shape, F32),
                   jax.ShapeDtypeStruct(hs_all.shape, F32), jax.ShapeDtypeStruct(ss_all.shape, F32)],
        input_output_aliases=aliases,
        compiler_params=_params("parallel"),
    )(*row_ins, hs_all, ss_all, *carried)


def _decode_out_kernel(x_ref, y_ref, xs_ref, o_ref, proj_ref, dexp_ref, snw_ref, gnw_ref, w1_ref, w2_ref, out_ref):
    y = y_ref[...] + dexp_ref[...] * xs_ref[...]
    gs = SSD_INNER // SSD_GROUPS
    ys = []
    for g in range(SSD_GROUPS):
        yg = y[:, g * gs:(g + 1) * gs] * _silu(proj_ref[:, COL_Z_SSD + g * gs:COL_Z_SSD + (g + 1) * gs])
        ms = jnp.mean(yg * yg, axis=-1, keepdims=True)
        ys.append(yg * lax.rsqrt(ms + RMS_EPS) * snw_ref[:, g * gs:(g + 1) * gs])
    os = []
    for h in range(GDN_HEADS):
        lo = h * GDN_HEAD_V
        oh = o_ref[:, lo:lo + GDN_HEAD_V]
        ms = jnp.mean(oh * oh, axis=-1, keepdims=True)
        os.append(oh * lax.rsqrt(ms + RMS_EPS) * gnw_ref[...] *
                  _silu(proj_ref[:, COL_Z_GDN + lo:COL_Z_GDN + lo + GDN_HEAD_V]))
    yn = jnp.concatenate(ys, axis=-1)
    on = jnp.concatenate(os, axis=-1)
    out_ref[...] = x_ref[...] + _mm(yn, w1_ref[...]) + _mm(on, w2_ref[...])


def _decode_out(x, y, xs, o, proj, lw):
    ins = [x, y, xs, o, proj, lw['ssd_d_exp'], lw['ssd_norm_w'], lw['gdn_norm_w'], lw['w_out_ssd'], lw['w_out_gdn']]
    return pl.pallas_call(
        _decode_out_kernel,
        grid=(1,),
        in_specs=[pl.BlockSpec(a.shape, lambda i, nd=a.ndim: (0,) * nd) for a in ins],
        out_specs=pl.BlockSpec(x.shape, lambda i: (0, 0)),
        out_shape=jax.ShapeDtypeStruct(x.shape, F32),
        compiler_params=_params("arbitrary"),
    )(*ins)


def _identity_taps(taps, n):
    return jnp.concatenate([jnp.zeros((taps - 1, n), F32), jnp.ones((1, n), F32)], axis=0)


def _prep_layer(i, p):
    w_in = p['w_in'][i]
    o_zs = 0
    o_xbc = o_zs + SSD_INNER
    o_dt = o_xbc + SSD_INNER + 2 * SSD_BC
    o_qkv = o_dt + SSD_HEADS
    o_zg = o_qkv + 3 * GDN_DIM
    o_b = o_zg + GDN_DIM
    o_a = o_b + GDN_HEADS
    used = SSD_HEADS + 2 * GDN_HEADS
    w_perm = jnp.concatenate([
        w_in[:, o_zs:o_zs + SSD_INNER],
        w_in[:, o_xbc:o_xbc + SSD_INNER],
        w_in[:, o_qkv:o_qkv + 3 * GDN_DIM],
        w_in[:, o_zg:o_zg + GDN_DIM],
        w_in[:, o_xbc + SSD_INNER:o_xbc + SSD_INNER + 2 * SSD_BC],
        w_in[:, o_dt:o_dt + SSD_HEADS],
        w_in[:, o_b:o_b + GDN_HEADS],
        w_in[:, o_a:o_a + GDN_HEADS],
        jnp.zeros((D_MODEL, PROJ_COLS - COL_SMALL - used), F32)], axis=1).astype(BF16)
    lane_pad = jnp.zeros((LANES - used,), F32)
    scw = p['ssd_conv_w'][i]
    scb = p['ssd_conv_b'][i].reshape(1, -1)
    gcw = p['gdn_conv_w'][i]
    return {
        'w_in': w_perm,
        'norm_mix_w': p['norm_mix_w'][i],
        'ssd_conv_w': scw, 'ssd_conv_b': scb, 'gdn_conv_w': gcw,
        'in_cw': jnp.concatenate([_identity_taps(SSD_CONV, SSD_INNER), scw[:, :SSD_INNER], gcw,
                                  _identity_taps(GDN_CONV, GDN_DIM), scw[:, SSD_INNER:],
                                  _identity_taps(SSD_CONV, PROJ_COLS - COL_SMALL)], axis=1),
        'in_cb': jnp.concatenate([jnp.zeros((1, COL_XS), F32), scb[:, :SSD_INNER],
                                  jnp.zeros((1, COL_B - COL_Q), F32), scb[:, SSD_INNER:],
                                  jnp.zeros((1, PROJ_COLS - COL_SMALL), F32)], axis=1),
        'gate_bias': jnp.concatenate([p['ssd_dt_bias'][i], jnp.zeros((GDN_HEADS,), F32), p['gdn_dt_bias'][i],
                                      lane_pad]).reshape(1, LANES),
        'gate_alog': jnp.concatenate([p['ssd_a_log'][i], jnp.zeros((GDN_HEADS,), F32), p['gdn_a_log'][i],
                                      lane_pad]).reshape(1, LANES),
        'ssd_d_exp': jnp.repeat(p['ssd_d'][i], SSD_HEAD_DIM).reshape(1, SSD_INNER),
        'ssd_norm_w': p['ssd_norm_w'][i].reshape(1, SSD_INNER),
        'gdn_norm_w': p['gdn_norm_w'][i].reshape(1, GDN_HEAD_V),
        'w_out_ssd': p['w_out'][i, :SSD_INNER].astype(BF16),
        'w_out_gdn': p['w_out'][i, SSD_INNER:].astype(BF16),
        'norm_xa_w': p['norm_xa_w'][i], 'norm_mem_w': p['norm_mem_w'][i],
        'xa_wq': p['xa_wq'][i].astype(BF16), 'xa_wo': p['xa_wo'][i].astype(BF16),
        'xa_wkv': jnp.concatenate([p['xa_wk'][i], p['xa_wv'][i]], axis=1).astype(BF16),
        'norm_ffn_w': p['norm_ffn_w'][i],
        'ffn_w_gu': jnp.concatenate([p['ffn_w_gate'][i], p['ffn_w_up'][i]], axis=1).astype(BF16),
        'ffn_conv_w': p['ffn_conv_w'][i], 'ffn_conv_b': p['ffn_conv_b'][i],
        'ffn_w_down': p['ffn_w_down'][i].astype(BF16),
    }


def _unpermute_conv_rows(proj_rows):
    ssd = jnp.concatenate([proj_rows[..., COL_XS:COL_XS + SSD_INNER],
                           proj_rows[..., COL_B:COL_B + SSD_BC],
                           proj_rows[..., COL_C:COL_C + SSD_BC]], axis=-1)
    gdn = proj_rows[..., COL_Q:COL_Q + 3 * GDN_DIM]
    return ssd, gdn


def _tile(m, pref):
    t = min(m, pref)
    while m % t:
        t //= 2
    return t


def kernel(x_prompt, x_sample, mem_prompt, cache_mem_k, cache_mem_v, state_ssd_conv, state_ssd, state_gdn_conv, state_gdn, state_ffn_conv, norm_mix_w, w_in, ssd_conv_w, ssd_conv_b, ssd_dt_bias, ssd_a_log, ssd_d, ssd_norm_w, gdn_conv_w, gdn_dt_bias, gdn_a_log, gdn_norm_w, w_out, norm_xa_w, norm_mem_w, xa_wq, xa_wk, xa_wv, xa_wo, norm_ffn_w, ffn_w_gate, ffn_w_up, ffn_conv_w, ffn_conv_b, ffn_w_down, final_norm_w):
    params = dict(norm_mix_w=norm_mix_w, w_in=w_in, ssd_conv_w=ssd_conv_w, ssd_conv_b=ssd_conv_b,
                  ssd_dt_bias=ssd_dt_bias, ssd_a_log=ssd_a_log, ssd_d=ssd_d, ssd_norm_w=ssd_norm_w,
                  gdn_conv_w=gdn_conv_w, gdn_dt_bias=gdn_dt_bias, gdn_a_log=gdn_a_log, gdn_norm_w=gdn_norm_w,
                  w_out=w_out, norm_xa_w=norm_xa_w, norm_mem_w=norm_mem_w, xa_wq=xa_wq, xa_wk=xa_wk, xa_wv=xa_wv,
                  xa_wo=xa_wo, norm_ffn_w=norm_ffn_w, ffn_w_gate=ffn_w_gate, ffn_w_up=ffn_w_up,
                  ffn_conv_w=ffn_conv_w, ffn_conv_b=ffn_conv_b, ffn_w_down=ffn_w_down)
    depth = w_in.shape[0]
    bsz, seq, _ = x_prompt.shape
    nb = x_sample.shape[0]
    n_mem = mem_prompt.shape[1]
    d_ff = ffn_w_down.shape[1]
    mp = bsz * seq

    xp = x_prompt.reshape(mp, D_MODEL)
    xs = x_sample.reshape(nb, D_MODEL)
    mem = mem_prompt.reshape(bsz * n_mem, D_MODEL)
    tm_p = _tile(seq, 1024)
    tm_c = _tile(seq, 256)
    tm_mem = _tile(bsz * n_mem, 1024)
    dec_tile = _tile(nb, DEC_TILE)

    hs_all = state_ssd.reshape(depth, nb, SSD_INNER, SSD_STATE)
    ss_all = state_gdn.reshape(depth, nb, GDN_HEADS * GDN_HEAD_K, GDN_HEAD_V)
    mk_all, mv_all = cache_mem_k, cache_mem_v
    new_states = ()

    mkp, mvp = [], []
    p_sc, p_sh, p_gc, p_gs, p_fc = [], [], [], [], []
    s_sc, s_gc, s_fc = [], [], []
    for i in range(depth):
        lw = _prep_layer(i, params)

        kv = _norm_mm(mem, lw['norm_mem_w'], lw['xa_wkv'], tm_mem)
        mk = kv[:, :D_MODEL].reshape(bsz, n_mem, D_MODEL)
        mv = kv[:, D_MODEL:].reshape(bsz, n_mem, D_MODEL)
        mkp.append(mk.reshape(bsz, n_mem, XA_HEADS, XA_HEAD_DIM))
        mvp.append(mv.reshape(bsz, n_mem, XA_HEADS, XA_HEAD_DIM))

        proj, tails = _norm_mm_conv(xp, lw['norm_mix_w'], lw['w_in'], lw['in_cw'], lw['in_cb'],
                                    [(COL_XS, COL_Z_GDN), (COL_B, COL_SMALL)], seq, tm_c)
        y, h_new, o, s_new = _mixer_prefill(proj, bsz, seq, lw)
        seq_tails = tails.reshape(bsz, seq // tm_c, HIST_ROWS, PROJ_COLS)[:, -1]
        ssd_tail, gdn_tail = _unpermute_conv_rows(seq_tails[:, HIST_ROWS - (SSD_CONV - 1):])
        p_sc.append(ssd_tail)
        p_gc.append(gdn_tail)
        p_sh.append(h_new.reshape(bsz, SSD_HEADS, SSD_HEAD_DIM, SSD_STATE))
        p_gs.append(s_new.reshape(bsz, GDN_HEADS, GDN_HEAD_K, GDN_HEAD_V))
        xp = _mm_res(xp, [y, o], [lw['w_out_ssd'], lw['w_out_gdn']], _tile(mp, 512))
        xp = _xattn_prefill(xp, lw['norm_xa_w'], lw['xa_wq'], mk, mv, lw['xa_wo'], bsz, seq, _tile(seq, 512))
        xp, gate_tails = _ffn_prefill(xp, lw['norm_ffn_w'], lw['ffn_w_gu'], lw['ffn_conv_w'], lw['ffn_conv_b'],
                                      lw['ffn_w_down'], seq, tm_c)
        p_fc.append(gate_tails.reshape(bsz, seq // tm_c, HIST_ROWS, d_ff)[:, -1, HIST_ROWS - (FFN_CONV - 1):])

        proj_s = _norm_mm(xs, lw['norm_mix_w'], lw['w_in'], nb)
        ssd_new, gdn_new = _unpermute_conv_rows(proj_s)
        s_sc.append(jnp.concatenate([state_ssd_conv[i][:, 1:], ssd_new[:, None]], axis=1))
        s_gc.append(jnp.concatenate([state_gdn_conv[i][:, 1:], gdn_new[:, None]], axis=1))
        rows = _decode_rows(proj_s, state_ssd_conv[i], state_gdn_conv[i], lw)
        y_s, o_s, hs_new, ss_new = _decode_state(rows, hs_all, ss_all, i, dec_tile, new_states)
        new_states = (hs_new, ss_new)
        xs = _decode_out(xs, y_s, rows[0], o_s, proj_s, lw)
        q_s = _norm_mm(xs, lw['norm_xa_w'], lw['xa_wq'], nb)
        att = _xattn_decode(q_s, mk_all, mv_all, i, _tile(nb, 4))
        xs = _mm_res(xs, [att], [lw['xa_wo']], nb)
        gu_s = _norm_mm(xs, lw['norm_ffn_w'], lw['ffn_w_gu'], nb)
        s_fc.append(jnp.concatenate([state_ffn_conv[i][:, 1:], gu_s[:, None, :d_ff]], axis=1))
        xs = _ffn_decode(xs, gu_s, state_ffn_conv[i][:, 0], state_ffn_conv[i][:, 1],
                         lw['ffn_conv_w'], lw['ffn_conv_b'], lw['ffn_w_down'])

    y_prompt = _rmsnorm(xp, final_norm_w, tm_p).reshape(bsz, seq, D_MODEL)
    y_sample = _rmsnorm(xs, final_norm_w, nb).reshape(nb, 1, D_MODEL)
    return (y_prompt, y_sample, jnp.stack(mkp), jnp.stack(mvp),
            jnp.stack(p_sc), jnp.stack(p_sh), jnp.stack(p_gc), jnp.stack(p_gs), jnp.stack(p_fc),
            jnp.stack(s_sc), new_states[0].reshape(state_ssd.shape), jnp.stack(s_gc),
            new_states[1].reshape(state_gdn.shape), jnp.stack(s_fc))
```

```python
import functools

import jax
import jax.numpy as jnp
from jax import lax
from jax.experimental import pallas as pl
from jax.experimental.pallas import tpu as pltpu

F32 = jnp.float32
BF16 = jnp.bfloat16

D_MODEL = 1024
SSD_HEADS = 16
SSD_HEAD_DIM = 64
SSD_STATE = 128
SSD_GROUPS = 2
SSD_INNER = SSD_HEADS * SSD_HEAD_DIM
SSD_BC = SSD_GROUPS * SSD_STATE
SSD_CONV = 4
GDN_HEADS = 8
GDN_HEAD_K = 128
GDN_HEAD_V = 128
GDN_DIM = GDN_HEADS * GDN_HEAD_V
GDN_CONV = 4
XA_HEADS = 4
XA_HEAD_DIM = D_MODEL // XA_HEADS
FFN_CONV = 3
RMS_EPS = 1e-6
L2_EPS = 1e-6

SUBLANES = 8
LANES = 128
MXU_COLS = 256
VMEM_LIMIT_BYTES = 56 * 1024 * 1024

COL_Z_SSD = 0
COL_XS = 1024
COL_Q = 2048
COL_K = 3072
COL_V = 4096
COL_Z_GDN = 5120
COL_B = 6144
COL_C = 6400
COL_SMALL = 6656
PROJ_COLS = 7168
LANE_DT = 0
LANE_BETA = 16
LANE_G = 24

MIXER_CHUNK = 128
DEC_TILE = 8


def _params(*sem):
    return pltpu.CompilerParams(dimension_semantics=sem, vmem_limit_bytes=VMEM_LIMIT_BYTES)


def _mm(a, b):
    return jnp.dot(a.astype(BF16), b.astype(BF16), preferred_element_type=F32)


def _mm_nt(a, b):
    return lax.dot_general(a.astype(BF16), b.astype(BF16), (((1,), (1,)), ((), ())),
                           preferred_element_type=F32)


def _split(x, parts):
    out = []
    r = x
    for _ in range(parts - 1):
        p = r.astype(BF16)
        out.append(p)
        r = r - p.astype(F32)
    out.append(r.astype(BF16))
    return out


def _mm_rhs_split(a_exact, b, parts=3):
    a16 = a_exact.astype(BF16)
    acc = None
    for p in _split(b, parts):
        t = jnp.dot(a16, p, preferred_element_type=F32)
        acc = t if acc is None else acc + t
    return acc


def _mm_lhs_split(a, b_exact, parts=3):
    b16 = b_exact.astype(BF16)
    acc = None
    for p in _split(a, parts):
        t = jnp.dot(p, b16, preferred_element_type=F32)
        acc = t if acc is None else acc + t
    return acc


def _silu(x):
    h = 0.5 * x
    return h + h * jnp.tanh(h)


def _sigmoid(x):
    return 1.0 / (1.0 + jnp.exp(-x))


def _softplus(x):
    return jnp.maximum(x, 0.0) + jnp.log1p(jnp.exp(-jnp.abs(x)))


def _col(arr, lane):
    idx = lax.broadcasted_iota(jnp.int32, arr.shape, 1)
    return jnp.sum(jnp.where(idx == lane, arr, 0.0), axis=1, keepdims=True)


def _gate_rows(sm, bias, alog):
    sp = _softplus(sm + bias)
    sg = _sigmoid(sm)
    lane = lax.broadcasted_iota(jnp.int32, alog.shape, 1)
    has_decay = (lane < LANE_BETA) | ((lane >= LANE_G) & (lane < LANE_G + GDN_HEADS))
    neg_a = jnp.where(has_decay, -jnp.exp(alog), 0.0)
    return sp, sg, sp * neg_a


def _norm_mm_kernel(x_ref, nw_ref, w_ref, o_ref, xn_ref):
    @pl.when(pl.program_id(1) == 0)
    def _():
        x = x_ref[...]
        ms = jnp.mean(x * x, axis=-1, keepdims=True)
        xn_ref[...] = (x * lax.rsqrt(ms + RMS_EPS) * nw_ref[...]).astype(BF16)

    o_ref[...] = jnp.dot(xn_ref[...], w_ref[...], preferred_element_type=F32)


MAX_COL_TILE_LANES = 11 * LANES


def _col_tile(n):
    return max(t for t in range(LANES, MAX_COL_TILE_LANES + 1, LANES) if n % t == 0)


def _norm_mm(x, nw, w, tm):
    m, k = x.shape
    n = w.shape[1]
    tn = _col_tile(n)
    assert m % tm == 0
    return pl.pallas_call(
        _norm_mm_kernel,
        grid=(m // tm, n // tn),
        in_specs=[pl.BlockSpec((tm, k), lambda i, j: (i, 0)),
                  pl.BlockSpec((1, k), lambda i, j: (0, 0)),
                  pl.BlockSpec((k, tn), lambda i, j: (0, j))],
        out_specs=pl.BlockSpec((tm, tn), lambda i, j: (i, j)),
        out_shape=jax.ShapeDtypeStruct((m, n), F32),
        scratch_shapes=[pltpu.VMEM((tm, k), BF16)],
        compiler_params=_params("parallel", "arbitrary"),
    )(x, nw.reshape(1, k), w)


HIST_ROWS = SUBLANES


CONV_BUFS = 4


def _norm_mm_conv_kernel(x_ref, nw_ref, w_ref, cw_ref, cb_ref, o_ref, tail_ref, ext_ref, hist_ref,
                         *, taps, tiles_per_seq, conv_ranges):
    i = pl.program_id(0)
    tm = x_ref.shape[0]
    n = w_ref.shape[1]

    @pl.when(i == 0)
    def _():
        hist_ref[...] = jnp.zeros(hist_ref.shape, F32)

    x = x_ref[...]
    ms = jnp.mean(x * x, axis=-1, keepdims=True)
    xn = (x * lax.rsqrt(ms + RMS_EPS) * nw_ref[...]).astype(BF16)
    starts_seq = i % tiles_per_seq == 0
    for ci, lo in enumerate(range(0, n, MXU_COLS)):
        cols = slice(lo, min(lo + MXU_COLS, n))
        acc = jnp.dot(xn, w_ref[:, cols], preferred_element_type=F32)
        tail_ref[:, cols] = acc[tm - HIST_ROWS:tm, :]
        if not any(a <= cols.start and cols.stop <= b for a, b in conv_ranges):
            o_ref[:, cols] = acc
            continue
        buf = ext_ref.at[ci % CONV_BUFS]
        width = cols.stop - cols.start
        buf[0:HIST_ROWS, 0:width] = jnp.where(starts_seq, 0.0, hist_ref[:, cols])
        buf[HIST_ROWS:HIST_ROWS + tm, 0:width] = acc
        hist_ref[:, cols] = acc[tm - HIST_ROWS:tm, :]
        o_ref[:, cols] = _silu(_causal_taps(buf[:, 0:width], cw_ref, cb_ref[:, cols], cols, taps, tm))


def _causal_taps(ext, cw_ref, bias, cols, taps, tm):
    rows = slice(HIST_ROWS, HIST_ROWS + tm)
    w = [cw_ref[k:k + 1, cols] for k in range(taps)]
    if taps == 4:
        lag1 = pltpu.roll(ext, 1, axis=0)
        older = pltpu.roll(ext * w[1] + lag1 * w[0], 2, axis=0)
        return bias + ext[rows, :] * w[3] + lag1[rows, :] * w[2] + older[rows, :]
    out = bias + ext[rows, :] * w[taps - 1]
    for lag in range(1, taps):
        out = out + pltpu.roll(ext, lag, axis=0)[rows, :] * w[taps - 1 - lag]
    return out


def _norm_mm_conv(x, nw, w, cw, cb, conv_ranges, seq, tm):
    m, k = x.shape
    n = w.shape[1]
    taps = cw.shape[0]
    assert m % tm == 0 and seq % tm == 0 and tm >= HIST_ROWS
    assert all(a % MXU_COLS == 0 and b % MXU_COLS == 0 for a, b in conv_ranges)
    whole = lambda a: pl.BlockSpec(a.shape, lambda i: (0, 0), pipeline_mode=pl.Buffered(1))
    nw2 = nw.reshape(1, k)
    return pl.pallas_call(
        functools.partial(_norm_mm_conv_kernel, taps=taps, tiles_per_seq=seq // tm,
                          conv_ranges=tuple(conv_ranges)),
        grid=(m // tm,),
        in_specs=[pl.BlockSpec((tm, k), lambda i: (i, 0)), whole(nw2), whole(w), whole(cw), whole(cb)],
        out_specs=[pl.BlockSpec((tm, n), lambda i: (i, 0)),
                   pl.BlockSpec((None, HIST_ROWS, n), lambda i: (i, 0, 0))],
        out_shape=[jax.ShapeDtypeStruct((m, n), F32),
                   jax.ShapeDtypeStruct((m // tm, HIST_ROWS, n), F32)],
        scratch_shapes=[pltpu.VMEM((CONV_BUFS, HIST_ROWS + tm, MXU_COLS), F32),
                        pltpu.VMEM((HIST_ROWS, n), F32)],
        compiler_params=_params("arbitrary"),
    )(x, nw2, w, cw, cb)


def _mm_res_kernel(*refs, n_in):
    res_ref = refs[0]
    a_refs = refs[1:1 + n_in]
    w_refs = refs[1 + n_in:1 + 2 * n_in]
    o_ref = refs[1 + 2 * n_in]
    acc = res_ref[...]
    for a_ref, w_ref in zip(a_refs, w_refs):
        acc = acc + jnp.dot(a_ref[...].astype(BF16), w_ref[...], preferred_element_type=F32)
    o_ref[...] = acc


def _mm_res(res, a_list, w_list, tm):
    m, n = res.shape
    assert m % tm == 0
    n_in = len(a_list)
    in_specs = [pl.BlockSpec((tm, n), lambda i: (i, 0))]
    in_specs += [pl.BlockSpec((tm, a.shape[1]), lambda i: (i, 0)) for a in a_list]
    in_specs += [pl.BlockSpec(w.shape, lambda i: (0, 0)) for w in w_list]
    return pl.pallas_call(
        functools.partial(_mm_res_kernel, n_in=n_in),
        grid=(m // tm,),
        in_specs=in_specs,
        out_specs=pl.BlockSpec((tm, n), lambda i: (i, 0)),
        out_shape=jax.ShapeDtypeStruct((m, n), F32),
        compiler_params=_params("parallel"),
    )(res, *a_list, *w_list)


def _rmsnorm_kernel(x_ref, nw_ref, o_ref):
    x = x_ref[...]
    ms = jnp.mean(x * x, axis=-1, keepdims=True)
    o_ref[...] = x * lax.rsqrt(ms + RMS_EPS) * nw_ref[...]


def _rmsnorm(x, nw, tm):
    m, k = x.shape
    return pl.pallas_call(
        _rmsnorm_kernel,
        grid=(m // tm,),
        in_specs=[pl.BlockSpec((tm, k), lambda i: (i, 0)), pl.BlockSpec((1, k), lambda i: (0, 0))],
        out_specs=pl.BlockSpec((tm, k), lambda i: (i, 0)),
        out_shape=jax.ShapeDtypeStruct((m, k), F32),
        compiler_params=_params("parallel"),
    )(x, nw.reshape(1, k))


def _causal_mask(cs, strict=False):
    r = lax.broadcasted_iota(jnp.int32, (cs, cs), 0)
    c = lax.broadcasted_iota(jnp.int32, (cs, cs), 1)
    return (r > c) if strict else (r >= c)


def _ssd_chunk(z_ref, xs_ref, b_ref, c_ref, sm_ref, bias_ref, alog_ref, dexp_ref, nw_ref, y_ref, hst, y_s):
    cs = xs_ref.shape[0]
    xs = xs_ref[...]
    bm = b_ref[...]
    cm = c_ref[...]

    sp, _, dta = _gate_rows(sm_ref[...], bias_ref[...], alog_ref[...])
    causal = _causal_mask(cs)
    cum = _mm_rhs_split(jnp.where(causal, 1.0, 0.0), dta)
    cum_row = cum.T

    lane_lo = lax.broadcasted_iota(jnp.int32, (1, LANES), 1) < SSD_HEAD_DIM
    row_lo = lax.broadcasted_iota(jnp.int32, (LANES, 1), 0) < SSD_HEAD_DIM
    heads_per_group = SSD_HEADS // SSD_GROUPS
    for g in range(SSD_GROUPS):
        cg = cm[:, g * SSD_STATE:(g + 1) * SSD_STATE]
        bg = bm[:, g * SSD_STATE:(g + 1) * SSD_STATE]
        cb = _mm_nt(cg, bg)
        rows_g = heads_per_group * SSD_HEAD_DIM
        y_state = _mm_nt(cg, hst[g * rows_g:(g + 1) * rows_g, :])
        for hp in range(heads_per_group // 2):
            h0 = g * heads_per_group + 2 * hp
            lo = h0 * SSD_HEAD_DIM
            xs_pair = xs[:, lo:lo + LANES]
            ci = [_col(cum, LANE_DT + h0), _col(cum, LANE_DT + h0 + 1)]
            dti = [_col(sp, LANE_DT + h0), _col(sp, LANE_DT + h0 + 1)]
            xdt_pair = xs_pair * jnp.where(lane_lo, dti[0], dti[1])
            y_pair = None
            for e in range(2):
                cj = cum_row[LANE_DT + h0 + e:LANE_DT + h0 + e + 1, :]
                seg = jnp.exp(jnp.where(causal, ci[e] - cj, -jnp.inf))
                keep = lane_lo if e == 0 else jnp.logical_not(lane_lo)
                t = _mm(cb * seg, jnp.where(keep, xdt_pair, 0.0))
                y_pair = t if y_pair is None else y_pair + t
            y_pair = y_pair + y_state[:, 2 * hp * SSD_HEAD_DIM:2 * hp * SSD_HEAD_DIM + LANES] * \
                jnp.where(lane_lo, jnp.exp(ci[0]), jnp.exp(ci[1]))
            y_pair = y_pair + dexp_ref[:, lo:lo + LANES] * xs_pair
            y_s[:, lo:lo + LANES] = y_pair
            cl = [ci[0][cs - 1:cs, :], ci[1][cs - 1:cs, :]]
            to_end = jnp.where(lane_lo, jnp.exp(cl[0] - ci[0]), jnp.exp(cl[1] - ci[1]))
            contrib = _mm((xdt_pair * to_end).T, bg)
            dec = jnp.where(row_lo, jnp.exp(cl[0]), jnp.exp(cl[1]))
            hst[lo:lo + LANES, :] = hst[lo:lo + LANES, :] * dec + contrib

    gs = SSD_INNER // SSD_GROUPS
    for g in range(SSD_GROUPS):
        yg = y_s[:, g * gs:(g + 1) * gs] * _silu(z_ref[:, g * gs:(g + 1) * gs])
        ms = jnp.mean(yg * yg, axis=-1, keepdims=True)
        y_ref[:, g * gs:(g + 1) * gs] = (yg * lax.rsqrt(ms + RMS_EPS) * nw_ref[:, g * gs:(g + 1) * gs]).astype(BF16)


INV_BASE = 16


def _same_block(cs, size):
    r = lax.broadcasted_iota(jnp.int32, (cs, cs), 0) // size
    c = lax.broadcasted_iota(jnp.int32, (cs, cs), 1) // size
    return r == c


def _bdot(a16, b16):
    return jnp.dot(a16, b16, preferred_element_type=F32)


def _mixer_prefill_kernel(q_ref, k_ref, v_ref, z_ref, sm_ref,
                          zs_ref, xs_ref, b_ref, c_ref,
                          bias_ref, alog_ref, nw_ref, dexp_ref, snw_ref,
                          o_ref, st_ref, y_ref, hst_ref,
                          sst, hst, y_s,
                          qd_s, kd_s, rhs_s, a_s, attn_s, x_s, d_s, u_s, w_s, glast_s, *, n_chunks):
    c_idx = pl.program_id(1)
    nseq, cs = q_ref.shape[0], q_ref.shape[1]
    n_units = nseq * GDN_HEADS
    heads = [(s, h, s * GDN_HEADS + h) for s in range(nseq) for h in range(GDN_HEADS)]
    qc, kc, vc = q_ref, k_ref, v_ref
    causal = _causal_mask(cs)
    strict = _causal_mask(cs, strict=True)

    def stage_a(par):
        sg, gc, gc_row = [], [], []
        for s in range(nseq):
            _, sg_s, g_step = _gate_rows(sm_ref[s], bias_ref[...], alog_ref[...])
            gc_s = _mm_rhs_split(jnp.where(causal, 1.0, 0.0), g_step)
            sg.append(sg_s)
            gc.append(gc_s)
            gc_row.append(gc_s.T)
            glast_s[par * nseq + s] = gc_s[cs - 1:cs, :]
        for s, h, u in heads:
            _stage_a_unit(par, s, h, u, sg, gc, gc_row)

    def _stage_a_unit(par, s, h, u, sg, gc, gc_row):
        u = par * n_units + u
        ps = par * nseq + s
        lo = h * GDN_HEAD_K
        qh = qc[s, :, lo:lo + GDN_HEAD_K]
        kh = kc[s, :, lo:lo + GDN_HEAD_K]
        vh = vc[s, :, lo:lo + GDN_HEAD_V]
        qn = qh * lax.rsqrt(jnp.sum(qh * qh, axis=-1, keepdims=True) + L2_EPS) * (GDN_HEAD_K ** -0.5)
        kn = kh * lax.rsqrt(jnp.sum(kh * kh, axis=-1, keepdims=True) + L2_EPS)
        gi = _col(gc[s], LANE_G + h)
        bi = _col(sg[s], LANE_BETA + h)
        gj = gc_row[s][LANE_G + h:LANE_G + h + 1, :]
        decay = jnp.exp(jnp.where(causal, gi - gj, -jnp.inf))
        egi = jnp.exp(gi)
        kb = kn * bi
        kn16 = kn.astype(BF16)
        a_s[u] = jnp.where(strict, lax.dot_general(kb.astype(BF16), kn16, (((1,), (1,)), ((), ())),
                                                   preferred_element_type=F32) * decay, 0.0)
        attn_s[u] = (lax.dot_general(qn.astype(BF16), kn16, (((1,), (1,)), ((), ())),
                                     preferred_element_type=F32) * decay).astype(BF16)
        qd_s[ps, :, lo:lo + GDN_HEAD_K] = (qn * egi).astype(BF16)
        kd_s[ps, :, lo:lo + GDN_HEAD_K] = (kn * jnp.exp(gi[cs - 1:cs, :] - gi)).astype(BF16)
        rhs_s[ps, :, 2 * lo:2 * lo + GDN_HEAD_V] = (vh * bi).astype(BF16)
        rhs_s[ps, :, 2 * lo + GDN_HEAD_V:2 * lo + 2 * GDN_HEAD_V] = (kb * egi).astype(BF16)

    def stages_bcd(par):
        base = _same_block(cs, INV_BASE)
        eye = jnp.where(causal & jnp.logical_not(strict), 1.0, 0.0)
        for _, _, u in heads:
            x = jnp.where(base, -a_s[par * n_units + u], 0.0)
            x_s[u] = x.astype(BF16)
            d_s[u] = eye + x
        span = 2
        while span < INV_BASE:
            for _, _, u in heads:
                x16 = x_s[u]
                x2 = _bdot(x16, x16).astype(BF16)
                x_s[u] = x2
                d = d_s[u]
                d_s[u] = d + _bdot(d.astype(BF16), x2)
            span *= 2
        size = INV_BASE
        while size < cs:
            off = _same_block(cs, 2 * size) & jnp.logical_not(_same_block(cs, size))
            for _, _, u in heads:
                d = d_s[u]
                d16 = d.astype(BF16)
                low = jnp.where(off, a_s[par * n_units + u], 0.0).astype(BF16)
                d_s[u] = d - _bdot(_bdot(d16, low).astype(BF16), d16)
            size *= 2

        for s, h, u in heads:
            lo = h * GDN_HEAD_K
            uw = _bdot(d_s[u].astype(BF16), rhs_s[par * nseq + s, :, 2 * lo:2 * lo + 2 * GDN_HEAD_V])
            u_s[s, :, lo:lo + GDN_HEAD_V] = uw[:, :GDN_HEAD_V]
            w_s[s, :, lo:lo + GDN_HEAD_V] = uw[:, GDN_HEAD_V:].astype(BF16)

        for s, h, u in heads:
            lo = h * GDN_HEAD_K
            ps = par * nseq + s
            st = sst[s, lo:lo + GDN_HEAD_K, :]
            s16 = st.astype(BF16)
            v_new = u_s[s, :, lo:lo + GDN_HEAD_V] - _bdot(w_s[s, :, lo:lo + GDN_HEAD_V], s16)
            v16 = v_new.astype(BF16)
            o_h = _bdot(qd_s[ps, :, lo:lo + GDN_HEAD_K], s16) + _bdot(attn_s[par * n_units + u], v16)
            g_last = _col(glast_s[ps], LANE_G + h)
            sst[s, lo:lo + GDN_HEAD_K, :] = st * jnp.exp(g_last) + lax.dot_general(
                kd_s[ps, :, lo:lo + GDN_HEAD_K], v16, (((0,), (0,)), ((), ())), preferred_element_type=F32)
            ms = jnp.mean(o_h * o_h, axis=-1, keepdims=True)
            o_ref[s, :, lo:lo + GDN_HEAD_V] = ((o_h * lax.rsqrt(ms + RMS_EPS) * nw_ref[...]) *
                                               _silu(z_ref[s, :, lo:lo + GDN_HEAD_V])).astype(BF16)

    def ssd_chunks():
        for s in range(nseq):
            _ssd_chunk(zs_ref.at[s], xs_ref.at[s], b_ref.at[s], c_ref.at[s], sm_ref.at[s], bias_ref, alog_ref,
                       dexp_ref, snw_ref, y_ref.at[s], hst.at[s], y_s.at[s])

    @pl.when(c_idx == 0)
    def _():
        sst[...] = jnp.zeros(sst.shape, F32)
        hst[...] = jnp.zeros(hst.shape, F32)
        stage_a(0)
        ssd_chunks()

    middle = (c_idx > 0) & (c_idx < n_chunks)
    for par in range(2):
        @pl.when(middle & (c_idx % 2 == par))
        def _(par=par):
            stages_bcd(1 - par)
            stage_a(par)
            ssd_chunks()

    @pl.when(c_idx == n_chunks)
    def _():
        stages_bcd((n_chunks - 1) % 2)
        st_ref[...] = sst[...]
        hst_ref[...] = hst[...]


GDN_SEQS_PER_STEP = 2


def _mixer_prefill(proj, bsz, seq, lw):
    cs = MIXER_CHUNK
    nseq = GDN_SEQS_PER_STEP if bsz % GDN_SEQS_PER_STEP == 0 else 1
    assert seq % cs == 0
    nc = seq // cs
    proj3 = proj.reshape(bsz, seq, PROJ_COLS)
    cur = lambda c: jnp.minimum(c, nc - 1)
    prev = lambda c: jnp.maximum(c - 1, 0)
    pcol = lambda width, off, ch: pl.BlockSpec((nseq, cs, width), lambda b, c: (b, ch(c), off // width))
    full = lambda a: pl.BlockSpec(a.shape, lambda b, c: (0,) * a.ndim)
    weights = [lw['gate_bias'], lw['gate_alog'], lw['gdn_norm_w'], lw['ssd_d_exp'], lw['ssd_norm_w']]
    units = nseq * GDN_HEADS
    o, st, y, hst = pl.pallas_call(
        functools.partial(_mixer_prefill_kernel, n_chunks=nc),
        grid=(bsz // nseq, nc + 1),
        in_specs=[pcol(GDN_DIM, COL_Q, cur), pcol(GDN_DIM, COL_K, cur), pcol(GDN_DIM, COL_V, cur),
                  pcol(GDN_DIM, COL_Z_GDN, prev), pcol(LANES, COL_SMALL, cur),
                  pcol(SSD_INNER, COL_Z_SSD, cur), pcol(SSD_INNER, COL_XS, cur), pcol(SSD_BC, COL_B, cur),
                  pcol(SSD_BC, COL_C, cur)] + [full(w) for w in weights],
        out_specs=[pl.BlockSpec((nseq, cs, GDN_DIM), lambda b, c: (b, prev(c), 0)),
                   pl.BlockSpec((nseq, GDN_HEADS * GDN_HEAD_K, GDN_HEAD_V), lambda b, c: (b, 0, 0)),
                   pl.BlockSpec((nseq, cs, SSD_INNER), lambda b, c: (b, cur(c), 0)),
                   pl.BlockSpec((nseq, SSD_INNER, SSD_STATE), lambda b, c: (b, 0, 0))],
        out_shape=[jax.ShapeDtypeStruct((bsz, seq, GDN_DIM), BF16),
                   jax.ShapeDtypeStruct((bsz, GDN_HEADS * GDN_HEAD_K, GDN_HEAD_V), F32),
                   jax.ShapeDtypeStruct((bsz, seq, SSD_INNER), BF16),
                   jax.ShapeDtypeStruct((bsz, SSD_INNER, SSD_STATE), F32)],
        scratch_shapes=[pltpu.VMEM((nseq, GDN_HEADS * GDN_HEAD_K, GDN_HEAD_V), F32),
                        pltpu.VMEM((nseq, SSD_INNER, SSD_STATE), F32),
                        pltpu.VMEM((nseq, cs, SSD_INNER), F32),
                        pltpu.VMEM((2 * nseq, cs, GDN_DIM), BF16),
                        pltpu.VMEM((2 * nseq, cs, GDN_DIM), BF16),
                        pltpu.VMEM((2 * nseq, cs, 2 * GDN_DIM), BF16),
                        pltpu.VMEM((2 * units, cs, cs), F32),
                        pltpu.VMEM((2 * units, cs, cs), BF16),
                        pltpu.VMEM((units, cs, cs), BF16),
                        pltpu.VMEM((units, cs, cs), F32),
                        pltpu.VMEM((nseq, cs, GDN_DIM), F32),
                        pltpu.VMEM((nseq, cs, GDN_DIM), BF16),
                        pltpu.VMEM((2 * nseq, 1, LANES), F32)],
        compiler_params=_params("parallel", "arbitrary"),
    )(*([proj3] * 9), *weights)
    return y.reshape(bsz * seq, SSD_INNER), hst, o.reshape(bsz * seq, GDN_DIM), st


def _softmax_rows(s):
    m = jnp.max(s, axis=-1, keepdims=True)
    e = jnp.exp(s - m)
    return e / jnp.sum(e, axis=-1, keepdims=True)


def _xattn_prefill_kernel(x_ref, y_ref, og_ref, w1_ref, w2_ref, nw_ref, wq_ref, mk_ref, mv_ref, wo_ref, o_ref):
    x = x_ref[...] + jnp.dot(y_ref[...], w1_ref[...], preferred_element_type=F32) + \
        jnp.dot(og_ref[...], w2_ref[...], preferred_element_type=F32)
    ms = jnp.mean(x * x, axis=-1, keepdims=True)
    h = x * lax.rsqrt(ms + RMS_EPS) * nw_ref[...]
    q = jnp.dot(h.astype(BF16), wq_ref[...], preferred_element_type=F32)
    outs = []
    for hd in range(XA_HEADS):
        lo = hd * XA_HEAD_DIM
        s = _mm_nt(q[:, lo:lo + XA_HEAD_DIM], mk_ref[:, lo:lo + XA_HEAD_DIM]) * (XA_HEAD_DIM ** -0.5)
        outs.append(_mm(_softmax_rows(s), mv_ref[:, lo:lo + XA_HEAD_DIM]))
    att = jnp.concatenate(outs, axis=-1)
    o_ref[...] = x + jnp.dot(att.astype(BF16), wo_ref[...], preferred_element_type=F32)


def _xattn_prefill(x, y, og, w1, w2, nw, wq, mk, mv, wo, bsz, seq, tq):
    n_mem = mk.shape[1]
    nt = seq // tq
    rows = lambda a: pl.BlockSpec((tq, a.shape[1]), lambda b, t: (b * nt + t, 0))
    return pl.pallas_call(
        _xattn_prefill_kernel,
        grid=(bsz, nt),
        in_specs=[rows(x), rows(y), rows(og),
                  pl.BlockSpec(w1.shape, lambda b, t: (0, 0)),
                  pl.BlockSpec(w2.shape, lambda b, t: (0, 0)),
                  pl.BlockSpec((1, D_MODEL), lambda b, t: (0, 0)),
                  pl.BlockSpec((D_MODEL, D_MODEL), lambda b, t: (0, 0)),
                  pl.BlockSpec((None, n_mem, D_MODEL), lambda b, t: (b, 0, 0)),
                  pl.BlockSpec((None, n_mem, D_MODEL), lambda b, t: (b, 0, 0)),
                  pl.BlockSpec((D_MODEL, D_MODEL), lambda b, t: (0, 0))],
        out_specs=pl.BlockSpec((tq, D_MODEL), lambda b, t: (b * nt + t, 0)),
        out_shape=jax.ShapeDtypeStruct(x.shape, F32),
        compiler_params=_params("parallel", "arbitrary"),
    )(x, y, og, w1, w2, nw.reshape(1, D_MODEL), wq, mk, mv, wo)


def _xattn_decode_kernel(q_ref, mk_ref, mv_ref, o_ref):
    tb = q_ref.shape[0]
    for i in range(tb):
        q = q_ref[i]
        qh = jnp.concatenate([q[:, hd * XA_HEAD_DIM:(hd + 1) * XA_HEAD_DIM] for hd in range(XA_HEADS)], axis=0)
        s = jnp.sum(mk_ref[i] * qh[None], axis=-1, keepdims=True) * (XA_HEAD_DIM ** -0.5)
        e = jnp.exp(s - jnp.max(s, axis=0, keepdims=True))
        p = e / jnp.sum(e, axis=0, keepdims=True)
        o_ref[i] = jnp.sum(p * mv_ref[i], axis=0)


def _xattn_decode(q, mk_all, mv_all, layer, tb):
    _, nb, n_mem, nh, hd = mk_all.shape
    return pl.pallas_call(
        _xattn_decode_kernel,
        grid=(nb // tb,),
        in_specs=[pl.BlockSpec((tb, 1, D_MODEL), lambda i: (i, 0, 0)),
                  pl.BlockSpec((None, tb, n_mem, nh, hd), lambda i: (layer, i, 0, 0, 0)),
                  pl.BlockSpec((None, tb, n_mem, nh, hd), lambda i: (layer, i, 0, 0, 0))],
        out_specs=pl.BlockSpec((tb, nh, hd), lambda i: (i, 0, 0)),
        out_shape=jax.ShapeDtypeStruct((nb, nh, hd), F32),
        compiler_params=_params("parallel"),
    )(q.reshape(nb, 1, D_MODEL), mk_all, mv_all).reshape(nb, D_MODEL)


def _ffn_prefill_kernel(x_ref, nw_ref, wgu_ref, cw_ref, cb_ref, wd_ref, o_ref, tail_ref,
                        ext_ref, hist_ref, hm_ref, *, tiles_per_seq):
    i = pl.program_id(0)
    tm = x_ref.shape[0]
    d_ff = wd_ref.shape[0]

    @pl.when(i == 0)
    def _():
        hist_ref[...] = jnp.zeros(hist_ref.shape, F32)

    x = x_ref[...]
    ms = jnp.mean(x * x, axis=-1, keepdims=True)
    xn = (x * lax.rsqrt(ms + RMS_EPS) * nw_ref[...]).astype(BF16)
    starts_seq = i % tiles_per_seq == 0
    for ci, lo in enumerate(range(0, d_ff, MXU_COLS)):
        cols = slice(lo, min(lo + MXU_COLS, d_ff))
        width = cols.stop - cols.start
        gate = jnp.dot(xn, wgu_ref[:, cols], preferred_element_type=F32)
        up = jnp.dot(xn, wgu_ref[:, d_ff + cols.start:d_ff + cols.stop], preferred_element_type=F32)
        tail_ref[:, cols] = gate[tm - HIST_ROWS:tm, :]
        buf = ext_ref.at[ci % CONV_BUFS]
        buf[0:HIST_ROWS, 0:width] = jnp.where(starts_seq, 0.0, hist_ref[:, cols])
        buf[HIST_ROWS:HIST_ROWS + tm, 0:width] = gate
        hist_ref[:, cols] = gate[tm - HIST_ROWS:tm, :]
        c = _causal_taps(buf[:, 0:width], cw_ref, cb_ref[:, cols], cols, FFN_CONV, tm)
        hm_ref[:, cols] = (_silu(c) * up).astype(BF16)
    o_ref[...] = x + jnp.dot(hm_ref[...], wd_ref[...], preferred_element_type=F32)


def _ffn_prefill(x, nw, wgu, cw, cb, wd, seq, tm):
    m = x.shape[0]
    d_ff = wd.shape[0]
    assert m % tm == 0 and seq % tm == 0 and tm >= HIST_ROWS
    whole = lambda a: pl.BlockSpec(a.shape, lambda i: (0, 0), pipeline_mode=pl.Buffered(1))
    ins = [nw.reshape(1, D_MODEL), wgu, cw, cb.reshape(1, d_ff), wd]
    return pl.pallas_call(
        functools.partial(_ffn_prefill_kernel, tiles_per_seq=seq // tm),
        grid=(m // tm,),
        in_specs=[pl.BlockSpec((tm, D_MODEL), lambda i: (i, 0))] + [whole(a) for a in ins],
        out_specs=[pl.BlockSpec((tm, D_MODEL), lambda i: (i, 0)),
                   pl.BlockSpec((None, HIST_ROWS, d_ff), lambda i: (i, 0, 0))],
        out_shape=[jax.ShapeDtypeStruct((m, D_MODEL), F32),
                   jax.ShapeDtypeStruct((m // tm, HIST_ROWS, d_ff), F32)],
        scratch_shapes=[pltpu.VMEM((CONV_BUFS, HIST_ROWS + tm, MXU_COLS), F32),
                        pltpu.VMEM((HIST_ROWS, d_ff), F32),
                        pltpu.VMEM((tm, d_ff), BF16)],
        compiler_params=_params("arbitrary"),
    )(x, *ins)


def _ffn_decode_kernel(x_ref, g_ref, h0_ref, h1_ref, u_ref, cw_ref, cb_ref, wd_ref, o_ref):
    acc = cb_ref[...] + h0_ref[...] * cw_ref[0:1, :] + h1_ref[...] * cw_ref[1:2, :] + g_ref[...] * cw_ref[2:3, :]
    hmid = _silu(acc) * u_ref[...]
    o_ref[...] = x_ref[...] + jnp.dot(hmid.astype(BF16), wd_ref[...], preferred_element_type=F32)


def _ffn_decode(x, gu, hist0, hist1, cw, cb, wd):
    m = x.shape[0]
    d_ff = wd.shape[0]
    rows = lambda width, col: pl.BlockSpec((m, width), lambda i: (0, col))
    return pl.pallas_call(
        _ffn_decode_kernel,
        grid=(1,),
        in_specs=[rows(D_MODEL, 0), rows(d_ff, 0), rows(d_ff, 0), rows(d_ff, 0), rows(d_ff, 1),
                  pl.BlockSpec((FFN_CONV, d_ff), lambda i: (0, 0)),
                  pl.BlockSpec((1, d_ff), lambda i: (0, 0)),
                  pl.BlockSpec((d_ff, D_MODEL), lambda i: (0, 0))],
        out_specs=rows(D_MODEL, 0),
        out_shape=jax.ShapeDtypeStruct((m, D_MODEL), F32),
        compiler_params=_params("arbitrary"),
    )(x, gu, hist0, hist1, gu, cw, cb.reshape(1, d_ff), wd)


def _expand_lanes(rows, first_lane, width, n_out, parts):
    k = lax.broadcasted_iota(jnp.int32, (LANES, n_out), 0)
    l = lax.broadcasted_iota(jnp.int32, (LANES, n_out), 1)
    sel = jnp.where(l // width == k - first_lane, 1.0, 0.0)
    return _mm_lhs_split(rows, sel, parts)


def _decode_rows_kernel(proj_ref, sh0, sh1, sh2, gh0, gh1, gh2,
                        scw_ref, scb_ref, gcw_ref, bias_ref, alog_ref,
                        xs_ref, xdt_ref, b_ref, c_ref, sdec_ref,
                        q_ref, k_ref, v_ref, beta_ref, gdec_ref):
    def conv(hists, new, cw, cb):
        acc = new * cw[3:4, :]
        for t, hr in enumerate(hists):
            acc = acc + hr[...] * cw[t:t + 1, :]
        return acc if cb is None else acc + cb

    xbc_new = jnp.concatenate([proj_ref[:, COL_XS:COL_XS + SSD_INNER],
                               proj_ref[:, COL_B:COL_B + SSD_BC],
                               proj_ref[:, COL_C:COL_C + SSD_BC]], axis=-1)
    xbc = _silu(conv((sh0, sh1, sh2), xbc_new, scw_ref, scb_ref[...]))
    sp, sg, log_dec = _gate_rows(proj_ref[:, COL_SMALL:COL_SMALL + LANES], bias_ref[...], alog_ref[...])
    xs = xbc[:, 0:SSD_INNER]
    xs_ref[...] = xs
    xdt_ref[...] = xs * _expand_lanes(sp, LANE_DT, SSD_HEAD_DIM, SSD_INNER, 3)
    b_ref[...] = xbc[:, SSD_INNER:SSD_INNER + SSD_BC]
    c_ref[...] = xbc[:, SSD_INNER + SSD_BC:SSD_INNER + 2 * SSD_BC]
    sdec_ref[...] = jnp.exp(_expand_lanes(log_dec, LANE_DT, SSD_HEAD_DIM, SSD_INNER, 3))

    qkv_new = proj_ref[:, COL_Q:COL_Q + 3 * GDN_DIM]
    qkv = _silu(conv((gh0, gh1, gh2), qkv_new, gcw_ref, None))
    for h in range(GDN_HEADS):
        lo = h * GDN_HEAD_K
        qh = qkv[:, lo:lo + GDN_HEAD_K]
        kh = qkv[:, GDN_DIM + lo:GDN_DIM + lo + GDN_HEAD_K]
        q_ref[:, lo:lo + GDN_HEAD_K] = qh * lax.rsqrt(jnp.sum(qh * qh, axis=-1, keepdims=True) + L2_EPS) * \
            (GDN_HEAD_K ** -0.5)
        k_ref[:, lo:lo + GDN_HEAD_K] = kh * lax.rsqrt(jnp.sum(kh * kh, axis=-1, keepdims=True) + L2_EPS)
    v_ref[...] = qkv[:, 2 * GDN_DIM:3 * GDN_DIM]
    beta_ref[...] = _expand_lanes(sg, LANE_BETA, GDN_HEAD_V, GDN_DIM, 3)
    gdec_ref[...] = jnp.exp(_expand_lanes(log_dec, LANE_G, GDN_HEAD_V, GDN_DIM, 3))


def _decode_rows(proj, ssd_hist, gdn_hist, lw):
    m = proj.shape[0]
    ins = [proj] + [ssd_hist[:, t] for t in range(SSD_CONV - 1)] + [gdn_hist[:, t] for t in range(GDN_CONV - 1)]
    ins += [lw['ssd_conv_w'], lw['ssd_conv_b'], lw['gdn_conv_w'], lw['gate_bias'], lw['gate_alog']]
    widths = [SSD_INNER, SSD_INNER, SSD_BC, SSD_BC, SSD_INNER, GDN_DIM, GDN_DIM, GDN_DIM, GDN_DIM, GDN_DIM]
    return pl.pallas_call(
        _decode_rows_kernel,
        grid=(1,),
        in_specs=[pl.BlockSpec(a.shape, lambda i, nd=a.ndim: (0,) * nd) for a in ins],
        out_specs=[pl.BlockSpec((m, w), lambda i: (0, 0)) for w in widths],
        out_shape=[jax.ShapeDtypeStruct((m, w), F32) for w in widths],
        compiler_params=_params("arbitrary"),
    )(*ins)


def _rows_to_cols(rows, i, parts):
    tb = rows.shape[0]
    r = lax.broadcasted_iota(jnp.int32, (tb, LANES), 0)
    sel = jnp.where(r == i, 1.0, 0.0).astype(BF16)
    acc = None
    for p in _split(rows, parts):
        t = lax.dot_general(p, sel, (((0,), (0,)), ((), ())), preferred_element_type=F32)
        acc = t if acc is None else acc + t
    return acc


def _only_row(rows, i):
    r = lax.broadcasted_iota(jnp.int32, rows.shape, 0)
    return jnp.where(r == i, rows, 0.0)


def _decode_state_kernel(xdt_ref, b_ref, c_ref, sdec_ref, q_ref, k_ref, v_ref, beta_ref, gdec_ref,
                         hs_ref, ss_ref, *rest):
    y_ref, o_ref, hs_out, ss_out = rest[-4:]
    tb = xdt_ref.shape[0]
    xdt = xdt_ref[...]
    bm = b_ref[...]
    cm = c_ref[...]
    sdec = sdec_ref[...]
    qn = q_ref[...]
    kn = k_ref[...]
    vv = v_ref[...]
    beta = beta_ref[...]
    gdec = gdec_ref[...]
    rows_g = SSD_INNER // SSD_GROUPS

    y_acc = jnp.zeros((tb, SSD_INNER), F32)
    ks_acc = jnp.zeros((tb, GDN_DIM), F32)
    for i in range(tb):
        dec_c = _rows_to_cols(sdec, i, 2)
        xdt_c = _rows_to_cols(xdt, i, 1)
        pieces = []
        for g in range(SSD_GROUPS):
            rs = slice(g * rows_g, (g + 1) * rows_g)
            bg = bm[i:i + 1, g * SSD_STATE:(g + 1) * SSD_STATE]
            h_new = hs_ref[i, rs, :] * dec_c[rs, :] + xdt_c[rs, :] * bg
            hs_out[i, rs, :] = h_new
            cg = _only_row(cm[:, g * SSD_STATE:(g + 1) * SSD_STATE], i)
            pieces.append(_mm_nt(cg, h_new))
        y_acc = y_acc + jnp.concatenate(pieces, axis=-1)
        g_c = _rows_to_cols(gdec, i, 2)
        s_dec = ss_ref[i] * g_c
        ss_out[i] = s_dec
        pieces = []
        for h in range(GDN_HEADS):
            rs = slice(h * GDN_HEAD_K, (h + 1) * GDN_HEAD_K)
            pieces.append(_mm(_only_row(kn[:, rs], i), s_dec[rs, :]))
        ks_acc = ks_acc + jnp.concatenate(pieces, axis=-1)
    y_ref[...] = y_acc

    delta = beta * (vv - ks_acc)
    o_acc = jnp.zeros((tb, GDN_DIM), F32)
    for i in range(tb):
        k_c = _rows_to_cols(kn, i, 1)
        pieces = []
        for h in range(GDN_HEADS):
            rs = slice(h * GDN_HEAD_K, (h + 1) * GDN_HEAD_K)
            d_row = delta[i:i + 1, rs]
            s_new = ss_out[i, rs, :] + k_c[rs, :] * d_row
            ss_out[i, rs, :] = s_new
            pieces.append(_mm(_only_row(qn[:, rs], i), s_new))
        o_acc = o_acc + jnp.concatenate(pieces, axis=-1)
    o_ref[...] = o_acc


def _decode_state(rows, hs_all, ss_all, layer, tb, carried):
    xs, xdt, bm, cm, sdec, qn, kn, vv, beta, gdec = rows
    nb = xdt.shape[0]
    row_ins = [xdt, bm, cm, sdec, qn, kn, vv, beta, gdec]
    rspec = lambda a: pl.BlockSpec((tb, a.shape[1]), lambda i: (i, 0))
    sspec = lambda a: pl.BlockSpec((None, tb) + a.shape[2:], lambda i: (layer, i, 0, 0))
    n_in = len(row_ins) + 2
    aliases = {n_in + j: 2 + j for j in range(len(carried))}
    return pl.pallas_call(
        _decode_state_kernel,
        grid=(nb // tb,),
        in_specs=[rspec(a) for a in row_ins] + [sspec(hs_all), sspec(ss_all)] +
                 [pl.BlockSpec(memory_space=pl.ANY) for _ in carried],
        out_specs=[rspec(xdt), rspec(qn), sspec(hs_all), sspec(ss_all)],
        out_shape=[jax.ShapeDtypeStruct(xdt.shape, F32), jax.ShapeDtypeStruct(qn.shape, F32),
                   jax.ShapeDtypeStruct(hs_all.shape, F32), jax.ShapeDtypeStruct(ss_all.shape, F32)],
        input_output_aliases=aliases,
        compiler_params=_params("parallel"),
    )(*row_ins, hs_all, ss_all, *carried)


def _decode_out_kernel(x_ref, y_ref, xs_ref, o_ref, proj_ref, dexp_ref, snw_ref, gnw_ref, w1_ref, w2_ref, out_ref):
    y = y_ref[...] + dexp_ref[...] * xs_ref[...]
    gs = SSD_INNER // SSD_GROUPS
    ys = []
    for g in range(SSD_GROUPS):
        yg = y[:, g * gs:(g + 1) * gs] * _silu(proj_ref[:, COL_Z_SSD + g * gs:COL_Z_SSD + (g + 1) * gs])
        ms = jnp.mean(yg * yg, axis=-1, keepdims=True)
        ys.append(yg * lax.rsqrt(ms + RMS_EPS) * snw_ref[:, g * gs:(g + 1) * gs])
    os = []
    for h in range(GDN_HEADS):
        lo = h * GDN_HEAD_V
        oh = o_ref[:, lo:lo + GDN_HEAD_V]
        ms = jnp.mean(oh * oh, axis=-1, keepdims=True)
        os.append(oh * lax.rsqrt(ms + RMS_EPS) * gnw_ref[...] *
                  _silu(proj_ref[:, COL_Z_GDN + lo:COL_Z_GDN + lo + GDN_HEAD_V]))
    yn = jnp.concatenate(ys, axis=-1)
    on = jnp.concatenate(os, axis=-1)
    out_ref[...] = x_ref[...] + _mm(yn, w1_ref[...]) + _mm(on, w2_ref[...])


def _decode_out(x, y, xs, o, proj, lw):
    ins = [x, y, xs, o, proj, lw['ssd_d_exp'], lw['ssd_norm_w'], lw['gdn_norm_w'], lw['w_out_ssd'], lw['w_out_gdn']]
    return pl.pallas_call(
        _decode_out_kernel,
        grid=(1,),
        in_specs=[pl.BlockSpec(a.shape, lambda i, nd=a.ndim: (0,) * nd) for a in ins],
        out_specs=pl.BlockSpec(x.shape, lambda i: (0, 0)),
        out_shape=jax.ShapeDtypeStruct(x.shape, F32),
        compiler_params=_params("arbitrary"),
    )(*ins)


def _identity_taps(taps, n):
    return jnp.concatenate([jnp.zeros((taps - 1, n), F32), jnp.ones((1, n), F32)], axis=0)


def _prep_layer(i, p):
    w_in = p['w_in'][i]
    o_zs = 0
    o_xbc = o_zs + SSD_INNER
    o_dt = o_xbc + SSD_INNER + 2 * SSD_BC
    o_qkv = o_dt + SSD_HEADS
    o_zg = o_qkv + 3 * GDN_DIM
    o_b = o_zg + GDN_DIM
    o_a = o_b + GDN_HEADS
    used = SSD_HEADS + 2 * GDN_HEADS
    w_perm = jnp.concatenate([
        w_in[:, o_zs:o_zs + SSD_INNER],
        w_in[:, o_xbc:o_xbc + SSD_INNER],
        w_in[:, o_qkv:o_qkv + 3 * GDN_DIM],
        w_in[:, o_zg:o_zg + GDN_DIM],
        w_in[:, o_xbc + SSD_INNER:o_xbc + SSD_INNER + 2 * SSD_BC],
        w_in[:, o_dt:o_dt + SSD_HEADS],
        w_in[:, o_b:o_b + GDN_HEADS],
        w_in[:, o_a:o_a + GDN_HEADS],
        jnp.zeros((D_MODEL, PROJ_COLS - COL_SMALL - used), F32)], axis=1).astype(BF16)
    lane_pad = jnp.zeros((LANES - used,), F32)
    scw = p['ssd_conv_w'][i]
    scb = p['ssd_conv_b'][i].reshape(1, -1)
    gcw = p['gdn_conv_w'][i]
    return {
        'w_in': w_perm,
        'norm_mix_w': p['norm_mix_w'][i],
        'ssd_conv_w': scw, 'ssd_conv_b': scb, 'gdn_conv_w': gcw,
        'in_cw': jnp.concatenate([_identity_taps(SSD_CONV, SSD_INNER), scw[:, :SSD_INNER], gcw,
                                  _identity_taps(GDN_CONV, GDN_DIM), scw[:, SSD_INNER:],
                                  _identity_taps(SSD_CONV, PROJ_COLS - COL_SMALL)], axis=1),
        'in_cb': jnp.concatenate([jnp.zeros((1, COL_XS), F32), scb[:, :SSD_INNER],
                                  jnp.zeros((1, COL_B - COL_Q), F32), scb[:, SSD_INNER:],
                                  jnp.zeros((1, PROJ_COLS - COL_SMALL), F32)], axis=1),
        'gate_bias': jnp.concatenate([p['ssd_dt_bias'][i], jnp.zeros((GDN_HEADS,), F32), p['gdn_dt_bias'][i],
                                      lane_pad]).reshape(1, LANES),
        'gate_alog': jnp.concatenate([p['ssd_a_log'][i], jnp.zeros((GDN_HEADS,), F32), p['gdn_a_log'][i],
                                      lane_pad]).reshape(1, LANES),
        'ssd_d_exp': jnp.repeat(p['ssd_d'][i], SSD_HEAD_DIM).reshape(1, SSD_INNER),
        'ssd_norm_w': p['ssd_norm_w'][i].reshape(1, SSD_INNER),
        'gdn_norm_w': p['gdn_norm_w'][i].reshape(1, GDN_HEAD_V),
        'w_out_ssd': p['w_out'][i, :SSD_INNER].astype(BF16),
        'w_out_gdn': p['w_out'][i, SSD_INNER:].astype(BF16),
        'norm_xa_w': p['norm_xa_w'][i], 'norm_mem_w': p['norm_mem_w'][i],
        'xa_wq': p['xa_wq'][i].astype(BF16), 'xa_wo': p['xa_wo'][i].astype(BF16),
        'xa_wkv': jnp.concatenate([p['xa_wk'][i], p['xa_wv'][i]], axis=1).astype(BF16),
        'norm_ffn_w': p['norm_ffn_w'][i],
        'ffn_w_gu': jnp.concatenate([p['ffn_w_gate'][i], p['ffn_w_up'][i]], axis=1).astype(BF16),
        'ffn_conv_w': p['ffn_conv_w'][i], 'ffn_conv_b': p['ffn_conv_b'][i],
        'ffn_w_down': p['ffn_w_down'][i].astype(BF16),
    }


def _unpermute_conv_rows(proj_rows):
    ssd = jnp.concatenate([proj_rows[..., COL_XS:COL_XS + SSD_INNER],
                           proj_rows[..., COL_B:COL_B + SSD_BC],
                           proj_rows[..., COL_C:COL_C + SSD_BC]], axis=-1)
    gdn = proj_rows[..., COL_Q:COL_Q + 3 * GDN_DIM]
    return ssd, gdn


def _tile(m, pref):
    t = min(m, pref)
    while m % t:
        t //= 2
    return t


def kernel(x_prompt, x_sample, mem_prompt, cache_mem_k, cache_mem_v, state_ssd_conv, state_ssd, state_gdn_conv, state_gdn, state_ffn_conv, norm_mix_w, w_in, ssd_conv_w, ssd_conv_b, ssd_dt_bias, ssd_a_log, ssd_d, ssd_norm_w, gdn_conv_w, gdn_dt_bias, gdn_a_log, gdn_norm_w, w_out, norm_xa_w, norm_mem_w, xa_wq, xa_wk, xa_wv, xa_wo, norm_ffn_w, ffn_w_gate, ffn_w_up, ffn_conv_w, ffn_conv_b, ffn_w_down, final_norm_w):
    params = dict(norm_mix_w=norm_mix_w, w_in=w_in, ssd_conv_w=ssd_conv_w, ssd_conv_b=ssd_conv_b,
                  ssd_dt_bias=ssd_dt_bias, ssd_a_log=ssd_a_log, ssd_d=ssd_d, ssd_norm_w=ssd_norm_w,
                  gdn_conv_w=gdn_conv_w, gdn_dt_bias=gdn_dt_bias, gdn_a_log=gdn_a_log, gdn_norm_w=gdn_norm_w,
                  w_out=w_out, norm_xa_w=norm_xa_w, norm_mem_w=norm_mem_w, xa_wq=xa_wq, xa_wk=xa_wk, xa_wv=xa_wv,
                  xa_wo=xa_wo, norm_ffn_w=norm_ffn_w, ffn_w_gate=ffn_w_gate, ffn_w_up=ffn_w_up,
                  ffn_conv_w=ffn_conv_w, ffn_conv_b=ffn_conv_b, ffn_w_down=ffn_w_down)
    depth = w_in.shape[0]
    bsz, seq, _ = x_prompt.shape
    nb = x_sample.shape[0]
    n_mem = mem_prompt.shape[1]
    d_ff = ffn_w_down.shape[1]
    mp = bsz * seq

    xp = x_prompt.reshape(mp, D_MODEL)
    xs = x_sample.reshape(nb, D_MODEL)
    mem = mem_prompt.reshape(bsz * n_mem, D_MODEL)
    tm_p = _tile(seq, 1024)
    tm_c = _tile(seq, 256)
    tm_mem = _tile(bsz * n_mem, 1024)
    dec_tile = _tile(nb, DEC_TILE)

    hs_all = state_ssd.reshape(depth, nb, SSD_INNER, SSD_STATE)
    ss_all = state_gdn.reshape(depth, nb, GDN_HEADS * GDN_HEAD_K, GDN_HEAD_V)
    mk_all, mv_all = cache_mem_k, cache_mem_v
    new_states = ()

    mkp, mvp = [], []
    p_sc, p_sh, p_gc, p_gs, p_fc = [], [], [], [], []
    s_sc, s_gc, s_fc = [], [], []
    for i in range(depth):
        lw = _prep_layer(i, params)

        kv = _norm_mm(mem, lw['norm_mem_w'], lw['xa_wkv'], tm_mem)
        mk = kv[:, :D_MODEL].reshape(bsz, n_mem, D_MODEL)
        mv = kv[:, D_MODEL:].reshape(bsz, n_mem, D_MODEL)
        mkp.append(mk.reshape(bsz, n_mem, XA_HEADS, XA_HEAD_DIM))
        mvp.append(mv.reshape(bsz, n_mem, XA_HEADS, XA_HEAD_DIM))

        proj, tails = _norm_mm_conv(xp, lw['norm_mix_w'], lw['w_in'], lw['in_cw'], lw['in_cb'],
                                    [(COL_XS, COL_Z_GDN), (COL_B, COL_SMALL)], seq, tm_c)
        y, h_new, o, s_new = _mixer_prefill(proj, bsz, seq, lw)
        seq_tails = tails.reshape(bsz, seq // tm_c, HIST_ROWS, PROJ_COLS)[:, -1]
        ssd_tail, gdn_tail = _unpermute_conv_rows(seq_tails[:, HIST_ROWS - (SSD_CONV - 1):])
        p_sc.append(ssd_tail)
        p_gc.append(gdn_tail)
        p_sh.append(h_new.reshape(bsz, SSD_HEADS, SSD_HEAD_DIM, SSD_STATE))
        p_gs.append(s_new.reshape(bsz, GDN_HEADS, GDN_HEAD_K, GDN_HEAD_V))
        xp = _xattn_prefill(xp, y, o, lw['w_out_ssd'], lw['w_out_gdn'], lw['norm_xa_w'], lw['xa_wq'], mk, mv,
                            lw['xa_wo'], bsz, seq, _tile(seq, 512))
        xp, gate_tails = _ffn_prefill(xp, lw['norm_ffn_w'], lw['ffn_w_gu'], lw['ffn_conv_w'], lw['ffn_conv_b'],
                                      lw['ffn_w_down'], seq, tm_c)
        p_fc.append(gate_tails.reshape(bsz, seq // tm_c, HIST_ROWS, d_ff)[:, -1, HIST_ROWS - (FFN_CONV - 1):])

        proj_s = _norm_mm(xs, lw['norm_mix_w'], lw['w_in'], nb)
        ssd_new, gdn_new = _unpermute_conv_rows(proj_s)
        s_sc.append(jnp.concatenate([state_ssd_conv[i][:, 1:], ssd_new[:, None]], axis=1))
        s_gc.append(jnp.concatenate([state_gdn_conv[i][:, 1:], gdn_new[:, None]], axis=1))
        rows = _decode_rows(proj_s, state_ssd_conv[i], state_gdn_conv[i], lw)
        y_s, o_s, hs_new, ss_new = _decode_state(rows, hs_all, ss_all, i, dec_tile, new_states)
        new_states = (hs_new, ss_new)
        xs = _decode_out(xs, y_s, rows[0], o_s, proj_s, lw)
        q_s = _norm_mm(xs, lw['norm_xa_w'], lw['xa_wq'], nb)
        att = _xattn_decode(q_s, mk_all, mv_all, i, _tile(nb, 4))
        xs = _mm_res(xs, [att], [lw['xa_wo']], nb)
        gu_s = _norm_mm(xs, lw['norm_ffn_w'], lw['ffn_w_gu'], nb)
        s_fc.append(jnp.concatenate([state_ffn_conv[i][:, 1:], gu_s[:, None, :d_ff]], axis=1))
        xs = _ffn_decode(xs, gu_s, state_ffn_conv[i][:, 0], state_ffn_conv[i][:, 1],
                         lw['ffn_conv_w'], lw['ffn_conv_b'], lw['ffn_w_down'])

    y_prompt = _rmsnorm(xp, final_norm_w, tm_p).reshape(bsz, seq, D_MODEL)
    y_sample = _rmsnorm(xs, final_norm_w, nb).reshape(nb, 1, D_MODEL)
    return (y_prompt, y_sample, jnp.stack(mkp), jnp.stack(mvp),
            jnp.stack(p_sc), jnp.stack(p_sh), jnp.stack(p_gc), jnp.stack(p_gs), jnp.stack(p_fc),
            jnp.stack(s_sc), new_states[0].reshape(state_ssd.shape), jnp.stack(s_gc),
            new_states[1].reshape(state_gdn.shape), jnp.stack(s_fc))
```

```python
import functools

import jax
import jax.numpy as jnp
from jax import lax
from jax.experimental import pallas as pl
from jax.experimental.pallas import tpu as pltpu

F32 = jnp.float32
BF16 = jnp.bfloat16

D_MODEL = 1024
SSD_HEADS = 16
SSD_HEAD_DIM = 64
SSD_STATE = 128
SSD_GROUPS = 2
SSD_INNER = SSD_HEADS * SSD_HEAD_DIM
SSD_BC = SSD_GROUPS * SSD_STATE
SSD_CONV = 4
GDN_HEADS = 8
GDN_HEAD_K = 128
GDN_HEAD_V = 128
GDN_DIM = GDN_HEADS * GDN_HEAD_V
GDN_CONV = 4
XA_HEADS = 4
XA_HEAD_DIM = D_MODEL // XA_HEADS
FFN_CONV = 3
RMS_EPS = 1e-6
L2_EPS = 1e-6

SUBLANES = 8
LANES = 128
MXU_COLS = 256
VMEM_LIMIT_BYTES = 56 * 1024 * 1024

COL_Z_SSD = 0
COL_XS = 1024
COL_Q = 2048
COL_K = 3072
COL_V = 4096
COL_Z_GDN = 5120
COL_B = 6144
COL_C = 6400
COL_SMALL = 6656
PROJ_COLS = 7168
LANE_DT = 0
LANE_BETA = 16
LANE_G = 24

MIXER_CHUNK = 128
DEC_TILE = 8


def _params(*sem):
    return pltpu.CompilerParams(dimension_semantics=sem, vmem_limit_bytes=VMEM_LIMIT_BYTES)


def _mm(a, b):
    return jnp.dot(a.astype(BF16), b.astype(BF16), preferred_element_type=F32)


def _mm_nt(a, b):
    return lax.dot_general(a.astype(BF16), b.astype(BF16), (((1,), (1,)), ((), ())),
                           preferred_element_type=F32)


def _split(x, parts):
    out = []
    r = x
    for _ in range(parts - 1):
        p = r.astype(BF16)
        out.append(p)
        r = r - p.astype(F32)
    out.append(r.astype(BF16))
    return out


def _mm_rhs_split(a_exact, b, parts=3):
    a16 = a_exact.astype(BF16)
    acc = None
    for p in _split(b, parts):
        t = jnp.dot(a16, p, preferred_element_type=F32)
        acc = t if acc is None else acc + t
    return acc


def _mm_lhs_split(a, b_exact, parts=3):
    b16 = b_exact.astype(BF16)
    acc = None
    for p in _split(a, parts):
        t = jnp.dot(p, b16, preferred_element_type=F32)
        acc = t if acc is None else acc + t
    return acc


def _silu(x):
    h = 0.5 * x
    return h + h * jnp.tanh(h)


def _sigmoid(x):
    return 1.0 / (1.0 + jnp.exp(-x))


def _softplus(x):
    return jnp.maximum(x, 0.0) + jnp.log1p(jnp.exp(-jnp.abs(x)))


def _col(arr, lane):
    idx = lax.broadcasted_iota(jnp.int32, arr.shape, 1)
    return jnp.sum(jnp.where(idx == lane, arr, 0.0), axis=1, keepdims=True)


def _gate_rows(sm, bias, alog):
    sp = _softplus(sm + bias)
    sg = _sigmoid(sm)
    lane = lax.broadcasted_iota(jnp.int32, alog.shape, 1)
    has_decay = (lane < LANE_BETA) | ((lane >= LANE_G) & (lane < LANE_G + GDN_HEADS))
    neg_a = jnp.where(has_decay, -jnp.exp(alog), 0.0)
    return sp, sg, sp * neg_a


def _norm_mm_kernel(x_ref, nw_ref, w_ref, o_ref, xn_ref):
    @pl.when(pl.program_id(1) == 0)
    def _():
        x = x_ref[...]
        ms = jnp.mean(x * x, axis=-1, keepdims=True)
        xn_ref[...] = (x * lax.rsqrt(ms + RMS_EPS) * nw_ref[...]).astype(BF16)

    o_ref[...] = jnp.dot(xn_ref[...], w_ref[...], preferred_element_type=F32)


MAX_COL_TILE_LANES = 11 * LANES


def _col_tile(n):
    return max(t for t in range(LANES, MAX_COL_TILE_LANES + 1, LANES) if n % t == 0)


def _norm_mm(x, nw, w, tm):
    m, k = x.shape
    n = w.shape[1]
    tn = _col_tile(n)
    assert m % tm == 0
    return pl.pallas_call(
        _norm_mm_kernel,
        grid=(m // tm, n // tn),
        in_specs=[pl.BlockSpec((tm, k), lambda i, j: (i, 0)),
                  pl.BlockSpec((1, k), lambda i, j: (0, 0)),
                  pl.BlockSpec((k, tn), lambda i, j: (0, j))],
        out_specs=pl.BlockSpec((tm, tn), lambda i, j: (i, j)),
        out_shape=jax.ShapeDtypeStruct((m, n), F32),
        scratch_shapes=[pltpu.VMEM((tm, k), BF16)],
        compiler_params=_params("parallel", "arbitrary"),
    )(x, nw.reshape(1, k), w)


HIST_ROWS = SUBLANES


CONV_BUFS = 4


def _norm_mm_conv_kernel(x_ref, nw_ref, w_ref, cw_ref, cb_ref, o_ref, tail_ref, ext_ref, hist_ref,
                         *, taps, tiles_per_seq, conv_ranges):
    i = pl.program_id(0)
    tm = x_ref.shape[0]
    n = w_ref.shape[1]

    @pl.when(i == 0)
    def _():
        hist_ref[...] = jnp.zeros(hist_ref.shape, F32)

    x = x_ref[...]
    ms = jnp.mean(x * x, axis=-1, keepdims=True)
    xn = (x * lax.rsqrt(ms + RMS_EPS) * nw_ref[...]).astype(BF16)
    starts_seq = i % tiles_per_seq == 0
    for ci, lo in enumerate(range(0, n, MXU_COLS)):
        cols = slice(lo, min(lo + MXU_COLS, n))
        acc = jnp.dot(xn, w_ref[:, cols], preferred_element_type=F32)
        tail_ref[:, cols] = acc[tm - HIST_ROWS:tm, :]
        hit = [biased for a, b, biased in conv_ranges if a <= cols.start and cols.stop <= b]
        if not hit:
            o_ref[:, cols] = acc
            continue
        bias = cb_ref[:, cols] if hit[0] else None
        buf = ext_ref.at[ci % CONV_BUFS]
        width = cols.stop - cols.start
        buf[0:HIST_ROWS, 0:width] = jnp.where(starts_seq, 0.0, hist_ref[:, cols])
        buf[HIST_ROWS:HIST_ROWS + tm, 0:width] = acc
        hist_ref[:, cols] = acc[tm - HIST_ROWS:tm, :]
        o_ref[:, cols] = _silu(_causal_taps(buf[:, 0:width], cw_ref, bias, cols, taps, tm))


def _causal_taps(ext, cw_ref, bias, cols, taps, tm):
    rows = slice(HIST_ROWS, HIST_ROWS + tm)
    w = [cw_ref[k:k + 1, cols] for k in range(taps)]
    if taps == 4:
        lag1 = pltpu.roll(ext, 1, axis=0)
        older = pltpu.roll(ext * w[1] + lag1 * w[0], 2, axis=0)
        out = ext[rows, :] * w[3] + lag1[rows, :] * w[2] + older[rows, :]
    else:
        out = ext[rows, :] * w[taps - 1]
        for lag in range(1, taps):
            out = out + pltpu.roll(ext, lag, axis=0)[rows, :] * w[taps - 1 - lag]
    return out if bias is None else out + bias


def _norm_mm_conv(x, nw, w, cw, cb, conv_ranges, seq, tm):
    m, k = x.shape
    n = w.shape[1]
    taps = cw.shape[0]
    assert m % tm == 0 and seq % tm == 0 and tm >= HIST_ROWS
    assert all(a % MXU_COLS == 0 and b % MXU_COLS == 0 for a, b, _ in conv_ranges)
    whole = lambda a: pl.BlockSpec(a.shape, lambda i: (0, 0), pipeline_mode=pl.Buffered(1))
    nw2 = nw.reshape(1, k)
    return pl.pallas_call(
        functools.partial(_norm_mm_conv_kernel, taps=taps, tiles_per_seq=seq // tm,
                          conv_ranges=tuple(conv_ranges)),
        grid=(m // tm,),
        in_specs=[pl.BlockSpec((tm, k), lambda i: (i, 0)), whole(nw2), whole(w), whole(cw), whole(cb)],
        out_specs=[pl.BlockSpec((tm, n), lambda i: (i, 0)),
                   pl.BlockSpec((None, HIST_ROWS, n), lambda i: (i, 0, 0))],
        out_shape=[jax.ShapeDtypeStruct((m, n), F32),
                   jax.ShapeDtypeStruct((m // tm, HIST_ROWS, n), F32)],
        scratch_shapes=[pltpu.VMEM((CONV_BUFS, HIST_ROWS + tm, MXU_COLS), F32),
                        pltpu.VMEM((HIST_ROWS, n), F32)],
        compiler_params=_params("arbitrary"),
    )(x, nw2, w, cw, cb)


def _mm_res_kernel(*refs, n_in):
    res_ref = refs[0]
    a_refs = refs[1:1 + n_in]
    w_refs = refs[1 + n_in:1 + 2 * n_in]
    o_ref = refs[1 + 2 * n_in]
    acc = res_ref[...]
    for a_ref, w_ref in zip(a_refs, w_refs):
        acc = acc + jnp.dot(a_ref[...].astype(BF16), w_ref[...], preferred_element_type=F32)
    o_ref[...] = acc


def _mm_res(res, a_list, w_list, tm):
    m, n = res.shape
    assert m % tm == 0
    n_in = len(a_list)
    in_specs = [pl.BlockSpec((tm, n), lambda i: (i, 0))]
    in_specs += [pl.BlockSpec((tm, a.shape[1]), lambda i: (i, 0)) for a in a_list]
    in_specs += [pl.BlockSpec(w.shape, lambda i: (0, 0)) for w in w_list]
    return pl.pallas_call(
        functools.partial(_mm_res_kernel, n_in=n_in),
        grid=(m // tm,),
        in_specs=in_specs,
        out_specs=pl.BlockSpec((tm, n), lambda i: (i, 0)),
        out_shape=jax.ShapeDtypeStruct((m, n), F32),
        compiler_params=_params("parallel"),
    )(res, *a_list, *w_list)


def _rmsnorm_kernel(x_ref, nw_ref, o_ref):
    x = x_ref[...]
    ms = jnp.mean(x * x, axis=-1, keepdims=True)
    o_ref[...] = x * lax.rsqrt(ms + RMS_EPS) * nw_ref[...]


def _rmsnorm(x, nw, tm):
    m, k = x.shape
    return pl.pallas_call(
        _rmsnorm_kernel,
        grid=(m // tm,),
        in_specs=[pl.BlockSpec((tm, k), lambda i: (i, 0)), pl.BlockSpec((1, k), lambda i: (0, 0))],
        out_specs=pl.BlockSpec((tm, k), lambda i: (i, 0)),
        out_shape=jax.ShapeDtypeStruct((m, k), F32),
        compiler_params=_params("parallel"),
    )(x, nw.reshape(1, k))


def _causal_mask(cs, strict=False):
    r = lax.broadcasted_iota(jnp.int32, (cs, cs), 0)
    c = lax.broadcasted_iota(jnp.int32, (cs, cs), 1)
    return (r > c) if strict else (r >= c)


def _ssd_chunk(z_ref, xs_ref, b_ref, c_ref, sm_ref, bias_ref, alog_ref, dexp_ref, nw_ref, y_ref, hst, y_s):
    cs = xs_ref.shape[0]
    xs = xs_ref[...]
    bm = b_ref[...]
    cm = c_ref[...]

    sp, _, dta = _gate_rows(sm_ref[...], bias_ref[...], alog_ref[...])
    causal = _causal_mask(cs)
    cum = _mm_rhs_split(jnp.where(causal, 1.0, 0.0), dta)
    cum_row = cum.T

    lane_lo = lax.broadcasted_iota(jnp.int32, (1, LANES), 1) < SSD_HEAD_DIM
    row_lo = lax.broadcasted_iota(jnp.int32, (LANES, 1), 0) < SSD_HEAD_DIM
    heads_per_group = SSD_HEADS // SSD_GROUPS
    for g in range(SSD_GROUPS):
        cg = cm[:, g * SSD_STATE:(g + 1) * SSD_STATE]
        bg = bm[:, g * SSD_STATE:(g + 1) * SSD_STATE]
        cb = _mm_nt(cg, bg)
        rows_g = heads_per_group * SSD_HEAD_DIM
        y_state = _mm_nt(cg, hst[g * rows_g:(g + 1) * rows_g, :])
        for hp in range(heads_per_group // 2):
            h0 = g * heads_per_group + 2 * hp
            lo = h0 * SSD_HEAD_DIM
            xs_pair = xs[:, lo:lo + LANES]
            ci = [_col(cum, LANE_DT + h0), _col(cum, LANE_DT + h0 + 1)]
            dti = [_col(sp, LANE_DT + h0), _col(sp, LANE_DT + h0 + 1)]
            xdt_pair = xs_pair * jnp.where(lane_lo, dti[0], dti[1])
            y_pair = None
            for e in range(2):
                cj = cum_row[LANE_DT + h0 + e:LANE_DT + h0 + e + 1, :]
                seg = jnp.exp(jnp.where(causal, ci[e] - cj, -jnp.inf))
                keep = lane_lo if e == 0 else jnp.logical_not(lane_lo)
                t = _mm(cb * seg, jnp.where(keep, xdt_pair, 0.0))
                y_pair = t if y_pair is None else y_pair + t
            y_pair = y_pair + y_state[:, 2 * hp * SSD_HEAD_DIM:2 * hp * SSD_HEAD_DIM + LANES] * \
                jnp.where(lane_lo, jnp.exp(ci[0]), jnp.exp(ci[1]))
            y_pair = y_pair + dexp_ref[:, lo:lo + LANES] * xs_pair
            y_s[:, lo:lo + LANES] = y_pair
            cl = [ci[0][cs - 1:cs, :], ci[1][cs - 1:cs, :]]
            to_end = jnp.where(lane_lo, jnp.exp(cl[0] - ci[0]), jnp.exp(cl[1] - ci[1]))
            contrib = _mm((xdt_pair * to_end).T, bg)
            dec = jnp.where(row_lo, jnp.exp(cl[0]), jnp.exp(cl[1]))
            hst[lo:lo + LANES, :] = hst[lo:lo + LANES, :] * dec + contrib

    gs = SSD_INNER // SSD_GROUPS
    for g in range(SSD_GROUPS):
        yg = y_s[:, g * gs:(g + 1) * gs] * _silu(z_ref[:, g * gs:(g + 1) * gs])
        ms = jnp.mean(yg * yg, axis=-1, keepdims=True)
        y_ref[:, g * gs:(g + 1) * gs] = (yg * lax.rsqrt(ms + RMS_EPS) * nw_ref[:, g * gs:(g + 1) * gs]).astype(BF16)


INV_BASE = 16


def _same_block(cs, size):
    r = lax.broadcasted_iota(jnp.int32, (cs, cs), 0) // size
    c = lax.broadcasted_iota(jnp.int32, (cs, cs), 1) // size
    return r == c


def _bdot(a16, b16):
    return jnp.dot(a16, b16, preferred_element_type=F32)


def _mixer_prefill_kernel(q_ref, k_ref, v_ref, z_ref, sm_ref,
                          zs_ref, xs_ref, b_ref, c_ref,
                          bias_ref, alog_ref, nw_ref, dexp_ref, snw_ref,
                          o_ref, st_ref, y_ref, hst_ref,
                          sst, hst, y_s,
                          qd_s, kd_s, rhs_s, a_s, attn_s, x_s, d_s, u_s, w_s, glast_s, *, n_chunks):
    c_idx = pl.program_id(1)
    nseq, cs = q_ref.shape[0], q_ref.shape[1]
    n_units = nseq * GDN_HEADS
    heads = [(s, h, s * GDN_HEADS + h) for s in range(nseq) for h in range(GDN_HEADS)]
    qc, kc, vc = q_ref, k_ref, v_ref
    causal = _causal_mask(cs)
    strict = _causal_mask(cs, strict=True)

    def stage_a(par):
        sg, gc, gc_row = [], [], []
        for s in range(nseq):
            _, sg_s, g_step = _gate_rows(sm_ref[s], bias_ref[...], alog_ref[...])
            gc_s = _mm_rhs_split(jnp.where(causal, 1.0, 0.0), g_step)
            sg.append(sg_s)
            gc.append(gc_s)
            gc_row.append(gc_s.T)
            glast_s[par * nseq + s] = gc_s[cs - 1:cs, :]
        for s, h, u in heads:
            _stage_a_unit(par, s, h, u, sg, gc, gc_row)

    def _stage_a_unit(par, s, h, u, sg, gc, gc_row):
        u = par * n_units + u
        ps = par * nseq + s
        lo = h * GDN_HEAD_K
        qh = qc[s, :, lo:lo + GDN_HEAD_K]
        kh = kc[s, :, lo:lo + GDN_HEAD_K]
        vh = vc[s, :, lo:lo + GDN_HEAD_V]
        qn = qh * lax.rsqrt(jnp.sum(qh * qh, axis=-1, keepdims=True) + L2_EPS) * (GDN_HEAD_K ** -0.5)
        kn = kh * lax.rsqrt(jnp.sum(kh * kh, axis=-1, keepdims=True) + L2_EPS)
        gi = _col(gc[s], LANE_G + h)
        bi = _col(sg[s], LANE_BETA + h)
        gj = gc_row[s][LANE_G + h:LANE_G + h + 1, :]
        decay = jnp.exp(jnp.where(causal, gi - gj, -jnp.inf))
        egi = jnp.exp(gi)
        kb = kn * bi
        kn16 = kn.astype(BF16)
        a_s[u] = jnp.where(strict, lax.dot_general(kb.astype(BF16), kn16, (((1,), (1,)), ((), ())),
                                                   preferred_element_type=F32) * decay, 0.0)
        attn_s[u] = (lax.dot_general(qn.astype(BF16), kn16, (((1,), (1,)), ((), ())),
                                     preferred_element_type=F32) * decay).astype(BF16)
        qd_s[ps, :, lo:lo + GDN_HEAD_K] = (qn * egi).astype(BF16)
        kd_s[ps, :, lo:lo + GDN_HEAD_K] = (kn * jnp.exp(gi[cs - 1:cs, :] - gi)).astype(BF16)
        rhs_s[ps, :, 2 * lo:2 * lo + GDN_HEAD_V] = (vh * bi).astype(BF16)
        rhs_s[ps, :, 2 * lo + GDN_HEAD_V:2 * lo + 2 * GDN_HEAD_V] = (kb * egi).astype(BF16)

    def stages_bcd(par):
        base = _same_block(cs, INV_BASE)
        eye = jnp.where(causal & jnp.logical_not(strict), 1.0, 0.0)
        for _, _, u in heads:
            x = jnp.where(base, -a_s[par * n_units + u], 0.0)
            x_s[u] = x.astype(BF16)
            d_s[u] = eye + x
        span = 2
        while span < INV_BASE:
            for _, _, u in heads:
                x16 = x_s[u]
                x2 = _bdot(x16, x16).astype(BF16)
                x_s[u] = x2
                d = d_s[u]
                d_s[u] = d + _bdot(d.astype(BF16), x2)
            span *= 2
        size = INV_BASE
        while size < cs:
            off = _same_block(cs, 2 * size) & jnp.logical_not(_same_block(cs, size))
            for _, _, u in heads:
                d = d_s[u]
                d16 = d.astype(BF16)
                low = jnp.where(off, a_s[par * n_units + u], 0.0).astype(BF16)
                d_s[u] = d - _bdot(_bdot(d16, low).astype(BF16), d16)
            size *= 2

        for s, h, u in heads:
            lo = h * GDN_HEAD_K
            uw = _bdot(d_s[u].astype(BF16), rhs_s[par * nseq + s, :, 2 * lo:2 * lo + 2 * GDN_HEAD_V])
            u_s[s, :, lo:lo + GDN_HEAD_V] = uw[:, :GDN_HEAD_V]
            w_s[s, :, lo:lo + GDN_HEAD_V] = uw[:, GDN_HEAD_V:].astype(BF16)

        for s, h, u in heads:
            lo = h * GDN_HEAD_K
            ps = par * nseq + s
            st = sst[s, lo:lo + GDN_HEAD_K, :]
            s16 = st.astype(BF16)
            v_new = u_s[s, :, lo:lo + GDN_HEAD_V] - _bdot(w_s[s, :, lo:lo + GDN_HEAD_V], s16)
            v16 = v_new.astype(BF16)
            o_h = _bdot(qd_s[ps, :, lo:lo + GDN_HEAD_K], s16) + _bdot(attn_s[par * n_units + u], v16)
            g_last = _col(glast_s[ps], LANE_G + h)
            sst[s, lo:lo + GDN_HEAD_K, :] = st * jnp.exp(g_last) + lax.dot_general(
                kd_s[ps, :, lo:lo + GDN_HEAD_K], v16, (((0,), (0,)), ((), ())), preferred_element_type=F32)
            ms = jnp.mean(o_h * o_h, axis=-1, keepdims=True)
            o_ref[s, :, lo:lo + GDN_HEAD_V] = ((o_h * lax.rsqrt(ms + RMS_EPS) * nw_ref[...]) *
                                               _silu(z_ref[s, :, lo:lo + GDN_HEAD_V])).astype(BF16)

    def ssd_chunks():
        for s in range(nseq):
            _ssd_chunk(zs_ref.at[s], xs_ref.at[s], b_ref.at[s], c_ref.at[s], sm_ref.at[s], bias_ref, alog_ref,
                       dexp_ref, snw_ref, y_ref.at[s], hst.at[s], y_s.at[s])

    @pl.when(c_idx == 0)
    def _():
        sst[...] = jnp.zeros(sst.shape, F32)
        hst[...] = jnp.zeros(hst.shape, F32)
        stage_a(0)
        ssd_chunks()

    middle = (c_idx > 0) & (c_idx < n_chunks)
    for par in range(2):
        @pl.when(middle & (c_idx % 2 == par))
        def _(par=par):
            stages_bcd(1 - par)
            stage_a(par)
            ssd_chunks()

    @pl.when(c_idx == n_chunks)
    def _():
        stages_bcd((n_chunks - 1) % 2)
        st_ref[...] = sst[...]
        hst_ref[...] = hst[...]


GDN_SEQS_PER_STEP = 2


def _mixer_prefill(proj, bsz, seq, lw):
    cs = MIXER_CHUNK
    nseq = GDN_SEQS_PER_STEP if bsz % GDN_SEQS_PER_STEP == 0 else 1
    assert seq % cs == 0
    nc = seq // cs
    proj3 = proj.reshape(bsz, seq, PROJ_COLS)
    cur = lambda c: jnp.minimum(c, nc - 1)
    prev = lambda c: jnp.maximum(c - 1, 0)
    pcol = lambda width, off, ch: pl.BlockSpec((nseq, cs, width), lambda b, c: (b, ch(c), off // width))
    full = lambda a: pl.BlockSpec(a.shape, lambda b, c: (0,) * a.ndim)
    weights = [lw['gate_bias'], lw['gate_alog'], lw['gdn_norm_w'], lw['ssd_d_exp'], lw['ssd_norm_w']]
    units = nseq * GDN_HEADS
    o, st, y, hst = pl.pallas_call(
        functools.partial(_mixer_prefill_kernel, n_chunks=nc),
        grid=(bsz // nseq, nc + 1),
        in_specs=[pcol(GDN_DIM, COL_Q, cur), pcol(GDN_DIM, COL_K, cur), pcol(GDN_DIM, COL_V, cur),
                  pcol(GDN_DIM, COL_Z_GDN, prev), pcol(LANES, COL_SMALL, cur),
                  pcol(SSD_INNER, COL_Z_SSD, cur), pcol(SSD_INNER, COL_XS, cur), pcol(SSD_BC, COL_B, cur),
                  pcol(SSD_BC, COL_C, cur)] + [full(w) for w in weights],
        out_specs=[pl.BlockSpec((nseq, cs, GDN_DIM), lambda b, c: (b, prev(c), 0)),
                   pl.BlockSpec((nseq, GDN_HEADS * GDN_HEAD_K, GDN_HEAD_V), lambda b, c: (b, 0, 0)),
                   pl.BlockSpec((nseq, cs, SSD_INNER), lambda b, c: (b, cur(c), 0)),
                   pl.BlockSpec((nseq, SSD_INNER, SSD_STATE), lambda b, c: (b, 0, 0))],
        out_shape=[jax.ShapeDtypeStruct((bsz, seq, GDN_DIM), BF16),
                   jax.ShapeDtypeStruct((bsz, GDN_HEADS * GDN_HEAD_K, GDN_HEAD_V), F32),
                   jax.ShapeDtypeStruct((bsz, seq, SSD_INNER), BF16),
                   jax.ShapeDtypeStruct((bsz, SSD_INNER, SSD_STATE), F32)],
        scratch_shapes=[pltpu.VMEM((nseq, GDN_HEADS * GDN_HEAD_K, GDN_HEAD_V), F32),
                        pltpu.VMEM((nseq, SSD_INNER, SSD_STATE), F32),
                        pltpu.VMEM((nseq, cs, SSD_INNER), F32),
                        pltpu.VMEM((2 * nseq, cs, GDN_DIM), BF16),
                        pltpu.VMEM((2 * nseq, cs, GDN_DIM), BF16),
                        pltpu.VMEM((2 * nseq, cs, 2 * GDN_DIM), BF16),
                        pltpu.VMEM((2 * units, cs, cs), F32),
                        pltpu.VMEM((2 * units, cs, cs), BF16),
                        pltpu.VMEM((units, cs, cs), BF16),
                        pltpu.VMEM((units, cs, cs), F32),
                        pltpu.VMEM((nseq, cs, GDN_DIM), F32),
                        pltpu.VMEM((nseq, cs, GDN_DIM), BF16),
                        pltpu.VMEM((2 * nseq, 1, LANES), F32)],
        compiler_params=_params("parallel", "arbitrary"),
    )(*([proj3] * 9), *weights)
    return y.reshape(bsz * seq, SSD_INNER), hst, o.reshape(bsz * seq, GDN_DIM), st


def _softmax_rows(s):
    m = jnp.max(s, axis=-1, keepdims=True)
    e = jnp.exp(s - m)
    return e / jnp.sum(e, axis=-1, keepdims=True)


def _xattn_prefill_kernel(x_ref, y_ref, og_ref, w1_ref, w2_ref, nw_ref, wq_ref, mk_ref, mv_ref, wo_ref, o_ref):
    x = x_ref[...] + jnp.dot(y_ref[...], w1_ref[...], preferred_element_type=F32) + \
        jnp.dot(og_ref[...], w2_ref[...], preferred_element_type=F32)
    ms = jnp.mean(x * x, axis=-1, keepdims=True)
    h = x * lax.rsqrt(ms + RMS_EPS) * nw_ref[...]
    q = jnp.dot(h.astype(BF16), wq_ref[...], preferred_element_type=F32)
    outs = []
    for hd in range(XA_HEADS):
        lo = hd * XA_HEAD_DIM
        s = _mm_nt(q[:, lo:lo + XA_HEAD_DIM], mk_ref[:, lo:lo + XA_HEAD_DIM]) * (XA_HEAD_DIM ** -0.5)
        outs.append(_mm(_softmax_rows(s), mv_ref[:, lo:lo + XA_HEAD_DIM]))
    att = jnp.concatenate(outs, axis=-1)
    o_ref[...] = x + jnp.dot(att.astype(BF16), wo_ref[...], preferred_element_type=F32)


def _xattn_prefill(x, y, og, w1, w2, nw, wq, mk, mv, wo, bsz, seq, tq):
    n_mem = mk.shape[1]
    nt = seq // tq
    rows = lambda a: pl.BlockSpec((tq, a.shape[1]), lambda b, t: (b * nt + t, 0))
    return pl.pallas_call(
        _xattn_prefill_kernel,
        grid=(bsz, nt),
        in_specs=[rows(x), rows(y), rows(og),
                  pl.BlockSpec(w1.shape, lambda b, t: (0, 0)),
                  pl.BlockSpec(w2.shape, lambda b, t: (0, 0)),
                  pl.BlockSpec((1, D_MODEL), lambda b, t: (0, 0)),
                  pl.BlockSpec((D_MODEL, D_MODEL), lambda b, t: (0, 0)),
                  pl.BlockSpec((None, n_mem, D_MODEL), lambda b, t: (b, 0, 0)),
                  pl.BlockSpec((None, n_mem, D_MODEL), lambda b, t: (b, 0, 0)),
                  pl.BlockSpec((D_MODEL, D_MODEL), lambda b, t: (0, 0))],
        out_specs=pl.BlockSpec((tq, D_MODEL), lambda b, t: (b * nt + t, 0)),
        out_shape=jax.ShapeDtypeStruct(x.shape, F32),
        compiler_params=_params("parallel", "arbitrary"),
    )(x, y, og, w1, w2, nw.reshape(1, D_MODEL), wq, mk, mv, wo)


def _xattn_decode_kernel(q_ref, mk_ref, mv_ref, o_ref):
    tb = q_ref.shape[0]
    for i in range(tb):
        q = q_ref[i]
        qh = jnp.concatenate([q[:, hd * XA_HEAD_DIM:(hd + 1) * XA_HEAD_DIM] for hd in range(XA_HEADS)], axis=0)
        s = jnp.sum(mk_ref[i] * qh[None], axis=-1, keepdims=True) * (XA_HEAD_DIM ** -0.5)
        e = jnp.exp(s - jnp.max(s, axis=0, keepdims=True))
        p = e / jnp.sum(e, axis=0, keepdims=True)
        o_ref[i] = jnp.sum(p * mv_ref[i], axis=0)


def _xattn_decode(q, mk_all, mv_all, layer, tb):
    _, nb, n_mem, nh, hd = mk_all.shape
    return pl.pallas_call(
        _xattn_decode_kernel,
        grid=(nb // tb,),
        in_specs=[pl.BlockSpec((tb, 1, D_MODEL), lambda i: (i, 0, 0)),
                  pl.BlockSpec((None, tb, n_mem, nh, hd), lambda i: (layer, i, 0, 0, 0)),
                  pl.BlockSpec((None, tb, n_mem, nh, hd), lambda i: (layer, i, 0, 0, 0))],
        out_specs=pl.BlockSpec((tb, nh, hd), lambda i: (i, 0, 0)),
        out_shape=jax.ShapeDtypeStruct((nb, nh, hd), F32),
        compiler_params=_params("parallel"),
    )(q.reshape(nb, 1, D_MODEL), mk_all, mv_all).reshape(nb, D_MODEL)


def _ffn_prefill_kernel(x_ref, nw_ref, wgu_ref, cw_ref, cb_ref, wd_ref, o_ref, tail_ref,
                        ext_ref, hist_ref, hm_ref, *, tiles_per_seq):
    i = pl.program_id(0)
    tm = x_ref.shape[0]
    d_ff = wd_ref.shape[0]

    @pl.when(i == 0)
    def _():
        hist_ref[...] = jnp.zeros(hist_ref.shape, F32)

    x = x_ref[...]
    ms = jnp.mean(x * x, axis=-1, keepdims=True)
    xn = (x * lax.rsqrt(ms + RMS_EPS) * nw_ref[...]).astype(BF16)
    starts_seq = i % tiles_per_seq == 0
    for ci, lo in enumerate(range(0, d_ff, MXU_COLS)):
        cols = slice(lo, min(lo + MXU_COLS, d_ff))
        width = cols.stop - cols.start
        gate = jnp.dot(xn, wgu_ref[:, cols], preferred_element_type=F32)
        up = jnp.dot(xn, wgu_ref[:, d_ff + cols.start:d_ff + cols.stop], preferred_element_type=F32)
        tail_ref[:, cols] = gate[tm - HIST_ROWS:tm, :]
        buf = ext_ref.at[ci % CONV_BUFS]
        buf[0:HIST_ROWS, 0:width] = jnp.where(starts_seq, 0.0, hist_ref[:, cols])
        buf[HIST_ROWS:HIST_ROWS + tm, 0:width] = gate
        hist_ref[:, cols] = gate[tm - HIST_ROWS:tm, :]
        c = _causal_taps(buf[:, 0:width], cw_ref, cb_ref[:, cols], cols, FFN_CONV, tm)
        hm_ref[:, cols] = (_silu(c) * up).astype(BF16)
    o_ref[...] = x + jnp.dot(hm_ref[...], wd_ref[...], preferred_element_type=F32)


def _ffn_prefill(x, nw, wgu, cw, cb, wd, seq, tm):
    m = x.shape[0]
    d_ff = wd.shape[0]
    assert m % tm == 0 and seq % tm == 0 and tm >= HIST_ROWS
    whole = lambda a: pl.BlockSpec(a.shape, lambda i: (0, 0), pipeline_mode=pl.Buffered(1))
    ins = [nw.reshape(1, D_MODEL), wgu, cw, cb.reshape(1, d_ff), wd]
    return pl.pallas_call(
        functools.partial(_ffn_prefill_kernel, tiles_per_seq=seq // tm),
        grid=(m // tm,),
        in_specs=[pl.BlockSpec((tm, D_MODEL), lambda i: (i, 0))] + [whole(a) for a in ins],
        out_specs=[pl.BlockSpec((tm, D_MODEL), lambda i: (i, 0)),
                   pl.BlockSpec((None, HIST_ROWS, d_ff), lambda i: (i, 0, 0))],
        out_shape=[jax.ShapeDtypeStruct((m, D_MODEL), F32),
                   jax.ShapeDtypeStruct((m // tm, HIST_ROWS, d_ff), F32)],
        scratch_shapes=[pltpu.VMEM((CONV_BUFS, HIST_ROWS + tm, MXU_COLS), F32),
                        pltpu.VMEM((HIST_ROWS, d_ff), F32),
                        pltpu.VMEM((tm, d_ff), BF16)],
        compiler_params=_params("arbitrary"),
    )(x, *ins)


def _ffn_decode_kernel(x_ref, g_ref, h0_ref, h1_ref, u_ref, cw_ref, cb_ref, wd_ref, o_ref):
    acc = cb_ref[...] + h0_ref[...] * cw_ref[0:1, :] + h1_ref[...] * cw_ref[1:2, :] + g_ref[...] * cw_ref[2:3, :]
    hmid = _silu(acc) * u_ref[...]
    o_ref[...] = x_ref[...] + jnp.dot(hmid.astype(BF16), wd_ref[...], preferred_element_type=F32)


def _ffn_decode(x, gu, hist0, hist1, cw, cb, wd):
    m = x.shape[0]
    d_ff = wd.shape[0]
    rows = lambda width, col: pl.BlockSpec((m, width), lambda i: (0, col))
    return pl.pallas_call(
        _ffn_decode_kernel,
        grid=(1,),
        in_specs=[rows(D_MODEL, 0), rows(d_ff, 0), rows(d_ff, 0), rows(d_ff, 0), rows(d_ff, 1),
                  pl.BlockSpec((FFN_CONV, d_ff), lambda i: (0, 0)),
                  pl.BlockSpec((1, d_ff), lambda i: (0, 0)),
                  pl.BlockSpec((d_ff, D_MODEL), lambda i: (0, 0))],
        out_specs=rows(D_MODEL, 0),
        out_shape=jax.ShapeDtypeStruct((m, D_MODEL), F32),
        compiler_params=_params("arbitrary"),
    )(x, gu, hist0, hist1, gu, cw, cb.reshape(1, d_ff), wd)


def _expand_lanes(rows, first_lane, width, n_out, parts):
    k = lax.broadcasted_iota(jnp.int32, (LANES, n_out), 0)
    l = lax.broadcasted_iota(jnp.int32, (LANES, n_out), 1)
    sel = jnp.where(l // width == k - first_lane, 1.0, 0.0)
    return _mm_lhs_split(rows, sel, parts)


def _decode_rows_kernel(proj_ref, sh0, sh1, sh2, gh0, gh1, gh2,
                        scw_ref, scb_ref, gcw_ref, bias_ref, alog_ref,
                        xs_ref, xdt_ref, b_ref, c_ref, sdec_ref,
                        q_ref, k_ref, v_ref, beta_ref, gdec_ref):
    def conv(hists, new, cw, cb):
        acc = new * cw[3:4, :]
        for t, hr in enumerate(hists):
            acc = acc + hr[...] * cw[t:t + 1, :]
        return acc if cb is None else acc + cb

    xbc_new = jnp.concatenate([proj_ref[:, COL_XS:COL_XS + SSD_INNER],
                               proj_ref[:, COL_B:COL_B + SSD_BC],
                               proj_ref[:, COL_C:COL_C + SSD_BC]], axis=-1)
    xbc = _silu(conv((sh0, sh1, sh2), xbc_new, scw_ref, scb_ref[...]))
    sp, sg, log_dec = _gate_rows(proj_ref[:, COL_SMALL:COL_SMALL + LANES], bias_ref[...], alog_ref[...])
    xs = xbc[:, 0:SSD_INNER]
    xs_ref[...] = xs
    xdt_ref[...] = xs * _expand_lanes(sp, LANE_DT, SSD_HEAD_DIM, SSD_INNER, 3)
    b_ref[...] = xbc[:, SSD_INNER:SSD_INNER + SSD_BC]
    c_ref[...] = xbc[:, SSD_INNER + SSD_BC:SSD_INNER + 2 * SSD_BC]
    sdec_ref[...] = jnp.exp(_expand_lanes(log_dec, LANE_DT, SSD_HEAD_DIM, SSD_INNER, 3))

    qkv_new = proj_ref[:, COL_Q:COL_Q + 3 * GDN_DIM]
    qkv = _silu(conv((gh0, gh1, gh2), qkv_new, gcw_ref, None))
    for h in range(GDN_HEADS):
        lo = h * GDN_HEAD_K
        qh = qkv[:, lo:lo + GDN_HEAD_K]
        kh = qkv[:, GDN_DIM + lo:GDN_DIM + lo + GDN_HEAD_K]
        q_ref[:, lo:lo + GDN_HEAD_K] = qh * lax.rsqrt(jnp.sum(qh * qh, axis=-1, keepdims=True) + L2_EPS) * \
            (GDN_HEAD_K ** -0.5)
        k_ref[:, lo:lo + GDN_HEAD_K] = kh * lax.rsqrt(jnp.sum(kh * kh, axis=-1, keepdims=True) + L2_EPS)
    v_ref[...] = qkv[:, 2 * GDN_DIM:3 * GDN_DIM]
    beta_ref[...] = _expand_lanes(sg, LANE_BETA, GDN_HEAD_V, GDN_DIM, 3)
    gdec_ref[...] = jnp.exp(_expand_lanes(log_dec, LANE_G, GDN_HEAD_V, GDN_DIM, 3))


def _decode_rows(proj, ssd_hist, gdn_hist, lw):
    m = proj.shape[0]
    ins = [proj] + [ssd_hist[:, t] for t in range(SSD_CONV - 1)] + [gdn_hist[:, t] for t in range(GDN_CONV - 1)]
    ins += [lw['ssd_conv_w'], lw['ssd_conv_b'], lw['gdn_conv_w'], lw['gate_bias'], lw['gate_alog']]
    widths = [SSD_INNER, SSD_INNER, SSD_BC, SSD_BC, SSD_INNER, GDN_DIM, GDN_DIM, GDN_DIM, GDN_DIM, GDN_DIM]
    return pl.pallas_call(
        _decode_rows_kernel,
        grid=(1,),
        in_specs=[pl.BlockSpec(a.shape, lambda i, nd=a.ndim: (0,) * nd) for a in ins],
        out_specs=[pl.BlockSpec((m, w), lambda i: (0, 0)) for w in widths],
        out_shape=[jax.ShapeDtypeStruct((m, w), F32) for w in widths],
        compiler_params=_params("arbitrary"),
    )(*ins)


def _rows_to_cols(rows, i, parts):
    tb = rows.shape[0]
    r = lax.broadcasted_iota(jnp.int32, (tb, LANES), 0)
    sel = jnp.where(r == i, 1.0, 0.0).astype(BF16)
    acc = None
    for p in _split(rows, parts):
        t = lax.dot_general(p, sel, (((0,), (0,)), ((), ())), preferred_element_type=F32)
        acc = t if acc is None else acc + t
    return acc


def _only_row(rows, i):
    r = lax.broadcasted_iota(jnp.int32, rows.shape, 0)
    return jnp.where(r == i, rows, 0.0)


def _decode_state_kernel(xdt_ref, b_ref, c_ref, sdec_ref, q_ref, k_ref, v_ref, beta_ref, gdec_ref,
                         hs_ref, ss_ref, *rest):
    y_ref, o_ref, hs_out, ss_out = rest[-4:]
    tb = xdt_ref.shape[0]
    xdt = xdt_ref[...]
    bm = b_ref[...]
    cm = c_ref[...]
    sdec = sdec_ref[...]
    qn = q_ref[...]
    kn = k_ref[...]
    vv = v_ref[...]
    beta = beta_ref[...]
    gdec = gdec_ref[...]
    rows_g = SSD_INNER // SSD_GROUPS

    y_acc = jnp.zeros((tb, SSD_INNER), F32)
    ks_acc = jnp.zeros((tb, GDN_DIM), F32)
    for i in range(tb):
        dec_c = _rows_to_cols(sdec, i, 2)
        xdt_c = _rows_to_cols(xdt, i, 1)
        pieces = []
        for g in range(SSD_GROUPS):
            rs = slice(g * rows_g, (g + 1) * rows_g)
            bg = bm[i:i + 1, g * SSD_STATE:(g + 1) * SSD_STATE]
            h_new = hs_ref[i, rs, :] * dec_c[rs, :] + xdt_c[rs, :] * bg
            hs_out[i, rs, :] = h_new
            cg = _only_row(cm[:, g * SSD_STATE:(g + 1) * SSD_STATE], i)
            pieces.append(_mm_nt(cg, h_new))
        y_acc = y_acc + jnp.concatenate(pieces, axis=-1)
        g_c = _rows_to_cols(gdec, i, 2)
        s_dec = ss_ref[i] * g_c
        ss_out[i] = s_dec
        pieces = []
        for h in range(GDN_HEADS):
            rs = slice(h * GDN_HEAD_K, (h + 1) * GDN_HEAD_K)
            pieces.append(_mm(_only_row(kn[:, rs], i), s_dec[rs, :]))
        ks_acc = ks_acc + jnp.concatenate(pieces, axis=-1)
    y_ref[...] = y_acc

    delta = beta * (vv - ks_acc)
    o_acc = jnp.zeros((tb, GDN_DIM), F32)
    for i in range(tb):
        k_c = _rows_to_cols(kn, i, 1)
        pieces = []
        for h in range(GDN_HEADS):
            rs = slice(h * GDN_HEAD_K, (h + 1) * GDN_HEAD_K)
            d_row = delta[i:i + 1, rs]
            s_new = ss_out[i, rs, :] + k_c[rs, :] * d_row
            ss_out[i, rs, :] = s_new
            pieces.append(_mm(_only_row(qn[:, rs], i), s_new))
        o_acc = o_acc + jnp.concatenate(pieces, axis=-1)
    o_ref[...] = o_acc


def _decode_state(rows, hs_all, ss_all, layer, tb, carried):
    xs, xdt, bm, cm, sdec, qn, kn, vv, beta, gdec = rows
    nb = xdt.shape[0]
    row_ins = [xdt, bm, cm, sdec, qn, kn, vv, beta, gdec]
    rspec = lambda a: pl.BlockSpec((tb, a.shape[1]), lambda i: (i, 0))
    sspec = lambda a: pl.BlockSpec((None, tb) + a.shape[2:], lambda i: (layer, i, 0, 0))
    n_in = len(row_ins) + 2
    aliases = {n_in + j: 2 + j for j in range(len(carried))}
    return pl.pallas_call(
        _decode_state_kernel,
        grid=(nb // tb,),
        in_specs=[rspec(a) for a in row_ins] + [sspec(hs_all), sspec(ss_all)] +
                 [pl.BlockSpec(memory_space=pl.ANY) for _ in carried],
        out_specs=[rspec(xdt), rspec(qn), sspec(hs_all), sspec(ss_all)],
        out_shape=[jax.ShapeDtypeStruct(xdt.shape, F32), jax.ShapeDtypeStruct(qn.shape, F32),
                   jax.ShapeDtypeStruct(hs_all.shape, F32), jax.ShapeDtypeStruct(ss_all.shape, F32)],
        input_output_aliases=aliases,
        compiler_params=_params("parallel"),
    )(*row_ins, hs_all, ss_all, *carried)


def _decode_out_kernel(x_ref, y_ref, xs_ref, o_ref, proj_ref, dexp_ref, snw_ref, gnw_ref, w1_ref, w2_ref, out_ref):
    y = y_ref[...] + dexp_ref[...] * xs_ref[...]
    gs = SSD_INNER // SSD_GROUPS
    ys = []
    for g in range(SSD_GROUPS):
        yg = y[:, g * gs:(g + 1) * gs] * _silu(proj_ref[:, COL_Z_SSD + g * gs:COL_Z_SSD + (g + 1) * gs])
        ms = jnp.mean(yg * yg, axis=-1, keepdims=True)
        ys.append(yg * lax.rsqrt(ms + RMS_EPS) * snw_ref[:, g * gs:(g + 1) * gs])
    os = []
    for h in range(GDN_HEADS):
        lo = h * GDN_HEAD_V
        oh = o_ref[:, lo:lo + GDN_HEAD_V]
        ms = jnp.mean(oh * oh, axis=-1, keepdims=True)
        os.append(oh * lax.rsqrt(ms + RMS_EPS) * gnw_ref[...] *
                  _silu(proj_ref[:, COL_Z_GDN + lo:COL_Z_GDN + lo + GDN_HEAD_V]))
    yn = jnp.concatenate(ys, axis=-1)
    on = jnp.concatenate(os, axis=-1)
    out_ref[...] = x_ref[...] + _mm(yn, w1_ref[...]) + _mm(on, w2_ref[...])


def _decode_out(x, y, xs, o, proj, lw):
    ins = [x, y, xs, o, proj, lw['ssd_d_exp'], lw['ssd_norm_w'], lw['gdn_norm_w'], lw['w_out_ssd'], lw['w_out_gdn']]
    return pl.pallas_call(
        _decode_out_kernel,
        grid=(1,),
        in_specs=[pl.BlockSpec(a.shape, lambda i, nd=a.ndim: (0,) * nd) for a in ins],
        out_specs=pl.BlockSpec(x.shape, lambda i: (0, 0)),
        out_shape=jax.ShapeDtypeStruct(x.shape, F32),
        compiler_params=_params("arbitrary"),
    )(*ins)


def _identity_taps(taps, n):
    return jnp.concatenate([jnp.zeros((taps - 1, n), F32), jnp.ones((1, n), F32)], axis=0)


def _prep_layer(i, p):
    w_in = p['w_in'][i]
    o_zs = 0
    o_xbc = o_zs + SSD_INNER
    o_dt = o_xbc + SSD_INNER + 2 * SSD_BC
    o_qkv = o_dt + SSD_HEADS
    o_zg = o_qkv + 3 * GDN_DIM
    o_b = o_zg + GDN_DIM
    o_a = o_b + GDN_HEADS
    used = SSD_HEADS + 2 * GDN_HEADS
    w_perm = jnp.concatenate([
        w_in[:, o_zs:o_zs + SSD_INNER],
        w_in[:, o_xbc:o_xbc + SSD_INNER],
        w_in[:, o_qkv:o_qkv + 3 * GDN_DIM],
        w_in[:, o_zg:o_zg + GDN_DIM],
        w_in[:, o_xbc + SSD_INNER:o_xbc + SSD_INNER + 2 * SSD_BC],
        w_in[:, o_dt:o_dt + SSD_HEADS],
        w_in[:, o_b:o_b + GDN_HEADS],
        w_in[:, o_a:o_a + GDN_HEADS],
        jnp.zeros((D_MODEL, PROJ_COLS - COL_SMALL - used), F32)], axis=1).astype(BF16)
    lane_pad = jnp.zeros((LANES - used,), F32)
    scw = p['ssd_conv_w'][i]
    scb = p['ssd_conv_b'][i].reshape(1, -1)
    gcw = p['gdn_conv_w'][i]
    return {
        'w_in': w_perm,
        'norm_mix_w': p['norm_mix_w'][i],
        'ssd_conv_w': scw, 'ssd_conv_b': scb, 'gdn_conv_w': gcw,
        'in_cw': jnp.concatenate([_identity_taps(SSD_CONV, SSD_INNER), scw[:, :SSD_INNER], gcw,
                                  _identity_taps(GDN_CONV, GDN_DIM), scw[:, SSD_INNER:],
                                  _identity_taps(SSD_CONV, PROJ_COLS - COL_SMALL)], axis=1),
        'in_cb': jnp.concatenate([jnp.zeros((1, COL_XS), F32), scb[:, :SSD_INNER],
                                  jnp.zeros((1, COL_B - COL_Q), F32), scb[:, SSD_INNER:],
                                  jnp.zeros((1, PROJ_COLS - COL_SMALL), F32)], axis=1),
        'gate_bias': jnp.concatenate([p['ssd_dt_bias'][i], jnp.zeros((GDN_HEADS,), F32), p['gdn_dt_bias'][i],
                                      lane_pad]).reshape(1, LANES),
        'gate_alog': jnp.concatenate([p['ssd_a_log'][i], jnp.zeros((GDN_HEADS,), F32), p['gdn_a_log'][i],
                                      lane_pad]).reshape(1, LANES),
        'ssd_d_exp': jnp.repeat(p['ssd_d'][i], SSD_HEAD_DIM).reshape(1, SSD_INNER),
        'ssd_norm_w': p['ssd_norm_w'][i].reshape(1, SSD_INNER),
        'gdn_norm_w': p['gdn_norm_w'][i].reshape(1, GDN_HEAD_V),
        'w_out_ssd': p['w_out'][i, :SSD_INNER].astype(BF16),
        'w_out_gdn': p['w_out'][i, SSD_INNER:].astype(BF16),
        'norm_xa_w': p['norm_xa_w'][i], 'norm_mem_w': p['norm_mem_w'][i],
        'xa_wq': p['xa_wq'][i].astype(BF16), 'xa_wo': p['xa_wo'][i].astype(BF16),
        'xa_wkv': jnp.concatenate([p['xa_wk'][i], p['xa_wv'][i]], axis=1).astype(BF16),
        'norm_ffn_w': p['norm_ffn_w'][i],
        'ffn_w_gu': jnp.concatenate([p['ffn_w_gate'][i], p['ffn_w_up'][i]], axis=1).astype(BF16),
        'ffn_conv_w': p['ffn_conv_w'][i], 'ffn_conv_b': p['ffn_conv_b'][i],
        'ffn_w_down': p['ffn_w_down'][i].astype(BF16),
    }


def _unpermute_conv_rows(proj_rows):
    ssd = jnp.concatenate([proj_rows[..., COL_XS:COL_XS + SSD_INNER],
                           proj_rows[..., COL_B:COL_B + SSD_BC],
                           proj_rows[..., COL_C:COL_C + SSD_BC]], axis=-1)
    gdn = proj_rows[..., COL_Q:COL_Q + 3 * GDN_DIM]
    return ssd, gdn


def _tile(m, pref):
    t = min(m, pref)
    while m % t:
        t //= 2
    return t


def kernel(x_prompt, x_sample, mem_prompt, cache_mem_k, cache_mem_v, state_ssd_conv, state_ssd, state_gdn_conv, state_gdn, state_ffn_conv, norm_mix_w, w_in, ssd_conv_w, ssd_conv_b, ssd_dt_bias, ssd_a_log, ssd_d, ssd_norm_w, gdn_conv_w, gdn_dt_bias, gdn_a_log, gdn_norm_w, w_out, norm_xa_w, norm_mem_w, xa_wq, xa_wk, xa_wv, xa_wo, norm_ffn_w, ffn_w_gate, ffn_w_up, ffn_conv_w, ffn_conv_b, ffn_w_down, final_norm_w):
    params = dict(norm_mix_w=norm_mix_w, w_in=w_in, ssd_conv_w=ssd_conv_w, ssd_conv_b=ssd_conv_b,
                  ssd_dt_bias=ssd_dt_bias, ssd_a_log=ssd_a_log, ssd_d=ssd_d, ssd_norm_w=ssd_norm_w,
                  gdn_conv_w=gdn_conv_w, gdn_dt_bias=gdn_dt_bias, gdn_a_log=gdn_a_log, gdn_norm_w=gdn_norm_w,
                  w_out=w_out, norm_xa_w=norm_xa_w, norm_mem_w=norm_mem_w, xa_wq=xa_wq, xa_wk=xa_wk, xa_wv=xa_wv,
                  xa_wo=xa_wo, norm_ffn_w=norm_ffn_w, ffn_w_gate=ffn_w_gate, ffn_w_up=ffn_w_up,
                  ffn_conv_w=ffn_conv_w, ffn_conv_b=ffn_conv_b, ffn_w_down=ffn_w_down)
    depth = w_in.shape[0]
    bsz, seq, _ = x_prompt.shape
    nb = x_sample.shape[0]
    n_mem = mem_prompt.shape[1]
    d_ff = ffn_w_down.shape[1]
    mp = bsz * seq

    xp = x_prompt.reshape(mp, D_MODEL)
    xs = x_sample.reshape(nb, D_MODEL)
    mem = mem_prompt.reshape(bsz * n_mem, D_MODEL)
    tm_p = _tile(seq, 1024)
    tm_c = _tile(seq, 256)
    tm_mem = _tile(bsz * n_mem, 1024)
    dec_tile = _tile(nb, DEC_TILE)

    hs_all = state_ssd.reshape(depth, nb, SSD_INNER, SSD_STATE)
    ss_all = state_gdn.reshape(depth, nb, GDN_HEADS * GDN_HEAD_K, GDN_HEAD_V)
    mk_all, mv_all = cache_mem_k, cache_mem_v
    new_states = ()

    mkp, mvp = [], []
    p_sc, p_sh, p_gc, p_gs, p_fc = [], [], [], [], []
    s_sc, s_gc, s_fc = [], [], []
    for i in range(depth):
        lw = _prep_layer(i, params)

        kv = _norm_mm(mem, lw['norm_mem_w'], lw['xa_wkv'], tm_mem)
        mk = kv[:, :D_MODEL].reshape(bsz, n_mem, D_MODEL)
        mv = kv[:, D_MODEL:].reshape(bsz, n_mem, D_MODEL)
        mkp.append(mk.reshape(bsz, n_mem, XA_HEADS, XA_HEAD_DIM))
        mvp.append(mv.reshape(bsz, n_mem, XA_HEADS, XA_HEAD_DIM))

        proj, tails = _norm_mm_conv(xp, lw['norm_mix_w'], lw['w_in'], lw['in_cw'], lw['in_cb'],
                                    [(COL_XS, COL_Q, True), (COL_Q, COL_Z_GDN, False), (COL_B, COL_SMALL, True)],
                                    seq, tm_c)
        y, h_new, o, s_new = _mixer_prefill(proj, bsz, seq, lw)
        seq_tails = tails.reshape(bsz, seq // tm_c, HIST_ROWS, PROJ_COLS)[:, -1]
        ssd_tail, gdn_tail = _unpermute_conv_rows(seq_tails[:, HIST_ROWS - (SSD_CONV - 1):])
        p_sc.append(ssd_tail)
        p_gc.append(gdn_tail)
        p_sh.append(h_new.reshape(bsz, SSD_HEADS, SSD_HEAD_DIM, SSD_STATE))
        p_gs.append(s_new.reshape(bsz, GDN_HEADS, GDN_HEAD_K, GDN_HEAD_V))
        xp = _xattn_prefill(xp, y, o, lw['w_out_ssd'], lw['w_out_gdn'], lw['norm_xa_w'], lw['xa_wq'], mk, mv,
                            lw['xa_wo'], bsz, seq, _tile(seq, 512))
        xp, gate_tails = _ffn_prefill(xp, lw['norm_ffn_w'], lw['ffn_w_gu'], lw['ffn_conv_w'], lw['ffn_conv_b'],
                                      lw['ffn_w_down'], seq, tm_c)
        p_fc.append(gate_tails.reshape(bsz, seq // tm_c, HIST_ROWS, d_ff)[:, -1, HIST_ROWS - (FFN_CONV - 1):])

        proj_s = _norm_mm(xs, lw['norm_mix_w'], lw['w_in'], nb)
        ssd_new, gdn_new = _unpermute_conv_rows(proj_s)
        s_sc.append(jnp.concatenate([state_ssd_conv[i][:, 1:], ssd_new[:, None]], axis=1))
        s_gc.append(jnp.concatenate([state_gdn_conv[i][:, 1:], gdn_new[:, None]], axis=1))
        rows = _decode_rows(proj_s, state_ssd_conv[i], state_gdn_conv[i], lw)
        y_s, o_s, hs_new, ss_new = _decode_state(rows, hs_all, ss_all, i, dec_tile, new_states)
        new_states = (hs_new, ss_new)
        xs = _decode_out(xs, y_s, rows[0], o_s, proj_s, lw)
        q_s = _norm_mm(xs, lw['norm_xa_w'], lw['xa_wq'], nb)
        att = _xattn_decode(q_s, mk_all, mv_all, i, _tile(nb, 4))
        xs = _mm_res(xs, [att], [lw['xa_wo']], nb)
        gu_s = _norm_mm(xs, lw['norm_ffn_w'], lw['ffn_w_gu'], nb)
        s_fc.append(jnp.concatenate([state_ffn_conv[i][:, 1:], gu_s[:, None, :d_ff]], axis=1))
        xs = _ffn_decode(xs, gu_s, state_ffn_conv[i][:, 0], state_ffn_conv[i][:, 1],
                         lw['ffn_conv_w'], lw['ffn_conv_b'], lw['ffn_w_down'])

    y_prompt = _rmsnorm(xp, final_norm_w, tm_p).reshape(bsz, seq, D_MODEL)
    y_sample = _rmsnorm(xs, final_norm_w, nb).reshape(nb, 1, D_MODEL)
    return (y_prompt, y_sample, jnp.stack(mkp), jnp.stack(mvp),
            jnp.stack(p_sc), jnp.stack(p_sh), jnp.stack(p_gc), jnp.stack(p_gs), jnp.stack(p_fc),
            jnp.stack(s_sc), new_states[0].reshape(state_ssd.shape), jnp.stack(s_gc),
            new_states[1].reshape(state_gdn.shape), jnp.stack(s_fc))
```

```python
import functools

import jax
import jax.numpy as jnp
from jax import lax
from jax.experimental import pallas as pl
from jax.experimental.pallas import tpu as pltpu

F32 = jnp.float32
BF16 = jnp.bfloat16

D_MODEL = 1024
SSD_HEADS = 16
SSD_HEAD_DIM = 64
SSD_STATE = 128
SSD_GROUPS = 2
SSD_INNER = SSD_HEADS * SSD_HEAD_DIM
SSD_BC = SSD_GROUPS * SSD_STATE
SSD_CONV = 4
GDN_HEADS = 8
GDN_HEAD_K = 128
GDN_HEAD_V = 128
GDN_DIM = GDN_HEADS * GDN_HEAD_V
GDN_CONV = 4
XA_HEADS = 4
XA_HEAD_DIM = D_MODEL // XA_HEADS
FFN_CONV = 3
RMS_EPS = 1e-6
L2_EPS = 1e-6

SUBLANES = 8
LANES = 128
MXU_COLS = 256
VMEM_LIMIT_BYTES = 56 * 1024 * 1024

COL_Z_SSD = 0
COL_XS = 1024
COL_Q = 2048
COL_K = 3072
COL_V = 4096
COL_Z_GDN = 5120
COL_B = 6144
COL_C = 6400
COL_SMALL = 6656
PROJ_COLS = 7168
LANE_DT = 0
LANE_BETA = 16
LANE_G = 24

MIXER_CHUNK = 128
DEC_TILE = 8


def _params(*sem):
    return pltpu.CompilerParams(dimension_semantics=sem, vmem_limit_bytes=VMEM_LIMIT_BYTES)


def _mm(a, b):
    return jnp.dot(a.astype(BF16), b.astype(BF16), preferred_element_type=F32)


def _mm_nt(a, b):
    return lax.dot_general(a.astype(BF16), b.astype(BF16), (((1,), (1,)), ((), ())),
                           preferred_element_type=F32)


def _split(x, parts):
    out = []
    r = x
    for _ in range(parts - 1):
        p = r.astype(BF16)
        out.append(p)
        r = r - p.astype(F32)
    out.append(r.astype(BF16))
    return out


def _mm_rhs_split(a_exact, b, parts=3):
    a16 = a_exact.astype(BF16)
    acc = None
    for p in _split(b, parts):
        t = jnp.dot(a16, p, preferred_element_type=F32)
        acc = t if acc is None else acc + t
    return acc


def _mm_lhs_split(a, b_exact, parts=3):
    b16 = b_exact.astype(BF16)
    acc = None
    for p in _split(a, parts):
        t = jnp.dot(p, b16, preferred_element_type=F32)
        acc = t if acc is None else acc + t
    return acc


def _silu(x):
    h = 0.5 * x
    return h + h * jnp.tanh(h)


def _sigmoid(x):
    return 1.0 / (1.0 + jnp.exp(-x))


def _softplus(x):
    return jnp.maximum(x, 0.0) + jnp.log1p(jnp.exp(-jnp.abs(x)))


def _col(arr, lane):
    idx = lax.broadcasted_iota(jnp.int32, arr.shape, 1)
    return jnp.sum(jnp.where(idx == lane, arr, 0.0), axis=1, keepdims=True)


def _gate_rows(sm, bias, alog):
    sp = _softplus(sm + bias)
    sg = _sigmoid(sm)
    lane = lax.broadcasted_iota(jnp.int32, alog.shape, 1)
    has_decay = (lane < LANE_BETA) | ((lane >= LANE_G) & (lane < LANE_G + GDN_HEADS))
    neg_a = jnp.where(has_decay, -jnp.exp(alog), 0.0)
    return sp, sg, sp * neg_a


def _norm_mm_kernel(x_ref, nw_ref, w_ref, o_ref, xn_ref):
    @pl.when(pl.program_id(1) == 0)
    def _():
        x = x_ref[...]
        ms = jnp.mean(x * x, axis=-1, keepdims=True)
        xn_ref[...] = (x * lax.rsqrt(ms + RMS_EPS) * nw_ref[...]).astype(BF16)

    o_ref[...] = jnp.dot(xn_ref[...], w_ref[...], preferred_element_type=F32)


MAX_COL_TILE_LANES = 11 * LANES


def _col_tile(n):
    return max(t for t in range(LANES, MAX_COL_TILE_LANES + 1, LANES) if n % t == 0)


def _norm_mm(x, nw, w, tm):
    m, k = x.shape
    n = w.shape[1]
    tn = _col_tile(n)
    assert m % tm == 0
    return pl.pallas_call(
        _norm_mm_kernel,
        grid=(m // tm, n // tn),
        in_specs=[pl.BlockSpec((tm, k), lambda i, j: (i, 0)),
                  pl.BlockSpec((1, k), lambda i, j: (0, 0)),
                  pl.BlockSpec((k, tn), lambda i, j: (0, j))],
        out_specs=pl.BlockSpec((tm, tn), lambda i, j: (i, j)),
        out_shape=jax.ShapeDtypeStruct((m, n), F32),
        scratch_shapes=[pltpu.VMEM((tm, k), BF16)],
        compiler_params=_params("parallel", "arbitrary"),
    )(x, nw.reshape(1, k), w)


HIST_ROWS = SUBLANES


CONV_BUFS = 4


def _norm_mm_conv_kernel(x_ref, nw_ref, w_ref, cw_ref, cb_ref, o_ref, tail_ref, ext_ref, hist_ref,
                         *, taps, tiles_per_seq, conv_ranges):
    i = pl.program_id(0)
    tm = x_ref.shape[0]
    n = w_ref.shape[1]

    @pl.when(i == 0)
    def _():
        hist_ref[...] = jnp.zeros(hist_ref.shape, F32)

    x = x_ref[...]
    ms = jnp.mean(x * x, axis=-1, keepdims=True)
    xn = (x * lax.rsqrt(ms + RMS_EPS) * nw_ref[...]).astype(BF16)
    starts_seq = i % tiles_per_seq == 0
    for ci, lo in enumerate(range(0, n, MXU_COLS)):
        cols = slice(lo, min(lo + MXU_COLS, n))
        acc = jnp.dot(xn, w_ref[:, cols], preferred_element_type=F32)
        tail_ref[:, cols] = acc[tm - HIST_ROWS:tm, :]
        hit = [biased for a, b, biased in conv_ranges if a <= cols.start and cols.stop <= b]
        if not hit:
            o_ref[:, cols] = acc
            continue
        bias = cb_ref[:, cols] if hit[0] else None
        buf = ext_ref.at[ci % CONV_BUFS]
        width = cols.stop - cols.start
        buf[0:HIST_ROWS, 0:width] = jnp.where(starts_seq, 0.0, hist_ref[:, cols])
        buf[HIST_ROWS:HIST_ROWS + tm, 0:width] = acc
        hist_ref[:, cols] = acc[tm - HIST_ROWS:tm, :]
        o_ref[:, cols] = _silu(_causal_taps(buf[:, 0:width], cw_ref, bias, cols, taps, tm))


def _causal_taps(ext, cw_ref, bias, cols, taps, tm):
    rows = slice(HIST_ROWS, HIST_ROWS + tm)
    w = [cw_ref[k:k + 1, cols] for k in range(taps)]
    if taps == 4:
        lag1 = pltpu.roll(ext, 1, axis=0)
        older = pltpu.roll(ext * w[1] + lag1 * w[0], 2, axis=0)
        out = ext[rows, :] * w[3] + lag1[rows, :] * w[2] + older[rows, :]
    else:
        out = ext[rows, :] * w[taps - 1]
        for lag in range(1, taps):
            out = out + pltpu.roll(ext, lag, axis=0)[rows, :] * w[taps - 1 - lag]
    return out if bias is None else out + bias


def _norm_mm_conv(x, nw, w, cw, cb, conv_ranges, seq, tm):
    m, k = x.shape
    n = w.shape[1]
    taps = cw.shape[0]
    assert m % tm == 0 and seq % tm == 0 and tm >= HIST_ROWS
    assert all(a % MXU_COLS == 0 and b % MXU_COLS == 0 for a, b, _ in conv_ranges)
    whole = lambda a: pl.BlockSpec(a.shape, lambda i: (0, 0), pipeline_mode=pl.Buffered(1))
    nw2 = nw.reshape(1, k)
    return pl.pallas_call(
        functools.partial(_norm_mm_conv_kernel, taps=taps, tiles_per_seq=seq // tm,
                          conv_ranges=tuple(conv_ranges)),
        grid=(m // tm,),
        in_specs=[pl.BlockSpec((tm, k), lambda i: (i, 0)), whole(nw2), whole(w), whole(cw), whole(cb)],
        out_specs=[pl.BlockSpec((tm, n), lambda i: (i, 0)),
                   pl.BlockSpec((None, HIST_ROWS, n), lambda i: (i, 0, 0))],
        out_shape=[jax.ShapeDtypeStruct((m, n), F32),
                   jax.ShapeDtypeStruct((m // tm, HIST_ROWS, n), F32)],
        scratch_shapes=[pltpu.VMEM((CONV_BUFS, HIST_ROWS + tm, MXU_COLS), F32),
                        pltpu.VMEM((HIST_ROWS, n), F32)],
        compiler_params=_params("arbitrary"),
    )(x, nw2, w, cw, cb)


def _mm_res_kernel(*refs, n_in):
    res_ref = refs[0]
    a_refs = refs[1:1 + n_in]
    w_refs = refs[1 + n_in:1 + 2 * n_in]
    o_ref = refs[1 + 2 * n_in]
    acc = res_ref[...]
    for a_ref, w_ref in zip(a_refs, w_refs):
        acc = acc + jnp.dot(a_ref[...].astype(BF16), w_ref[...], preferred_element_type=F32)
    o_ref[...] = acc


def _mm_res(res, a_list, w_list, tm):
    m, n = res.shape
    assert m % tm == 0
    n_in = len(a_list)
    in_specs = [pl.BlockSpec((tm, n), lambda i: (i, 0))]
    in_specs += [pl.BlockSpec((tm, a.shape[1]), lambda i: (i, 0)) for a in a_list]
    in_specs += [pl.BlockSpec(w.shape, lambda i: (0, 0)) for w in w_list]
    return pl.pallas_call(
        functools.partial(_mm_res_kernel, n_in=n_in),
        grid=(m // tm,),
        in_specs=in_specs,
        out_specs=pl.BlockSpec((tm, n), lambda i: (i, 0)),
        out_shape=jax.ShapeDtypeStruct((m, n), F32),
        compiler_params=_params("parallel"),
    )(res, *a_list, *w_list)


def _rmsnorm_kernel(x_ref, nw_ref, o_ref):
    x = x_ref[...]
    ms = jnp.mean(x * x, axis=-1, keepdims=True)
    o_ref[...] = x * lax.rsqrt(ms + RMS_EPS) * nw_ref[...]


def _rmsnorm(x, nw, tm):
    m, k = x.shape
    return pl.pallas_call(
        _rmsnorm_kernel,
        grid=(m // tm,),
        in_specs=[pl.BlockSpec((tm, k), lambda i: (i, 0)), pl.BlockSpec((1, k), lambda i: (0, 0))],
        out_specs=pl.BlockSpec((tm, k), lambda i: (i, 0)),
        out_shape=jax.ShapeDtypeStruct((m, k), F32),
        compiler_params=_params("parallel"),
    )(x, nw.reshape(1, k))


def _causal_mask(cs, strict=False):
    r = lax.broadcasted_iota(jnp.int32, (cs, cs), 0)
    c = lax.broadcasted_iota(jnp.int32, (cs, cs), 1)
    return (r > c) if strict else (r >= c)


def _ssd_chunk(z_ref, xs_ref, b_ref, c_ref, sm_ref, bias_ref, alog_ref, dexp_ref, nw_ref, y_ref, hst, y_s):
    cs = xs_ref.shape[0]
    xs = xs_ref[...]
    bm = b_ref[...]
    cm = c_ref[...]

    sp, _, dta = _gate_rows(sm_ref[...], bias_ref[...], alog_ref[...])
    causal = _causal_mask(cs)
    cum = _mm_rhs_split(jnp.where(causal, 1.0, 0.0), dta)
    cum_row = cum.T

    lane_lo = lax.broadcasted_iota(jnp.int32, (1, LANES), 1) < SSD_HEAD_DIM
    row_lo = lax.broadcasted_iota(jnp.int32, (LANES, 1), 0) < SSD_HEAD_DIM
    heads_per_group = SSD_HEADS // SSD_GROUPS
    for g in range(SSD_GROUPS):
        cg = cm[:, g * SSD_STATE:(g + 1) * SSD_STATE]
        bg = bm[:, g * SSD_STATE:(g + 1) * SSD_STATE]
        cb = _mm_nt(cg, bg)
        rows_g = heads_per_group * SSD_HEAD_DIM
        y_state = _mm_nt(cg, hst[g * rows_g:(g + 1) * rows_g, :])
        for hp in range(heads_per_group // 2):
            h0 = g * heads_per_group + 2 * hp
            lo = h0 * SSD_HEAD_DIM
            xs_pair = xs[:, lo:lo + LANES]
            ci = [_col(cum, LANE_DT + h0), _col(cum, LANE_DT + h0 + 1)]
            dti = [_col(sp, LANE_DT + h0), _col(sp, LANE_DT + h0 + 1)]
            xdt_pair = xs_pair * jnp.where(lane_lo, dti[0], dti[1])
            y_pair = None
            for e in range(2):
                cj = cum_row[LANE_DT + h0 + e:LANE_DT + h0 + e + 1, :]
                seg = jnp.exp(jnp.where(causal, ci[e] - cj, -jnp.inf))
                keep = lane_lo if e == 0 else jnp.logical_not(lane_lo)
                t = _mm(cb * seg, jnp.where(keep, xdt_pair, 0.0))
                y_pair = t if y_pair is None else y_pair + t
            y_pair = y_pair + y_state[:, 2 * hp * SSD_HEAD_DIM:2 * hp * SSD_HEAD_DIM + LANES] * \
                jnp.where(lane_lo, jnp.exp(ci[0]), jnp.exp(ci[1]))
            y_pair = y_pair + dexp_ref[:, lo:lo + LANES] * xs_pair
            y_s[:, lo:lo + LANES] = y_pair
            cl = [ci[0][cs - 1:cs, :], ci[1][cs - 1:cs, :]]
            to_end = jnp.where(lane_lo, jnp.exp(cl[0] - ci[0]), jnp.exp(cl[1] - ci[1]))
            contrib = _mm((xdt_pair * to_end).T, bg)
            dec = jnp.where(row_lo, jnp.exp(cl[0]), jnp.exp(cl[1]))
            hst[lo:lo + LANES, :] = hst[lo:lo + LANES, :] * dec + contrib

    gs = SSD_INNER // SSD_GROUPS
    for g in range(SSD_GROUPS):
        yg = y_s[:, g * gs:(g + 1) * gs] * _silu(z_ref[:, g * gs:(g + 1) * gs])
        ms = jnp.mean(yg * yg, axis=-1, keepdims=True)
        y_ref[:, g * gs:(g + 1) * gs] = (yg * lax.rsqrt(ms + RMS_EPS) * nw_ref[:, g * gs:(g + 1) * gs]).astype(BF16)


INV_BASE = 16


def _same_block(cs, size):
    r = lax.broadcasted_iota(jnp.int32, (cs, cs), 0) // size
    c = lax.broadcasted_iota(jnp.int32, (cs, cs), 1) // size
    return r == c


def _bdot(a16, b16):
    return jnp.dot(a16, b16, preferred_element_type=F32)


def _mixer_prefill_kernel(q_ref, k_ref, v_ref, z_ref, sm_ref,
                          zs_ref, xs_ref, b_ref, c_ref,
                          bias_ref, alog_ref, nw_ref, dexp_ref, snw_ref,
                          o_ref, st_ref, y_ref, hst_ref,
                          sst, hst, y_s,
                          qd_s, kd_s, rhs_s, a_s, attn_s, x_s, d_s, u_s, w_s, glast_s, *, n_chunks):
    c_idx = pl.program_id(1)
    nseq, cs = q_ref.shape[0], q_ref.shape[1]
    n_units = nseq * GDN_HEADS
    heads = [(s, h, s * GDN_HEADS + h) for s in range(nseq) for h in range(GDN_HEADS)]
    qc, kc, vc = q_ref, k_ref, v_ref
    causal = _causal_mask(cs)
    strict = _causal_mask(cs, strict=True)

    def stage_a(par):
        sg, gc, gc_row = [], [], []
        for s in range(nseq):
            _, sg_s, g_step = _gate_rows(sm_ref[s], bias_ref[...], alog_ref[...])
            gc_s = _mm_rhs_split(jnp.where(causal, 1.0, 0.0), g_step)
            sg.append(sg_s)
            gc.append(gc_s)
            gc_row.append(gc_s.T)
            glast_s[par * nseq + s] = gc_s[cs - 1:cs, :]
        for s, h, u in heads:
            _stage_a_unit(par, s, h, u, sg, gc, gc_row)

    def _stage_a_unit(par, s, h, u, sg, gc, gc_row):
        u = par * n_units + u
        ps = par * nseq + s
        lo = h * GDN_HEAD_K
        qh = qc[s, :, lo:lo + GDN_HEAD_K]
        kh = kc[s, :, lo:lo + GDN_HEAD_K]
        vh = vc[s, :, lo:lo + GDN_HEAD_V]
        qn = qh * lax.rsqrt(jnp.sum(qh * qh, axis=-1, keepdims=True) + L2_EPS) * (GDN_HEAD_K ** -0.5)
        kn = kh * lax.rsqrt(jnp.sum(kh * kh, axis=-1, keepdims=True) + L2_EPS)
        gi = _col(gc[s], LANE_G + h)
        bi = _col(sg[s], LANE_BETA + h)
        gj = gc_row[s][LANE_G + h:LANE_G + h + 1, :]
        decay = jnp.exp(jnp.where(causal, gi - gj, -jnp.inf))
        egi = jnp.exp(gi)
        kb = kn * bi
        kn16 = kn.astype(BF16)
        a_s[u] = jnp.where(strict, lax.dot_general(kb.astype(BF16), kn16, (((1,), (1,)), ((), ())),
                                                   preferred_element_type=F32) * decay, 0.0)
        attn_s[u] = (lax.dot_general(qn.astype(BF16), kn16, (((1,), (1,)), ((), ())),
                                     preferred_element_type=F32) * decay).astype(BF16)
        qd_s[ps, :, lo:lo + GDN_HEAD_K] = (qn * egi).astype(BF16)
        kd_s[ps, :, lo:lo + GDN_HEAD_K] = (kn * jnp.exp(gi[cs - 1:cs, :] - gi)).astype(BF16)
        rhs_s[ps, :, 2 * lo:2 * lo + GDN_HEAD_V] = (vh * bi).astype(BF16)
        rhs_s[ps, :, 2 * lo + GDN_HEAD_V:2 * lo + 2 * GDN_HEAD_V] = (kb * egi).astype(BF16)

    def stages_bcd(par):
        base = _same_block(cs, INV_BASE)
        eye = jnp.where(causal & jnp.logical_not(strict), 1.0, 0.0)
        for _, _, u in heads:
            x = jnp.where(base, -a_s[par * n_units + u], 0.0)
            x_s[u] = x.astype(BF16)
            d_s[u] = eye + x
        zero16 = jnp.zeros((cs, cs), BF16)

        def pair_dot(a0, a1, b0, b1):
            rhs = jnp.concatenate([jnp.concatenate([b0, zero16], axis=1),
                                   jnp.concatenate([zero16, b1], axis=1)], axis=0)
            out = _bdot(jnp.concatenate([a0, a1], axis=1), rhs)
            return out[:, :cs], out[:, cs:]

        pairs = [(u, u + 1) for u in range(0, n_units, 2)]
        span = 2
        while span < INV_BASE:
            for u, v in pairs:
                xu, xv = x_s[u], x_s[v]
                x2u, x2v = (t.astype(BF16) for t in pair_dot(xu, xv, xu, xv))
                x_s[u] = x2u
                x_s[v] = x2v
                du, dv = d_s[u], d_s[v]
                pu, pv = pair_dot(du.astype(BF16), dv.astype(BF16), x2u, x2v)
                d_s[u] = du + pu
                d_s[v] = dv + pv
            span *= 2
        size = INV_BASE
        while size < cs:
            off = _same_block(cs, 2 * size) & jnp.logical_not(_same_block(cs, size))
            for u, v in pairs:
                du, dv = d_s[u], d_s[v]
                du16, dv16 = du.astype(BF16), dv.astype(BF16)
                lu = jnp.where(off, a_s[par * n_units + u], 0.0).astype(BF16)
                lv = jnp.where(off, a_s[par * n_units + v], 0.0).astype(BF16)
                tu, tv = (t.astype(BF16) for t in pair_dot(du16, dv16, lu, lv))
                pu, pv = pair_dot(tu, tv, du16, dv16)
                d_s[u] = du - pu
                d_s[v] = dv - pv
            size *= 2

        for s, h, u in heads:
            lo = h * GDN_HEAD_K
            uw = _bdot(d_s[u].astype(BF16), rhs_s[par * nseq + s, :, 2 * lo:2 * lo + 2 * GDN_HEAD_V])
            u_s[s, :, lo:lo + GDN_HEAD_V] = uw[:, :GDN_HEAD_V]
            w_s[s, :, lo:lo + GDN_HEAD_V] = uw[:, GDN_HEAD_V:].astype(BF16)

        for s, h, u in heads:
            lo = h * GDN_HEAD_K
            ps = par * nseq + s
            st = sst[s, lo:lo + GDN_HEAD_K, :]
            s16 = st.astype(BF16)
            v_new = u_s[s, :, lo:lo + GDN_HEAD_V] - _bdot(w_s[s, :, lo:lo + GDN_HEAD_V], s16)
            v16 = v_new.astype(BF16)
            o_h = _bdot(qd_s[ps, :, lo:lo + GDN_HEAD_K], s16) + _bdot(attn_s[par * n_units + u], v16)
            g_last = _col(glast_s[ps], LANE_G + h)
            sst[s, lo:lo + GDN_HEAD_K, :] = st * jnp.exp(g_last) + lax.dot_general(
                kd_s[ps, :, lo:lo + GDN_HEAD_K], v16, (((0,), (0,)), ((), ())), preferred_element_type=F32)
            ms = jnp.mean(o_h * o_h, axis=-1, keepdims=True)
            o_ref[s, :, lo:lo + GDN_HEAD_V] = ((o_h * lax.rsqrt(ms + RMS_EPS) * nw_ref[...]) *
                                               _silu(z_ref[s, :, lo:lo + GDN_HEAD_V])).astype(BF16)

    def ssd_chunks():
        for s in range(nseq):
            _ssd_chunk(zs_ref.at[s], xs_ref.at[s], b_ref.at[s], c_ref.at[s], sm_ref.at[s], bias_ref, alog_ref,
                       dexp_ref, snw_ref, y_ref.at[s], hst.at[s], y_s.at[s])

    @pl.when(c_idx == 0)
    def _():
        sst[...] = jnp.zeros(sst.shape, F32)
        hst[...] = jnp.zeros(hst.shape, F32)
        stage_a(0)
        ssd_chunks()

    middle = (c_idx > 0) & (c_idx < n_chunks)
    for par in range(2):
        @pl.when(middle & (c_idx % 2 == par))
        def _(par=par):
            stages_bcd(1 - par)
            stage_a(par)
            ssd_chunks()

    @pl.when(c_idx == n_chunks)
    def _():
        stages_bcd((n_chunks - 1) % 2)
        st_ref[...] = sst[...]
        hst_ref[...] = hst[...]


GDN_SEQS_PER_STEP = 2


def _mixer_prefill(proj, bsz, seq, lw):
    cs = MIXER_CHUNK
    nseq = GDN_SEQS_PER_STEP if bsz % GDN_SEQS_PER_STEP == 0 else 1
    assert seq % cs == 0
    nc = seq // cs
    proj3 = proj.reshape(bsz, seq, PROJ_COLS)
    cur = lambda c: jnp.minimum(c, nc - 1)
    prev = lambda c: jnp.maximum(c - 1, 0)
    pcol = lambda width, off, ch: pl.BlockSpec((nseq, cs, width), lambda b, c: (b, ch(c), off // width))
    full = lambda a: pl.BlockSpec(a.shape, lambda b, c: (0,) * a.ndim)
    weights = [lw['gate_bias'], lw['gate_alog'], lw['gdn_norm_w'], lw['ssd_d_exp'], lw['ssd_norm_w']]
    units = nseq * GDN_HEADS
    o, st, y, hst = pl.pallas_call(
        functools.partial(_mixer_prefill_kernel, n_chunks=nc),
        grid=(bsz // nseq, nc + 1),
        in_specs=[pcol(GDN_DIM, COL_Q, cur), pcol(GDN_DIM, COL_K, cur), pcol(GDN_DIM, COL_V, cur),
                  pcol(GDN_DIM, COL_Z_GDN, prev), pcol(LANES, COL_SMALL, cur),
                  pcol(SSD_INNER, COL_Z_SSD, cur), pcol(SSD_INNER, COL_XS, cur), pcol(SSD_BC, COL_B, cur),
                  pcol(SSD_BC, COL_C, cur)] + [full(w) for w in weights],
        out_specs=[pl.BlockSpec((nseq, cs, GDN_DIM), lambda b, c: (b, prev(c), 0)),
                   pl.BlockSpec((nseq, GDN_HEADS * GDN_HEAD_K, GDN_HEAD_V), lambda b, c: (b, 0, 0)),
                   pl.BlockSpec((nseq, cs, SSD_INNER), lambda b, c: (b, cur(c), 0)),
                   pl.BlockSpec((nseq, SSD_INNER, SSD_STATE), lambda b, c: (b, 0, 0))],
        out_shape=[jax.ShapeDtypeStruct((bsz, seq, GDN_DIM), BF16),
                   jax.ShapeDtypeStruct((bsz, GDN_HEADS * GDN_HEAD_K, GDN_HEAD_V), F32),
                   jax.ShapeDtypeStruct((bsz, seq, SSD_INNER), BF16),
                   jax.ShapeDtypeStruct((bsz, SSD_INNER, SSD_STATE), F32)],
        scratch_shapes=[pltpu.VMEM((nseq, GDN_HEADS * GDN_HEAD_K, GDN_HEAD_V), F32),
                        pltpu.VMEM((nseq, SSD_INNER, SSD_STATE), F32),
                        pltpu.VMEM((nseq, cs, SSD_INNER), F32),
                        pltpu.VMEM((2 * nseq, cs, GDN_DIM), BF16),
                        pltpu.VMEM((2 * nseq, cs, GDN_DIM), BF16),
                        pltpu.VMEM((2 * nseq, cs, 2 * GDN_DIM), BF16),
                        pltpu.VMEM((2 * units, cs, cs), F32),
                        pltpu.VMEM((2 * units, cs, cs), BF16),
                        pltpu.VMEM((units, cs, cs), BF16),
                        pltpu.VMEM((units, cs, cs), F32),
                        pltpu.VMEM((nseq, cs, GDN_DIM), F32),
                        pltpu.VMEM((nseq, cs, GDN_DIM), BF16),
                        pltpu.VMEM((2 * nseq, 1, LANES), F32)],
        compiler_params=_params("parallel", "arbitrary"),
    )(*([proj3] * 9), *weights)
    return y.reshape(bsz * seq, SSD_INNER), hst, o.reshape(bsz * seq, GDN_DIM), st


def _softmax_rows(s):
    m = jnp.max(s, axis=-1, keepdims=True)
    e = jnp.exp(s - m)
    return e / jnp.sum(e, axis=-1, keepdims=True)


def _xattn_prefill_kernel(x_ref, y_ref, og_ref, w1_ref, w2_ref, nw_ref, wq_ref, mk_ref, mv_ref, wo_ref, o_ref):
    x = x_ref[...] + jnp.dot(y_ref[...], w1_ref[...], preferred_element_type=F32) + \
        jnp.dot(og_ref[...], w2_ref[...], preferred_element_type=F32)
    ms = jnp.mean(x * x, axis=-1, keepdims=True)
    h = x * lax.rsqrt(ms + RMS_EPS) * nw_ref[...]
    q = jnp.dot(h.astype(BF16), wq_ref[...], preferred_element_type=F32)
    outs = []
    for hd in range(XA_HEADS):
        lo = hd * XA_HEAD_DIM
        s = _mm_nt(q[:, lo:lo + XA_HEAD_DIM], mk_ref[:, lo:lo + XA_HEAD_DIM]) * (XA_HEAD_DIM ** -0.5)
        outs.append(_mm(_softmax_rows(s), mv_ref[:, lo:lo + XA_HEAD_DIM]))
    att = jnp.concatenate(outs, axis=-1)
    o_ref[...] = x + jnp.dot(att.astype(BF16), wo_ref[...], preferred_element_type=F32)


def _xattn_prefill(x, y, og, w1, w2, nw, wq, mk, mv, wo, bsz, seq, tq):
    n_mem = mk.shape[1]
    nt = seq // tq
    rows = lambda a: pl.BlockSpec((tq, a.shape[1]), lambda b, t: (b * nt + t, 0))
    return pl.pallas_call(
        _xattn_prefill_kernel,
        grid=(bsz, nt),
        in_specs=[rows(x), rows(y), rows(og),
                  pl.BlockSpec(w1.shape, lambda b, t: (0, 0)),
                  pl.BlockSpec(w2.shape, lambda b, t: (0, 0)),
                  pl.BlockSpec((1, D_MODEL), lambda b, t: (0, 0)),
                  pl.BlockSpec((D_MODEL, D_MODEL), lambda b, t: (0, 0)),
                  pl.BlockSpec((None, n_mem, D_MODEL), lambda b, t: (b, 0, 0)),
                  pl.BlockSpec((None, n_mem, D_MODEL), lambda b, t: (b, 0, 0)),
                  pl.BlockSpec((D_MODEL, D_MODEL), lambda b, t: (0, 0))],
        out_specs=pl.BlockSpec((tq, D_MODEL), lambda b, t: (b * nt + t, 0)),
        out_shape=jax.ShapeDtypeStruct(x.shape, F32),
        compiler_params=_params("parallel", "arbitrary"),
    )(x, y, og, w1, w2, nw.reshape(1, D_MODEL), wq, mk, mv, wo)


def _xattn_decode_kernel(q_ref, mk_ref, mv_ref, o_ref):
    tb = q_ref.shape[0]
    for i in range(tb):
        q = q_ref[i]
        qh = jnp.concatenate([q[:, hd * XA_HEAD_DIM:(hd + 1) * XA_HEAD_DIM] for hd in range(XA_HEADS)], axis=0)
        s = jnp.sum(mk_ref[i] * qh[None], axis=-1, keepdims=True) * (XA_HEAD_DIM ** -0.5)
        e = jnp.exp(s - jnp.max(s, axis=0, keepdims=True))
        p = e / jnp.sum(e, axis=0, keepdims=True)
        o_ref[i] = jnp.sum(p * mv_ref[i], axis=0)


def _xattn_decode(q, mk_all, mv_all, layer, tb):
    _, nb, n_mem, nh, hd = mk_all.shape
    return pl.pallas_call(
        _xattn_decode_kernel,
        grid=(nb // tb,),
        in_specs=[pl.BlockSpec((tb, 1, D_MODEL), lambda i: (i, 0, 0)),
                  pl.BlockSpec((None, tb, n_mem, nh, hd), lambda i: (layer, i, 0, 0, 0)),
                  pl.BlockSpec((None, tb, n_mem, nh, hd), lambda i: (layer, i, 0, 0, 0))],
        out_specs=pl.BlockSpec((tb, nh, hd), lambda i: (i, 0, 0)),
        out_shape=jax.ShapeDtypeStruct((nb, nh, hd), F32),
        compiler_params=_params("parallel"),
    )(q.reshape(nb, 1, D_MODEL), mk_all, mv_all).reshape(nb, D_MODEL)


def _ffn_prefill_kernel(x_ref, nw_ref, wgu_ref, cw_ref, cb_ref, wd_ref, o_ref, tail_ref,
                        ext_ref, hist_ref, hm_ref, *, tiles_per_seq):
    i = pl.program_id(0)
    tm = x_ref.shape[0]
    d_ff = wd_ref.shape[0]

    @pl.when(i == 0)
    def _():
        hist_ref[...] = jnp.zeros(hist_ref.shape, F32)

    x = x_ref[...]
    ms = jnp.mean(x * x, axis=-1, keepdims=True)
    xn = (x * lax.rsqrt(ms + RMS_EPS) * nw_ref[...]).astype(BF16)
    starts_seq = i % tiles_per_seq == 0
    for ci, lo in enumerate(range(0, d_ff, MXU_COLS)):
        cols = slice(lo, min(lo + MXU_COLS, d_ff))
        width = cols.stop - cols.start
        gate = jnp.dot(xn, wgu_ref[:, cols], preferred_element_type=F32)
        up = jnp.dot(xn, wgu_ref[:, d_ff + cols.start:d_ff + cols.stop], preferred_element_type=F32)
        tail_ref[:, cols] = gate[tm - HIST_ROWS:tm, :]
        buf = ext_ref.at[ci % CONV_BUFS]
        buf[0:HIST_ROWS, 0:width] = jnp.where(starts_seq, 0.0, hist_ref[:, cols])
        buf[HIST_ROWS:HIST_ROWS + tm, 0:width] = gate
        hist_ref[:, cols] = gate[tm - HIST_ROWS:tm, :]
        c = _causal_taps(buf[:, 0:width], cw_ref, cb_ref[:, cols], cols, FFN_CONV, tm)
        hm_ref[:, cols] = (_silu(c) * up).astype(BF16)
    o_ref[...] = x + jnp.dot(hm_ref[...], wd_ref[...], preferred_element_type=F32)


def _ffn_prefill(x, nw, wgu, cw, cb, wd, seq, tm):
    m = x.shape[0]
    d_ff = wd.shape[0]
    assert m % tm == 0 and seq % tm == 0 and tm >= HIST_ROWS
    whole = lambda a: pl.BlockSpec(a.shape, lambda i: (0, 0), pipeline_mode=pl.Buffered(1))
    ins = [nw.reshape(1, D_MODEL), wgu, cw, cb.reshape(1, d_ff), wd]
    return pl.pallas_call(
        functools.partial(_ffn_prefill_kernel, tiles_per_seq=seq // tm),
        grid=(m // tm,),
        in_specs=[pl.BlockSpec((tm, D_MODEL), lambda i: (i, 0))] + [whole(a) for a in ins],
        out_specs=[pl.BlockSpec((tm, D_MODEL), lambda i: (i, 0)),
                   pl.BlockSpec((None, HIST_ROWS, d_ff), lambda i: (i, 0, 0))],
        out_shape=[jax.ShapeDtypeStruct((m, D_MODEL), F32),
                   jax.ShapeDtypeStruct((m // tm, HIST_ROWS, d_ff), F32)],
        scratch_shapes=[pltpu.VMEM((CONV_BUFS, HIST_ROWS + tm, MXU_COLS), F32),
                        pltpu.VMEM((HIST_ROWS, d_ff), F32),
                        pltpu.VMEM((tm, d_ff), BF16)],
        compiler_params=_params("arbitrary"),
    )(x, *ins)


def _ffn_decode_kernel(x_ref, g_ref, h0_ref, h1_ref, u_ref, cw_ref, cb_ref, wd_ref, o_ref):
    acc = cb_ref[...] + h0_ref[...] * cw_ref[0:1, :] + h1_ref[...] * cw_ref[1:2, :] + g_ref[...] * cw_ref[2:3, :]
    hmid = _silu(acc) * u_ref[...]
    o_ref[...] = x_ref[...] + jnp.dot(hmid.astype(BF16), wd_ref[...], preferred_element_type=F32)


def _ffn_decode(x, gu, hist0, hist1, cw, cb, wd):
    m = x.shape[0]
    d_ff = wd.shape[0]
    rows = lambda width, col: pl.BlockSpec((m, width), lambda i: (0, col))
    return pl.pallas_call(
        _ffn_decode_kernel,
        grid=(1,),
        in_specs=[rows(D_MODEL, 0), rows(d_ff, 0), rows(d_ff, 0), rows(d_ff, 0), rows(d_ff, 1),
                  pl.BlockSpec((FFN_CONV, d_ff), lambda i: (0, 0)),
                  pl.BlockSpec((1, d_ff), lambda i: (0, 0)),
                  pl.BlockSpec((d_ff, D_MODEL), lambda i: (0, 0))],
        out_specs=rows(D_MODEL, 0),
        out_shape=jax.ShapeDtypeStruct((m, D_MODEL), F32),
        compiler_params=_params("arbitrary"),
    )(x, gu, hist0, hist1, gu, cw, cb.reshape(1, d_ff), wd)


def _expand_lanes(rows, first_lane, width, n_out, parts):
    k = lax.broadcasted_iota(jnp.int32, (LANES, n_out), 0)
    l = lax.broadcasted_iota(jnp.int32, (LANES, n_out), 1)
    sel = jnp.where(l // width == k - first_lane, 1.0, 0.0)
    return _mm_lhs_split(rows, sel, parts)


def _decode_rows_kernel(proj_ref, sh0, sh1, sh2, gh0, gh1, gh2,
                        scw_ref, scb_ref, gcw_ref, bias_ref, alog_ref,
                        xs_ref, xdt_ref, b_ref, c_ref, sdec_ref,
                        q_ref, k_ref, v_ref, beta_ref, gdec_ref):
    def conv(hists, new, cw, cb):
        acc = new * cw[3:4, :]
        for t, hr in enumerate(hists):
            acc = acc + hr[...] * cw[t:t + 1, :]
        return acc if cb is None else acc + cb

    xbc_new = jnp.concatenate([proj_ref[:, COL_XS:COL_XS + SSD_INNER],
                               proj_ref[:, COL_B:COL_B + SSD_BC],
                               proj_ref[:, COL_C:COL_C + SSD_BC]], axis=-1)
    xbc = _silu(conv((sh0, sh1, sh2), xbc_new, scw_ref, scb_ref[...]))
    sp, sg, log_dec = _gate_rows(proj_ref[:, COL_SMALL:COL_SMALL + LANES], bias_ref[...], alog_ref[...])
    xs = xbc[:, 0:SSD_INNER]
    xs_ref[...] = xs
    xdt_ref[...] = xs * _expand_lanes(sp, LANE_DT, SSD_HEAD_DIM, SSD_INNER, 3)
    b_ref[...] = xbc[:, SSD_INNER:SSD_INNER + SSD_BC]
    c_ref[...] = xbc[:, SSD_INNER + SSD_BC:SSD_INNER + 2 * SSD_BC]
    sdec_ref[...] = jnp.exp(_expand_lanes(log_dec, LANE_DT, SSD_HEAD_DIM, SSD_INNER, 3))

    qkv_new = proj_ref[:, COL_Q:COL_Q + 3 * GDN_DIM]
    qkv = _silu(conv((gh0, gh1, gh2), qkv_new, gcw_ref, None))
    for h in range(GDN_HEADS):
        lo = h * GDN_HEAD_K
        qh = qkv[:, lo:lo + GDN_HEAD_K]
        kh = qkv[:, GDN_DIM + lo:GDN_DIM + lo + GDN_HEAD_K]
        q_ref[:, lo:lo + GDN_HEAD_K] = qh * lax.rsqrt(jnp.sum(qh * qh, axis=-1, keepdims=True) + L2_EPS) * \
            (GDN_HEAD_K ** -0.5)
        k_ref[:, lo:lo + GDN_HEAD_K] = kh * lax.rsqrt(jnp.sum(kh * kh, axis=-1, keepdims=True) + L2_EPS)
    v_ref[...] = qkv[:, 2 * GDN_DIM:3 * GDN_DIM]
    beta_ref[...] = _expand_lanes(sg, LANE_BETA, GDN_HEAD_V, GDN_DIM, 3)
    gdec_ref[...] = jnp.exp(_expand_lanes(log_dec, LANE_G, GDN_HEAD_V, GDN_DIM, 3))


def _decode_rows(proj, ssd_hist, gdn_hist, lw):
    m = proj.shape[0]
    ins = [proj] + [ssd_hist[:, t] for t in range(SSD_CONV - 1)] + [gdn_hist[:, t] for t in range(GDN_CONV - 1)]
    ins += [lw['ssd_conv_w'], lw['ssd_conv_b'], lw['gdn_conv_w'], lw['gate_bias'], lw['gate_alog']]
    widths = [SSD_INNER, SSD_INNER, SSD_BC, SSD_BC, SSD_INNER, GDN_DIM, GDN_DIM, GDN_DIM, GDN_DIM, GDN_DIM]
    return pl.pallas_call(
        _decode_rows_kernel,
        grid=(1,),
        in_specs=[pl.BlockSpec(a.shape, lambda i, nd=a.ndim: (0,) * nd) for a in ins],
        out_specs=[pl.BlockSpec((m, w), lambda i: (0, 0)) for w in widths],
        out_shape=[jax.ShapeDtypeStruct((m, w), F32) for w in widths],
        compiler_params=_params("arbitrary"),
    )(*ins)


def _rows_to_cols(rows, i, parts):
    tb = rows.shape[0]
    r = lax.broadcasted_iota(jnp.int32, (tb, LANES), 0)
    sel = jnp.where(r == i, 1.0, 0.0).astype(BF16)
    acc = None
    for p in _split(rows, parts):
        t = lax.dot_general(p, sel, (((0,), (0,)), ((), ())), preferred_element_type=F32)
        acc = t if acc is None else acc + t
    return acc


def _only_row(rows, i):
    r = lax.broadcasted_iota(jnp.int32, rows.shape, 0)
    return jnp.where(r == i, rows, 0.0)


def _decode_state_kernel(xdt_ref, b_ref, c_ref, sdec_ref, q_ref, k_ref, v_ref, beta_ref, gdec_ref,
                         hs_ref, ss_ref, *rest):
    y_ref, o_ref, hs_out, ss_out = rest[-4:]
    tb = xdt_ref.shape[0]
    xdt = xdt_ref[...]
    bm = b_ref[...]
    cm = c_ref[...]
    sdec = sdec_ref[...]
    qn = q_ref[...]
    kn = k_ref[...]
    vv = v_ref[...]
    beta = beta_ref[...]
    gdec = gdec_ref[...]
    rows_g = SSD_INNER // SSD_GROUPS

    y_acc = jnp.zeros((tb, SSD_INNER), F32)
    ks_acc = jnp.zeros((tb, GDN_DIM), F32)
    for i in range(tb):
        dec_c = _rows_to_cols(sdec, i, 2)
        xdt_c = _rows_to_cols(xdt, i, 1)
        pieces = []
        for g in range(SSD_GROUPS):
            rs = slice(g * rows_g, (g + 1) * rows_g)
            bg = bm[i:i + 1, g * SSD_STATE:(g + 1) * SSD_STATE]
            h_new = hs_ref[i, rs, :] * dec_c[rs, :] + xdt_c[rs, :] * bg
            hs_out[i, rs, :] = h_new
            cg = _only_row(cm[:, g * SSD_STATE:(g + 1) * SSD_STATE], i)
            pieces.append(_mm_nt(cg, h_new))
        y_acc = y_acc + jnp.concatenate(pieces, axis=-1)
        g_c = _rows_to_cols(gdec, i, 2)
        s_dec = ss_ref[i] * g_c
        ss_out[i] = s_dec
        pieces = []
        for h in range(GDN_HEADS):
            rs = slice(h * GDN_HEAD_K, (h + 1) * GDN_HEAD_K)
            pieces.append(_mm(_only_row(kn[:, rs], i), s_dec[rs, :]))
        ks_acc = ks_acc + jnp.concatenate(pieces, axis=-1)
    y_ref[...] = y_acc

    delta = beta * (vv - ks_acc)
    o_acc = jnp.zeros((tb, GDN_DIM), F32)
    for i in range(tb):
        k_c = _rows_to_cols(kn, i, 1)
        pieces = []
        for h in range(GDN_HEADS):
            rs = slice(h * GDN_HEAD_K, (h + 1) * GDN_HEAD_K)
            d_row = delta[i:i + 1, rs]
            s_new = ss_out[i, rs, :] + k_c[rs, :] * d_row
            ss_out[i, rs, :] = s_new
            pieces.append(_mm(_only_row(qn[:, rs], i), s_new))
        o_acc = o_acc + jnp.concatenate(pieces, axis=-1)
    o_ref[...] = o_acc


def _decode_state(rows, hs_all, ss_all, layer, tb, carried):
    xs, xdt, bm, cm, sdec, qn, kn, vv, beta, gdec = rows
    nb = xdt.shape[0]
    row_ins = [xdt, bm, cm, sdec, qn, kn, vv, beta, gdec]
    rspec = lambda a: pl.BlockSpec((tb, a.shape[1]), lambda i: (i, 0))
    sspec = lambda a: pl.BlockSpec((None, tb) + a.shape[2:], lambda i: (layer, i, 0, 0))
    n_in = len(row_ins) + 2
    aliases = {n_in + j: 2 + j for j in range(len(carried))}
    return pl.pallas_call(
        _decode_state_kernel,
        grid=(nb // tb,),
        in_specs=[rspec(a) for a in row_ins] + [sspec(hs_all), sspec(ss_all)] +
                 [pl.BlockSpec(memory_space=pl.ANY) for _ in carried],
        out_specs=[rspec(xdt), rspec(qn), sspec(hs_all), sspec(ss_all)],
        out_shape=[jax.ShapeDtypeStruct(xdt.shape, F32), jax.ShapeDtypeStruct(qn.shape, F32),
                   jax.ShapeDtypeStruct(hs_all.shape, F32), jax.ShapeDtypeStruct(ss_all.shape, F32)],
        input_output_aliases=aliases,
        compiler_params=_params("parallel"),
    )(*row_ins, hs_all, ss_all, *carried)


def _decode_out_kernel(x_ref, y_ref, xs_ref, o_ref, proj_ref, dexp_ref, snw_ref, gnw_ref, w1_ref, w2_ref, out_ref):
    y = y_ref[...] + dexp_ref[...] * xs_ref[...]
    gs = SSD_INNER // SSD_GROUPS
    ys = []
    for g in range(SSD_GROUPS):
        yg = y[:, g * gs:(g + 1) * gs] * _silu(proj_ref[:, COL_Z_SSD + g * gs:COL_Z_SSD + (g + 1) * gs])
        ms = jnp.mean(yg * yg, axis=-1, keepdims=True)
        ys.append(yg * lax.rsqrt(ms + RMS_EPS) * snw_ref[:, g * gs:(g + 1) * gs])
    os = []
    for h in range(GDN_HEADS):
        lo = h * GDN_HEAD_V
        oh = o_ref[:, lo:lo + GDN_HEAD_V]
        ms = jnp.mean(oh * oh, axis=-1, keepdims=True)
        os.append(oh * lax.rsqrt(ms + RMS_EPS) * gnw_ref[...] *
                  _silu(proj_ref[:, COL_Z_GDN + lo:COL_Z_GDN + lo + GDN_HEAD_V]))
    yn = jnp.concatenate(ys, axis=-1)
    on = jnp.concatenate(os, axis=-1)
    out_ref[...] = x_ref[...] + _mm(yn, w1_ref[...]) + _mm(on, w2_ref[...])


def _decode_out(x, y, xs, o, proj, lw):
    ins = [x, y, xs, o, proj, lw['ssd_d_exp'], lw['ssd_norm_w'], lw['gdn_norm_w'], lw['w_out_ssd'], lw['w_out_gdn']]
    return pl.pallas_call(
        _decode_out_kernel,
        grid=(1,),
        in_specs=[pl.BlockSpec(a.shape, lambda i, nd=a.ndim: (0,) * nd) for a in ins],
        out_specs=pl.BlockSpec(x.shape, lambda i: (0, 0)),
        out_shape=jax.ShapeDtypeStruct(x.shape, F32),
        compiler_params=_params("arbitrary"),
    )(*ins)


def _identity_taps(taps, n):
    return jnp.concatenate([jnp.zeros((taps - 1, n), F32), jnp.ones((1, n), F32)], axis=0)


def _prep_layer(i, p):
    w_in = p['w_in'][i]
    o_zs = 0
    o_xbc = o_zs + SSD_INNER
    o_dt = o_xbc + SSD_INNER + 2 * SSD_BC
    o_qkv = o_dt + SSD_HEADS
    o_zg = o_qkv + 3 * GDN_DIM
    o_b = o_zg + GDN_DIM
    o_a = o_b + GDN_HEADS
    used = SSD_HEADS + 2 * GDN_HEADS
    w_perm = jnp.concatenate([
        w_in[:, o_zs:o_zs + SSD_INNER],
        w_in[:, o_xbc:o_xbc + SSD_INNER],
        w_in[:, o_qkv:o_qkv + 3 * GDN_DIM],
        w_in[:, o_zg:o_zg + GDN_DIM],
        w_in[:, o_xbc + SSD_INNER:o_xbc + SSD_INNER + 2 * SSD_BC],
        w_in[:, o_dt:o_dt + SSD_HEADS],
        w_in[:, o_b:o_b + GDN_HEADS],
        w_in[:, o_a:o_a + GDN_HEADS],
        jnp.zeros((D_MODEL, PROJ_COLS - COL_SMALL - used), F32)], axis=1).astype(BF16)
    lane_pad = jnp.zeros((LANES - used,), F32)
    scw = p['ssd_conv_w'][i]
    scb = p['ssd_conv_b'][i].reshape(1, -1)
    gcw = p['gdn_conv_w'][i]
    return {
        'w_in': w_perm,
        'norm_mix_w': p['norm_mix_w'][i],
        'ssd_conv_w': scw, 'ssd_conv_b': scb, 'gdn_conv_w': gcw,
        'in_cw': jnp.concatenate([_identity_taps(SSD_CONV, SSD_INNER), scw[:, :SSD_INNER], gcw,
                                  _identity_taps(GDN_CONV, GDN_DIM), scw[:, SSD_INNER:],
                                  _identity_taps(SSD_CONV, PROJ_COLS - COL_SMALL)], axis=1),
        'in_cb': jnp.concatenate([jnp.zeros((1, COL_XS), F32), scb[:, :SSD_INNER],
                                  jnp.zeros((1, COL_B - COL_Q), F32), scb[:, SSD_INNER:],
                                  jnp.zeros((1, PROJ_COLS - COL_SMALL), F32)], axis=1),
        'gate_bias': jnp.concatenate([p['ssd_dt_bias'][i], jnp.zeros((GDN_HEADS,), F32), p['gdn_dt_bias'][i],
                                      lane_pad]).reshape(1, LANES),
        'gate_alog': jnp.concatenate([p['ssd_a_log'][i], jnp.zeros((GDN_HEADS,), F32), p['gdn_a_log'][i],
                                      lane_pad]).reshape(1, LANES),
        'ssd_d_exp': jnp.repeat(p['ssd_d'][i], SSD_HEAD_DIM).reshape(1, SSD_INNER),
        'ssd_norm_w': p['ssd_norm_w'][i].reshape(1, SSD_INNER),
        'gdn_norm_w': p['gdn_norm_w'][i].reshape(1, GDN_HEAD_V),
        'w_out_ssd': p['w_out'][i, :SSD_INNER].astype(BF16),
        'w_out_gdn': p['w_out'][i, SSD_INNER:].astype(BF16),
        'norm_xa_w': p['norm_xa_w'][i], 'norm_mem_w': p['norm_mem_w'][i],
        'xa_wq': p['xa_wq'][i].astype(BF16), 'xa_wo': p['xa_wo'][i].astype(BF16),
        'xa_wkv': jnp.concatenate([p['xa_wk'][i], p['xa_wv'][i]], axis=1).astype(BF16),
        'norm_ffn_w': p['norm_ffn_w'][i],
        'ffn_w_gu': jnp.concatenate([p['ffn_w_gate'][i], p['ffn_w_up'][i]], axis=1).astype(BF16),
        'ffn_conv_w': p['ffn_conv_w'][i], 'ffn_conv_b': p['ffn_conv_b'][i],
        'ffn_w_down': p['ffn_w_down'][i].astype(BF16),
    }


def _unpermute_conv_rows(proj_rows):
    ssd = jnp.concatenate([proj_rows[..., COL_XS:COL_XS + SSD_INNER],
                           proj_rows[..., COL_B:COL_B + SSD_BC],
                           proj_rows[..., COL_C:COL_C + SSD_BC]], axis=-1)
    gdn = proj_rows[..., COL_Q:COL_Q + 3 * GDN_DIM]
    return ssd, gdn


def _tile(m, pref):
    t = min(m, pref)
    while m % t:
        t //= 2
    return t


def kernel(x_prompt, x_sample, mem_prompt, cache_mem_k, cache_mem_v, state_ssd_conv, state_ssd, state_gdn_conv, state_gdn, state_ffn_conv, norm_mix_w, w_in, ssd_conv_w, ssd_conv_b, ssd_dt_bias, ssd_a_log, ssd_d, ssd_norm_w, gdn_conv_w, gdn_dt_bias, gdn_a_log, gdn_norm_w, w_out, norm_xa_w, norm_mem_w, xa_wq, xa_wk, xa_wv, xa_wo, norm_ffn_w, ffn_w_gate, ffn_w_up, ffn_conv_w, ffn_conv_b, ffn_w_down, final_norm_w):
    params = dict(norm_mix_w=norm_mix_w, w_in=w_in, ssd_conv_w=ssd_conv_w, ssd_conv_b=ssd_conv_b,
                  ssd_dt_bias=ssd_dt_bias, ssd_a_log=ssd_a_log, ssd_d=ssd_d, ssd_norm_w=ssd_norm_w,
                  gdn_conv_w=gdn_conv_w, gdn_dt_bias=gdn_dt_bias, gdn_a_log=gdn_a_log, gdn_norm_w=gdn_norm_w,
                  w_out=w_out, norm_xa_w=norm_xa_w, norm_mem_w=norm_mem_w, xa_wq=xa_wq, xa_wk=xa_wk, xa_wv=xa_wv,
                  xa_wo=xa_wo, norm_ffn_w=norm_ffn_w, ffn_w_gate=ffn_w_gate, ffn_w_up=ffn_w_up,
                  ffn_conv_w=ffn_conv_w, ffn_conv_b=ffn_conv_b, ffn_w_down=ffn_w_down)
    depth = w_in.shape[0]
    bsz, seq, _ = x_prompt.shape
    nb = x_sample.shape[0]
    n_mem = mem_prompt.shape[1]
    d_ff = ffn_w_down.shape[1]
    mp = bsz * seq

    xp = x_prompt.reshape(mp, D_MODEL)
    xs = x_sample.reshape(nb, D_MODEL)
    mem = mem_prompt.reshape(bsz * n_mem, D_MODEL)
    tm_p = _tile(seq, 1024)
    tm_c = _tile(seq, 256)
    tm_mem = _tile(bsz * n_mem, 1024)
    dec_tile = _tile(nb, DEC_TILE)

    hs_all = state_ssd.reshape(depth, nb, SSD_INNER, SSD_STATE)
    ss_all = state_gdn.reshape(depth, nb, GDN_HEADS * GDN_HEAD_K, GDN_HEAD_V)
    mk_all, mv_all = cache_mem_k, cache_mem_v
    new_states = ()

    mkp, mvp = [], []
    p_sc, p_sh, p_gc, p_gs, p_fc = [], [], [], [], []
    s_sc, s_gc, s_fc = [], [], []
    for i in range(depth):
        lw = _prep_layer(i, params)

        kv = _norm_mm(mem, lw['norm_mem_w'], lw['xa_wkv'], tm_mem)
        mk = kv[:, :D_MODEL].reshape(bsz, n_mem, D_MODEL)
        mv = kv[:, D_MODEL:].reshape(bsz, n_mem, D_MODEL)
        mkp.append(mk.reshape(bsz, n_mem, XA_HEADS, XA_HEAD_DIM))
        mvp.append(mv.reshape(bsz, n_mem, XA_HEADS, XA_HEAD_DIM))

        proj, tails = _norm_mm_conv(xp, lw['norm_mix_w'], lw['w_in'], lw['in_cw'], lw['in_cb'],
                                    [(COL_XS, COL_Q, True), (COL_Q, COL_Z_GDN, False), (COL_B, COL_SMALL, True)],
                                    seq, tm_c)
        y, h_new, o, s_new = _mixer_prefill(proj, bsz, seq, lw)
        seq_tails = tails.reshape(bsz, seq // tm_c, HIST_ROWS, PROJ_COLS)[:, -1]
        ssd_tail, gdn_tail = _unpermute_conv_rows(seq_tails[:, HIST_ROWS - (SSD_CONV - 1):])
        p_sc.append(ssd_tail)
        p_gc.append(gdn_tail)
        p_sh.append(h_new.reshape(bsz, SSD_HEADS, SSD_HEAD_DIM, SSD_STATE))
        p_gs.append(s_new.reshape(bsz, GDN_HEADS, GDN_HEAD_K, GDN_HEAD_V))
        xp = _xattn_prefill(xp, y, o, lw['w_out_ssd'], lw['w_out_gdn'], lw['norm_xa_w'], lw['xa_wq'], mk, mv,
                            lw['xa_wo'], bsz, seq, _tile(seq, 512))
        xp, gate_tails = _ffn_prefill(xp, lw['norm_ffn_w'], lw['ffn_w_gu'], lw['ffn_conv_w'], lw['ffn_conv_b'],
                                      lw['ffn_w_down'], seq, tm_c)
        p_fc.append(gate_tails.reshape(bsz, seq // tm_c, HIST_ROWS, d_ff)[:, -1, HIST_ROWS - (FFN_CONV - 1):])

        proj_s = _norm_mm(xs, lw['norm_mix_w'], lw['w_in'], nb)
        ssd_new, gdn_new = _unpermute_conv_rows(proj_s)
        s_sc.append(jnp.concatenate([state_ssd_conv[i][:, 1:], ssd_new[:, None]], axis=1))
        s_gc.append(jnp.concatenate([state_gdn_conv[i][:, 1:], gdn_new[:, None]], axis=1))
        rows = _decode_rows(proj_s, state_ssd_conv[i], state_gdn_conv[i], lw)
        y_s, o_s, hs_new, ss_new = _decode_state(rows, hs_all, ss_all, i, dec_tile, new_states)
        new_states = (hs_new, ss_new)
        xs = _decode_out(xs, y_s, rows[0], o_s, proj_s, lw)
        q_s = _norm_mm(xs, lw['norm_xa_w'], lw['xa_wq'], nb)
        att = _xattn_decode(q_s, mk_all, mv_all, i, _tile(nb, 4))
        xs = _mm_res(xs, [att], [lw['xa_wo']], nb)
        gu_s = _norm_mm(xs, lw['norm_ffn_w'], lw['ffn_w_gu'], nb)
        s_fc.append(jnp.concatenate([state_ffn_conv[i][:, 1:], gu_s[:, None, :d_ff]], axis=1))
        xs = _ffn_decode(xs, gu_s, state_ffn_conv[i][:, 0], state_ffn_conv[i][:, 1],
                         lw['ffn_conv_w'], lw['ffn_conv_b'], lw['ffn_w_down'])

    y_prompt = _rmsnorm(xp, final_norm_w, tm_p).reshape(bsz, seq, D_MODEL)
    y_sample = _rmsnorm(xs, final_norm_w, nb).reshape(nb, 1, D_MODEL)
    return (y_prompt, y_sample, jnp.stack(mkp), jnp.stack(mvp),
            jnp.stack(p_sc), jnp.stack(p_sh), jnp.stack(p_gc), jnp.stack(p_gs), jnp.stack(p_fc),
            jnp.stack(s_sc), new_states[0].reshape(state_ssd.shape), jnp.stack(s_gc),
            new_states[1].reshape(state_gdn.shape), jnp.stack(s_fc))
```
